```python
import math
import jax, jax.numpy as jnp
from jax import lax
import numpy as np

D_MODEL = 2048
BATCH = 2
SEQ = 4096
DEPTH = 2

RMS_EPS = 1e-6

GLA_WIDTH = D_MODEL // 2
GLA_HEADS = 4
GLA_KEY_WIDTH = GLA_WIDTH // 2
GLA_DK = GLA_KEY_WIDTH // GLA_HEADS
GLA_DV = GLA_WIDTH // GLA_HEADS
GLA_GATE_RANK = 16
GLA_GATE_TAU = 16.0
GLA_CHUNK = 64
GLA_SPLITS = (GLA_KEY_WIDTH, GLA_KEY_WIDTH, GLA_WIDTH, GLA_WIDTH, GLA_GATE_RANK)
GLA_COLS = sum(GLA_SPLITS)

RWKV_WIDTH = D_MODEL - GLA_WIDTH
RWKV_HEAD = 64
RWKV_HEADS = RWKV_WIDTH // RWKV_HEAD
RWKV_DECAY_LORA = max(32, int(round(1.8 * RWKV_WIDTH ** 0.5 / 32)) * 32)
RWKV_AAA_LORA = max(32, int(round(1.8 * RWKV_WIDTH ** 0.5 / 32)) * 32)
RWKV_GATE_LORA = max(32, int(round(0.6 * RWKV_WIDTH ** 0.8 / 32)) * 32)
RWKV_SPLITS = (RWKV_WIDTH, RWKV_WIDTH, RWKV_WIDTH, RWKV_DECAY_LORA, RWKV_AAA_LORA, RWKV_GATE_LORA)
RWKV_COLS = sum(RWKV_SPLITS)
RWKV_LN_EPS = RWKV_HEAD * 1e-5

MIX_IN_COLS = GLA_COLS + RWKV_COLS

MOBA_HEADS = 16
MOBA_HD = D_MODEL // MOBA_HEADS
MOBA_BLOCK = 256
MOBA_TOPK = 3
MOBA_Q_CHUNK = 16
NEG_INF = -1e30

FFN_HIDDEN = -(-8 * D_MODEL // (3 * 256)) * 256

kernel_name = 'hybrid_gla_rwkv7_moba_alibi_swiglu'


def rmsnorm(x, g):
    xf = x.astype(jnp.float32)
    y = xf * lax.rsqrt(jnp.mean(xf * xf, axis=-1, keepdims=True) + RMS_EPS)
    return (y * g.astype(jnp.float32)).astype(x.dtype)


def _split(t, sizes):
    out, start = [], 0
    for s in sizes:
        out.append(t[..., start:start + s])
        start += s
    return out


def token_shift(p):
    return jnp.pad(p, ((0, 0), (1, 0), (0, 0)))[:, :-1]


def gla_chunked(q, k, v, log_a):
    B, S, H, DK = q.shape
    DV = v.shape[-1]
    n = S // GLA_CHUNK

    def blocks(t):
        return t.reshape(B, n, GLA_CHUNK, H, t.shape[-1]).transpose(0, 3, 1, 2, 4).astype(jnp.float32)

    q, k, v, log_a = blocks(q), blocks(k), blocks(v), blocks(log_a)
    q = q * (DK ** -0.5)
    b = jnp.cumsum(log_a, axis=3)
    b_last = b[:, :, :, -1:, :]
    q_dec = q * jnp.exp(b)
    k_inv = k * jnp.exp(-b)
    causal = jnp.tril(jnp.ones((GLA_CHUNK, GLA_CHUNK), dtype=bool))
    att = jnp.where(causal, jnp.einsum('bhncd,bhnsd->bhncs', q_dec, k_inv), 0.0)
    o_intra = jnp.einsum('bhncs,bhnsv->bhncv', att, v)
    d_state = jnp.einsum('bhncd,bhncv->bhndv', k * jnp.exp(b_last - b), v)
    chunk_decay = jnp.exp(b_last[:, :, :, 0, :])

    def step(state, inp):
        dec, ds = inp
        return dec[..., None] * state + ds, state

    _, s_in = lax.scan(step, jnp.zeros((B, H, DK, DV), jnp.float32),
                       (jnp.moveaxis(chunk_decay, 2, 0), jnp.moveaxis(d_state, 2, 0)))
    s_in = jnp.moveaxis(s_in, 0, 2)
    o = o_intra + jnp.einsum('bhncd,bhndv->bhncv', q_dec, s_in)
    return o.transpose(0, 2, 3, 1, 4).reshape(B, S, H, DV)


def rwkv7_scan(r, w, k, v, a, b):
    B, S, H, N = r.shape

    def step(state, inp):
        r_t, w_t, k_t, v_t, a_t, b_t = inp
        sa = jnp.einsum('bhvk,bhk->bhv', state, a_t)
        state = (state * w_t[:, :, None, :] + sa[..., None] * b_t[:, :, None, :]
                 + v_t[..., None] * k_t[:, :, None, :])
        return state, jnp.einsum('bhvk,bhk->bhv', state, r_t)

    xs = tuple(jnp.moveaxis(t, 1, 0) for t in (r, w, k, v, a, b))
    _, y = lax.scan(step, jnp.zeros((B, H, N, N), jnp.float32), xs)
    return jnp.moveaxis(y, 0, 1)


def gla_rwkv_mixer(h, w_in, gla_gate_w2, gla_gate_b, gla_norm, rwkv_mu, rwkv_w0, rwkv_w2,
                   rwkv_a0, rwkv_a2, rwkv_g2, rwkv_k_k, rwkv_k_a, rwkv_r_k, rwkv_ln_w,
                   rwkv_ln_b, w_out):
    B, S, _ = h.shape
    p = (h @ w_in).astype(jnp.float32)
    p_gla, p_rwkv = p[..., :GLA_COLS], p[..., GLA_COLS:]

    q, k, v, og, gate_low = _split(p_gla, GLA_SPLITS)
    log_a = jax.nn.log_sigmoid(gate_low @ gla_gate_w2 + gla_gate_b) / GLA_GATE_TAU
    o_gla = gla_chunked(q.reshape(B, S, GLA_HEADS, GLA_DK), k.reshape(B, S, GLA_HEADS, GLA_DK),
                        v.reshape(B, S, GLA_HEADS, GLA_DV), log_a.reshape(B, S, GLA_HEADS, GLA_DK))
    o_gla = rmsnorm(o_gla, gla_norm).reshape(B, S, GLA_WIDTH) * jax.nn.silu(og)

    p_rwkv = p_rwkv + (token_shift(p_rwkv) - p_rwkv) * rwkv_mu
    r, k, v, w_low, a_low, g_low = _split(p_rwkv, RWKV_SPLITS)
    w_log = -jax.nn.softplus(-(rwkv_w0 + jnp.tanh(w_low) @ rwkv_w2)) - 0.5
    decay = jnp.exp(-jnp.exp(w_log))
    a = jax.nn.sigmoid(rwkv_a0 + a_low @ rwkv_a2)
    g = jax.nn.sigmoid(g_low) @ rwkv_g2
    hs = (B, S, RWKV_HEADS, RWKV_HEAD)
    kk = (k * rwkv_k_k).reshape(hs)
    kk = kk / jnp.maximum(jnp.sqrt(jnp.sum(kk * kk, axis=-1, keepdims=True)), 1e-12)
    k = k * (1.0 + (a - 1.0) * rwkv_k_a)
    r_h, k_h, v_h, a_h = r.reshape(hs), k.reshape(hs), v.reshape(hs), a.reshape(hs)
    y = rwkv7_scan(r_h, decay.reshape(hs), k_h, v_h, -kk, kk * a_h)
    mu = jnp.mean(y, axis=-1, keepdims=True)
    var = jnp.mean(jnp.square(y - mu), axis=-1, keepdims=True)
    y = ((y - mu) * lax.rsqrt(var + RWKV_LN_EPS) * rwkv_ln_w.reshape(RWKV_HEADS, RWKV_HEAD)
         + rwkv_ln_b.reshape(RWKV_HEADS, RWKV_HEAD))
    bonus = jnp.sum(r_h * k_h * rwkv_r_k, axis=-1, keepdims=True) * v_h
    o_rwkv = (y + bonus).reshape(B, S, RWKV_WIDTH) * g

    out = jnp.concatenate([o_gla, o_rwkv], axis=-1) @ w_out
    return out.astype(h.dtype)


def moba_attention(h, w_qkv, w_out):
    B, S, _ = h.shape
    H, D, BS, QC = MOBA_HEADS, MOBA_HD, MOBA_BLOCK, MOBA_Q_CHUNK
    qkv = (h @ w_qkv).reshape(B, S, 3, H, D)
    q, k, v = [qkv[:, :, i].transpose(0, 2, 1, 3).astype(jnp.float32) for i in range(3)]
    nb = -(-S // BS)
    pad = nb * BS - S
    k_blk = jnp.pad(k, ((0, 0), (0, 0), (0, pad), (0, 0))).reshape(B, H, nb, BS, D)
    v_blk = jnp.pad(v, ((0, 0), (0, 0), (0, pad), (0, 0))).reshape(B, H, nb, BS, D)
    k_mean = jnp.mean(k_blk, axis=3)
    n_sel = min(MOBA_TOPK, max(nb - 1, 0))
    scale = D ** -0.5
    slopes = jnp.exp2(-8.0 * jnp.arange(1, H + 1, dtype=jnp.float32) / H)[None, :, None, None]
    bidx = jnp.arange(B)[:, None, None, None]
    hidx = jnp.arange(H)[None, :, None, None]

    def chunk(ci):
        q0 = ci * QC
        blk = q0 // BS
        qc = lax.dynamic_slice_in_dim(q, q0, QC, axis=2)
        t = q0 + jnp.arange(QC)
        k_own = lax.dynamic_index_in_dim(k_blk, blk, axis=2, keepdims=False)
        v_own = lax.dynamic_index_in_dim(v_blk, blk, axis=2, keepdims=False)
        dist_own = (t[:, None] - (blk * BS + jnp.arange(BS))[None, :]).astype(jnp.float32)
        logit_own = jnp.einsum('bhqd,bhkd->bhqk', qc, k_own) * scale - slopes * dist_own
        logit_own = jnp.where(dist_own >= 0, logit_own, NEG_INF)
        if n_sel == 0:
            probs = jax.nn.softmax(logit_own, axis=-1)
            return jnp.einsum('bhqk,bhkd->bhqd', probs, v_own)
        gate = jnp.einsum('bhqd,bhnd->bhqn', qc, k_mean)
        gate = jnp.where(jnp.arange(nb) < blk, gate, NEG_INF)
        _, sel = lax.top_k(gate, n_sel)
        valid = jnp.arange(n_sel) < blk
        k_sel = k_blk[bidx, hidx, sel]
        v_sel = v_blk[bidx, hidx, sel]
        s_sel = sel[..., None] * BS + jnp.arange(BS)
        dist_sel = (t[:, None, None] - s_sel).astype(jnp.float32)
        logit_sel = jnp.einsum('bhqd,bhqnkd->bhqnk', qc, k_sel) * scale - slopes[..., None] * dist_sel
        logit_sel = jnp.where(valid[:, None], logit_sel, NEG_INF).reshape(B, H, QC, n_sel * BS)
        probs = jax.nn.softmax(jnp.concatenate([logit_sel, logit_own], axis=-1), axis=-1)
        p_sel = probs[..., :n_sel * BS].reshape(B, H, QC, n_sel, BS)
        return (jnp.einsum('bhqnk,bhqnkd->bhqd', p_sel, v_sel)
                + jnp.einsum('bhqk,bhkd->bhqd', probs[..., n_sel * BS:], v_own))

    out = lax.map(chunk, jnp.arange(S // QC))
    out = out.transpose(1, 0, 3, 2, 4).reshape(B, S, H * D)
    return (out @ w_out).astype(h.dtype)


def swiglu(h, w_gate, w_up, w_down):
    return (jax.nn.silu(h @ w_gate) * (h @ w_up)) @ w_down


def setup_inputs(seed: int = 0) -> dict:
    key = jax.random.key(seed)
    ks = iter(jax.random.split(key, 32))
    n_even = (DEPTH + 1) // 2
    n_odd = DEPTH // 2
    D, F = D_MODEL, FFN_HIDDEN
    f32 = jnp.float32

    def nrm(shape, scale):
        return jax.random.normal(next(ks), shape, f32) * scale

    return {
        'x': nrm((BATCH, SEQ, D), 1.0),
        'norm_mix': 1.0 + nrm((DEPTH, D), 0.02),
        'norm_ffn': 1.0 + nrm((DEPTH, D), 0.02),
        'norm_final': 1.0 + nrm((D,), 0.02),
        'mix_in_w': nrm((n_even, D, MIX_IN_COLS), D ** -0.5),
        'gla_gate_w2': nrm((n_even, GLA_GATE_RANK, GLA_KEY_WIDTH), GLA_GATE_RANK ** -0.5),
        'gla_gate_b': nrm((n_even, GLA_KEY_WIDTH), 0.1),
        'gla_norm': 1.0 + nrm((n_even, GLA_DV), 0.02),
        'rwkv_mu': jax.random.uniform(next(ks), (n_even, RWKV_COLS), f32),
        'rwkv_w0': nrm((n_even, RWKV_WIDTH), 0.5),
        'rwkv_w2': nrm((n_even, RWKV_DECAY_LORA, RWKV_WIDTH), 0.1),
        'rwkv_a0': nrm((n_even, RWKV_WIDTH), 0.1),
        'rwkv_a2': nrm((n_even, RWKV_AAA_LORA, RWKV_WIDTH), 0.5 * RWKV_AAA_LORA ** -0.5),
        'rwkv_g2': nrm((n_even, RWKV_GATE_LORA, RWKV_WIDTH), RWKV_GATE_LORA ** -0.5),
        'rwkv_k_k': 0.85 + nrm((n_even, RWKV_WIDTH), 0.05),
        'rwkv_k_a': 1.0 + nrm((n_even, RWKV_WIDTH), 0.05),
        'rwkv_r_k': nrm((n_even, RWKV_HEADS, RWKV_HEAD), 0.1),
        'rwkv_ln_w': 1.0 + nrm((n_even, RWKV_WIDTH), 0.02),
        'rwkv_ln_b': nrm((n_even, RWKV_WIDTH), 0.01),
        'mix_out_w': nrm((n_even, D, D), D ** -0.5),
        'attn_qkv_w': nrm((n_odd, D, 3 * D), D ** -0.5),
        'attn_out_w': nrm((n_odd, D, D), D ** -0.5),
        'ffn_gate_w': nrm((DEPTH, D, F), D ** -0.5),
        'ffn_up_w': nrm((DEPTH, D, F), D ** -0.5),
        'ffn_down_w': nrm((DEPTH, F, D), F ** -0.5),
    }


def reference(x, norm_mix, norm_ffn, norm_final, mix_in_w, gla_gate_w2, gla_gate_b, gla_norm,
              rwkv_mu, rwkv_w0, rwkv_w2, rwkv_a0, rwkv_a2, rwkv_g2, rwkv_k_k, rwkv_k_a, rwkv_r_k,
              rwkv_ln_w, rwkv_ln_b, mix_out_w, attn_qkv_w, attn_out_w, ffn_gate_w, ffn_up_w,
              ffn_down_w):
    h = x
    for layer in range(DEPTH):
        i = layer // 2
        hn = rmsnorm(h, norm_mix[layer])
        if layer % 2 == 0:
            h = h + gla_rwkv_mixer(hn, mix_in_w[i], gla_gate_w2[i], gla_gate_b[i], gla_norm[i],
                                   rwkv_mu[i], rwkv_w0[i], rwkv_w2[i], rwkv_a0[i], rwkv_a2[i],
                                   rwkv_g2[i], rwkv_k_k[i], rwkv_k_a[i], rwkv_r_k[i],
                                   rwkv_ln_w[i], rwkv_ln_b[i], mix_out_w[i])
        else:
            h = h + moba_attention(hn, attn_qkv_w[i], attn_out_w[i])
        h = h + swiglu(rmsnorm(h, norm_ffn[layer]), ffn_gate_w[layer], ffn_up_w[layer], ffn_down_w[layer])
    return rmsnorm(h, norm_final)
```

```python
import functools

import jax
import jax.numpy as jnp
from jax import lax
from jax.experimental import pallas as pl
from jax.experimental.pallas import tpu as pltpu

F32 = jnp.float32
BF16 = jnp.bfloat16
HI = lax.Precision.HIGHEST

D_MODEL = 2048
RMS_EPS = 1e-6

GLA_HEADS = 4
GLA_DK = 128
GLA_DV = 256
GLA_KEY_WIDTH = GLA_HEADS * GLA_DK
GLA_WIDTH = GLA_HEADS * GLA_DV
GLA_GATE_RANK = 16
GLA_GATE_TAU = 16.0
GLA_CHUNK = 64

RWKV_WIDTH = 1024
RWKV_HEAD = 64
RWKV_PAIR = 2 * RWKV_HEAD
RWKV_PAIRS = RWKV_WIDTH // RWKV_PAIR
RWKV_DECAY_LORA = 64
RWKV_AAA_LORA = 64
RWKV_GATE_LORA = 160
RWKV_LN_EPS = RWKV_HEAD * 1e-5
RWKV_CHUNK = 64

MOBA_HEADS = 16
MOBA_HD = 128
MOBA_BLOCK = 256
MOBA_TOPK = 3
NEG_INF = -1e30

FFN_HIDDEN = 5632

LANES = 128

COL_GLA_Q = 0
COL_GLA_K = 512
COL_GLA_V = 1024
COL_GLA_OG = 2048
COL_RWKV_R = 3072
COL_RWKV_K = 4096
COL_RWKV_V = 5120
COL_LOW_GATE = 6144
COL_LOW_WA = 6272
COL_LOW_G = 6400
MIX_COLS = 6656

VMEM_LIMIT = 56 * 1024 * 1024


def _mm(a, b):
    return jnp.dot(a.astype(BF16), b.astype(BF16), preferred_element_type=F32)


def _mm_nt(a, b):
    return lax.dot_general(a.astype(BF16), b.astype(BF16), (((1,), (1,)), ((), ())),
                           preferred_element_type=F32)


def _mm_hi(a, b):
    return jnp.dot(a, b, preferred_element_type=F32, precision=HI)


def _mm_nt_hi(a, b):
    return lax.dot_general(a, b, (((1,), (1,)), ((), ())), preferred_element_type=F32,
                           precision=HI)


def _sigmoid(x):
    return 1.0 / (1.0 + jnp.exp(-x))


def _softplus(x):
    return jnp.maximum(x, 0.0) + jnp.log1p(jnp.exp(-jnp.abs(x)))


def _iota2(shape, axis):
    return lax.broadcasted_iota(jnp.int32, shape, axis)


def _norm_matmul_kernel(x_ref, g_ref, w_ref, o_ref, xn_ref):
    @pl.when(pl.program_id(1) == 0)
    def _():
        x = x_ref[...]
        ms = jnp.mean(x * x, axis=-1, keepdims=True)
        xn_ref[...] = (x * lax.rsqrt(ms + RMS_EPS) * g_ref[...]).astype(BF16)

    o_ref[...] = jnp.dot(xn_ref[...], w_ref[...], preferred_element_type=F32).astype(o_ref.dtype)


def norm_matmul(x, g, w, *, tm, tn, out_dtype=F32):
    M, D = x.shape
    N = w.shape[1]
    return pl.pallas_call(
        _norm_matmul_kernel,
        grid=(M // tm, N // tn),
        in_specs=[
            pl.BlockSpec((tm, D), lambda i, j: (i, 0)),
            pl.BlockSpec((1, D), lambda i, j: (0, 0)),
            pl.BlockSpec((D, tn), lambda i, j: (0, j)),
        ],
        out_specs=pl.BlockSpec((tm, tn), lambda i, j: (i, j)),
        out_shape=jax.ShapeDtypeStruct((M, N), out_dtype),
        scratch_shapes=[pltpu.VMEM((tm, D), BF16)],
        compiler_params=pltpu.CompilerParams(
            dimension_semantics=("parallel", "arbitrary"), vmem_limit_bytes=VMEM_LIMIT),
    )(x, g.reshape(1, D), w)


def _proj_res_kernel(*refs, n_in):
    res_ref = refs[0]
    o_ref = refs[1 + 2 * n_in]
    acc = res_ref[...]
    for i in range(n_in):
        acc = acc + jnp.dot(refs[1 + 2 * i][...], refs[2 + 2 * i][...],
                            preferred_element_type=F32)
    o_ref[...] = acc


def proj_residual(res, pairs, *, tm, tn):
    M, N = res.shape
    in_specs = [pl.BlockSpec((tm, tn), lambda i, j: (i, j))]
    args = [res]
    for a, w in pairs:
        K = a.shape[1]
        in_specs.append(pl.BlockSpec((tm, K), lambda i, j: (i, 0)))
        in_specs.append(pl.BlockSpec((K, tn), lambda i, j: (0, j)))
        args += [a, w]
    return pl.pallas_call(
        functools.partial(_proj_res_kernel, n_in=len(pairs)),
        grid=(M // tm, N // tn),
        in_specs=in_specs,
        out_specs=pl.BlockSpec((tm, tn), lambda i, j: (i, j)),
        out_shape=jax.ShapeDtypeStruct((M, N), F32),
        compiler_params=pltpu.CompilerParams(
            dimension_semantics=("parallel", "arbitrary"), vmem_limit_bytes=VMEM_LIMIT),
    )(*args)


def _ffn_kernel(x_ref, g_ref, wg_ref, wu_ref, wd_ref, gf_ref, o_ref, xn_ref, *, final_norm):
    j = pl.program_id(1)

    @pl.when(j == 0)
    def _():
        x = x_ref[...]
        ms = jnp.mean(x * x, axis=-1, keepdims=True)
        xn_ref[...] = (x * lax.rsqrt(ms + RMS_EPS) * g_ref[...]).astype(BF16)
        o_ref[...] = x

    xn = xn_ref[...]
    gate = jnp.dot(xn, wg_ref[...], preferred_element_type=F32)
    up = jnp.dot(xn, wu_ref[...], preferred_element_type=F32)
    act = (gate * _sigmoid(gate) * up).astype(BF16)
    o_ref[...] += jnp.dot(act, wd_ref[...], preferred_element_type=F32)

    if final_norm:
        @pl.when(j == pl.num_programs(1) - 1)
        def _():
            h = o_ref[...]
            ms = jnp.mean(h * h, axis=-1, keepdims=True)
            o_ref[...] = h * lax.rsqrt(ms + RMS_EPS) * gf_ref[...]


def ffn_residual(x, g, wg, wu, wd, g_final, *, tm, tf, final_norm):
    M, D = x.shape
    F = wg.shape[1]
    return pl.pallas_call(
        functools.partial(_ffn_kernel, final_norm=final_norm),
        grid=(M // tm, F // tf),
        in_specs=[
            pl.BlockSpec((tm, D), lambda i, j: (i, 0)),
            pl.BlockSpec((1, D), lambda i, j: (0, 0)),
            pl.BlockSpec((D, tf), lambda i, j: (0, j)),
            pl.BlockSpec((D, tf), lambda i, j: (0, j)),
            pl.BlockSpec((tf, D), lambda i, j: (j, 0)),
            pl.BlockSpec((1, D), lambda i, j: (0, 0)),
        ],
        out_specs=pl.BlockSpec((tm, D), lambda i, j: (i, 0)),
        out_shape=jax.ShapeDtypeStruct((M, D), F32),
        scratch_shapes=[pltpu.VMEM((tm, D), BF16)],
        compiler_params=pltpu.CompilerParams(
            dimension_semantics=("parallel", "arbitrary"), vmem_limit_bytes=VMEM_LIMIT),
    )(x, g.reshape(1, D), wg, wu, wd, g_final.reshape(1, D))


def _gla_kernel(q_ref, k_ref, v_ref, og_ref, gl_ref, w2_ref, b_ref, gn_ref, o_ref, st_ref,
                *, block, chunk):
    @pl.when(pl.program_id(2) == 0)
    def _():
        st_ref[...] = jnp.zeros_like(st_ref)

    logit = _mm_hi(gl_ref[0], w2_ref[...]) + b_ref[...]
    log_a = -_softplus(-logit) * (1.0 / GLA_GATE_TAU)
    tri = _iota2((chunk, chunk), 0) >= _iota2((chunk, chunk), 1)
    tri_f = tri.astype(F32)
    scale = GLA_DK ** -0.5
    gn = gn_ref[...]

    for c in range(block // chunk):
        sl = pl.ds(c * chunk, chunk)
        la = log_a[c * chunk:(c + 1) * chunk]
        b = _mm_hi(tri_f, la)
        b_last = b[chunk - 1:chunk]
        q = q_ref[0, sl, :] * scale
        k = k_ref[0, sl, :]
        v = v_ref[0, sl, :]
        q_dec = q * jnp.exp(b)
        k_inv = k * jnp.exp(-b)
        k_dec = k * jnp.exp(b_last - b)
        att = jnp.where(tri, _mm_nt(q_dec, k_inv), 0.0)
        st = st_ref[...]
        o = _mm(att, v) + _mm_nt(q_dec, st)
        st_ref[...] = st * jnp.exp(b_last) + _mm(v.T, k_dec)
        ms = jnp.mean(o * o, axis=-1, keepdims=True)
        og = og_ref[0, sl, :]
        o_ref[0, sl, :] = (o * lax.rsqrt(ms + RMS_EPS) * gn * (og * _sigmoid(og))).astype(o_ref.dtype)


def gla_mixer(p3, gate_w2p, gate_b, gla_norm, *, block):
    B, S, _ = p3.shape
    kq, kk_, kv, kog = (COL_GLA_Q // GLA_DK, COL_GLA_K // GLA_DK, COL_GLA_V // GLA_DV,
                        COL_GLA_OG // GLA_DV)
    klow = COL_LOW_GATE // LANES
    return pl.pallas_call(
        functools.partial(_gla_kernel, block=block, chunk=GLA_CHUNK),
        grid=(B, GLA_HEADS, S // block),
        in_specs=[
            pl.BlockSpec((1, block, GLA_DK), lambda b, h, s: (b, s, kq + h)),
            pl.BlockSpec((1, block, GLA_DK), lambda b, h, s: (b, s, kk_ + h)),
            pl.BlockSpec((1, block, GLA_DV), lambda b, h, s: (b, s, kv + h)),
            pl.BlockSpec((1, block, GLA_DV), lambda b, h, s: (b, s, kog + h)),
            pl.BlockSpec((1, block, LANES), lambda b, h, s: (b, s, klow)),
            pl.BlockSpec((LANES, GLA_DK), lambda b, h, s: (0, h)),
            pl.BlockSpec((1, GLA_DK), lambda b, h, s: (0, h)),
            pl.BlockSpec((1, GLA_DV), lambda b, h, s: (0, 0)),
        ],
        out_specs=pl.BlockSpec((1, block, GLA_DV), lambda b, h, s: (b, s, h)),
        out_shape=jax.ShapeDtypeStruct((B, S, GLA_WIDTH), BF16),
        scratch_shapes=[pltpu.VMEM((GLA_DV, GLA_DK), F32)],
        compiler_params=pltpu.CompilerParams(
            dimension_semantics=("parallel", "parallel", "arbitrary"),
            vmem_limit_bytes=VMEM_LIMIT),
    )(p3, p3, p3, p3, p3, gate_w2p, gate_b.reshape(1, -1), gla_norm.reshape(1, -1))


def _unit_lower_inverse(a, rowi, coli):
    def same_block(s):
        sh = s.bit_length() - 1
        return (rowi >> sh) == (coli >> sh)

    eye = (rowi == coli).astype(F32)
    a8 = jnp.where(same_block(8), a, 0.0)
    inv = eye + a8
    p = _mm_hi(a8, a8)
    inv = inv + _mm_hi(inv, p)
    p = _mm_hi(p, p)
    inv = inv + _mm_hi(inv, p)
    s = 8
    while s < RWKV_CHUNK:
        e = jnp.where(same_block(2 * s) & jnp.logical_not(same_block(s)), a, 0.0)
        inv = inv + _mm_hi(_mm_hi(inv, e), inv)
        s *= 2
    return inv


def _rwkv_kernel(r_ref, k_ref, v_ref, wa_ref, g_ref, vec_ref, muwa_ref, mug_ref, w2_ref, a2_ref,
                 g2_ref, o_ref, st_ref, prev_ref, prevg_ref, *, block, chunk):
    T = chunk
    first = pl.program_id(2) == 0

    @pl.when(first)
    def _():
        st_ref[...] = jnp.zeros_like(st_ref)
        prev_ref[...] = jnp.zeros_like(prev_ref)
        prevg_ref[...] = jnp.zeros_like(prevg_ref)

    row0 = _iota2((block, 1), 0) == 0

    def lerp(x, prev_row, mu):
        shifted = jnp.where(row0, prev_row, pltpu.roll(x, 1, axis=0))
        return x + (shifted - x) * mu

    vec = vec_ref[...]
    w0, a0, k_k, k_a, r_k, ln_w, ln_b = (vec[i:i + 1] for i in range(7))
    mu_r, mu_k, mu_v = (vec[i:i + 1] for i in range(7, 10))

    r_raw, k_raw, v_raw, wa_raw, g_raw = r_ref[0], k_ref[0], v_ref[0], wa_ref[0], g_ref[0]
    prev = prev_ref[...]
    r_all = lerp(r_raw, prev[0:1], mu_r)
    k_all = lerp(k_raw, prev[1:2], mu_k)
    v_all = lerp(v_raw, prev[2:3], mu_v)
    wa = lerp(wa_raw, prev[3:4], muwa_ref[...])
    g_low = lerp(g_raw, prevg_ref[0:1], mug_ref[...])
    prev_ref[0:1] = r_raw[block - 1:block]
    prev_ref[1:2] = k_raw[block - 1:block]
    prev_ref[2:3] = v_raw[block - 1:block]
    prev_ref[3:4] = wa_raw[block - 1:block]
    prevg_ref[0:1] = g_raw[block - 1:block]

    z = w0 + _mm_hi(jnp.tanh(wa), w2_ref[...])
    logw_all = -jnp.exp(-_softplus(-z) - 0.5)
    lr_all = _sigmoid(a0 + _mm_hi(wa, a2_ref[...]))
    gate_all = _mm_hi(_sigmoid(g_low), g2_ref[...])

    lane = _iota2((1, RWKV_PAIR), 1)
    m0 = (lane < RWKV_HEAD).astype(F32)
    m1 = 1.0 - m0
    rowi = _iota2((2 * T, 2 * T), 0)
    coli = _iota2((2 * T, 2 * T), 1)
    strict = rowi > coli
    incl = rowi >= coli
    head_ones = ((rowi < RWKV_HEAD) == (coli < RWKV_HEAD)).astype(F32)
    tri_t = (_iota2((T, T), 0) >= _iota2((T, T), 1)).astype(F32)

    def hsum(x):
        return _mm_hi(x, head_ones)

    def stack(x):
        return jnp.concatenate([x * m0, x * m1], axis=0)

    for c in range(block // T):
        lo, hi = c * T, (c + 1) * T
        r, k, v = r_all[lo:hi], k_all[lo:hi], v_all[lo:hi]
        lw, lr = logw_all[lo:hi], lr_all[lo:hi]

        kkp = k * k_k
        kk = kkp / jnp.maximum(jnp.sqrt(hsum(kkp * kkp)), 1e-12)
        k2 = k * (1.0 + (lr - 1.0) * k_a)
        b_vec = kk * lr

        cum = _mm_hi(tri_t, lw)
        cum_last = cum[T - 1:T]
        w_inv = jnp.exp(-cum)
        w_tail = jnp.exp(cum_last - cum)
        a_s = stack(-kk * jnp.exp(cum - lw))
        b_s = stack(b_vec * w_inv)
        k_s = stack(k2 * w_inv)
        r_s = stack(r * jnp.exp(cum))
        v_s = stack(v)
        bk_tail = jnp.concatenate([stack(b_vec * w_tail), stack(k2 * w_tail)], axis=0)

        a_ab = jnp.where(strict, _mm_nt_hi(a_s, b_s), 0.0)
        a_ak = jnp.where(strict, _mm_nt_hi(a_s, k_s), 0.0)
        a_rb = jnp.where(incl, _mm_nt_hi(r_s, b_s), 0.0)
        a_rk = jnp.where(incl, _mm_nt_hi(r_s, k_s), 0.0)
        inv = _unit_lower_inverse(a_ab, rowi, coli)

        st = st_ref[...]
        u = _mm_hi(inv, _mm_nt_hi(a_s, st) + _mm_hi(a_ak, v_s))
        y_s = _mm_nt_hi(r_s, st) + _mm_hi(a_rb, u) + _mm_hi(a_rk, v_s)
        uv = jnp.concatenate([u, v_s], axis=0)
        st_ref[...] = st * jnp.exp(cum_last) + _mm_hi(uv.T, bk_tail)

        y = y_s[:T] + y_s[T:]
        mu = hsum(y) * (1.0 / RWKV_HEAD)
        d = y - mu
        var = hsum(d * d) * (1.0 / RWKV_HEAD)
        yn = d * lax.rsqrt(var + RWKV_LN_EPS) * ln_w + ln_b
        bonus = hsum(r * k2 * r_k) * v
        o_ref[0, pl.ds(lo, T), :] = ((yn + bonus) * gate_all[lo:hi]).astype(o_ref.dtype)


def rwkv_mixer(p3, vecs, mu_wa, mu_g, w2p, a2p, g2p, *, block):
    B, S, _ = p3.shape
    kr, kk_, kv = COL_RWKV_R // RWKV_PAIR, COL_RWKV_K // RWKV_PAIR, COL_RWKV_V // RWKV_PAIR
    kwa = COL_LOW_WA // LANES
    kg = COL_LOW_G // (2 * LANES)
    return pl.pallas_call(
        functools.partial(_rwkv_kernel, block=block, chunk=RWKV_CHUNK),
        grid=(B, RWKV_PAIRS, S // block),
        in_specs=[
            pl.BlockSpec((1, block, RWKV_PAIR), lambda b, j, s: (b, s, kr + j)),
            pl.BlockSpec((1, block, RWKV_PAIR), lambda b, j, s: (b, s, kk_ + j)),
            pl.BlockSpec((1, block, RWKV_PAIR), lambda b, j, s: (b, s, kv + j)),
            pl.BlockSpec((1, block, LANES), lambda b, j, s: (b, s, kwa)),
            pl.BlockSpec((1, block, 2 * LANES), lambda b, j, s: (b, s, kg)),
            pl.BlockSpec((16, RWKV_PAIR), lambda b, j, s: (0, j)),
            pl.BlockSpec((1, LANES), lambda b, j, s: (0, 0)),
            pl.BlockSpec((1, 2 * LANES), lambda b, j, s: (0, 0)),
            pl.BlockSpec((LANES, RWKV_PAIR), lambda b, j, s: (0, j)),
            pl.BlockSpec((LANES, RWKV_PAIR), lambda b, j, s: (0, j)),
            pl.BlockSpec((2 * LANES, RWKV_PAIR), lambda b, j, s: (0, j)),
        ],
        out_specs=pl.BlockSpec((1, block, RWKV_PAIR), lambda b, j, s: (b, s, j)),
        out_shape=jax.ShapeDtypeStruct((B, S, RWKV_WIDTH), BF16),
        scratch_shapes=[
            pltpu.VMEM((RWKV_PAIR, RWKV_PAIR), F32),
            pltpu.VMEM((8, LANES), F32),
            pltpu.VMEM((8, 2 * LANES), F32),
        ],
        compiler_params=pltpu.CompilerParams(
            dimension_semantics=("parallel", "parallel", "arbitrary"),
            vmem_limit_bytes=VMEM_LIMIT),
    )(p3, p3, p3, p3, p3, vecs, mu_wa, mu_g, w2p, a2p, g2p)


def _moba_kernel(q_ref, k_ref, v_ref, slope_ref, o_ref, kb_ref, vb_ref, kmean_ref,
                 m_ref, l_ref, acc_ref, *, nb):
    BS = MOBA_BLOCK
    i = pl.program_id(2)

    @pl.when(i == 0)
    def _():
        kmean_ref[...] = jnp.zeros_like(kmean_ref)
        for j in range(nb):
            kj = k_ref[0, pl.ds(j * BS, BS), :]
            kmean_ref[j:j + 1, :] = jnp.mean(kj, axis=0, keepdims=True)
            kb_ref[pl.ds(j * BS, BS), :] = kj.astype(BF16)
            vb_ref[pl.ds(j * BS, BS), :] = v_ref[0, pl.ds(j * BS, BS), :].astype(BF16)

    q = q_ref[0]
    q_b = q.astype(BF16)
    scale = MOBA_HD ** -0.5
    slope = slope_ref[0]
    rel = (_iota2((BS, BS), 0) - _iota2((BS, BS), 1)).astype(F32)

    colid = _iota2((BS, LANES), 1)
    gate = _mm_nt_hi(q, kmean_ref[...])
    gate = jnp.where(colid < i, gate, NEG_INF)
    sel = []
    for r in range(MOBA_TOPK):
        mx = jnp.max(gate, axis=-1, keepdims=True)
        idx = jnp.min(jnp.where(gate == mx, colid, LANES), axis=-1, keepdims=True)
        sel.append(jnp.where(r < i, idx, -1))
        gate = jnp.where(colid == idx, -jnp.inf, gate)

    s_own = _mm_nt(q_b, kb_ref[pl.ds(i * BS, BS), :]) * scale - slope * rel
    s_own = jnp.where(rel >= 0, s_own, NEG_INF)
    m0 = jnp.max(s_own, axis=-1, keepdims=True)
    p0 = jnp.exp(s_own - m0)
    m_ref[...] = m0
    l_ref[...] = jnp.sum(p0, axis=-1, keepdims=True)
    acc_ref[...] = _mm(p0, vb_ref[pl.ds(i * BS, BS), :])

    def past_block(j, carry):
        picked = (sel[0] == j) | (sel[1] == j) | (sel[2] == j)
        dist = rel + ((i - j) * BS).astype(F32)
        s = _mm_nt(q_b, kb_ref[pl.ds(j * BS, BS), :]) * scale - slope * dist
        s = jnp.where(picked, s, NEG_INF)
        m_old = m_ref[...]
        m_new = jnp.maximum(m_old, jnp.max(s, axis=-1, keepdims=True))
        alpha = jnp.exp(m_old - m_new)
        p = jnp.exp(s - m_new)
        l_ref[...] = alpha * l_ref[...] + jnp.sum(p, axis=-1, keepdims=True)
        acc_ref[...] = alpha * acc_ref[...] + _mm(p, vb_ref[pl.ds(j * BS, BS), :])
        m_ref[...] = m_new
        return carry

    lax.fori_loop(0, i, past_block, 0)
    o_ref[0] = (acc_ref[...] / l_ref[...]).astype(o_ref.dtype)


def moba_attention(qkv3):
    B, S, _ = qkv3.shape
    H, D, BS = MOBA_HEADS, MOBA_HD, MOBA_BLOCK
    nb = S // BS
    slopes = jnp.exp2(-8.0 * jnp.arange(1, H + 1, dtype=F32) / H)
    slopes = jnp.broadcast_to(slopes[:, None, None], (H, 1, BS))
    return pl.pallas_call(
        functools.partial(_moba_kernel, nb=nb),
        grid=(B, H, nb),
        in_specs=[
            pl.BlockSpec((1, BS, D), lambda b, h, i: (b, i, h)),
            pl.BlockSpec((1, S, D), lambda b, h, i: (b, 0, H + h)),
            pl.BlockSpec((1, S, D), lambda b, h, i: (b, 0, 2 * H + h)),
            pl.BlockSpec((1, 1, BS), lambda b, h, i: (h, 0, 0)),
        ],
        out_specs=pl.BlockSpec((1, BS, D), lambda b, h, i: (b, i, h)),
        out_shape=jax.ShapeDtypeStruct((B, S, H * D), BF16),
        scratch_shapes=[
            pltpu.VMEM((S, D), BF16),
            pltpu.VMEM((S, D), BF16),
            pltpu.VMEM((LANES, D), F32),
            pltpu.VMEM((BS, 1), F32),
            pltpu.VMEM((BS, 1), F32),
            pltpu.VMEM((BS, D), F32),
        ],
        compiler_params=pltpu.CompilerParams(
            dimension_semantics=("parallel", "parallel", "arbitrary"),
            vmem_limit_bytes=VMEM_LIMIT),
    )(qkv3, qkv3, qkv3, slopes)


def _pad_cols(w, n):
    return jnp.pad(w, ((0, 0), (0, n - w.shape[1])))


def _pad_rows(w, before, total):
    return jnp.pad(w, ((before, total - before - w.shape[0]), (0, 0)))


def _mix_in_layout(w):
    gk, gw = GLA_KEY_WIDTH, GLA_WIDTH
    g_end = 2 * gk + 2 * gw
    gla_main = w[:, :g_end]
    gla_gate = w[:, g_end:g_end + GLA_GATE_RANK]
    r0 = g_end + GLA_GATE_RANK
    rw_main = w[:, r0:r0 + 3 * RWKV_WIDTH]
    l0 = r0 + 3 * RWKV_WIDTH
    wa = w[:, l0:l0 + RWKV_DECAY_LORA + RWKV_AAA_LORA]
    g0 = l0 + RWKV_DECAY_LORA + RWKV_AAA_LORA
    gl = w[:, g0:g0 + RWKV_GATE_LORA]
    return jnp.concatenate(
        [gla_main, rw_main, _pad_cols(gla_gate, LANES), wa, _pad_cols(gl, 2 * LANES)], axis=1)


def mixer_layer0(hn_proj, gla_gate_w2, gla_gate_b, gla_norm, rwkv_mu, rwkv_w0, rwkv_w2, rwkv_a0,
                 rwkv_a2, rwkv_g2, rwkv_k_k, rwkv_k_a, rwkv_r_k, rwkv_ln_w, rwkv_ln_b,
                 *, gla_block, rwkv_block):
    W = RWKV_WIDTH
    o_gla = gla_mixer(hn_proj, _pad_rows(gla_gate_w2, 0, LANES), gla_gate_b, gla_norm,
                      block=gla_block)
    mu_r, mu_k, mu_v = rwkv_mu[:W], rwkv_mu[W:2 * W], rwkv_mu[2 * W:3 * W]
    mu_low = rwkv_mu[3 * W:]
    n_wa = RWKV_DECAY_LORA + RWKV_AAA_LORA
    vecs = jnp.stack([rwkv_w0, rwkv_a0, rwkv_k_k, rwkv_k_a, rwkv_r_k.reshape(-1), rwkv_ln_w,
                      rwkv_ln_b, mu_r, mu_k, mu_v] + [jnp.zeros((W,), F32)] * 6)
    mu_wa = mu_low[:n_wa].reshape(1, -1)
    mu_g = _pad_cols(mu_low[n_wa:].reshape(1, -1), 2 * LANES)
    w2p = _pad_rows(rwkv_w2, 0, LANES)
    a2p = _pad_rows(rwkv_a2, RWKV_DECAY_LORA, LANES)
    g2p = _pad_rows(rwkv_g2, 0, 2 * LANES)
    o_rwkv = rwkv_mixer(hn_proj, vecs, mu_wa, mu_g, w2p, a2p, g2p, block=rwkv_block)
    return o_gla, o_rwkv


def kernel(x, norm_mix, norm_ffn, norm_final, mix_in_w, gla_gate_w2, gla_gate_b, gla_norm, rwkv_mu, rwkv_w0, rwkv_w2, rwkv_a0, rwkv_a2, rwkv_g2, rwkv_k_k, rwkv_k_a, rwkv_r_k, rwkv_ln_w, rwkv_ln_b, mix_out_w, attn_qkv_w, attn_out_w, ffn_gate_w, ffn_up_w, ffn_down_w):
    B, S, D = x.shape
    M = B * S
    tm, tn, tf = 512, 512, 512
    h = x.reshape(M, D)

    w_in = _mix_in_layout(mix_in_w[0]).astype(BF16)
    p = norm_matmul(h, norm_mix[0], w_in, tm=tm, tn=tn).reshape(B, S, MIX_COLS)
    o_gla, o_rwkv = mixer_layer0(
        p, gla_gate_w2[0], gla_gate_b[0], gla_norm[0], rwkv_mu[0], rwkv_w0[0], rwkv_w2[0],
        rwkv_a0[0], rwkv_a2[0], rwkv_g2[0], rwkv_k_k[0], rwkv_k_a[0], rwkv_r_k[0], rwkv_ln_w[0],
        rwkv_ln_b[0], gla_block=512, rwkv_block=256)
    w_out = mix_out_w[0].astype(BF16)
    h = proj_residual(h, [(o_gla.reshape(M, GLA_WIDTH), w_out[:GLA_WIDTH]),
                          (o_rwkv.reshape(M, RWKV_WIDTH), w_out[GLA_WIDTH:])], tm=tm, tn=tn)
    h = ffn_residual(h, norm_ffn[0], ffn_gate_w[0].astype(BF16), ffn_up_w[0].astype(BF16),
                     ffn_down_w[0].astype(BF16), norm_final, tm=tm, tf=tf, final_norm=False)

    qkv = norm_matmul(h, norm_mix[1], attn_qkv_w[0].astype(BF16), tm=tm, tn=tn)
    o_attn = moba_attention(qkv.reshape(B, S, 3 * D))
    h = proj_residual(h, [(o_attn.reshape(M, D), attn_out_w[0].astype(BF16))], tm=tm, tn=tn)
    h = ffn_residual(h, norm_ffn[1], ffn_gate_w[1].astype(BF16), ffn_up_w[1].astype(BF16),
                     ffn_down_w[1].astype(BF16), norm_final, tm=tm, tf=tf, final_norm=True)
    return h.reshape(B, S, D)
```

```python
import functools

import jax
import jax.numpy as jnp
from jax import lax
from jax.experimental import pallas as pl
from jax.experimental.pallas import tpu as pltpu

F32 = jnp.float32
BF16 = jnp.bfloat16
HI = lax.Precision.HIGHEST

D_MODEL = 2048
RMS_EPS = 1e-6

GLA_HEADS = 4
GLA_DK = 128
GLA_DV = 256
GLA_KEY_WIDTH = GLA_HEADS * GLA_DK
GLA_WIDTH = GLA_HEADS * GLA_DV
GLA_GATE_RANK = 16
GLA_GATE_TAU = 16.0
GLA_CHUNK = 64

RWKV_WIDTH = 1024
RWKV_HEAD = 64
RWKV_PAIR = 2 * RWKV_HEAD
RWKV_PAIRS = RWKV_WIDTH // RWKV_PAIR
RWKV_DECAY_LORA = 64
RWKV_AAA_LORA = 64
RWKV_GATE_LORA = 160
RWKV_LN_EPS = RWKV_HEAD * 1e-5
RWKV_CHUNK = 64

MOBA_HEADS = 16
MOBA_HD = 128
MOBA_BLOCK = 256
MOBA_TOPK = 3
NEG_INF = -1e30

FFN_HIDDEN = 5632

LANES = 128

COL_GLA_Q = 0
COL_GLA_K = 512
COL_GLA_V = 1024
COL_GLA_OG = 2048
COL_RWKV_R = 3072
COL_RWKV_K = 4096
COL_RWKV_V = 5120
COL_LOW_GATE = 6144
COL_LOW_WA = 6272
COL_LOW_G = 6400
MIX_COLS = 6656

VMEM_LIMIT = 56 * 1024 * 1024


def _mm(a, b):
    return jnp.dot(a.astype(BF16), b.astype(BF16), preferred_element_type=F32)


def _mm_nt(a, b):
    return lax.dot_general(a.astype(BF16), b.astype(BF16), (((1,), (1,)), ((), ())),
                           preferred_element_type=F32)


def _mm_hi(a, b):
    return jnp.dot(a, b, preferred_element_type=F32, precision=HI)


def _mm_nt_hi(a, b):
    return lax.dot_general(a, b, (((1,), (1,)), ((), ())), preferred_element_type=F32,
                           precision=HI)


def _dot(a, b):
    return jnp.dot(a, b, preferred_element_type=F32)


def _dot_tn(a, b):
    return lax.dot_general(a, b, (((0,), (0,)), ((), ())), preferred_element_type=F32)


def _split(x, parts):
    out = []
    for _ in range(parts - 1):
        hi = x.astype(BF16)
        out.append(hi)
        x = x - hi.astype(F32)
    out.append(x.astype(BF16))
    return out


def _mm_split(a, b, *, a_parts=1, b_parts=1):
    a_p = _split(a, a_parts) if a_parts > 1 else [a.astype(BF16)]
    b_p = _split(b, b_parts) if b_parts > 1 else [b.astype(BF16)]
    acc = None
    for i, ai in enumerate(a_p):
        for j, bj in enumerate(b_p):
            if i + j < max(a_parts, b_parts):
                t = _dot(ai, bj)
                acc = t if acc is None else acc + t
    return acc


def _sigmoid(x):
    return 1.0 / (1.0 + jnp.exp(-x))


def _softplus(x):
    return jnp.maximum(x, 0.0) + jnp.log1p(jnp.exp(-jnp.abs(x)))


def _iota2(shape, axis):
    return lax.broadcasted_iota(jnp.int32, shape, axis)


def _norm_matmul_kernel(x_ref, g_ref, w_ref, o_ref, xn_ref):
    @pl.when(pl.program_id(1) == 0)
    def _():
        x = x_ref[...]
        ms = jnp.mean(x * x, axis=-1, keepdims=True)
        xn_ref[...] = (x * lax.rsqrt(ms + RMS_EPS) * g_ref[...]).astype(BF16)

    o_ref[...] = jnp.dot(xn_ref[...], w_ref[...], preferred_element_type=F32).astype(o_ref.dtype)


def norm_matmul(x, g, w, *, tm, tn, out_dtype=F32):
    M, D = x.shape
    N = w.shape[1]
    return pl.pallas_call(
        _norm_matmul_kernel,
        grid=(M // tm, N // tn),
        in_specs=[
            pl.BlockSpec((tm, D), lambda i, j: (i, 0)),
            pl.BlockSpec((1, D), lambda i, j: (0, 0)),
            pl.BlockSpec((D, tn), lambda i, j: (0, j)),
        ],
        out_specs=pl.BlockSpec((tm, tn), lambda i, j: (i, j)),
        out_shape=jax.ShapeDtypeStruct((M, N), out_dtype),
        scratch_shapes=[pltpu.VMEM((tm, D), BF16)],
        compiler_params=pltpu.CompilerParams(
            dimension_semantics=("parallel", "arbitrary"), vmem_limit_bytes=VMEM_LIMIT),
    )(x, g.reshape(1, D), w)


def _proj_res_kernel(*refs, n_in):
    res_ref = refs[0]
    o_ref = refs[1 + 2 * n_in]
    acc = res_ref[...]
    for i in range(n_in):
        acc = acc + jnp.dot(refs[1 + 2 * i][...], refs[2 + 2 * i][...],
                            preferred_element_type=F32)
    o_ref[...] = acc


def proj_residual(res, pairs, *, tm, tn):
    M, N = res.shape
    in_specs = [pl.BlockSpec((tm, tn), lambda i, j: (i, j))]
    args = [res]
    for a, w in pairs:
        K = a.shape[1]
        in_specs.append(pl.BlockSpec((tm, K), lambda i, j: (i, 0)))
        in_specs.append(pl.BlockSpec((K, tn), lambda i, j: (0, j)))
        args += [a, w]
    return pl.pallas_call(
        functools.partial(_proj_res_kernel, n_in=len(pairs)),
        grid=(M // tm, N // tn),
        in_specs=in_specs,
        out_specs=pl.BlockSpec((tm, tn), lambda i, j: (i, j)),
        out_shape=jax.ShapeDtypeStruct((M, N), F32),
        compiler_params=pltpu.CompilerParams(
            dimension_semantics=("parallel", "arbitrary"), vmem_limit_bytes=VMEM_LIMIT),
    )(*args)


def _ffn_kernel(x_ref, g_ref, wg_ref, wu_ref, wd_ref, gf_ref, o_ref, xn_ref, *, final_norm):
    j = pl.program_id(1)

    @pl.when(j == 0)
    def _():
        x = x_ref[...]
        ms = jnp.mean(x * x, axis=-1, keepdims=True)
        xn_ref[...] = (x * lax.rsqrt(ms + RMS_EPS) * g_ref[...]).astype(BF16)
        o_ref[...] = x

    xn = xn_ref[...]
    gate = jnp.dot(xn, wg_ref[...], preferred_element_type=F32)
    up = jnp.dot(xn, wu_ref[...], preferred_element_type=F32)
    act = (gate * _sigmoid(gate) * up).astype(BF16)
    o_ref[...] += jnp.dot(act, wd_ref[...], preferred_element_type=F32)

    if final_norm:
        @pl.when(j == pl.num_programs(1) - 1)
        def _():
            h = o_ref[...]
            ms = jnp.mean(h * h, axis=-1, keepdims=True)
            o_ref[...] = h * lax.rsqrt(ms + RMS_EPS) * gf_ref[...]


def ffn_residual(x, g, wg, wu, wd, g_final, *, tm, tf, final_norm):
    M, D = x.shape
    F = wg.shape[1]
    return pl.pallas_call(
        functools.partial(_ffn_kernel, final_norm=final_norm),
        grid=(M // tm, F // tf),
        in_specs=[
            pl.BlockSpec((tm, D), lambda i, j: (i, 0)),
            pl.BlockSpec((1, D), lambda i, j: (0, 0)),
            pl.BlockSpec((D, tf), lambda i, j: (0, j)),
            pl.BlockSpec((D, tf), lambda i, j: (0, j)),
            pl.BlockSpec((tf, D), lambda i, j: (j, 0)),
            pl.BlockSpec((1, D), lambda i, j: (0, 0)),
        ],
        out_specs=pl.BlockSpec((tm, D), lambda i, j: (i, 0)),
        out_shape=jax.ShapeDtypeStruct((M, D), F32),
        scratch_shapes=[pltpu.VMEM((tm, D), BF16)],
        compiler_params=pltpu.CompilerParams(
            dimension_semantics=("parallel", "arbitrary"), vmem_limit_bytes=VMEM_LIMIT),
    )(x, g.reshape(1, D), wg, wu, wd, g_final.reshape(1, D))


def _gla_kernel(q_ref, k_ref, v_ref, og_ref, gl_ref, w2_ref, b_ref, gn_ref, o_ref, st_ref,
                *, block, chunk):
    @pl.when(pl.program_id(2) == 0)
    def _():
        st_ref[...] = jnp.zeros_like(st_ref)

    logit = _mm_hi(gl_ref[0], w2_ref[...]) + b_ref[...]
    log_a = -_softplus(-logit) * (1.0 / GLA_GATE_TAU)
    tri = _iota2((chunk, chunk), 0) >= _iota2((chunk, chunk), 1)
    tri_f = tri.astype(F32)
    scale = GLA_DK ** -0.5
    gn = gn_ref[...]

    for c in range(block // chunk):
        sl = pl.ds(c * chunk, chunk)
        la = log_a[c * chunk:(c + 1) * chunk]
        b = _mm_hi(tri_f, la)
        b_last = b[chunk - 1:chunk]
        q = q_ref[0, sl, :] * scale
        k = k_ref[0, sl, :]
        v = v_ref[0, sl, :]
        q_dec = q * jnp.exp(b)
        k_inv = k * jnp.exp(-b)
        k_dec = k * jnp.exp(b_last - b)
        att = jnp.where(tri, _mm_nt(q_dec, k_inv), 0.0)
        st = st_ref[...]
        o = _mm(att, v) + _mm_nt(q_dec, st)
        st_ref[...] = st * jnp.exp(b_last) + _mm(v.T, k_dec)
        ms = jnp.mean(o * o, axis=-1, keepdims=True)
        og = og_ref[0, sl, :]
        o_ref[0, sl, :] = (o * lax.rsqrt(ms + RMS_EPS) * gn * (og * _sigmoid(og))).astype(o_ref.dtype)


def gla_mixer(p3, gate_w2p, gate_b, gla_norm, *, block):
    B, S, _ = p3.shape
    kq, kk_, kv, kog = (COL_GLA_Q // GLA_DK, COL_GLA_K // GLA_DK, COL_GLA_V // GLA_DV,
                        COL_GLA_OG // GLA_DV)
    klow = COL_LOW_GATE // LANES
    return pl.pallas_call(
        functools.partial(_gla_kernel, block=block, chunk=GLA_CHUNK),
        grid=(B, GLA_HEADS, S // block),
        in_specs=[
            pl.BlockSpec((1, block, GLA_DK), lambda b, h, s: (b, s, kq + h)),
            pl.BlockSpec((1, block, GLA_DK), lambda b, h, s: (b, s, kk_ + h)),
            pl.BlockSpec((1, block, GLA_DV), lambda b, h, s: (b, s, kv + h)),
            pl.BlockSpec((1, block, GLA_DV), lambda b, h, s: (b, s, kog + h)),
            pl.BlockSpec((1, block, LANES), lambda b, h, s: (b, s, klow)),
            pl.BlockSpec((LANES, GLA_DK), lambda b, h, s: (0, h)),
            pl.BlockSpec((1, GLA_DK), lambda b, h, s: (0, h)),
            pl.BlockSpec((1, GLA_DV), lambda b, h, s: (0, 0)),
        ],
        out_specs=pl.BlockSpec((1, block, GLA_DV), lambda b, h, s: (b, s, h)),
        out_shape=jax.ShapeDtypeStruct((B, S, GLA_WIDTH), BF16),
        scratch_shapes=[pltpu.VMEM((GLA_DV, GLA_DK), F32)],
        compiler_params=pltpu.CompilerParams(
            dimension_semantics=("parallel", "parallel", "arbitrary"),
            vmem_limit_bytes=VMEM_LIMIT),
    )(p3, p3, p3, p3, p3, gate_w2p, gate_b.reshape(1, -1), gla_norm.reshape(1, -1))


def _unit_lower_inverses(mats, rowi, coli):
    def same_block(s):
        sh = s.bit_length() - 1
        return (rowi >> sh) == (coli >> sh)

    eye = (rowi == coli).astype(F32)
    a8 = [jnp.where(same_block(8), a, 0.0).astype(BF16) for a in mats]
    inv = [eye + a.astype(F32) for a in a8]
    p = [_dot(a, a) for a in a8]
    inv = [x + _mm(x, y) for x, y in zip(inv, p)]
    p = [_mm(y, y) for y in p]
    inv = [x + _mm(x, y) for x, y in zip(inv, p)]
    s = 8
    while s < RWKV_CHUNK:
        off = same_block(2 * s) & jnp.logical_not(same_block(s))
        e = [jnp.where(off, a, 0.0).astype(BF16) for a in mats]
        inv_b = [x.astype(BF16) for x in inv]
        t = [_dot(x, y).astype(BF16) for x, y in zip(inv_b, e)]
        inv = [x + _dot(y, z) for x, y, z in zip(inv, t, inv_b)]
        s *= 2
    return inv


def _rwkv_kernel(r_ref, k_ref, v_ref, wa_ref, g_ref, vec_ref, muwa_ref, mug_ref, w2_ref, a2_ref,
                 g2_ref, o_ref, st_ref, prev_ref, prevg_ref, *, block, chunk):
    T = chunk
    first = pl.program_id(2) == 0

    @pl.when(first)
    def _():
        st_ref[...] = jnp.zeros_like(st_ref)
        prev_ref[...] = jnp.zeros_like(prev_ref)
        prevg_ref[...] = jnp.zeros_like(prevg_ref)

    row0 = _iota2((block, 1), 0) == 0

    def lerp(x, prev_row, mu):
        shifted = jnp.where(row0, prev_row, pltpu.roll(x, 1, axis=0))
        return x + (shifted - x) * mu

    vec = vec_ref[...]
    w0, a0, k_k, k_a, r_k, ln_w, ln_b = (vec[i:i + 1] for i in range(7))
    mu_r, mu_k, mu_v = (vec[i:i + 1] for i in range(7, 10))

    r_raw, k_raw, v_raw, wa_raw, g_raw = r_ref[0], k_ref[0], v_ref[0], wa_ref[0], g_ref[0]
    prev = prev_ref[...]
    r_all = lerp(r_raw, prev[0:1], mu_r)
    k_all = lerp(k_raw, prev[1:2], mu_k)
    v_all = lerp(v_raw, prev[2:3], mu_v)
    wa = lerp(wa_raw, prev[3:4], muwa_ref[...])
    g_low = lerp(g_raw, prevg_ref[0:1], mug_ref[...])
    prev_ref[0:1] = r_raw[block - 1:block]
    prev_ref[1:2] = k_raw[block - 1:block]
    prev_ref[2:3] = v_raw[block - 1:block]
    prev_ref[3:4] = wa_raw[block - 1:block]
    prevg_ref[0:1] = g_raw[block - 1:block]

    z = w0 + _mm_split(jnp.tanh(wa), w2_ref[...], a_parts=2, b_parts=2)
    logw_all = -jnp.exp(-_softplus(-z) - 0.5)
    lr_all = _sigmoid(a0 + _mm_split(wa, a2_ref[...], a_parts=2, b_parts=2))
    gate_all = _mm_split(_sigmoid(g_low), g2_ref[...], a_parts=2, b_parts=2)

    lane = _iota2((1, RWKV_PAIR), 1)
    m0 = (lane < RWKV_HEAD).astype(F32)
    m1 = 1.0 - m0
    rowi = _iota2((2 * T, 2 * T), 0)
    coli = _iota2((2 * T, 2 * T), 1)
    strict = rowi > coli
    incl = rowi >= coli
    head_ones = ((rowi < RWKV_HEAD) == (coli < RWKV_HEAD)).astype(BF16)
    tri_t = (_iota2((T, T), 0) >= _iota2((T, T), 1)).astype(BF16)
    ones_t = jnp.ones((T, RWKV_PAIR), BF16)

    def hsum(x):
        return _mm_split(x, head_ones, a_parts=2)

    def stack(x):
        return jnp.concatenate([x * m0, x * m1], axis=0)

    n_chunks = block // T

    pre = []
    for c in range(n_chunks):
        lo, hi = c * T, (c + 1) * T
        r, k, v = r_all[lo:hi], k_all[lo:hi], v_all[lo:hi]
        lw, lr = logw_all[lo:hi], lr_all[lo:hi]

        kkp = k * k_k
        kk = kkp / jnp.maximum(jnp.sqrt(hsum(kkp * kkp)), 1e-12)
        k2 = k * (1.0 + (lr - 1.0) * k_a)
        b_vec = kk * lr

        lw_parts = _split(lw, 3)
        cum = _dot(tri_t, lw_parts[0]) + _dot(tri_t, lw_parts[1]) + _dot(tri_t, lw_parts[2])
        cum_last = cum[T - 1:T]
        decay = jnp.exp(_dot_tn(lw_parts[0], ones_t) + _dot_tn(lw_parts[1], ones_t)
                        + _dot_tn(lw_parts[2], ones_t))
        w_inv = jnp.exp(-cum)
        w_tail = jnp.exp(cum_last - cum)
        a_s = stack(-kk * jnp.exp(cum - lw)).astype(BF16)
        r_s = stack(r * jnp.exp(cum)).astype(BF16)
        b_s = stack(b_vec * w_inv)
        k_s = stack(k2 * w_inv)
        v_s = stack(v).astype(BF16)
        bk_tail = jnp.concatenate([stack(b_vec * w_tail), stack(k2 * w_tail)],
                                  axis=0).astype(BF16)
        prod = _mm_nt(jnp.concatenate([a_s, r_s], axis=0),
                      jnp.concatenate([b_s, k_s], axis=0))
        pre.append(dict(
            a_s=a_s, r_s=r_s, v_s=v_s, bk_tail=bk_tail, decay=decay,
            a_ab=jnp.where(strict, prod[:2 * T, :2 * T], 0.0),
            a_ak=jnp.where(strict, prod[:2 * T, 2 * T:], 0.0).astype(BF16),
            a_rb=jnp.where(incl, prod[2 * T:, :2 * T], 0.0).astype(BF16),
            a_rk=jnp.where(incl, prod[2 * T:, 2 * T:], 0.0).astype(BF16),
            bonus=hsum(r * k2 * r_k) * v))

    invs = [x.astype(BF16) for x in _unit_lower_inverses([p["a_ab"] for p in pre], rowi, coli)]
    akv = [_dot(p["a_ak"], p["v_s"]).astype(BF16) for p in pre]
    inv_a = [_dot(x, p["a_s"]).astype(BF16) for x, p in zip(invs, pre)]
    u_free = [_dot(x, y) for x, y in zip(invs, akv)]
    y_free = [_dot(p["a_rk"], p["v_s"]) for p in pre]

    st = st_ref[...]
    ys = []
    for c, p in enumerate(pre):
        st_b = st.astype(BF16)
        u = (_dot(inv_a[c], st_b) + u_free[c]).astype(BF16)
        ys.append(_dot(p["r_s"], st_b) + _dot(p["a_rb"], u) + y_free[c])
        uv = jnp.concatenate([u, p["v_s"]], axis=0)
        st = st * p["decay"] + _dot_tn(p["bk_tail"], uv)
    st_ref[...] = st

    for c, p in enumerate(pre):
        lo, hi = c * T, (c + 1) * T
        y = ys[c][:T] + ys[c][T:]
        mu = hsum(y) * (1.0 / RWKV_HEAD)
        d = y - mu
        var = hsum(d * d) * (1.0 / RWKV_HEAD)
        yn = d * lax.rsqrt(var + RWKV_LN_EPS) * ln_w + ln_b
        o_ref[0, pl.ds(lo, T), :] = ((yn + p["bonus"]) * gate_all[lo:hi]).astype(o_ref.dtype)


def rwkv_mixer(p3, vecs, mu_wa, mu_g, w2p, a2p, g2p, *, block):
    B, S, _ = p3.shape
    kr, kk_, kv = COL_RWKV_R // RWKV_PAIR, COL_RWKV_K // RWKV_PAIR, COL_RWKV_V // RWKV_PAIR
    kwa = COL_LOW_WA // LANES
    kg = COL_LOW_G // (2 * LANES)
    return pl.pallas_call(
        functools.partial(_rwkv_kernel, block=block, chunk=RWKV_CHUNK),
        grid=(B, RWKV_PAIRS, S // block),
        in_specs=[
            pl.BlockSpec((1, block, RWKV_PAIR), lambda b, j, s: (b, s, kr + j)),
            pl.BlockSpec((1, block, RWKV_PAIR), lambda b, j, s: (b, s, kk_ + j)),
            pl.BlockSpec((1, block, RWKV_PAIR), lambda b, j, s: (b, s, kv + j)),
            pl.BlockSpec((1, block, LANES), lambda b, j, s: (b, s, kwa)),
            pl.BlockSpec((1, block, 2 * LANES), lambda b, j, s: (b, s, kg)),
            pl.BlockSpec((16, RWKV_PAIR), lambda b, j, s: (0, j)),
            pl.BlockSpec((1, LANES), lambda b, j, s: (0, 0)),
            pl.BlockSpec((1, 2 * LANES), lambda b, j, s: (0, 0)),
            pl.BlockSpec((LANES, RWKV_PAIR), lambda b, j, s: (0, j)),
            pl.BlockSpec((LANES, RWKV_PAIR), lambda b, j, s: (0, j)),
            pl.BlockSpec((2 * LANES, RWKV_PAIR), lambda b, j, s: (0, j)),
        ],
        out_specs=pl.BlockSpec((1, block, RWKV_PAIR), lambda b, j, s: (b, s, j)),
        out_shape=jax.ShapeDtypeStruct((B, S, RWKV_WIDTH), BF16),
        scratch_shapes=[
            pltpu.VMEM((RWKV_PAIR, RWKV_PAIR), F32),
            pltpu.VMEM((8, LANES), F32),
            pltpu.VMEM((8, 2 * LANES), F32),
        ],
        compiler_params=pltpu.CompilerParams(
            dimension_semantics=("parallel", "parallel", "arbitrary"),
            vmem_limit_bytes=VMEM_LIMIT),
    )(p3, p3, p3, p3, p3, vecs, mu_wa, mu_g, w2p, a2p, g2p)


def _moba_kernel(q_ref, k_ref, v_ref, slope_ref, o_ref, kb_ref, vt_ref, kmean_ref,
                 m_ref, l_ref, acc_ref, *, nb):
    BS = MOBA_BLOCK
    i = pl.program_id(2)

    @pl.when(i == 0)
    def _():
        kmean_ref[...] = jnp.zeros_like(kmean_ref)
        for j in range(nb):
            kj = k_ref[0, pl.ds(j * BS, BS), :]
            kmean_ref[j:j + 1, :] = jnp.mean(kj, axis=0, keepdims=True)
            kb_ref[j] = kj.astype(BF16)
            vt_ref[j] = v_ref[0, pl.ds(j * BS, BS), :].T.astype(BF16)

    q_t = q_ref[0].T
    q_tb = q_t.astype(BF16)
    scale = MOBA_HD ** -0.5
    slope = slope_ref[0]
    rel = (_iota2((BS, BS), 1) - _iota2((BS, BS), 0)).astype(F32)

    nbp = kmean_ref.shape[0]
    blk = _iota2((nbp, BS), 0)
    gate = _mm_split(kmean_ref[...], q_t, a_parts=3, b_parts=3)
    gate = jnp.where(blk < i, gate, NEG_INF)
    sel = []
    for r in range(MOBA_TOPK):
        mx = jnp.max(gate, axis=0, keepdims=True)
        idx = jnp.min(jnp.where(gate == mx, blk, nbp), axis=0, keepdims=True)
        sel.append(jnp.where(r < i, idx, -1))
        gate = jnp.where(blk == idx, -jnp.inf, gate)

    s_own = _dot(kb_ref[i], q_tb) * scale - slope * rel
    s_own = jnp.where(rel >= 0, s_own, NEG_INF)
    m0 = jnp.max(s_own, axis=0, keepdims=True)
    p0 = jnp.exp(s_own - m0)
    m_ref[...] = m0
    l_ref[...] = jnp.sum(p0, axis=0, keepdims=True)
    acc_ref[...] = _dot(vt_ref[i], p0.astype(BF16))

    def past_block(j, carry):
        picked = (sel[0] == j) | (sel[1] == j) | (sel[2] == j)
        dist = rel + ((i - j) * BS).astype(F32)
        s = _dot(kb_ref[j], q_tb) * scale - slope * dist
        s = jnp.where(picked, s, NEG_INF)
        m_old = m_ref[...]
        m_new = jnp.maximum(m_old, jnp.max(s, axis=0, keepdims=True))
        alpha = jnp.exp(m_old - m_new)
        p = jnp.exp(s - m_new)
        l_ref[...] = alpha * l_ref[...] + jnp.sum(p, axis=0, keepdims=True)
        acc_ref[...] = alpha * acc_ref[...] + _dot(vt_ref[j], p.astype(BF16))
        m_ref[...] = m_new
        return carry

    lax.fori_loop(0, i, past_block, 0)
    o_ref[0] = (acc_ref[...] / l_ref[...]).T.astype(o_ref.dtype)


def moba_attention(qkv3):
    B, S, _ = qkv3.shape
    H, D, BS = MOBA_HEADS, MOBA_HD, MOBA_BLOCK
    nb = S // BS
    nbp = -(-nb // 8) * 8
    slopes = jnp.exp2(-8.0 * jnp.arange(1, H + 1, dtype=F32) / H)
    slopes = jnp.broadcast_to(slopes[:, None, None], (H, 1, BS))
    return pl.pallas_call(
        functools.partial(_moba_kernel, nb=nb),
        grid=(B, H, nb),
        in_specs=[
            pl.BlockSpec((1, BS, D), lambda b, h, i: (b, i, h)),
            pl.BlockSpec((1, S, D), lambda b, h, i: (b, 0, H + h)),
            pl.BlockSpec((1, S, D), lambda b, h, i: (b, 0, 2 * H + h)),
            pl.BlockSpec((1, 1, BS), lambda b, h, i: (h, 0, 0)),
        ],
        out_specs=pl.BlockSpec((1, BS, D), lambda b, h, i: (b, i, h)),
        out_shape=jax.ShapeDtypeStruct((B, S, H * D), BF16),
        scratch_shapes=[
            pltpu.VMEM((nb, BS, D), BF16),
            pltpu.VMEM((nb, D, BS), BF16),
            pltpu.VMEM((nbp, D), F32),
            pltpu.VMEM((1, BS), F32),
            pltpu.VMEM((1, BS), F32),
            pltpu.VMEM((D, BS), F32),
        ],
        compiler_params=pltpu.CompilerParams(
            dimension_semantics=("parallel", "parallel", "arbitrary"),
            vmem_limit_bytes=VMEM_LIMIT),
    )(qkv3, qkv3, qkv3, slopes)


def _pad_cols(w, n):
    return jnp.pad(w, ((0, 0), (0, n - w.shape[1])))


def _pad_rows(w, before, total):
    return jnp.pad(w, ((before, total - before - w.shape[0]), (0, 0)))


def _mix_in_layout(w):
    gk, gw = GLA_KEY_WIDTH, GLA_WIDTH
    g_end = 2 * gk + 2 * gw
    gla_main = w[:, :g_end]
    gla_gate = w[:, g_end:g_end + GLA_GATE_RANK]
    r0 = g_end + GLA_GATE_RANK
    rw_main = w[:, r0:r0 + 3 * RWKV_WIDTH]
    l0 = r0 + 3 * RWKV_WIDTH
    wa = w[:, l0:l0 + RWKV_DECAY_LORA + RWKV_AAA_LORA]
    g0 = l0 + RWKV_DECAY_LORA + RWKV_AAA_LORA
    gl = w[:, g0:g0 + RWKV_GATE_LORA]
    return jnp.concatenate(
        [gla_main, rw_main, _pad_cols(gla_gate, LANES), wa, _pad_cols(gl, 2 * LANES)], axis=1)


def mixer_layer0(hn_proj, gla_gate_w2, gla_gate_b, gla_norm, rwkv_mu, rwkv_w0, rwkv_w2, rwkv_a0,
                 rwkv_a2, rwkv_g2, rwkv_k_k, rwkv_k_a, rwkv_r_k, rwkv_ln_w, rwkv_ln_b,
                 *, gla_block, rwkv_block):
    W = RWKV_WIDTH
    o_gla = gla_mixer(hn_proj, _pad_rows(gla_gate_w2, 0, LANES), gla_gate_b, gla_norm,
                      block=gla_block)
    mu_r, mu_k, mu_v = rwkv_mu[:W], rwkv_mu[W:2 * W], rwkv_mu[2 * W:3 * W]
    mu_low = rwkv_mu[3 * W:]
    n_wa = RWKV_DECAY_LORA + RWKV_AAA_LORA
    vecs = jnp.stack([rwkv_w0, rwkv_a0, rwkv_k_k, rwkv_k_a, rwkv_r_k.reshape(-1), rwkv_ln_w,
                      rwkv_ln_b, mu_r, mu_k, mu_v] + [jnp.zeros((W,), F32)] * 6)
    mu_wa = mu_low[:n_wa].reshape(1, -1)
    mu_g = _pad_cols(mu_low[n_wa:].reshape(1, -1), 2 * LANES)
    w2p = _pad_rows(rwkv_w2, 0, LANES)
    a2p = _pad_rows(rwkv_a2, RWKV_DECAY_LORA, LANES)
    g2p = _pad_rows(rwkv_g2, 0, 2 * LANES)
    o_rwkv = rwkv_mixer(hn_proj, vecs, mu_wa, mu_g, w2p, a2p, g2p, block=rwkv_block)
    return o_gla, o_rwkv


def kernel(x, norm_mix, norm_ffn, norm_final, mix_in_w, gla_gate_w2, gla_gate_b, gla_norm, rwkv_mu, rwkv_w0, rwkv_w2, rwkv_a0, rwkv_a2, rwkv_g2, rwkv_k_k, rwkv_k_a, rwkv_r_k, rwkv_ln_w, rwkv_ln_b, mix_out_w, attn_qkv_w, attn_out_w, ffn_gate_w, ffn_up_w, ffn_down_w):
    B, S, D = x.shape
    M = B * S
    tm, tn, tf = 512, 512, 512
    h = x.reshape(M, D)

    w_in = _mix_in_layout(mix_in_w[0]).astype(BF16)
    p = norm_matmul(h, norm_mix[0], w_in, tm=tm, tn=tn).reshape(B, S, MIX_COLS)
    o_gla, o_rwkv = mixer_layer0(
        p, gla_gate_w2[0], gla_gate_b[0], gla_norm[0], rwkv_mu[0], rwkv_w0[0], rwkv_w2[0],
        rwkv_a0[0], rwkv_a2[0], rwkv_g2[0], rwkv_k_k[0], rwkv_k_a[0], rwkv_r_k[0], rwkv_ln_w[0],
        rwkv_ln_b[0], gla_block=512, rwkv_block=512)
    w_out = mix_out_w[0].astype(BF16)
    h = proj_residual(h, [(o_gla.reshape(M, GLA_WIDTH), w_out[:GLA_WIDTH]),
                          (o_rwkv.reshape(M, RWKV_WIDTH), w_out[GLA_WIDTH:])], tm=tm, tn=tn)
    h = ffn_residual(h, norm_ffn[0], ffn_gate_w[0].astype(BF16), ffn_up_w[0].astype(BF16),
                     ffn_down_w[0].astype(BF16), norm_final, tm=tm, tf=tf, final_norm=False)

    qkv = norm_matmul(h, norm_mix[1], attn_qkv_w[0].astype(BF16), tm=tm, tn=tn)
    o_attn = moba_attention(qkv.reshape(B, S, 3 * D))
    h = proj_residual(h, [(o_attn.reshape(M, D), attn_out_w[0].astype(BF16))], tm=tm, tn=tn)
    h = ffn_residual(h, norm_ffn[1], ffn_gate_w[1].astype(BF16), ffn_up_w[1].astype(BF16),
                     ffn_down_w[1].astype(BF16), norm_final, tm=tm, tf=tf, final_norm=True)
    return h.reshape(B, S, D)
```

```python
import functools

import jax
import jax.numpy as jnp
from jax import lax
from jax.experimental import pallas as pl
from jax.experimental.pallas import tpu as pltpu

F32 = jnp.float32
BF16 = jnp.bfloat16
HI = lax.Precision.HIGHEST

D_MODEL = 2048
RMS_EPS = 1e-6

GLA_HEADS = 4
GLA_DK = 128
GLA_DV = 256
GLA_KEY_WIDTH = GLA_HEADS * GLA_DK
GLA_WIDTH = GLA_HEADS * GLA_DV
GLA_GATE_RANK = 16
GLA_GATE_TAU = 16.0
GLA_CHUNK = 64

RWKV_WIDTH = 1024
RWKV_HEAD = 64
RWKV_PAIR = 2 * RWKV_HEAD
RWKV_PAIRS = RWKV_WIDTH // RWKV_PAIR
RWKV_DECAY_LORA = 64
RWKV_AAA_LORA = 64
RWKV_GATE_LORA = 160
RWKV_LN_EPS = RWKV_HEAD * 1e-5
RWKV_CHUNK = 64

MOBA_HEADS = 16
MOBA_HD = 128
MOBA_BLOCK = 256
MOBA_TOPK = 3
NEG_INF = -1e30

FFN_HIDDEN = 5632

LANES = 128

COL_GLA_Q = 0
COL_GLA_K = 512
COL_GLA_V = 1024
COL_GLA_OG = 2048
COL_RWKV_R = 3072
COL_RWKV_K = 4096
COL_RWKV_V = 5120
COL_LOW_GATE = 6144
COL_LOW_WA = 6272
COL_LOW_G = 6400
MIX_COLS = 6656

VMEM_LIMIT = 56 * 1024 * 1024


def _mm(a, b):
    return jnp.dot(a.astype(BF16), b.astype(BF16), preferred_element_type=F32)


def _mm_nt(a, b):
    return lax.dot_general(a.astype(BF16), b.astype(BF16), (((1,), (1,)), ((), ())),
                           preferred_element_type=F32)


def _mm_hi(a, b):
    return jnp.dot(a, b, preferred_element_type=F32, precision=HI)


def _mm_nt_hi(a, b):
    return lax.dot_general(a, b, (((1,), (1,)), ((), ())), preferred_element_type=F32,
                           precision=HI)


def _dot(a, b):
    return jnp.dot(a, b, preferred_element_type=F32)


def _dot_tn(a, b):
    return lax.dot_general(a, b, (((0,), (0,)), ((), ())), preferred_element_type=F32)


def _split(x, parts):
    out = []
    for _ in range(parts - 1):
        hi = x.astype(BF16)
        out.append(hi)
        x = x - hi.astype(F32)
    out.append(x.astype(BF16))
    return out


def _mm_split(a, b, *, a_parts=1, b_parts=1):
    a_p = _split(a, a_parts) if a_parts > 1 else [a.astype(BF16)]
    b_p = _split(b, b_parts) if b_parts > 1 else [b.astype(BF16)]
    acc = None
    for i, ai in enumerate(a_p):
        for j, bj in enumerate(b_p):
            if i + j < max(a_parts, b_parts):
                t = _dot(ai, bj)
                acc = t if acc is None else acc + t
    return acc


def _sigmoid(x):
    return 1.0 / (1.0 + jnp.exp(-x))


def _softplus(x):
    return jnp.maximum(x, 0.0) + jnp.log1p(jnp.exp(-jnp.abs(x)))


def _iota2(shape, axis):
    return lax.broadcasted_iota(jnp.int32, shape, axis)


def _norm_matmul_kernel(x_ref, g_ref, w_ref, o_ref, xn_ref):
    @pl.when(pl.program_id(1) == 0)
    def _():
        x = x_ref[...]
        ms = jnp.mean(x * x, axis=-1, keepdims=True)
        xn_ref[...] = (x * lax.rsqrt(ms + RMS_EPS) * g_ref[...]).astype(BF16)

    o_ref[...] = jnp.dot(xn_ref[...], w_ref[...], preferred_element_type=F32).astype(o_ref.dtype)


def norm_matmul(x, g, w, *, tm, tn, out_dtype=F32):
    M, D = x.shape
    N = w.shape[1]
    return pl.pallas_call(
        _norm_matmul_kernel,
        grid=(M // tm, N // tn),
        in_specs=[
            pl.BlockSpec((tm, D), lambda i, j: (i, 0)),
            pl.BlockSpec((1, D), lambda i, j: (0, 0)),
            pl.BlockSpec((D, tn), lambda i, j: (0, j)),
        ],
        out_specs=pl.BlockSpec((tm, tn), lambda i, j: (i, j)),
        out_shape=jax.ShapeDtypeStruct((M, N), out_dtype),
        scratch_shapes=[pltpu.VMEM((tm, D), BF16)],
        compiler_params=pltpu.CompilerParams(
            dimension_semantics=("parallel", "arbitrary"), vmem_limit_bytes=VMEM_LIMIT),
    )(x, g.reshape(1, D), w)


def _proj_res_kernel(*refs, n_in):
    res_ref = refs[0]
    o_ref = refs[1 + 2 * n_in]
    acc = res_ref[...]
    for i in range(n_in):
        acc = acc + jnp.dot(refs[1 + 2 * i][...], refs[2 + 2 * i][...],
                            preferred_element_type=F32)
    o_ref[...] = acc


def proj_residual(res, pairs, *, tm, tn):
    M, N = res.shape
    in_specs = [pl.BlockSpec((tm, tn), lambda i, j: (i, j))]
    args = [res]
    for a, w in pairs:
        K = a.shape[1]
        in_specs.append(pl.BlockSpec((tm, K), lambda i, j: (i, 0)))
        in_specs.append(pl.BlockSpec((K, tn), lambda i, j: (0, j)))
        args += [a, w]
    return pl.pallas_call(
        functools.partial(_proj_res_kernel, n_in=len(pairs)),
        grid=(M // tm, N // tn),
        in_specs=in_specs,
        out_specs=pl.BlockSpec((tm, tn), lambda i, j: (i, j)),
        out_shape=jax.ShapeDtypeStruct((M, N), F32),
        compiler_params=pltpu.CompilerParams(
            dimension_semantics=("parallel", "arbitrary"), vmem_limit_bytes=VMEM_LIMIT),
    )(*args)


def _ffn_kernel(x_ref, g_ref, wg_ref, wu_ref, wd_ref, gf_ref, o_ref, xn_ref, *, final_norm):
    j = pl.program_id(1)

    @pl.when(j == 0)
    def _():
        x = x_ref[...]
        ms = jnp.mean(x * x, axis=-1, keepdims=True)
        xn_ref[...] = (x * lax.rsqrt(ms + RMS_EPS) * g_ref[...]).astype(BF16)
        o_ref[...] = x

    xn = xn_ref[...]
    gate = jnp.dot(xn, wg_ref[...], preferred_element_type=F32)
    up = jnp.dot(xn, wu_ref[...], preferred_element_type=F32)
    act = (gate * _sigmoid(gate) * up).astype(BF16)
    o_ref[...] += jnp.dot(act, wd_ref[...], preferred_element_type=F32)

    if final_norm:
        @pl.when(j == pl.num_programs(1) - 1)
        def _():
            h = o_ref[...]
            ms = jnp.mean(h * h, axis=-1, keepdims=True)
            o_ref[...] = h * lax.rsqrt(ms + RMS_EPS) * gf_ref[...]


def ffn_residual(x, g, wg, wu, wd, g_final, *, tm, tf, final_norm):
    M, D = x.shape
    F = wg.shape[1]
    return pl.pallas_call(
        functools.partial(_ffn_kernel, final_norm=final_norm),
        grid=(M // tm, F // tf),
        in_specs=[
            pl.BlockSpec((tm, D), lambda i, j: (i, 0)),
            pl.BlockSpec((1, D), lambda i, j: (0, 0)),
            pl.BlockSpec((D, tf), lambda i, j: (0, j)),
            pl.BlockSpec((D, tf), lambda i, j: (0, j)),
            pl.BlockSpec((tf, D), lambda i, j: (j, 0)),
            pl.BlockSpec((1, D), lambda i, j: (0, 0)),
        ],
        out_specs=pl.BlockSpec((tm, D), lambda i, j: (i, 0)),
        out_shape=jax.ShapeDtypeStruct((M, D), F32),
        scratch_shapes=[pltpu.VMEM((tm, D), BF16)],
        compiler_params=pltpu.CompilerParams(
            dimension_semantics=("parallel", "arbitrary"), vmem_limit_bytes=VMEM_LIMIT),
    )(x, g.reshape(1, D), wg, wu, wd, g_final.reshape(1, D))


def _gla_kernel(q_ref, k_ref, v_ref, og_ref, gl_ref, w2_ref, b_ref, gn_ref, o_ref, st_ref,
                *, block, chunk):
    @pl.when(pl.program_id(2) == 0)
    def _():
        st_ref[...] = jnp.zeros_like(st_ref)

    logit = _mm_hi(gl_ref[0], w2_ref[...]) + b_ref[...]
    log_a = -_softplus(-logit) * (1.0 / GLA_GATE_TAU)
    tri = _iota2((chunk, chunk), 0) >= _iota2((chunk, chunk), 1)
    tri_f = tri.astype(F32)
    scale = GLA_DK ** -0.5
    gn = gn_ref[...]

    for c in range(block // chunk):
        sl = pl.ds(c * chunk, chunk)
        la = log_a[c * chunk:(c + 1) * chunk]
        b = _mm_hi(tri_f, la)
        b_last = b[chunk - 1:chunk]
        q = q_ref[0, sl, :] * scale
        k = k_ref[0, sl, :]
        v = v_ref[0, sl, :]
        q_dec = q * jnp.exp(b)
        k_inv = k * jnp.exp(-b)
        k_dec = k * jnp.exp(b_last - b)
        att = jnp.where(tri, _mm_nt(q_dec, k_inv), 0.0)
        st = st_ref[...]
        o = _mm(att, v) + _mm_nt(q_dec, st)
        st_ref[...] = st * jnp.exp(b_last) + _mm(v.T, k_dec)
        ms = jnp.mean(o * o, axis=-1, keepdims=True)
        og = og_ref[0, sl, :]
        o_ref[0, sl, :] = (o * lax.rsqrt(ms + RMS_EPS) * gn * (og * _sigmoid(og))).astype(o_ref.dtype)


def gla_mixer(p3, gate_w2p, gate_b, gla_norm, *, block):
    B, S, _ = p3.shape
    kq, kk_, kv, kog = (COL_GLA_Q // GLA_DK, COL_GLA_K // GLA_DK, COL_GLA_V // GLA_DV,
                        COL_GLA_OG // GLA_DV)
    klow = COL_LOW_GATE // LANES
    return pl.pallas_call(
        functools.partial(_gla_kernel, block=block, chunk=GLA_CHUNK),
        grid=(B, GLA_HEADS, S // block),
        in_specs=[
            pl.BlockSpec((1, block, GLA_DK), lambda b, h, s: (b, s, kq + h)),
            pl.BlockSpec((1, block, GLA_DK), lambda b, h, s: (b, s, kk_ + h)),
            pl.BlockSpec((1, block, GLA_DV), lambda b, h, s: (b, s, kv + h)),
            pl.BlockSpec((1, block, GLA_DV), lambda b, h, s: (b, s, kog + h)),
            pl.BlockSpec((1, block, LANES), lambda b, h, s: (b, s, klow)),
            pl.BlockSpec((LANES, GLA_DK), lambda b, h, s: (0, h)),
            pl.BlockSpec((1, GLA_DK), lambda b, h, s: (0, h)),
            pl.BlockSpec((1, GLA_DV), lambda b, h, s: (0, 0)),
        ],
        out_specs=pl.BlockSpec((1, block, GLA_DV), lambda b, h, s: (b, s, h)),
        out_shape=jax.ShapeDtypeStruct((B, S, GLA_WIDTH), BF16),
        scratch_shapes=[pltpu.VMEM((GLA_DV, GLA_DK), F32)],
        compiler_params=pltpu.CompilerParams(
            dimension_semantics=("parallel", "parallel", "arbitrary"),
            vmem_limit_bytes=VMEM_LIMIT),
    )(p3, p3, p3, p3, p3, gate_w2p, gate_b.reshape(1, -1), gla_norm.reshape(1, -1))


def _unit_lower_inverses(mats, rowi, coli):
    def same_block(s):
        sh = s.bit_length() - 1
        return (rowi >> sh) == (coli >> sh)

    eye = (rowi == coli).astype(F32)
    a8 = [jnp.where(same_block(8), a, 0.0).astype(BF16) for a in mats]
    inv = [eye + a.astype(F32) for a in a8]
    p = [_dot(a, a) for a in a8]
    inv = [x + _mm(x, y) for x, y in zip(inv, p)]
    p = [_mm(y, y) for y in p]
    inv = [x + _mm(x, y) for x, y in zip(inv, p)]
    s = 8
    while s < RWKV_CHUNK:
        off = same_block(2 * s) & jnp.logical_not(same_block(s))
        e = [jnp.where(off, a, 0.0).astype(BF16) for a in mats]
        inv_b = [x.astype(BF16) for x in inv]
        t = [_dot(x, y).astype(BF16) for x, y in zip(inv_b, e)]
        inv = [x + _dot(y, z) for x, y, z in zip(inv, t, inv_b)]
        s *= 2
    return inv


def _rwkv_kernel(r_ref, k_ref, v_ref, wa_ref, g_ref, vec_ref, muwa_ref, mug_ref, w2_ref, a2_ref,
                 g2_ref, o_ref, st_ref, prev_ref, prevg_ref, *, block, chunk):
    T = chunk
    first = pl.program_id(2) == 0

    @pl.when(first)
    def _():
        st_ref[...] = jnp.zeros_like(st_ref)
        prev_ref[...] = jnp.zeros_like(prev_ref)
        prevg_ref[...] = jnp.zeros_like(prevg_ref)

    row0 = _iota2((block, 1), 0) == 0

    def lerp(x, prev_row, mu):
        shifted = jnp.where(row0, prev_row, pltpu.roll(x, 1, axis=0))
        return x + (shifted - x) * mu

    vec = vec_ref[...]
    w0, a0, k_k, k_a, r_k, ln_w, ln_b = (vec[i:i + 1] for i in range(7))
    mu_r, mu_k, mu_v = (vec[i:i + 1] for i in range(7, 10))

    r_raw, k_raw, v_raw, wa_raw, g_raw = r_ref[0], k_ref[0], v_ref[0], wa_ref[0], g_ref[0]
    prev = prev_ref[...]
    r_all = lerp(r_raw, prev[0:1], mu_r)
    k_all = lerp(k_raw, prev[1:2], mu_k)
    v_all = lerp(v_raw, prev[2:3], mu_v)
    wa = lerp(wa_raw, prev[3:4], muwa_ref[...])
    g_low = lerp(g_raw, prevg_ref[0:1], mug_ref[...])
    prev_ref[0:1] = r_raw[block - 1:block]
    prev_ref[1:2] = k_raw[block - 1:block]
    prev_ref[2:3] = v_raw[block - 1:block]
    prev_ref[3:4] = wa_raw[block - 1:block]
    prevg_ref[0:1] = g_raw[block - 1:block]

    z = w0 + _mm_split(jnp.tanh(wa), w2_ref[...], a_parts=2, b_parts=2)
    logw_all = -jnp.exp(-_softplus(-z) - 0.5)
    lr_all = _sigmoid(a0 + _mm_split(wa, a2_ref[...], a_parts=2, b_parts=2))
    gate_all = _mm_split(_sigmoid(g_low), g2_ref[...], a_parts=2, b_parts=2)

    lane = _iota2((1, RWKV_PAIR), 1)
    m0 = (lane < RWKV_HEAD).astype(F32)
    m1 = 1.0 - m0
    rowi = _iota2((2 * T, 2 * T), 0)
    coli = _iota2((2 * T, 2 * T), 1)
    strict = rowi > coli
    incl = rowi >= coli
    head_ones = ((rowi < RWKV_HEAD) == (coli < RWKV_HEAD)).astype(BF16)
    tri_t = (_iota2((T, T), 0) >= _iota2((T, T), 1)).astype(BF16)
    ones_t = jnp.ones((T, RWKV_PAIR), BF16)

    def hsum(x):
        return _mm_split(x, head_ones, a_parts=2)

    def stack(x):
        return jnp.concatenate([x * m0, x * m1], axis=0)

    n_chunks = block // T

    pre = []
    for c in range(n_chunks):
        lo, hi = c * T, (c + 1) * T
        r, k, v = r_all[lo:hi], k_all[lo:hi], v_all[lo:hi]
        lw, lr = logw_all[lo:hi], lr_all[lo:hi]

        kkp = k * k_k
        kk = kkp / jnp.maximum(jnp.sqrt(hsum(kkp * kkp)), 1e-12)
        k2 = k * (1.0 + (lr - 1.0) * k_a)
        b_vec = kk * lr

        lw_parts = _split(lw, 3)
        cum = _dot(tri_t, lw_parts[0]) + _dot(tri_t, lw_parts[1]) + _dot(tri_t, lw_parts[2])
        cum_last = cum[T - 1:T]
        decay = jnp.exp(_dot_tn(lw_parts[0], ones_t) + _dot_tn(lw_parts[1], ones_t)
                        + _dot_tn(lw_parts[2], ones_t))
        w_inv = jnp.exp(-cum)
        w_tail = jnp.exp(cum_last - cum)
        a_s = stack(-kk * jnp.exp(cum - lw)).astype(BF16)
        r_s = stack(r * jnp.exp(cum)).astype(BF16)
        b_s = stack(b_vec * w_inv)
        k_s = stack(k2 * w_inv)
        v_s = stack(v).astype(BF16)
        bk_tail = jnp.concatenate([stack(b_vec * w_tail), stack(k2 * w_tail)],
                                  axis=0).astype(BF16)
        prod = _mm_nt(jnp.concatenate([a_s, r_s], axis=0),
                      jnp.concatenate([b_s, k_s], axis=0))
        pre.append(dict(
            a_s=a_s, r_s=r_s, v_s=v_s, bk_tail=bk_tail, decay=decay,
            a_ab=jnp.where(strict, prod[:2 * T, :2 * T], 0.0),
            a_ak=jnp.where(strict, prod[:2 * T, 2 * T:], 0.0).astype(BF16),
            a_rb=jnp.where(incl, prod[2 * T:, :2 * T], 0.0).astype(BF16),
            a_rk=jnp.where(incl, prod[2 * T:, 2 * T:], 0.0).astype(BF16),
            bonus=hsum(r * k2 * r_k) * v))

    invs = [x.astype(BF16) for x in _unit_lower_inverses([p["a_ab"] for p in pre], rowi, coli)]
    akv = [_dot(p["a_ak"], p["v_s"]).astype(BF16) for p in pre]
    inv_a = [_dot(x, p["a_s"]).astype(BF16) for x, p in zip(invs, pre)]
    u_free = [_dot(x, y) for x, y in zip(invs, akv)]
    y_free = [_dot(p["a_rk"], p["v_s"]) for p in pre]

    st = st_ref[...]
    ys = []
    for c, p in enumerate(pre):
        st_b = st.astype(BF16)
        u = (_dot(inv_a[c], st_b) + u_free[c]).astype(BF16)
        ys.append(_dot(p["r_s"], st_b) + _dot(p["a_rb"], u) + y_free[c])
        uv = jnp.concatenate([u, p["v_s"]], axis=0)
        st = st * p["decay"] + _dot_tn(p["bk_tail"], uv)
    st_ref[...] = st

    for c, p in enumerate(pre):
        lo, hi = c * T, (c + 1) * T
        y = ys[c][:T] + ys[c][T:]
        mu = hsum(y) * (1.0 / RWKV_HEAD)
        d = y - mu
        var = hsum(d * d) * (1.0 / RWKV_HEAD)
        yn = d * lax.rsqrt(var + RWKV_LN_EPS) * ln_w + ln_b
        o_ref[0, pl.ds(lo, T), :] = ((yn + p["bonus"]) * gate_all[lo:hi]).astype(o_ref.dtype)


def rwkv_mixer(p3, vecs, mu_wa, mu_g, w2p, a2p, g2p, *, block):
    B, S, _ = p3.shape
    kr, kk_, kv = COL_RWKV_R // RWKV_PAIR, COL_RWKV_K // RWKV_PAIR, COL_RWKV_V // RWKV_PAIR
    kwa = COL_LOW_WA // LANES
    kg = COL_LOW_G // (2 * LANES)
    return pl.pallas_call(
        functools.partial(_rwkv_kernel, block=block, chunk=RWKV_CHUNK),
        grid=(B, RWKV_PAIRS, S // block),
        in_specs=[
            pl.BlockSpec((1, block, RWKV_PAIR), lambda b, j, s: (b, s, kr + j)),
            pl.BlockSpec((1, block, RWKV_PAIR), lambda b, j, s: (b, s, kk_ + j)),
            pl.BlockSpec((1, block, RWKV_PAIR), lambda b, j, s: (b, s, kv + j)),
            pl.BlockSpec((1, block, LANES), lambda b, j, s: (b, s, kwa)),
            pl.BlockSpec((1, block, 2 * LANES), lambda b, j, s: (b, s, kg)),
            pl.BlockSpec((16, RWKV_PAIR), lambda b, j, s: (0, j)),
            pl.BlockSpec((1, LANES), lambda b, j, s: (0, 0)),
            pl.BlockSpec((1, 2 * LANES), lambda b, j, s: (0, 0)),
            pl.BlockSpec((LANES, RWKV_PAIR), lambda b, j, s: (0, j)),
            pl.BlockSpec((LANES, RWKV_PAIR), lambda b, j, s: (0, j)),
            pl.BlockSpec((2 * LANES, RWKV_PAIR), lambda b, j, s: (0, j)),
        ],
        out_specs=pl.BlockSpec((1, block, RWKV_PAIR), lambda b, j, s: (b, s, j)),
        out_shape=jax.ShapeDtypeStruct((B, S, RWKV_WIDTH), BF16),
        scratch_shapes=[
            pltpu.VMEM((RWKV_PAIR, RWKV_PAIR), F32),
            pltpu.VMEM((8, LANES), F32),
            pltpu.VMEM((8, 2 * LANES), F32),
        ],
        compiler_params=pltpu.CompilerParams(
            dimension_semantics=("parallel", "parallel", "arbitrary"),
            vmem_limit_bytes=VMEM_LIMIT),
    )(p3, p3, p3, p3, p3, vecs, mu_wa, mu_g, w2p, a2p, g2p)


def _moba_kernel(q_ref, k_ref, v_ref, slope_ref, o_ref, kb_ref, vt_ref, kmean_ref,
                 m_ref, l_ref, acc_ref, s_ref, p_ref, *, nb):
    BS = MOBA_BLOCK
    i = pl.program_id(2)

    @pl.when(i == 0)
    def _():
        kmean_ref[...] = jnp.zeros_like(kmean_ref)
        for j in range(nb):
            kj = k_ref[0, pl.ds(j * BS, BS), :]
            kmean_ref[j:j + 1, :] = jnp.mean(kj, axis=0, keepdims=True)
            kb_ref[j] = kj.astype(BF16)
            vt_ref[j] = v_ref[0, pl.ds(j * BS, BS), :].T.astype(BF16)

    q_t = q_ref[0].T
    q_tb = q_t.astype(BF16)
    scale = MOBA_HD ** -0.5
    slope = slope_ref[0]
    rel = (_iota2((BS, BS), 1) - _iota2((BS, BS), 0)).astype(F32)

    nbp = kmean_ref.shape[0]
    blk = _iota2((nbp, BS), 0)
    gate = _mm_split(kmean_ref[...], q_t, a_parts=3, b_parts=3)
    gate = jnp.where(blk < i, gate, NEG_INF)
    sel = []
    for r in range(MOBA_TOPK):
        mx = jnp.max(gate, axis=0, keepdims=True)
        idx = jnp.min(jnp.where(gate == mx, blk, nbp), axis=0, keepdims=True)
        sel.append(jnp.where(r < i, idx, -1))
        gate = jnp.where(blk == idx, -jnp.inf, gate)

    srel = slope * rel

    s_own = _dot(kb_ref[i], q_tb) * scale - srel
    s_own = jnp.where(rel >= 0, s_own, NEG_INF)
    m0 = jnp.max(s_own, axis=0, keepdims=True)
    p0 = jnp.exp(s_own - m0)
    m_ref[...] = m0
    l_ref[...] = jnp.sum(p0, axis=0, keepdims=True)
    p_ref[...] = p0.astype(BF16)
    acc_ref[...] = jnp.zeros_like(acc_ref)
    s_ref[0] = _dot(kb_ref[0], q_tb)

    def past_block(j, prev):
        s_cur = s_ref[j % 2]
        pv = _dot(vt_ref[prev], p_ref[...])
        s_ref[(j + 1) % 2] = _dot(kb_ref[jnp.minimum(j + 1, nb - 1)], q_tb)
        picked = (sel[0] == j) | (sel[1] == j) | (sel[2] == j)
        s = jnp.where(picked, s_cur * scale - srel, NEG_INF)
        off = slope * ((i - j) * BS).astype(F32)
        m_old = m_ref[...]
        m_new = jnp.maximum(m_old, jnp.max(s, axis=0, keepdims=True) - off)
        alpha = jnp.exp(m_old - m_new)
        p = jnp.exp(s - (m_new + off))
        l_ref[...] = alpha * l_ref[...] + jnp.sum(p, axis=0, keepdims=True)
        acc_ref[...] = alpha * (acc_ref[...] + pv)
        p_ref[...] = p.astype(BF16)
        m_ref[...] = m_new
        return j

    last = lax.fori_loop(0, i, past_block, i)
    acc = acc_ref[...] + _dot(vt_ref[last], p_ref[...])
    o_ref[0] = (acc / l_ref[...]).T.astype(o_ref.dtype)


def moba_attention(qkv3):
    B, S, _ = qkv3.shape
    H, D, BS = MOBA_HEADS, MOBA_HD, MOBA_BLOCK
    nb = S // BS
    nbp = -(-nb // 8) * 8
    slopes = jnp.exp2(-8.0 * jnp.arange(1, H + 1, dtype=F32) / H)
    slopes = jnp.broadcast_to(slopes[:, None, None], (H, 1, BS))
    return pl.pallas_call(
        functools.partial(_moba_kernel, nb=nb),
        grid=(B, H, nb),
        in_specs=[
            pl.BlockSpec((1, BS, D), lambda b, h, i: (b, i, h)),
            pl.BlockSpec((1, S, D), lambda b, h, i: (b, 0, H + h)),
            pl.BlockSpec((1, S, D), lambda b, h, i: (b, 0, 2 * H + h)),
            pl.BlockSpec((1, 1, BS), lambda b, h, i: (h, 0, 0)),
        ],
        out_specs=pl.BlockSpec((1, BS, D), lambda b, h, i: (b, i, h)),
        out_shape=jax.ShapeDtypeStruct((B, S, H * D), BF16),
        scratch_shapes=[
            pltpu.VMEM((nb, BS, D), BF16),
            pltpu.VMEM((nb, D, BS), BF16),
            pltpu.VMEM((nbp, D), F32),
            pltpu.VMEM((1, BS), F32),
            pltpu.VMEM((1, BS), F32),
            pltpu.VMEM((D, BS), F32),
            pltpu.VMEM((2, BS, BS), F32),
            pltpu.VMEM((BS, BS), BF16),
        ],
        compiler_params=pltpu.CompilerParams(
            dimension_semantics=("parallel", "parallel", "arbitrary"),
            vmem_limit_bytes=VMEM_LIMIT),
    )(qkv3, qkv3, qkv3, slopes)


def _pad_cols(w, n):
    return jnp.pad(w, ((0, 0), (0, n - w.shape[1])))


def _pad_rows(w, before, total):
    return jnp.pad(w, ((before, total - before - w.shape[0]), (0, 0)))


def _mix_in_layout(w):
    gk, gw = GLA_KEY_WIDTH, GLA_WIDTH
    g_end = 2 * gk + 2 * gw
    gla_main = w[:, :g_end]
    gla_gate = w[:, g_end:g_end + GLA_GATE_RANK]
    r0 = g_end + GLA_GATE_RANK
    rw_main = w[:, r0:r0 + 3 * RWKV_WIDTH]
    l0 = r0 + 3 * RWKV_WIDTH
    wa = w[:, l0:l0 + RWKV_DECAY_LORA + RWKV_AAA_LORA]
    g0 = l0 + RWKV_DECAY_LORA + RWKV_AAA_LORA
    gl = w[:, g0:g0 + RWKV_GATE_LORA]
    return jnp.concatenate(
        [gla_main, rw_main, _pad_cols(gla_gate, LANES), wa, _pad_cols(gl, 2 * LANES)], axis=1)


def mixer_layer0(hn_proj, gla_gate_w2, gla_gate_b, gla_norm, rwkv_mu, rwkv_w0, rwkv_w2, rwkv_a0,
                 rwkv_a2, rwkv_g2, rwkv_k_k, rwkv_k_a, rwkv_r_k, rwkv_ln_w, rwkv_ln_b,
                 *, gla_block, rwkv_block):
    W = RWKV_WIDTH
    o_gla = gla_mixer(hn_proj, _pad_rows(gla_gate_w2, 0, LANES), gla_gate_b, gla_norm,
                      block=gla_block)
    mu_r, mu_k, mu_v = rwkv_mu[:W], rwkv_mu[W:2 * W], rwkv_mu[2 * W:3 * W]
    mu_low = rwkv_mu[3 * W:]
    n_wa = RWKV_DECAY_LORA + RWKV_AAA_LORA
    vecs = jnp.stack([rwkv_w0, rwkv_a0, rwkv_k_k, rwkv_k_a, rwkv_r_k.reshape(-1), rwkv_ln_w,
                      rwkv_ln_b, mu_r, mu_k, mu_v] + [jnp.zeros((W,), F32)] * 6)
    mu_wa = mu_low[:n_wa].reshape(1, -1)
    mu_g = _pad_cols(mu_low[n_wa:].reshape(1, -1), 2 * LANES)
    w2p = _pad_rows(rwkv_w2, 0, LANES)
    a2p = _pad_rows(rwkv_a2, RWKV_DECAY_LORA, LANES)
    g2p = _pad_rows(rwkv_g2, 0, 2 * LANES)
    o_rwkv = rwkv_mixer(hn_proj, vecs, mu_wa, mu_g, w2p, a2p, g2p, block=rwkv_block)
    return o_gla, o_rwkv


def kernel(x, norm_mix, norm_ffn, norm_final, mix_in_w, gla_gate_w2, gla_gate_b, gla_norm, rwkv_mu, rwkv_w0, rwkv_w2, rwkv_a0, rwkv_a2, rwkv_g2, rwkv_k_k, rwkv_k_a, rwkv_r_k, rwkv_ln_w, rwkv_ln_b, mix_out_w, attn_qkv_w, attn_out_w, ffn_gate_w, ffn_up_w, ffn_down_w):
    B, S, D = x.shape
    M = B * S
    tm, tn, tf = 512, 512, 512
    h = x.reshape(M, D)

    w_in = _mix_in_layout(mix_in_w[0]).astype(BF16)
    p = norm_matmul(h, norm_mix[0], w_in, tm=tm, tn=tn).reshape(B, S, MIX_COLS)
    o_gla, o_rwkv = mixer_layer0(
        p, gla_gate_w2[0], gla_gate_b[0], gla_norm[0], rwkv_mu[0], rwkv_w0[0], rwkv_w2[0],
        rwkv_a0[0], rwkv_a2[0], rwkv_g2[0], rwkv_k_k[0], rwkv_k_a[0], rwkv_r_k[0], rwkv_ln_w[0],
        rwkv_ln_b[0], gla_block=512, rwkv_block=512)
    w_out = mix_out_w[0].astype(BF16)
    h = proj_residual(h, [(o_gla.reshape(M, GLA_WIDTH), w_out[:GLA_WIDTH]),
                          (o_rwkv.reshape(M, RWKV_WIDTH), w_out[GLA_WIDTH:])], tm=tm, tn=tn)
    h = ffn_residual(h, norm_ffn[0], ffn_gate_w[0].astype(BF16), ffn_up_w[0].astype(BF16),
                     ffn_down_w[0].astype(BF16), norm_final, tm=tm, tf=tf, final_norm=False)

    qkv = norm_matmul(h, norm_mix[1], attn_qkv_w[0].astype(BF16), tm=tm, tn=tn)
    o_attn = moba_attention(qkv.reshape(B, S, 3 * D))
    h = proj_residual(h, [(o_attn.reshape(M, D), attn_out_w[0].astype(BF16))], tm=tm, tn=tn)
    h = ffn_residual(h, norm_ffn[1], ffn_gate_w[1].astype(BF16), ffn_up_w[1].astype(BF16),
                     ffn_down_w[1].astype(BF16), norm_final, tm=tm, tf=tf, final_norm=True)
    return h.reshape(B, S, D)
```

```python
import functools

import jax
import jax.numpy as jnp
from jax import lax
from jax.experimental import pallas as pl
from jax.experimental.pallas import tpu as pltpu

F32 = jnp.float32
BF16 = jnp.bfloat16
HI = lax.Precision.HIGHEST

D_MODEL = 2048
RMS_EPS = 1e-6

GLA_HEADS = 4
GLA_DK = 128
GLA_DV = 256
GLA_KEY_WIDTH = GLA_HEADS * GLA_DK
GLA_WIDTH = GLA_HEADS * GLA_DV
GLA_GATE_RANK = 16
GLA_GATE_TAU = 16.0
GLA_CHUNK = 64

RWKV_WIDTH = 1024
RWKV_HEAD = 64
RWKV_PAIR = 2 * RWKV_HEAD
RWKV_PAIRS = RWKV_WIDTH // RWKV_PAIR
RWKV_DECAY_LORA = 64
RWKV_AAA_LORA = 64
RWKV_GATE_LORA = 160
RWKV_LN_EPS = RWKV_HEAD * 1e-5
RWKV_CHUNK = 64

MOBA_HEADS = 16
MOBA_HD = 128
MOBA_BLOCK = 256
MOBA_TOPK = 3
NEG_INF = -1e30

FFN_HIDDEN = 5632

LANES = 128

COL_GLA_Q = 0
COL_GLA_K = 512
COL_GLA_V = 1024
COL_GLA_OG = 2048
COL_RWKV_R = 3072
COL_RWKV_K = 4096
COL_RWKV_V = 5120
COL_LOW_GATE = 6144
COL_LOW_WA = 6272
COL_LOW_G = 6400
MIX_COLS = 6656

VMEM_LIMIT = 56 * 1024 * 1024


def _mm(a, b):
    return jnp.dot(a.astype(BF16), b.astype(BF16), preferred_element_type=F32)


def _mm_nt(a, b):
    return lax.dot_general(a.astype(BF16), b.astype(BF16), (((1,), (1,)), ((), ())),
                           preferred_element_type=F32)


def _mm_hi(a, b):
    return jnp.dot(a, b, preferred_element_type=F32, precision=HI)


def _mm_nt_hi(a, b):
    return lax.dot_general(a, b, (((1,), (1,)), ((), ())), preferred_element_type=F32,
                           precision=HI)


def _dot(a, b):
    return jnp.dot(a, b, preferred_element_type=F32)


def _dot_tn(a, b):
    return lax.dot_general(a, b, (((0,), (0,)), ((), ())), preferred_element_type=F32)


def _split(x, parts):
    out = []
    for _ in range(parts - 1):
        hi = x.astype(BF16)
        out.append(hi)
        x = x - hi.astype(F32)
    out.append(x.astype(BF16))
    return out


def _mm_split(a, b, *, a_parts=1, b_parts=1):
    a_p = _split(a, a_parts) if a_parts > 1 else [a.astype(BF16)]
    b_p = _split(b, b_parts) if b_parts > 1 else [b.astype(BF16)]
    acc = None
    for i, ai in enumerate(a_p):
        for j, bj in enumerate(b_p):
            if i + j < max(a_parts, b_parts):
                t = _dot(ai, bj)
                acc = t if acc is None else acc + t
    return acc


def _sigmoid(x):
    return 1.0 / (1.0 + jnp.exp(-x))


def _softplus(x):
    return jnp.maximum(x, 0.0) + jnp.log1p(jnp.exp(-jnp.abs(x)))


def _iota2(shape, axis):
    return lax.broadcasted_iota(jnp.int32, shape, axis)


def _norm_matmul_kernel(x_ref, g_ref, w_ref, o_ref, xn_ref):
    @pl.when(pl.program_id(1) == 0)
    def _():
        x = x_ref[...]
        ms = jnp.mean(x * x, axis=-1, keepdims=True)
        xn_ref[...] = (x * lax.rsqrt(ms + RMS_EPS) * g_ref[...]).astype(BF16)

    o_ref[...] = jnp.dot(xn_ref[...], w_ref[...], preferred_element_type=F32).astype(o_ref.dtype)


def norm_matmul(x, g, w, *, tm, tn, out_dtype=F32):
    M, D = x.shape
    N = w.shape[1]
    return pl.pallas_call(
        _norm_matmul_kernel,
        grid=(M // tm, N // tn),
        in_specs=[
            pl.BlockSpec((tm, D), lambda i, j: (i, 0)),
            pl.BlockSpec((1, D), lambda i, j: (0, 0)),
            pl.BlockSpec((D, tn), lambda i, j: (0, j)),
        ],
        out_specs=pl.BlockSpec((tm, tn), lambda i, j: (i, j)),
        out_shape=jax.ShapeDtypeStruct((M, N), out_dtype),
        scratch_shapes=[pltpu.VMEM((tm, D), BF16)],
        compiler_params=pltpu.CompilerParams(
            dimension_semantics=("parallel", "arbitrary"), vmem_limit_bytes=VMEM_LIMIT),
    )(x, g.reshape(1, D), w)


def _proj_res_kernel(*refs, n_in):
    res_ref = refs[0]
    o_ref = refs[1 + 2 * n_in]
    acc = res_ref[...]
    for i in range(n_in):
        acc = acc + jnp.dot(refs[1 + 2 * i][...], refs[2 + 2 * i][...],
                            preferred_element_type=F32)
    o_ref[...] = acc


def proj_residual(res, pairs, *, tm, tn):
    M, N = res.shape
    in_specs = [pl.BlockSpec((tm, tn), lambda i, j: (i, j))]
    args = [res]
    for a, w in pairs:
        K = a.shape[1]
        in_specs.append(pl.BlockSpec((tm, K), lambda i, j: (i, 0)))
        in_specs.append(pl.BlockSpec((K, tn), lambda i, j: (0, j)))
        args += [a, w]
    return pl.pallas_call(
        functools.partial(_proj_res_kernel, n_in=len(pairs)),
        grid=(M // tm, N // tn),
        in_specs=in_specs,
        out_specs=pl.BlockSpec((tm, tn), lambda i, j: (i, j)),
        out_shape=jax.ShapeDtypeStruct((M, N), F32),
        compiler_params=pltpu.CompilerParams(
            dimension_semantics=("parallel", "arbitrary"), vmem_limit_bytes=VMEM_LIMIT),
    )(*args)


def _ffn_kernel(x_ref, g_ref, wg_ref, wu_ref, wd_ref, gf_ref, o_ref, xn_ref, *, final_norm):
    j = pl.program_id(1)

    @pl.when(j == 0)
    def _():
        x = x_ref[...]
        ms = jnp.mean(x * x, axis=-1, keepdims=True)
        xn_ref[...] = (x * lax.rsqrt(ms + RMS_EPS) * g_ref[...]).astype(BF16)
        o_ref[...] = x

    xn = xn_ref[...]
    gate = jnp.dot(xn, wg_ref[...], preferred_element_type=F32)
    up = jnp.dot(xn, wu_ref[...], preferred_element_type=F32)
    act = (gate * _sigmoid(gate) * up).astype(BF16)
    o_ref[...] += jnp.dot(act, wd_ref[...], preferred_element_type=F32)

    if final_norm:
        @pl.when(j == pl.num_programs(1) - 1)
        def _():
            h = o_ref[...]
            ms = jnp.mean(h * h, axis=-1, keepdims=True)
            o_ref[...] = h * lax.rsqrt(ms + RMS_EPS) * gf_ref[...]


def ffn_residual(x, g, wg, wu, wd, g_final, *, tm, tf, final_norm):
    M, D = x.shape
    F = wg.shape[1]
    return pl.pallas_call(
        functools.partial(_ffn_kernel, final_norm=final_norm),
        grid=(M // tm, F // tf),
        in_specs=[
            pl.BlockSpec((tm, D), lambda i, j: (i, 0)),
            pl.BlockSpec((1, D), lambda i, j: (0, 0)),
            pl.BlockSpec((D, tf), lambda i, j: (0, j)),
            pl.BlockSpec((D, tf), lambda i, j: (0, j)),
            pl.BlockSpec((tf, D), lambda i, j: (j, 0)),
            pl.BlockSpec((1, D), lambda i, j: (0, 0)),
        ],
        out_specs=pl.BlockSpec((tm, D), lambda i, j: (i, 0)),
        out_shape=jax.ShapeDtypeStruct((M, D), F32),
        scratch_shapes=[pltpu.VMEM((tm, D), BF16)],
        compiler_params=pltpu.CompilerParams(
            dimension_semantics=("parallel", "arbitrary"), vmem_limit_bytes=VMEM_LIMIT),
    )(x, g.reshape(1, D), wg, wu, wd, g_final.reshape(1, D))


def _gla_kernel(q_ref, k_ref, v_ref, og_ref, gl_ref, w2_ref, b_ref, gn_ref, o_ref, st_ref,
                *, block, chunk):
    @pl.when(pl.program_id(2) == 0)
    def _():
        st_ref[...] = jnp.zeros_like(st_ref)

    logit = _mm_split(gl_ref[0], w2_ref[...], a_parts=2, b_parts=2) + b_ref[...]
    log_a = -_softplus(-logit) * (1.0 / GLA_GATE_TAU)
    tri = _iota2((chunk, chunk), 0) >= _iota2((chunk, chunk), 1)
    tri_b = tri.astype(BF16)
    scale = GLA_DK ** -0.5
    gn = gn_ref[...]
    n_chunks = block // chunk

    pre = []
    for c in range(n_chunks):
        sl = pl.ds(c * chunk, chunk)
        la = _split(log_a[c * chunk:(c + 1) * chunk], 3)
        b = _dot(tri_b, la[0]) + _dot(tri_b, la[1]) + _dot(tri_b, la[2])
        b_last = b[chunk - 1:chunk]
        q = q_ref[0, sl, :] * scale
        k = k_ref[0, sl, :]
        v = v_ref[0, sl, :]
        q_dec = (q * jnp.exp(b)).astype(BF16)
        k_inv = k * jnp.exp(-b)
        k_dec = (k * jnp.exp(b_last - b)).astype(BF16)
        att = jnp.where(tri, _mm_nt(q_dec, k_inv), 0.0)
        pre.append(dict(q_dec=q_dec, k_dec=k_dec, v_t=v.T.astype(BF16), o_intra=_mm(att, v),
                        decay=jnp.exp(b_last)))

    st = st_ref[...]
    outs = []
    for p in pre:
        outs.append(p["o_intra"] + _mm_nt(p["q_dec"], st))
        st = st * p["decay"] + _dot(p["v_t"], p["k_dec"])
    st_ref[...] = st

    for c, o in enumerate(outs):
        sl = pl.ds(c * chunk, chunk)
        ms = jnp.mean(o * o, axis=-1, keepdims=True)
        og = og_ref[0, sl, :]
        o_ref[0, sl, :] = (o * lax.rsqrt(ms + RMS_EPS) * gn * (og * _sigmoid(og))).astype(o_ref.dtype)


def gla_mixer(p3, gate_w2p, gate_b, gla_norm, *, block):
    B, S, _ = p3.shape
    kq, kk_, kv, kog = (COL_GLA_Q // GLA_DK, COL_GLA_K // GLA_DK, COL_GLA_V // GLA_DV,
                        COL_GLA_OG // GLA_DV)
    klow = COL_LOW_GATE // LANES
    return pl.pallas_call(
        functools.partial(_gla_kernel, block=block, chunk=GLA_CHUNK),
        grid=(B, GLA_HEADS, S // block),
        in_specs=[
            pl.BlockSpec((1, block, GLA_DK), lambda b, h, s: (b, s, kq + h)),
            pl.BlockSpec((1, block, GLA_DK), lambda b, h, s: (b, s, kk_ + h)),
            pl.BlockSpec((1, block, GLA_DV), lambda b, h, s: (b, s, kv + h)),
            pl.BlockSpec((1, block, GLA_DV), lambda b, h, s: (b, s, kog + h)),
            pl.BlockSpec((1, block, LANES), lambda b, h, s: (b, s, klow)),
            pl.BlockSpec((LANES, GLA_DK), lambda b, h, s: (0, h)),
            pl.BlockSpec((1, GLA_DK), lambda b, h, s: (0, h)),
            pl.BlockSpec((1, GLA_DV), lambda b, h, s: (0, 0)),
        ],
        out_specs=pl.BlockSpec((1, block, GLA_DV), lambda b, h, s: (b, s, h)),
        out_shape=jax.ShapeDtypeStruct((B, S, GLA_WIDTH), BF16),
        scratch_shapes=[pltpu.VMEM((GLA_DV, GLA_DK), F32)],
        compiler_params=pltpu.CompilerParams(
            dimension_semantics=("parallel", "parallel", "arbitrary"),
            vmem_limit_bytes=VMEM_LIMIT),
    )(p3, p3, p3, p3, p3, gate_w2p, gate_b.reshape(1, -1), gla_norm.reshape(1, -1))


def _unit_lower_inverses(mats, rowi, coli):
    def same_block(s):
        sh = s.bit_length() - 1
        return (rowi >> sh) == (coli >> sh)

    eye = (rowi == coli).astype(F32)
    a8 = [jnp.where(same_block(8), a, 0.0).astype(BF16) for a in mats]
    inv = [eye + a.astype(F32) for a in a8]
    p = [_dot(a, a) for a in a8]
    inv = [x + _mm(x, y) for x, y in zip(inv, p)]
    p = [_mm(y, y) for y in p]
    inv = [x + _mm(x, y) for x, y in zip(inv, p)]
    s = 8
    while s < RWKV_CHUNK:
        off = same_block(2 * s) & jnp.logical_not(same_block(s))
        e = [jnp.where(off, a, 0.0).astype(BF16) for a in mats]
        inv_b = [x.astype(BF16) for x in inv]
        t = [_dot(x, y).astype(BF16) for x, y in zip(inv_b, e)]
        inv = [x + _dot(y, z) for x, y, z in zip(inv, t, inv_b)]
        s *= 2
    return inv


def _rwkv_kernel(r_ref, k_ref, v_ref, wa_ref, g_ref, vec_ref, muwa_ref, mug_ref, w2_ref, a2_ref,
                 g2_ref, o_ref, st_ref, prev_ref, prevg_ref, *, block, chunk):
    T = chunk
    first = pl.program_id(2) == 0

    @pl.when(first)
    def _():
        st_ref[...] = jnp.zeros_like(st_ref)
        prev_ref[...] = jnp.zeros_like(prev_ref)
        prevg_ref[...] = jnp.zeros_like(prevg_ref)

    row0 = _iota2((block, 1), 0) == 0

    def lerp(x, prev_row, mu):
        shifted = jnp.where(row0, prev_row, pltpu.roll(x, 1, axis=0))
        return x + (shifted - x) * mu

    vec = vec_ref[...]
    w0, a0, k_k, k_a, r_k, ln_w, ln_b = (vec[i:i + 1] for i in range(7))
    mu_r, mu_k, mu_v = (vec[i:i + 1] for i in range(7, 10))

    r_raw, k_raw, v_raw, wa_raw, g_raw = r_ref[0], k_ref[0], v_ref[0], wa_ref[0], g_ref[0]
    prev = prev_ref[...]
    r_all = lerp(r_raw, prev[0:1], mu_r)
    k_all = lerp(k_raw, prev[1:2], mu_k)
    v_all = lerp(v_raw, prev[2:3], mu_v)
    wa = lerp(wa_raw, prev[3:4], muwa_ref[...])
    g_low = lerp(g_raw, prevg_ref[0:1], mug_ref[...])
    prev_ref[0:1] = r_raw[block - 1:block]
    prev_ref[1:2] = k_raw[block - 1:block]
    prev_ref[2:3] = v_raw[block - 1:block]
    prev_ref[3:4] = wa_raw[block - 1:block]
    prevg_ref[0:1] = g_raw[block - 1:block]

    z = w0 + _mm_split(jnp.tanh(wa), w2_ref[...], a_parts=2, b_parts=2)
    logw_all = -jnp.exp(-_softplus(-z) - 0.5)
    lr_all = _sigmoid(a0 + _mm_split(wa, a2_ref[...], a_parts=2, b_parts=2))
    gate_all = _mm_split(_sigmoid(g_low), g2_ref[...], a_parts=2, b_parts=2)

    lane = _iota2((1, RWKV_PAIR), 1)
    m0 = (lane < RWKV_HEAD).astype(F32)
    m1 = 1.0 - m0
    rowi = _iota2((2 * T, 2 * T), 0)
    coli = _iota2((2 * T, 2 * T), 1)
    strict = rowi > coli
    incl = rowi >= coli
    head_ones = ((rowi < RWKV_HEAD) == (coli < RWKV_HEAD)).astype(BF16)
    tri_t = (_iota2((T, T), 0) >= _iota2((T, T), 1)).astype(BF16)
    ones_t = jnp.ones((T, RWKV_PAIR), BF16)

    def hsum(x):
        return _mm_split(x, head_ones, a_parts=2)

    def stack(x):
        return jnp.concatenate([x * m0, x * m1], axis=0)

    n_chunks = block // T

    pre = []
    for c in range(n_chunks):
        lo, hi = c * T, (c + 1) * T
        r, k, v = r_all[lo:hi], k_all[lo:hi], v_all[lo:hi]
        lw, lr = logw_all[lo:hi], lr_all[lo:hi]

        kkp = k * k_k
        kk = kkp / jnp.maximum(jnp.sqrt(hsum(kkp * kkp)), 1e-12)
        k2 = k * (1.0 + (lr - 1.0) * k_a)
        b_vec = kk * lr

        lw_parts = _split(lw, 3)
        cum = _dot(tri_t, lw_parts[0]) + _dot(tri_t, lw_parts[1]) + _dot(tri_t, lw_parts[2])
        cum_last = cum[T - 1:T]
        decay = jnp.exp(_dot_tn(lw_parts[0], ones_t) + _dot_tn(lw_parts[1], ones_t)
                        + _dot_tn(lw_parts[2], ones_t))
        w_inv = jnp.exp(-cum)
        w_tail = jnp.exp(cum_last - cum)
        a_s = stack(-kk * jnp.exp(cum - lw)).astype(BF16)
        r_s = stack(r * jnp.exp(cum)).astype(BF16)
        b_s = stack(b_vec * w_inv)
        k_s = stack(k2 * w_inv)
        v_s = stack(v).astype(BF16)
        bk_tail = jnp.concatenate([stack(b_vec * w_tail), stack(k2 * w_tail)],
                                  axis=0).astype(BF16)
        prod = _mm_nt(jnp.concatenate([a_s, r_s], axis=0),
                      jnp.concatenate([b_s, k_s], axis=0))
        pre.append(dict(
            a_s=a_s, r_s=r_s, v_s=v_s, bk_tail=bk_tail, decay=decay,
            a_ab=jnp.where(strict, prod[:2 * T, :2 * T], 0.0),
            a_ak=jnp.where(strict, prod[:2 * T, 2 * T:], 0.0).astype(BF16),
            a_rb=jnp.where(incl, prod[2 * T:, :2 * T], 0.0).astype(BF16),
            a_rk=jnp.where(incl, prod[2 * T:, 2 * T:], 0.0).astype(BF16),
            bonus=hsum(r * k2 * r_k) * v))

    invs = [x.astype(BF16) for x in _unit_lower_inverses([p["a_ab"] for p in pre], rowi, coli)]
    akv = [_dot(p["a_ak"], p["v_s"]).astype(BF16) for p in pre]
    inv_a = [_dot(x, p["a_s"]).astype(BF16) for x, p in zip(invs, pre)]
    u_free = [_dot(x, y) for x, y in zip(invs, akv)]
    y_free = [_dot(p["a_rk"], p["v_s"]) for p in pre]

    st = st_ref[...]
    ys = []
    for c, p in enumerate(pre):
        st_b = st.astype(BF16)
        u = (_dot(inv_a[c], st_b) + u_free[c]).astype(BF16)
        ys.append(_dot(p["r_s"], st_b) + _dot(p["a_rb"], u) + y_free[c])
        uv = jnp.concatenate([u, p["v_s"]], axis=0)
        st = st * p["decay"] + _dot_tn(p["bk_tail"], uv)
    st_ref[...] = st

    for c, p in enumerate(pre):
        lo, hi = c * T, (c + 1) * T
        y = ys[c][:T] + ys[c][T:]
        mu = hsum(y) * (1.0 / RWKV_HEAD)
        d = y - mu
        var = hsum(d * d) * (1.0 / RWKV_HEAD)
        yn = d * lax.rsqrt(var + RWKV_LN_EPS) * ln_w + ln_b
        o_ref[0, pl.ds(lo, T), :] = ((yn + p["bonus"]) * gate_all[lo:hi]).astype(o_ref.dtype)


def rwkv_mixer(p3, vecs, mu_wa, mu_g, w2p, a2p, g2p, *, block):
    B, S, _ = p3.shape
    kr, kk_, kv = COL_RWKV_R // RWKV_PAIR, COL_RWKV_K // RWKV_PAIR, COL_RWKV_V // RWKV_PAIR
    kwa = COL_LOW_WA // LANES
    kg = COL_LOW_G // (2 * LANES)
    return pl.pallas_call(
        functools.partial(_rwkv_kernel, block=block, chunk=RWKV_CHUNK),
        grid=(B, RWKV_PAIRS, S // block),
        in_specs=[
            pl.BlockSpec((1, block, RWKV_PAIR), lambda b, j, s: (b, s, kr + j)),
            pl.BlockSpec((1, block, RWKV_PAIR), lambda b, j, s: (b, s, kk_ + j)),
            pl.BlockSpec((1, block, RWKV_PAIR), lambda b, j, s: (b, s, kv + j)),
            pl.BlockSpec((1, block, LANES), lambda b, j, s: (b, s, kwa)),
            pl.BlockSpec((1, block, 2 * LANES), lambda b, j, s: (b, s, kg)),
            pl.BlockSpec((16, RWKV_PAIR), lambda b, j, s: (0, j)),
            pl.BlockSpec((1, LANES), lambda b, j, s: (0, 0)),
            pl.BlockSpec((1, 2 * LANES), lambda b, j, s: (0, 0)),
            pl.BlockSpec((LANES, RWKV_PAIR), lambda b, j, s: (0, j)),
            pl.BlockSpec((LANES, RWKV_PAIR), lambda b, j, s: (0, j)),
            pl.BlockSpec((2 * LANES, RWKV_PAIR), lambda b, j, s: (0, j)),
        ],
        out_specs=pl.BlockSpec((1, block, RWKV_PAIR), lambda b, j, s: (b, s, j)),
        out_shape=jax.ShapeDtypeStruct((B, S, RWKV_WIDTH), BF16),
        scratch_shapes=[
            pltpu.VMEM((RWKV_PAIR, RWKV_PAIR), F32),
            pltpu.VMEM((8, LANES), F32),
            pltpu.VMEM((8, 2 * LANES), F32),
        ],
        compiler_params=pltpu.CompilerParams(
            dimension_semantics=("parallel", "parallel", "arbitrary"),
            vmem_limit_bytes=VMEM_LIMIT),
    )(p3, p3, p3, p3, p3, vecs, mu_wa, mu_g, w2p, a2p, g2p)


def _moba_kernel(q_ref, k_ref, v_ref, slope_ref, o_ref, kb_ref, vt_ref, kmean_ref,
                 m_ref, l_ref, acc_ref, s_ref, p_ref, *, nb):
    BS = MOBA_BLOCK
    i = pl.program_id(2)

    @pl.when(i == 0)
    def _():
        kmean_ref[...] = jnp.zeros_like(kmean_ref)
        for j in range(nb):
            kj = k_ref[0, pl.ds(j * BS, BS), :]
            kmean_ref[j:j + 1, :] = jnp.mean(kj, axis=0, keepdims=True)
            kb_ref[j] = kj.astype(BF16)
            vt_ref[j] = v_ref[0, pl.ds(j * BS, BS), :].T.astype(BF16)

    q_t = q_ref[0].T
    q_tb = q_t.astype(BF16)
    scale = MOBA_HD ** -0.5
    slope = slope_ref[0]
    rel = (_iota2((BS, BS), 1) - _iota2((BS, BS), 0)).astype(F32)

    nbp = kmean_ref.shape[0]
    blk = _iota2((nbp, BS), 0)
    gate = _mm_split(kmean_ref[...], q_t, a_parts=3, b_parts=3)
    gate = jnp.where(blk < i, gate, NEG_INF)
    sel = []
    for r in range(MOBA_TOPK):
        mx = jnp.max(gate, axis=0, keepdims=True)
        idx = jnp.min(jnp.where(gate == mx, blk, nbp), axis=0, keepdims=True)
        sel.append(jnp.where(r < i, idx, -1))
        gate = jnp.where(blk == idx, -jnp.inf, gate)

    srel = slope * rel

    s_own = _dot(kb_ref[i], q_tb) * scale - srel
    s_own = jnp.where(rel >= 0, s_own, NEG_INF)
    m0 = jnp.max(s_own, axis=0, keepdims=True)
    p0 = jnp.exp(s_own - m0)
    m_ref[...] = m0
    l_ref[...] = jnp.sum(p0, axis=0, keepdims=True)
    p_ref[...] = p0.astype(BF16)
    acc_ref[...] = jnp.zeros_like(acc_ref)
    s_ref[0] = _dot(kb_ref[0], q_tb)

    def past_block(j, prev):
        s_cur = s_ref[j % 2]
        pv = _dot(vt_ref[prev], p_ref[...])
        s_ref[(j + 1) % 2] = _dot(kb_ref[jnp.minimum(j + 1, nb - 1)], q_tb)
        picked = (sel[0] == j) | (sel[1] == j) | (sel[2] == j)
        s = jnp.where(picked, s_cur * scale - srel, NEG_INF)
        off = slope * ((i - j) * BS).astype(F32)
        m_old = m_ref[...]
        m_new = jnp.maximum(m_old, jnp.max(s, axis=0, keepdims=True) - off)
        alpha = jnp.exp(m_old - m_new)
        p = jnp.exp(s - (m_new + off))
        l_ref[...] = alpha * l_ref[...] + jnp.sum(p, axis=0, keepdims=True)
        acc_ref[...] = alpha * (acc_ref[...] + pv)
        p_ref[...] = p.astype(BF16)
        m_ref[...] = m_new
        return j

    last = lax.fori_loop(0, i, past_block, i)
    acc = acc_ref[...] + _dot(vt_ref[last], p_ref[...])
    o_ref[0] = (acc / l_ref[...]).T.astype(o_ref.dtype)


def moba_attention(qkv3):
    B, S, _ = qkv3.shape
    H, D, BS = MOBA_HEADS, MOBA_HD, MOBA_BLOCK
    nb = S // BS
    nbp = -(-nb // 8) * 8
    slopes = jnp.exp2(-8.0 * jnp.arange(1, H + 1, dtype=F32) / H)
    slopes = jnp.broadcast_to(slopes[:, None, None], (H, 1, BS))
    return pl.pallas_call(
        functools.partial(_moba_kernel, nb=nb),
        grid=(B, H, nb),
        in_specs=[
            pl.BlockSpec((1, BS, D), lambda b, h, i: (b, i, h)),
            pl.BlockSpec((1, S, D), lambda b, h, i: (b, 0, H + h)),
            pl.BlockSpec((1, S, D), lambda b, h, i: (b, 0, 2 * H + h)),
            pl.BlockSpec((1, 1, BS), lambda b, h, i: (h, 0, 0)),
        ],
        out_specs=pl.BlockSpec((1, BS, D), lambda b, h, i: (b, i, h)),
        out_shape=jax.ShapeDtypeStruct((B, S, H * D), BF16),
        scratch_shapes=[
            pltpu.VMEM((nb, BS, D), BF16),
            pltpu.VMEM((nb, D, BS), BF16),
            pltpu.VMEM((nbp, D), F32),
            pltpu.VMEM((1, BS), F32),
            pltpu.VMEM((1, BS), F32),
            pltpu.VMEM((D, BS), F32),
            pltpu.VMEM((2, BS, BS), F32),
            pltpu.VMEM((BS, BS), BF16),
        ],
        compiler_params=pltpu.CompilerParams(
            dimension_semantics=("parallel", "parallel", "arbitrary"),
            vmem_limit_bytes=VMEM_LIMIT),
    )(qkv3, qkv3, qkv3, slopes)


def _pad_cols(w, n):
    return jnp.pad(w, ((0, 0), (0, n - w.shape[1])))


def _pad_rows(w, before, total):
    return jnp.pad(w, ((before, total - before - w.shape[0]), (0, 0)))


def _mix_in_layout(w):
    gk, gw = GLA_KEY_WIDTH, GLA_WIDTH
    g_end = 2 * gk + 2 * gw
    gla_main = w[:, :g_end]
    gla_gate = w[:, g_end:g_end + GLA_GATE_RANK]
    r0 = g_end + GLA_GATE_RANK
    rw_main = w[:, r0:r0 + 3 * RWKV_WIDTH]
    l0 = r0 + 3 * RWKV_WIDTH
    wa = w[:, l0:l0 + RWKV_DECAY_LORA + RWKV_AAA_LORA]
    g0 = l0 + RWKV_DECAY_LORA + RWKV_AAA_LORA
    gl = w[:, g0:g0 + RWKV_GATE_LORA]
    return jnp.concatenate(
        [gla_main, rw_main, _pad_cols(gla_gate, LANES), wa, _pad_cols(gl, 2 * LANES)], axis=1)


def mixer_layer0(hn_proj, gla_gate_w2, gla_gate_b, gla_norm, rwkv_mu, rwkv_w0, rwkv_w2, rwkv_a0,
                 rwkv_a2, rwkv_g2, rwkv_k_k, rwkv_k_a, rwkv_r_k, rwkv_ln_w, rwkv_ln_b,
                 *, gla_block, rwkv_block):
    W = RWKV_WIDTH
    o_gla = gla_mixer(hn_proj, _pad_rows(gla_gate_w2, 0, LANES), gla_gate_b, gla_norm,
                      block=gla_block)
    mu_r, mu_k, mu_v = rwkv_mu[:W], rwkv_mu[W:2 * W], rwkv_mu[2 * W:3 * W]
    mu_low = rwkv_mu[3 * W:]
    n_wa = RWKV_DECAY_LORA + RWKV_AAA_LORA
    vecs = jnp.stack([rwkv_w0, rwkv_a0, rwkv_k_k, rwkv_k_a, rwkv_r_k.reshape(-1), rwkv_ln_w,
                      rwkv_ln_b, mu_r, mu_k, mu_v] + [jnp.zeros((W,), F32)] * 6)
    mu_wa = mu_low[:n_wa].reshape(1, -1)
    mu_g = _pad_cols(mu_low[n_wa:].reshape(1, -1), 2 * LANES)
    w2p = _pad_rows(rwkv_w2, 0, LANES)
    a2p = _pad_rows(rwkv_a2, RWKV_DECAY_LORA, LANES)
    g2p = _pad_rows(rwkv_g2, 0, 2 * LANES)
    o_rwkv = rwkv_mixer(hn_proj, vecs, mu_wa, mu_g, w2p, a2p, g2p, block=rwkv_block)
    return o_gla, o_rwkv


def kernel(x, norm_mix, norm_ffn, norm_final, mix_in_w, gla_gate_w2, gla_gate_b, gla_norm, rwkv_mu, rwkv_w0, rwkv_w2, rwkv_a0, rwkv_a2, rwkv_g2, rwkv_k_k, rwkv_k_a, rwkv_r_k, rwkv_ln_w, rwkv_ln_b, mix_out_w, attn_qkv_w, attn_out_w, ffn_gate_w, ffn_up_w, ffn_down_w):
    B, S, D = x.shape
    M = B * S
    tm, tn, tf = 512, 512, 512
    tm_in = min(1024, M)
    h = x.reshape(M, D)

    w_in = _mix_in_layout(mix_in_w[0]).astype(BF16)
    p = norm_matmul(h, norm_mix[0], w_in, tm=tm_in, tn=tn).reshape(B, S, MIX_COLS)
    o_gla, o_rwkv = mixer_layer0(
        p, gla_gate_w2[0], gla_gate_b[0], gla_norm[0], rwkv_mu[0], rwkv_w0[0], rwkv_w2[0],
        rwkv_a0[0], rwkv_a2[0], rwkv_g2[0], rwkv_k_k[0], rwkv_k_a[0], rwkv_r_k[0], rwkv_ln_w[0],
        rwkv_ln_b[0], gla_block=512, rwkv_block=512)
    w_out = mix_out_w[0].astype(BF16)
    h = proj_residual(h, [(o_gla.reshape(M, GLA_WIDTH), w_out[:GLA_WIDTH]),
                          (o_rwkv.reshape(M, RWKV_WIDTH), w_out[GLA_WIDTH:])], tm=tm, tn=D)
    h = ffn_residual(h, norm_ffn[0], ffn_gate_w[0].astype(BF16), ffn_up_w[0].astype(BF16),
                     ffn_down_w[0].astype(BF16), norm_final, tm=tm, tf=tf, final_norm=False)

    qkv = norm_matmul(h, norm_mix[1], attn_qkv_w[0].astype(BF16), tm=tm_in, tn=tn)
    o_attn = moba_attention(qkv.reshape(B, S, 3 * D))
    h = proj_residual(h, [(o_attn.reshape(M, D), attn_out_w[0].astype(BF16))], tm=tm, tn=D)
    h = ffn_residual(h, norm_ffn[1], ffn_gate_w[1].astype(BF16), ffn_up_w[1].astype(BF16),
                     ffn_down_w[1].astype(BF16), norm_final, tm=tm, tf=tf, final_norm=True)
    return h.reshape(B, S, D)
```

```python
import functools

import jax
import jax.numpy as jnp
from jax import lax
from jax.experimental import pallas as pl
from jax.experimental.pallas import tpu as pltpu

F32 = jnp.float32
BF16 = jnp.bfloat16
HI = lax.Precision.HIGHEST

D_MODEL = 2048
RMS_EPS = 1e-6

GLA_HEADS = 4
GLA_DK = 128
GLA_DV = 256
GLA_KEY_WIDTH = GLA_HEADS * GLA_DK
GLA_WIDTH = GLA_HEADS * GLA_DV
GLA_GATE_RANK = 16
GLA_GATE_TAU = 16.0
GLA_CHUNK = 64

RWKV_WIDTH = 1024
RWKV_HEAD = 64
RWKV_PAIR = 2 * RWKV_HEAD
RWKV_PAIRS = RWKV_WIDTH // RWKV_PAIR
RWKV_DECAY_LORA = 64
RWKV_AAA_LORA = 64
RWKV_GATE_LORA = 160
RWKV_LN_EPS = RWKV_HEAD * 1e-5
RWKV_CHUNK = 64

MOBA_HEADS = 16
MOBA_HD = 128
MOBA_BLOCK = 256
MOBA_TOPK = 3
NEG_INF = -1e30

FFN_HIDDEN = 5632

LANES = 128

COL_GLA_Q = 0
COL_GLA_K = 512
COL_GLA_V = 1024
COL_GLA_OG = 2048
COL_RWKV_R = 3072
COL_RWKV_K = 4096
COL_RWKV_V = 5120
COL_LOW_GATE = 6144
COL_LOW_WA = 6272
COL_LOW_G = 6400
MIX_COLS = 6656

VMEM_LIMIT = 56 * 1024 * 1024


def _mm(a, b):
    return jnp.dot(a.astype(BF16), b.astype(BF16), preferred_element_type=F32)


def _mm_nt(a, b):
    return lax.dot_general(a.astype(BF16), b.astype(BF16), (((1,), (1,)), ((), ())),
                           preferred_element_type=F32)


def _mm_hi(a, b):
    return jnp.dot(a, b, preferred_element_type=F32, precision=HI)


def _mm_nt_hi(a, b):
    return lax.dot_general(a, b, (((1,), (1,)), ((), ())), preferred_element_type=F32,
                           precision=HI)


def _dot(a, b):
    return jnp.dot(a, b, preferred_element_type=F32)


def _dot_tn(a, b):
    return lax.dot_general(a, b, (((0,), (0,)), ((), ())), preferred_element_type=F32)


def _split(x, parts):
    out = []
    for _ in range(parts - 1):
        hi = x.astype(BF16)
        out.append(hi)
        x = x - hi.astype(F32)
    out.append(x.astype(BF16))
    return out


def _mm_split(a, b, *, a_parts=1, b_parts=1):
    a_p = _split(a, a_parts) if a_parts > 1 else [a.astype(BF16)]
    b_p = _split(b, b_parts) if b_parts > 1 else [b.astype(BF16)]
    acc = None
    for i, ai in enumerate(a_p):
        for j, bj in enumerate(b_p):
            if i + j < max(a_parts, b_parts):
                t = _dot(ai, bj)
                acc = t if acc is None else acc + t
    return acc


def _sigmoid(x):
    return 1.0 / (1.0 + jnp.exp(-x))


def _softplus(x):
    return jnp.maximum(x, 0.0) + jnp.log1p(jnp.exp(-jnp.abs(x)))


def _iota2(shape, axis):
    return lax.broadcasted_iota(jnp.int32, shape, axis)


def _norm_matmul_kernel(x_ref, g_ref, w_ref, o_ref, xn_ref):
    @pl.when(pl.program_id(1) == 0)
    def _():
        x = x_ref[...]
        ms = jnp.mean(x * x, axis=-1, keepdims=True)
        xn_ref[...] = (x * lax.rsqrt(ms + RMS_EPS) * g_ref[...]).astype(BF16)

    o_ref[...] = jnp.dot(xn_ref[...], w_ref[...], preferred_element_type=F32).astype(o_ref.dtype)


def norm_matmul(x, g, w, *, tm, tn, out_dtype=F32):
    M, D = x.shape
    N = w.shape[1]
    return pl.pallas_call(
        _norm_matmul_kernel,
        grid=(M // tm, N // tn),
        in_specs=[
            pl.BlockSpec((tm, D), lambda i, j: (i, 0)),
            pl.BlockSpec((1, D), lambda i, j: (0, 0)),
            pl.BlockSpec((D, tn), lambda i, j: (0, j)),
        ],
        out_specs=pl.BlockSpec((tm, tn), lambda i, j: (i, j)),
        out_shape=jax.ShapeDtypeStruct((M, N), out_dtype),
        scratch_shapes=[pltpu.VMEM((tm, D), BF16)],
        compiler_params=pltpu.CompilerParams(
            dimension_semantics=("parallel", "arbitrary"), vmem_limit_bytes=VMEM_LIMIT),
    )(x, g.reshape(1, D), w)


def _proj_res_kernel(*refs, n_in):
    res_ref = refs[0]
    o_ref = refs[1 + 2 * n_in]
    acc = res_ref[...]
    for i in range(n_in):
        acc = acc + jnp.dot(refs[1 + 2 * i][...], refs[2 + 2 * i][...],
                            preferred_element_type=F32)
    o_ref[...] = acc


def proj_residual(res, pairs, *, tm, tn):
    M, N = res.shape
    in_specs = [pl.BlockSpec((tm, tn), lambda i, j: (i, j))]
    args = [res]
    for a, w in pairs:
        K = a.shape[1]
        in_specs.append(pl.BlockSpec((tm, K), lambda i, j: (i, 0)))
        in_specs.append(pl.BlockSpec((K, tn), lambda i, j: (0, j)))
        args += [a, w]
    return pl.pallas_call(
        functools.partial(_proj_res_kernel, n_in=len(pairs)),
        grid=(M // tm, N // tn),
        in_specs=in_specs,
        out_specs=pl.BlockSpec((tm, tn), lambda i, j: (i, j)),
        out_shape=jax.ShapeDtypeStruct((M, N), F32),
        compiler_params=pltpu.CompilerParams(
            dimension_semantics=("parallel", "arbitrary"), vmem_limit_bytes=VMEM_LIMIT),
    )(*args)


def _ffn_kernel(x_ref, g_ref, wg_ref, wu_ref, wd_ref, gf_ref, o_ref, xn_ref, *, final_norm):
    j = pl.program_id(1)

    @pl.when(j == 0)
    def _():
        x = x_ref[...]
        ms = jnp.mean(x * x, axis=-1, keepdims=True)
        xn_ref[...] = (x * lax.rsqrt(ms + RMS_EPS) * g_ref[...]).astype(BF16)
        o_ref[...] = x

    xn = xn_ref[...]
    gate = jnp.dot(xn, wg_ref[...], preferred_element_type=F32)
    up = jnp.dot(xn, wu_ref[...], preferred_element_type=F32)
    act = (gate * _sigmoid(gate) * up).astype(BF16)
    o_ref[...] += jnp.dot(act, wd_ref[...], preferred_element_type=F32)

    if final_norm:
        @pl.when(j == pl.num_programs(1) - 1)
        def _():
            h = o_ref[...]
            ms = jnp.mean(h * h, axis=-1, keepdims=True)
            o_ref[...] = h * lax.rsqrt(ms + RMS_EPS) * gf_ref[...]


def ffn_residual(x, g, wg, wu, wd, g_final, *, tm, tf, final_norm):
    M, D = x.shape
    F = wg.shape[1]
    return pl.pallas_call(
        functools.partial(_ffn_kernel, final_norm=final_norm),
        grid=(M // tm, F // tf),
        in_specs=[
            pl.BlockSpec((tm, D), lambda i, j: (i, 0)),
            pl.BlockSpec((1, D), lambda i, j: (0, 0)),
            pl.BlockSpec((D, tf), lambda i, j: (0, j)),
            pl.BlockSpec((D, tf), lambda i, j: (0, j)),
            pl.BlockSpec((tf, D), lambda i, j: (j, 0)),
            pl.BlockSpec((1, D), lambda i, j: (0, 0)),
        ],
        out_specs=pl.BlockSpec((tm, D), lambda i, j: (i, 0)),
        out_shape=jax.ShapeDtypeStruct((M, D), F32),
        scratch_shapes=[pltpu.VMEM((tm, D), BF16)],
        compiler_params=pltpu.CompilerParams(
            dimension_semantics=("parallel", "arbitrary"), vmem_limit_bytes=VMEM_LIMIT),
    )(x, g.reshape(1, D), wg, wu, wd, g_final.reshape(1, D))


def _gla_kernel(q_ref, k_ref, v_ref, og_ref, gl_ref, w2_ref, b_ref, gn_ref, o_ref, st_ref,
                *, block, chunk):
    @pl.when(pl.program_id(2) == 0)
    def _():
        st_ref[...] = jnp.zeros_like(st_ref)

    logit = _mm_split(gl_ref[0], w2_ref[...], a_parts=2, b_parts=2) + b_ref[...]
    log_a = -_softplus(-logit) * (1.0 / GLA_GATE_TAU)
    tri = _iota2((chunk, chunk), 0) >= _iota2((chunk, chunk), 1)
    tri_b = tri.astype(BF16)
    scale = GLA_DK ** -0.5
    gn = gn_ref[...]
    n_chunks = block // chunk

    pre = []
    for c in range(n_chunks):
        sl = pl.ds(c * chunk, chunk)
        la = _split(log_a[c * chunk:(c + 1) * chunk], 3)
        b = _dot(tri_b, la[0]) + _dot(tri_b, la[1]) + _dot(tri_b, la[2])
        b_last = b[chunk - 1:chunk]
        q = q_ref[0, sl, :] * scale
        k = k_ref[0, sl, :]
        v = v_ref[0, sl, :]
        q_dec = (q * jnp.exp(b)).astype(BF16)
        k_inv = k * jnp.exp(-b)
        k_dec = (k * jnp.exp(b_last - b)).astype(BF16)
        att = jnp.where(tri, _mm_nt(q_dec, k_inv), 0.0)
        pre.append(dict(q_dec=q_dec, k_dec=k_dec, v_t=v.T.astype(BF16), o_intra=_mm(att, v),
                        decay=jnp.exp(b_last)))

    st = st_ref[...]
    outs = []
    for p in pre:
        outs.append(p["o_intra"] + _mm_nt(p["q_dec"], st))
        st = st * p["decay"] + _dot(p["v_t"], p["k_dec"])
    st_ref[...] = st

    for c, o in enumerate(outs):
        sl = pl.ds(c * chunk, chunk)
        ms = jnp.mean(o * o, axis=-1, keepdims=True)
        og = og_ref[0, sl, :]
        o_ref[0, sl, :] = (o * lax.rsqrt(ms + RMS_EPS) * gn * (og * _sigmoid(og))).astype(o_ref.dtype)


def gla_mixer(p3, gate_w2p, gate_b, gla_norm, *, block):
    B, S, _ = p3.shape
    kq, kk_, kv, kog = (COL_GLA_Q // GLA_DK, COL_GLA_K // GLA_DK, COL_GLA_V // GLA_DV,
                        COL_GLA_OG // GLA_DV)
    klow = COL_LOW_GATE // LANES
    return pl.pallas_call(
        functools.partial(_gla_kernel, block=block, chunk=GLA_CHUNK),
        grid=(B, GLA_HEADS, S // block),
        in_specs=[
            pl.BlockSpec((1, block, GLA_DK), lambda b, h, s: (b, s, kq + h)),
            pl.BlockSpec((1, block, GLA_DK), lambda b, h, s: (b, s, kk_ + h)),
            pl.BlockSpec((1, block, GLA_DV), lambda b, h, s: (b, s, kv + h)),
            pl.BlockSpec((1, block, GLA_DV), lambda b, h, s: (b, s, kog + h)),
            pl.BlockSpec((1, block, LANES), lambda b, h, s: (b, s, klow)),
            pl.BlockSpec((LANES, GLA_DK), lambda b, h, s: (0, h)),
            pl.BlockSpec((1, GLA_DK), lambda b, h, s: (0, h)),
            pl.BlockSpec((1, GLA_DV), lambda b, h, s: (0, 0)),
        ],
        out_specs=pl.BlockSpec((1, block, GLA_DV), lambda b, h, s: (b, s, h)),
        out_shape=jax.ShapeDtypeStruct((B, S, GLA_WIDTH), BF16),
        scratch_shapes=[pltpu.VMEM((GLA_DV, GLA_DK), F32)],
        compiler_params=pltpu.CompilerParams(
            dimension_semantics=("parallel", "parallel", "arbitrary"),
            vmem_limit_bytes=VMEM_LIMIT),
    )(p3, p3, p3, p3, p3, gate_w2p, gate_b.reshape(1, -1), gla_norm.reshape(1, -1))


def _unit_lower_inverses(mats, rowi, coli):
    def same_block(s):
        sh = s.bit_length() - 1
        return (rowi >> sh) == (coli >> sh)

    eye = (rowi == coli).astype(F32)
    a8 = [jnp.where(same_block(8), a, 0.0).astype(BF16) for a in mats]
    inv = [eye + a.astype(F32) for a in a8]
    p = [_dot(a, a) for a in a8]
    inv = [x + _mm(x, y) for x, y in zip(inv, p)]
    p = [_mm(y, y) for y in p]
    inv = [x + _mm(x, y) for x, y in zip(inv, p)]
    s = 8
    while s < RWKV_CHUNK:
        off = same_block(2 * s) & jnp.logical_not(same_block(s))
        e = [jnp.where(off, a, 0.0).astype(BF16) for a in mats]
        inv_b = [x.astype(BF16) for x in inv]
        t = [_dot(x, y).astype(BF16) for x, y in zip(inv_b, e)]
        inv = [x + _dot(y, z) for x, y, z in zip(inv, t, inv_b)]
        s *= 2
    return inv


def _rwkv_kernel(r_ref, k_ref, v_ref, wa_ref, g_ref, vec_ref, muwa_ref, mug_ref, w2_ref, a2_ref,
                 g2_ref, o_ref, st_ref, prev_ref, prevg_ref, *, block, chunk):
    T = chunk
    first = pl.program_id(2) == 0

    @pl.when(first)
    def _():
        st_ref[...] = jnp.zeros_like(st_ref)
        prev_ref[...] = jnp.zeros_like(prev_ref)
        prevg_ref[...] = jnp.zeros_like(prevg_ref)

    row0 = _iota2((block, 1), 0) == 0

    def lerp(x, prev_row, mu):
        shifted = jnp.where(row0, prev_row, pltpu.roll(x, 1, axis=0))
        return x + (shifted - x) * mu

    vec = vec_ref[...]
    w0, a0, k_k, k_a, r_k, ln_w, ln_b = (vec[i:i + 1] for i in range(7))
    mu_r, mu_k, mu_v = (vec[i:i + 1] for i in range(7, 10))

    r_raw, k_raw, v_raw, wa_raw, g_raw = r_ref[0], k_ref[0], v_ref[0], wa_ref[0], g_ref[0]
    prev = prev_ref[...]
    r_all = lerp(r_raw, prev[0:1], mu_r)
    k_all = lerp(k_raw, prev[1:2], mu_k)
    v_all = lerp(v_raw, prev[2:3], mu_v)
    wa = lerp(wa_raw, prev[3:4], muwa_ref[...])
    g_low = lerp(g_raw, prevg_ref[0:1], mug_ref[...])
    prev_ref[0:1] = r_raw[block - 1:block]
    prev_ref[1:2] = k_raw[block - 1:block]
    prev_ref[2:3] = v_raw[block - 1:block]
    prev_ref[3:4] = wa_raw[block - 1:block]
    prevg_ref[0:1] = g_raw[block - 1:block]

    z = w0 + _mm_split(jnp.tanh(wa), w2_ref[...], a_parts=2, b_parts=2)
    logw_all = -jnp.exp(-_softplus(-z) - 0.5)
    lr_all = _sigmoid(a0 + _mm_split(wa, a2_ref[...], a_parts=2, b_parts=2))
    gate_all = _mm_split(_sigmoid(g_low), g2_ref[...], a_parts=2, b_parts=2)

    lane = _iota2((1, RWKV_PAIR), 1)
    m0 = (lane < RWKV_HEAD).astype(F32)
    m1 = 1.0 - m0
    rowi = _iota2((2 * T, 2 * T), 0)
    coli = _iota2((2 * T, 2 * T), 1)
    strict = rowi > coli
    incl = rowi >= coli
    head_ones = ((rowi < RWKV_HEAD) == (coli < RWKV_HEAD)).astype(BF16)
    tri_t = (_iota2((T, T), 0) >= _iota2((T, T), 1)).astype(BF16)
    ones_t = jnp.ones((T, RWKV_PAIR), BF16)

    def hsum(x):
        return _mm_split(x, head_ones, a_parts=2)

    def stack(x):
        return jnp.concatenate([x * m0, x * m1], axis=0)

    n_chunks = block // T

    pre = []
    for c in range(n_chunks):
        lo, hi = c * T, (c + 1) * T
        r, k, v = r_all[lo:hi], k_all[lo:hi], v_all[lo:hi]
        lw, lr = logw_all[lo:hi], lr_all[lo:hi]

        kkp = k * k_k
        kk = kkp / jnp.maximum(jnp.sqrt(hsum(kkp * kkp)), 1e-12)
        k2 = k * (1.0 + (lr - 1.0) * k_a)
        b_vec = kk * lr

        lw_parts = _split(lw, 3)
        cum = _dot(tri_t, lw_parts[0]) + _dot(tri_t, lw_parts[1]) + _dot(tri_t, lw_parts[2])
        cum_last = cum[T - 1:T]
        decay = jnp.exp(_dot_tn(lw_parts[0], ones_t) + _dot_tn(lw_parts[1], ones_t)
                        + _dot_tn(lw_parts[2], ones_t))
        w_inv = jnp.exp(-cum)
        w_tail = jnp.exp(cum_last - cum)
        a_s = stack(-kk * jnp.exp(cum - lw)).astype(BF16)
        r_s = stack(r * jnp.exp(cum)).astype(BF16)
        b_s = stack(b_vec * w_inv)
        k_s = stack(k2 * w_inv)
        v_s = stack(v).astype(BF16)
        bk_tail = jnp.concatenate([stack(b_vec * w_tail), stack(k2 * w_tail)],
                                  axis=0).astype(BF16)
        prod = _mm_nt(jnp.concatenate([a_s, r_s], axis=0),
                      jnp.concatenate([b_s, k_s], axis=0))
        pre.append(dict(
            a_s=a_s, r_s=r_s, v_s=v_s, bk_tail=bk_tail, decay=decay,
            a_ab=jnp.where(strict, prod[:2 * T, :2 * T], 0.0),
            a_ak=jnp.where(strict, prod[:2 * T, 2 * T:], 0.0).astype(BF16),
            a_rb=jnp.where(incl, prod[2 * T:, :2 * T], 0.0).astype(BF16),
            a_rk=jnp.where(incl, prod[2 * T:, 2 * T:], 0.0).astype(BF16),
            bonus=hsum(r * k2 * r_k) * v))

    invs = [x.astype(BF16) for x in _unit_lower_inverses([p["a_ab"] for p in pre], rowi, coli)]
    akv = [_dot(p["a_ak"], p["v_s"]).astype(BF16) for p in pre]
    inv_a = [_dot(x, p["a_s"]).astype(BF16) for x, p in zip(invs, pre)]
    u_free = [_dot(x, y) for x, y in zip(invs, akv)]
    y_free = [_dot(p["a_rk"], p["v_s"]) for p in pre]

    st = st_ref[...]
    ys = []
    for c, p in enumerate(pre):
        st_b = st.astype(BF16)
        u = (_dot(inv_a[c], st_b) + u_free[c]).astype(BF16)
        ys.append(_dot(p["r_s"], st_b) + _dot(p["a_rb"], u) + y_free[c])
        uv = jnp.concatenate([u, p["v_s"]], axis=0)
        st = st * p["decay"] + _dot_tn(p["bk_tail"], uv)
    st_ref[...] = st

    for c, p in enumerate(pre):
        lo, hi = c * T, (c + 1) * T
        y = ys[c][:T] + ys[c][T:]
        mu = hsum(y) * (1.0 / RWKV_HEAD)
        d = y - mu
        var = hsum(d * d) * (1.0 / RWKV_HEAD)
        yn = d * lax.rsqrt(var + RWKV_LN_EPS) * ln_w + ln_b
        o_ref[0, pl.ds(lo, T), :] = ((yn + p["bonus"]) * gate_all[lo:hi]).astype(o_ref.dtype)


def rwkv_mixer(p3, vecs, mu_wa, mu_g, w2p, a2p, g2p, *, block):
    B, S, _ = p3.shape
    kr, kk_, kv = COL_RWKV_R // RWKV_PAIR, COL_RWKV_K // RWKV_PAIR, COL_RWKV_V // RWKV_PAIR
    kwa = COL_LOW_WA // LANES
    kg = COL_LOW_G // (2 * LANES)
    return pl.pallas_call(
        functools.partial(_rwkv_kernel, block=block, chunk=RWKV_CHUNK),
        grid=(B, RWKV_PAIRS, S // block),
        in_specs=[
            pl.BlockSpec((1, block, RWKV_PAIR), lambda b, j, s: (b, s, kr + j)),
            pl.BlockSpec((1, block, RWKV_PAIR), lambda b, j, s: (b, s, kk_ + j)),
            pl.BlockSpec((1, block, RWKV_PAIR), lambda b, j, s: (b, s, kv + j)),
            pl.BlockSpec((1, block, LANES), lambda b, j, s: (b, s, kwa)),
            pl.BlockSpec((1, block, 2 * LANES), lambda b, j, s: (b, s, kg)),
            pl.BlockSpec((16, RWKV_PAIR), lambda b, j, s: (0, j)),
            pl.BlockSpec((1, LANES), lambda b, j, s: (0, 0)),
            pl.BlockSpec((1, 2 * LANES), lambda b, j, s: (0, 0)),
            pl.BlockSpec((LANES, RWKV_PAIR), lambda b, j, s: (0, j)),
            pl.BlockSpec((LANES, RWKV_PAIR), lambda b, j, s: (0, j)),
            pl.BlockSpec((2 * LANES, RWKV_PAIR), lambda b, j, s: (0, j)),
        ],
        out_specs=pl.BlockSpec((1, block, RWKV_PAIR), lambda b, j, s: (b, s, j)),
        out_shape=jax.ShapeDtypeStruct((B, S, RWKV_WIDTH), BF16),
        scratch_shapes=[
            pltpu.VMEM((RWKV_PAIR, RWKV_PAIR), F32),
            pltpu.VMEM((8, LANES), F32),
            pltpu.VMEM((8, 2 * LANES), F32),
        ],
        compiler_params=pltpu.CompilerParams(
            dimension_semantics=("parallel", "parallel", "arbitrary"),
            vmem_limit_bytes=VMEM_LIMIT),
    )(p3, p3, p3, p3, p3, vecs, mu_wa, mu_g, w2p, a2p, g2p)


def _moba_kernel(q_ref, k_ref, v_ref, slope_ref, o_ref, kb_ref, vt_ref, kmean_ref,
                 m_ref, l_ref, acc_ref, s_ref, p_ref, *, nb):
    BS = MOBA_BLOCK
    i = pl.program_id(2)

    @pl.when(i == 0)
    def _():
        kmean_ref[...] = jnp.zeros_like(kmean_ref)
        for j in range(nb):
            half = slice((j % 2) * BS, (j % 2 + 1) * BS)
            kj = k_ref[0, pl.ds(j * BS, BS), :]
            kmean_ref[j:j + 1, :] = jnp.mean(kj, axis=0, keepdims=True)
            kb_ref[j // 2, half, :] = kj.astype(BF16)
            vt_ref[j // 2, :, half] = v_ref[0, pl.ds(j * BS, BS), :].T.astype(BF16)

    q_t = q_ref[0].T
    q_tb = q_t.astype(BF16)
    scale = MOBA_HD ** -0.5
    slope = slope_ref[0]
    rel = (_iota2((BS, BS), 1) - _iota2((BS, BS), 0)).astype(F32)

    nbp = kmean_ref.shape[0]
    blk = _iota2((nbp, BS), 0)
    gate = _mm_split(kmean_ref[...], q_t, a_parts=3, b_parts=3)
    gate = jnp.where(blk < i, gate, NEG_INF)
    sel = []
    for r in range(MOBA_TOPK):
        mx = jnp.max(gate, axis=0, keepdims=True)
        idx = jnp.min(jnp.where(gate == mx, blk, nbp), axis=0, keepdims=True)
        sel.append(jnp.where(r < i, idx, -1))
        gate = jnp.where(blk == idx, -jnp.inf, gate)

    srel_a = slope * rel
    srel_b = slope * (rel - float(BS))

    own_pair = i // 2
    own_half = pl.multiple_of((i % 2) * BS, BS)
    other_half = pl.multiple_of((1 - i % 2) * BS, BS)
    s_own = _dot(kb_ref[own_pair, pl.ds(own_half, BS), :], q_tb) * scale - srel_a
    s_own = jnp.where(rel >= 0, s_own, NEG_INF)
    m0 = jnp.max(s_own, axis=0, keepdims=True)
    p0 = jnp.exp(s_own - m0)
    m_ref[...] = m0
    l_ref[...] = jnp.sum(p0, axis=0, keepdims=True)
    p_ref[pl.ds(own_half, BS), :] = p0.astype(BF16)
    p_ref[pl.ds(other_half, BS), :] = jnp.zeros((BS, BS), BF16)
    acc_ref[...] = jnp.zeros_like(acc_ref)
    s_ref[0] = _dot(kb_ref[0], q_tb)

    def past_pair(t, prev):
        s_cur = s_ref[t % 2]
        pv = _dot(vt_ref[prev], p_ref[...])
        s_ref[(t + 1) % 2] = _dot(kb_ref[jnp.minimum(t + 1, nb // 2 - 1)], q_tb)
        ja, jb = 2 * t, 2 * t + 1
        picked_a = (sel[0] == ja) | (sel[1] == ja) | (sel[2] == ja)
        picked_b = (sel[0] == jb) | (sel[1] == jb) | (sel[2] == jb)
        s_a = jnp.where(picked_a, s_cur[:BS] * scale - srel_a, NEG_INF)
        s_b = jnp.where(picked_b, s_cur[BS:] * scale - srel_b, NEG_INF)
        off = slope * ((i - ja) * BS).astype(F32)
        m_old = m_ref[...]
        mx = jnp.maximum(jnp.max(s_a, axis=0, keepdims=True), jnp.max(s_b, axis=0, keepdims=True))
        m_new = jnp.maximum(m_old, mx - off)
        alpha = jnp.exp(m_old - m_new)
        shift = m_new + off
        p_a = jnp.exp(s_a - shift)
        p_b = jnp.exp(s_b - shift)
        l_ref[...] = (alpha * l_ref[...] + jnp.sum(p_a, axis=0, keepdims=True)
                      + jnp.sum(p_b, axis=0, keepdims=True))
        acc_ref[...] = alpha * (acc_ref[...] + pv)
        p_ref[:BS, :] = p_a.astype(BF16)
        p_ref[BS:, :] = p_b.astype(BF16)
        m_ref[...] = m_new
        return t

    last = lax.fori_loop(0, (i + 1) // 2, past_pair, own_pair)
    acc = acc_ref[...] + _dot(vt_ref[last], p_ref[...])
    o_ref[0] = (acc / l_ref[...]).T.astype(o_ref.dtype)


def moba_attention(qkv3):
    B, S, _ = qkv3.shape
    H, D, BS = MOBA_HEADS, MOBA_HD, MOBA_BLOCK
    nb = S // BS
    assert nb % 2 == 0, "key blocks are stored in pairs"
    nbp = -(-nb // 8) * 8
    slopes = jnp.exp2(-8.0 * jnp.arange(1, H + 1, dtype=F32) / H)
    slopes = jnp.broadcast_to(slopes[:, None, None], (H, 1, BS))
    return pl.pallas_call(
        functools.partial(_moba_kernel, nb=nb),
        grid=(B, H, nb),
        in_specs=[
            pl.BlockSpec((1, BS, D), lambda b, h, i: (b, i, h)),
            pl.BlockSpec((1, S, D), lambda b, h, i: (b, 0, H + h)),
            pl.BlockSpec((1, S, D), lambda b, h, i: (b, 0, 2 * H + h)),
            pl.BlockSpec((1, 1, BS), lambda b, h, i: (h, 0, 0)),
        ],
        out_specs=pl.BlockSpec((1, BS, D), lambda b, h, i: (b, i, h)),
        out_shape=jax.ShapeDtypeStruct((B, S, H * D), BF16),
        scratch_shapes=[
            pltpu.VMEM((nb // 2, 2 * BS, D), BF16),
            pltpu.VMEM((nb // 2, D, 2 * BS), BF16),
            pltpu.VMEM((nbp, D), F32),
            pltpu.VMEM((1, BS), F32),
            pltpu.VMEM((1, BS), F32),
            pltpu.VMEM((D, BS), F32),
            pltpu.VMEM((2, 2 * BS, BS), F32),
            pltpu.VMEM((2 * BS, BS), BF16),
        ],
        compiler_params=pltpu.CompilerParams(
            dimension_semantics=("parallel", "parallel", "arbitrary"),
            vmem_limit_bytes=VMEM_LIMIT),
    )(qkv3, qkv3, qkv3, slopes)


def _pad_cols(w, n):
    return jnp.pad(w, ((0, 0), (0, n - w.shape[1])))


def _pad_rows(w, before, total):
    return jnp.pad(w, ((before, total - before - w.shape[0]), (0, 0)))


def _mix_in_layout(w):
    gk, gw = GLA_KEY_WIDTH, GLA_WIDTH
    g_end = 2 * gk + 2 * gw
    gla_main = w[:, :g_end]
    gla_gate = w[:, g_end:g_end + GLA_GATE_RANK]
    r0 = g_end + GLA_GATE_RANK
    rw_main = w[:, r0:r0 + 3 * RWKV_WIDTH]
    l0 = r0 + 3 * RWKV_WIDTH
    wa = w[:, l0:l0 + RWKV_DECAY_LORA + RWKV_AAA_LORA]
    g0 = l0 + RWKV_DECAY_LORA + RWKV_AAA_LORA
    gl = w[:, g0:g0 + RWKV_GATE_LORA]
    return jnp.concatenate(
        [gla_main, rw_main, _pad_cols(gla_gate, LANES), wa, _pad_cols(gl, 2 * LANES)], axis=1)


def mixer_layer0(hn_proj, gla_gate_w2, gla_gate_b, gla_norm, rwkv_mu, rwkv_w0, rwkv_w2, rwkv_a0,
                 rwkv_a2, rwkv_g2, rwkv_k_k, rwkv_k_a, rwkv_r_k, rwkv_ln_w, rwkv_ln_b,
                 *, gla_block, rwkv_block):
    W = RWKV_WIDTH
    o_gla = gla_mixer(hn_proj, _pad_rows(gla_gate_w2, 0, LANES), gla_gate_b, gla_norm,
                      block=gla_block)
    mu_r, mu_k, mu_v = rwkv_mu[:W], rwkv_mu[W:2 * W], rwkv_mu[2 * W:3 * W]
    mu_low = rwkv_mu[3 * W:]
    n_wa = RWKV_DECAY_LORA + RWKV_AAA_LORA
    vecs = jnp.stack([rwkv_w0, rwkv_a0, rwkv_k_k, rwkv_k_a, rwkv_r_k.reshape(-1), rwkv_ln_w,
                      rwkv_ln_b, mu_r, mu_k, mu_v] + [jnp.zeros((W,), F32)] * 6)
    mu_wa = mu_low[:n_wa].reshape(1, -1)
    mu_g = _pad_cols(mu_low[n_wa:].reshape(1, -1), 2 * LANES)
    w2p = _pad_rows(rwkv_w2, 0, LANES)
    a2p = _pad_rows(rwkv_a2, RWKV_DECAY_LORA, LANES)
    g2p = _pad_rows(rwkv_g2, 0, 2 * LANES)
    o_rwkv = rwkv_mixer(hn_proj, vecs, mu_wa, mu_g, w2p, a2p, g2p, block=rwkv_block)
    return o_gla, o_rwkv


def kernel(x, norm_mix, norm_ffn, norm_final, mix_in_w, gla_gate_w2, gla_gate_b, gla_norm, rwkv_mu, rwkv_w0, rwkv_w2, rwkv_a0, rwkv_a2, rwkv_g2, rwkv_k_k, rwkv_k_a, rwkv_r_k, rwkv_ln_w, rwkv_ln_b, mix_out_w, attn_qkv_w, attn_out_w, ffn_gate_w, ffn_up_w, ffn_down_w):
    B, S, D = x.shape
    M = B * S
    tm, tn, tf = 512, 512, 512
    tm_in = min(1024, M)
    h = x.reshape(M, D)

    w_in = _mix_in_layout(mix_in_w[0]).astype(BF16)
    p = norm_matmul(h, norm_mix[0], w_in, tm=tm_in, tn=tn).reshape(B, S, MIX_COLS)
    o_gla, o_rwkv = mixer_layer0(
        p, gla_gate_w2[0], gla_gate_b[0], gla_norm[0], rwkv_mu[0], rwkv_w0[0], rwkv_w2[0],
        rwkv_a0[0], rwkv_a2[0], rwkv_g2[0], rwkv_k_k[0], rwkv_k_a[0], rwkv_r_k[0], rwkv_ln_w[0],
        rwkv_ln_b[0], gla_block=512, rwkv_block=512)
    w_out = mix_out_w[0].astype(BF16)
    h = proj_residual(h, [(o_gla.reshape(M, GLA_WIDTH), w_out[:GLA_WIDTH]),
                          (o_rwkv.reshape(M, RWKV_WIDTH), w_out[GLA_WIDTH:])], tm=tm, tn=D)
    h = ffn_residual(h, norm_ffn[0], ffn_gate_w[0].astype(BF16), ffn_up_w[0].astype(BF16),
                     ffn_down_w[0].astype(BF16), norm_final, tm=tm, tf=tf, final_norm=False)

    qkv = norm_matmul(h, norm_mix[1], attn_qkv_w[0].astype(BF16), tm=tm_in, tn=tn)
    o_attn = moba_attention(qkv.reshape(B, S, 3 * D))
    h = proj_residual(h, [(o_attn.reshape(M, D), attn_out_w[0].astype(BF16))], tm=tm, tn=D)
    h = ffn_residual(h, norm_ffn[1], ffn_gate_w[1].astype(BF16), ffn_up_w[1].astype(BF16),
                     ffn_down_w[1].astype(BF16), norm_final, tm=tm, tf=tf, final_norm=True)
    return h.reshape(B, S, D)
```

```python
import functools

import jax
import jax.numpy as jnp
from jax import lax
from jax.experimental import pallas as pl
from jax.experimental.pallas import tpu as pltpu

F32 = jnp.float32
BF16 = jnp.bfloat16
HI = lax.Precision.HIGHEST

D_MODEL = 2048
RMS_EPS = 1e-6

GLA_HEADS = 4
GLA_DK = 128
GLA_DV = 256
GLA_KEY_WIDTH = GLA_HEADS * GLA_DK
GLA_WIDTH = GLA_HEADS * GLA_DV
GLA_GATE_RANK = 16
GLA_GATE_TAU = 16.0
GLA_CHUNK = 64

RWKV_WIDTH = 1024
RWKV_HEAD = 64
RWKV_PAIR = 2 * RWKV_HEAD
RWKV_PAIRS = RWKV_WIDTH // RWKV_PAIR
RWKV_DECAY_LORA = 64
RWKV_AAA_LORA = 64
RWKV_GATE_LORA = 160
RWKV_LN_EPS = RWKV_HEAD * 1e-5
RWKV_CHUNK = 64

MOBA_HEADS = 16
MOBA_HD = 128
MOBA_BLOCK = 256
MOBA_TOPK = 3
NEG_INF = -1e30

FFN_HIDDEN = 5632

LANES = 128

COL_GLA_Q = 0
COL_GLA_K = 512
COL_GLA_V = 1024
COL_GLA_OG = 2048
COL_RWKV_R = 3072
COL_RWKV_K = 4096
COL_RWKV_V = 5120
COL_LOW_GATE = 6144
COL_LOW_WA = 6272
COL_LOW_G = 6400
MIX_COLS = 6656

VMEM_LIMIT = 56 * 1024 * 1024


def _mm(a, b):
    return jnp.dot(a.astype(BF16), b.astype(BF16), preferred_element_type=F32)


def _mm_nt(a, b):
    return lax.dot_general(a.astype(BF16), b.astype(BF16), (((1,), (1,)), ((), ())),
                           preferred_element_type=F32)


def _mm_hi(a, b):
    return jnp.dot(a, b, preferred_element_type=F32, precision=HI)


def _mm_nt_hi(a, b):
    return lax.dot_general(a, b, (((1,), (1,)), ((), ())), preferred_element_type=F32,
                           precision=HI)


def _dot(a, b):
    return jnp.dot(a, b, preferred_element_type=F32)


def _dot_tn(a, b):
    return lax.dot_general(a, b, (((0,), (0,)), ((), ())), preferred_element_type=F32)


def _split(x, parts):
    out = []
    for _ in range(parts - 1):
        hi = x.astype(BF16)
        out.append(hi)
        x = x - hi.astype(F32)
    out.append(x.astype(BF16))
    return out


def _mm_split(a, b, *, a_parts=1, b_parts=1):
    a_p = _split(a, a_parts) if a_parts > 1 else [a.astype(BF16)]
    b_p = _split(b, b_parts) if b_parts > 1 else [b.astype(BF16)]
    acc = None
    for i, ai in enumerate(a_p):
        for j, bj in enumerate(b_p):
            if i + j < max(a_parts, b_parts):
                t = _dot(ai, bj)
                acc = t if acc is None else acc + t
    return acc


def _sigmoid(x):
    return 1.0 / (1.0 + jnp.exp(-x))


def _softplus(x):
    return jnp.maximum(x, 0.0) + jnp.log1p(jnp.exp(-jnp.abs(x)))


def _iota2(shape, axis):
    return lax.broadcasted_iota(jnp.int32, shape, axis)


def _norm_matmul_kernel(x_ref, g_ref, w_ref, o_ref, xn_ref, *, group):
    @pl.when(pl.program_id(1) == 0)
    def _():
        x = x_ref[...]
        ms = jnp.mean(x * x, axis=-1, keepdims=True)
        xn_ref[...] = (x * lax.rsqrt(ms + RMS_EPS) * g_ref[...]).astype(BF16)

    res = jnp.dot(xn_ref[...], w_ref[...], preferred_element_type=F32).astype(o_ref.dtype)
    if group is None:
        o_ref[...] = res
    else:
        for c in range(res.shape[1] // group):
            o_ref[c] = res[:, c * group:(c + 1) * group]


def norm_matmul(x, g, w, *, tm, tn, out_dtype=F32, group=None):
    M, D = x.shape
    N = w.shape[1]
    if group is None:
        out_specs = pl.BlockSpec((tm, tn), lambda i, j: (i, j))
        out_shape = jax.ShapeDtypeStruct((M, N), out_dtype)
    else:
        out_specs = pl.BlockSpec((tn // group, tm, group), lambda i, j: (j, i, 0))
        out_shape = jax.ShapeDtypeStruct((N // group, M, group), out_dtype)
    return pl.pallas_call(
        functools.partial(_norm_matmul_kernel, group=group),
        grid=(M // tm, N // tn),
        in_specs=[
            pl.BlockSpec((tm, D), lambda i, j: (i, 0)),
            pl.BlockSpec((1, D), lambda i, j: (0, 0)),
            pl.BlockSpec((D, tn), lambda i, j: (0, j)),
        ],
        out_specs=out_specs,
        out_shape=out_shape,
        scratch_shapes=[pltpu.VMEM((tm, D), BF16)],
        compiler_params=pltpu.CompilerParams(
            dimension_semantics=("parallel", "arbitrary"), vmem_limit_bytes=VMEM_LIMIT),
    )(x, g.reshape(1, D), w)


def _proj_res_kernel(*refs, n_in):
    res_ref = refs[0]
    o_ref = refs[1 + 2 * n_in]
    acc = res_ref[...]
    for i in range(n_in):
        acc = acc + jnp.dot(refs[1 + 2 * i][...], refs[2 + 2 * i][...],
                            preferred_element_type=F32)
    o_ref[...] = acc


def proj_residual(res, pairs, *, tm, tn):
    M, N = res.shape
    in_specs = [pl.BlockSpec((tm, tn), lambda i, j: (i, j))]
    args = [res]
    for a, w in pairs:
        K = a.shape[1]
        in_specs.append(pl.BlockSpec((tm, K), lambda i, j: (i, 0)))
        in_specs.append(pl.BlockSpec((K, tn), lambda i, j: (0, j)))
        args += [a, w]
    return pl.pallas_call(
        functools.partial(_proj_res_kernel, n_in=len(pairs)),
        grid=(M // tm, N // tn),
        in_specs=in_specs,
        out_specs=pl.BlockSpec((tm, tn), lambda i, j: (i, j)),
        out_shape=jax.ShapeDtypeStruct((M, N), F32),
        compiler_params=pltpu.CompilerParams(
            dimension_semantics=("parallel", "arbitrary"), vmem_limit_bytes=VMEM_LIMIT),
    )(*args)


def _ffn_kernel(x_ref, g_ref, wg_ref, wu_ref, wd_ref, gf_ref, o_ref, xn_ref, *, final_norm):
    j = pl.program_id(1)

    @pl.when(j == 0)
    def _():
        x = x_ref[...]
        ms = jnp.mean(x * x, axis=-1, keepdims=True)
        xn_ref[...] = (x * lax.rsqrt(ms + RMS_EPS) * g_ref[...]).astype(BF16)
        o_ref[...] = x

    xn = xn_ref[...]
    gate = jnp.dot(xn, wg_ref[...], preferred_element_type=F32)
    up = jnp.dot(xn, wu_ref[...], preferred_element_type=F32)
    act = (gate * _sigmoid(gate) * up).astype(BF16)
    o_ref[...] += jnp.dot(act, wd_ref[...], preferred_element_type=F32)

    if final_norm:
        @pl.when(j == pl.num_programs(1) - 1)
        def _():
            h = o_ref[...]
            ms = jnp.mean(h * h, axis=-1, keepdims=True)
            o_ref[...] = h * lax.rsqrt(ms + RMS_EPS) * gf_ref[...]


def ffn_residual(x, g, wg, wu, wd, g_final, *, tm, tf, final_norm):
    M, D = x.shape
    F = wg.shape[1]
    return pl.pallas_call(
        functools.partial(_ffn_kernel, final_norm=final_norm),
        grid=(M // tm, F // tf),
        in_specs=[
            pl.BlockSpec((tm, D), lambda i, j: (i, 0)),
            pl.BlockSpec((1, D), lambda i, j: (0, 0)),
            pl.BlockSpec((D, tf), lambda i, j: (0, j)),
            pl.BlockSpec((D, tf), lambda i, j: (0, j)),
            pl.BlockSpec((tf, D), lambda i, j: (j, 0)),
            pl.BlockSpec((1, D), lambda i, j: (0, 0)),
        ],
        out_specs=pl.BlockSpec((tm, D), lambda i, j: (i, 0)),
        out_shape=jax.ShapeDtypeStruct((M, D), F32),
        scratch_shapes=[pltpu.VMEM((tm, D), BF16)],
        compiler_params=pltpu.CompilerParams(
            dimension_semantics=("parallel", "arbitrary"), vmem_limit_bytes=VMEM_LIMIT),
    )(x, g.reshape(1, D), wg, wu, wd, g_final.reshape(1, D))


def _gla_kernel(q_ref, k_ref, v_ref, og_ref, gl_ref, w2_ref, b_ref, gn_ref, o_ref, st_ref,
                *, block, chunk):
    @pl.when(pl.program_id(2) == 0)
    def _():
        st_ref[...] = jnp.zeros_like(st_ref)

    logit = _mm_split(gl_ref[0], w2_ref[...], a_parts=2, b_parts=2) + b_ref[...]
    log_a = -_softplus(-logit) * (1.0 / GLA_GATE_TAU)
    tri = _iota2((chunk, chunk), 0) >= _iota2((chunk, chunk), 1)
    tri_b = tri.astype(BF16)
    scale = GLA_DK ** -0.5
    gn = gn_ref[...]
    n_chunks = block // chunk

    pre = []
    for c in range(n_chunks):
        sl = pl.ds(c * chunk, chunk)
        la = _split(log_a[c * chunk:(c + 1) * chunk], 3)
        b = _dot(tri_b, la[0]) + _dot(tri_b, la[1]) + _dot(tri_b, la[2])
        b_last = b[chunk - 1:chunk]
        q = q_ref[0, sl, :] * scale
        k = k_ref[0, sl, :]
        v = v_ref[0, sl, :]
        q_dec = (q * jnp.exp(b)).astype(BF16)
        k_inv = k * jnp.exp(-b)
        k_dec = (k * jnp.exp(b_last - b)).astype(BF16)
        att = jnp.where(tri, _mm_nt(q_dec, k_inv), 0.0)
        pre.append(dict(q_dec=q_dec, k_dec=k_dec, v_t=v.T.astype(BF16), o_intra=_mm(att, v),
                        decay=jnp.exp(b_last)))

    st = st_ref[...]
    outs = []
    for p in pre:
        outs.append(p["o_intra"] + _mm_nt(p["q_dec"], st))
        st = st * p["decay"] + _dot(p["v_t"], p["k_dec"])
    st_ref[...] = st

    for c, o in enumerate(outs):
        sl = pl.ds(c * chunk, chunk)
        ms = jnp.mean(o * o, axis=-1, keepdims=True)
        og = og_ref[0, sl, :]
        o_ref[0, sl, :] = (o * lax.rsqrt(ms + RMS_EPS) * gn * (og * _sigmoid(og))).astype(o_ref.dtype)


def gla_mixer(p3, gate_w2p, gate_b, gla_norm, *, block):
    B, S, _ = p3.shape
    kq, kk_, kv, kog = (COL_GLA_Q // GLA_DK, COL_GLA_K // GLA_DK, COL_GLA_V // GLA_DV,
                        COL_GLA_OG // GLA_DV)
    klow = COL_LOW_GATE // LANES
    return pl.pallas_call(
        functools.partial(_gla_kernel, block=block, chunk=GLA_CHUNK),
        grid=(B, GLA_HEADS, S // block),
        in_specs=[
            pl.BlockSpec((1, block, GLA_DK), lambda b, h, s: (b, s, kq + h)),
            pl.BlockSpec((1, block, GLA_DK), lambda b, h, s: (b, s, kk_ + h)),
            pl.BlockSpec((1, block, GLA_DV), lambda b, h, s: (b, s, kv + h)),
            pl.BlockSpec((1, block, GLA_DV), lambda b, h, s: (b, s, kog + h)),
            pl.BlockSpec((1, block, LANES), lambda b, h, s: (b, s, klow)),
            pl.BlockSpec((LANES, GLA_DK), lambda b, h, s: (0, h)),
            pl.BlockSpec((1, GLA_DK), lambda b, h, s: (0, h)),
            pl.BlockSpec((1, GLA_DV), lambda b, h, s: (0, 0)),
        ],
        out_specs=pl.BlockSpec((1, block, GLA_DV), lambda b, h, s: (b, s, h)),
        out_shape=jax.ShapeDtypeStruct((B, S, GLA_WIDTH), BF16),
        scratch_shapes=[pltpu.VMEM((GLA_DV, GLA_DK), F32)],
        compiler_params=pltpu.CompilerParams(
            dimension_semantics=("parallel", "parallel", "arbitrary"),
            vmem_limit_bytes=VMEM_LIMIT),
    )(p3, p3, p3, p3, p3, gate_w2p, gate_b.reshape(1, -1), gla_norm.reshape(1, -1))


def _unit_lower_inverses(mats, rowi, coli):
    def same_block(s):
        sh = s.bit_length() - 1
        return (rowi >> sh) == (coli >> sh)

    eye = (rowi == coli).astype(F32)
    a8 = [jnp.where(same_block(8), a, 0.0).astype(BF16) for a in mats]
    inv = [eye + a.astype(F32) for a in a8]
    p = [_dot(a, a) for a in a8]
    inv = [x + _mm(x, y) for x, y in zip(inv, p)]
    p = [_mm(y, y) for y in p]
    inv = [x + _mm(x, y) for x, y in zip(inv, p)]
    s = 8
    while s < RWKV_CHUNK:
        off = same_block(2 * s) & jnp.logical_not(same_block(s))
        e = [jnp.where(off, a, 0.0).astype(BF16) for a in mats]
        inv_b = [x.astype(BF16) for x in inv]
        t = [_dot(x, y).astype(BF16) for x, y in zip(inv_b, e)]
        inv = [x + _dot(y, z) for x, y, z in zip(inv, t, inv_b)]
        s *= 2
    return inv


def _rwkv_kernel(r_ref, k_ref, v_ref, wa_ref, g_ref, vec_ref, muwa_ref, mug_ref, w2_ref, a2_ref,
                 g2_ref, o_ref, st_ref, prev_ref, prevg_ref, *, block, chunk):
    T = chunk
    first = pl.program_id(2) == 0

    @pl.when(first)
    def _():
        st_ref[...] = jnp.zeros_like(st_ref)
        prev_ref[...] = jnp.zeros_like(prev_ref)
        prevg_ref[...] = jnp.zeros_like(prevg_ref)

    row0 = _iota2((block, 1), 0) == 0

    def lerp(x, prev_row, mu):
        shifted = jnp.where(row0, prev_row, pltpu.roll(x, 1, axis=0))
        return x + (shifted - x) * mu

    vec = vec_ref[...]
    w0, a0, k_k, k_a, r_k, ln_w, ln_b = (vec[i:i + 1] for i in range(7))
    mu_r, mu_k, mu_v = (vec[i:i + 1] for i in range(7, 10))

    r_raw, k_raw, v_raw, wa_raw, g_raw = r_ref[0], k_ref[0], v_ref[0], wa_ref[0], g_ref[0]
    prev = prev_ref[...]
    r_all = lerp(r_raw, prev[0:1], mu_r)
    k_all = lerp(k_raw, prev[1:2], mu_k)
    v_all = lerp(v_raw, prev[2:3], mu_v)
    wa = lerp(wa_raw, prev[3:4], muwa_ref[...])
    g_low = lerp(g_raw, prevg_ref[0:1], mug_ref[...])
    prev_ref[0:1] = r_raw[block - 1:block]
    prev_ref[1:2] = k_raw[block - 1:block]
    prev_ref[2:3] = v_raw[block - 1:block]
    prev_ref[3:4] = wa_raw[block - 1:block]
    prevg_ref[0:1] = g_raw[block - 1:block]

    z = w0 + _mm_split(jnp.tanh(wa), w2_ref[...], a_parts=2, b_parts=2)
    logw_all = -jnp.exp(-_softplus(-z) - 0.5)
    lr_all = _sigmoid(a0 + _mm_split(wa, a2_ref[...], a_parts=2, b_parts=2))
    gate_all = _mm_split(_sigmoid(g_low), g2_ref[...], a_parts=2, b_parts=2)

    lane = _iota2((1, RWKV_PAIR), 1)
    m0 = (lane < RWKV_HEAD).astype(F32)
    m1 = 1.0 - m0
    rowi = _iota2((2 * T, 2 * T), 0)
    coli = _iota2((2 * T, 2 * T), 1)
    strict = rowi > coli
    incl = rowi >= coli
    head_ones = ((rowi < RWKV_HEAD) == (coli < RWKV_HEAD)).astype(BF16)
    tri_t = (_iota2((T, T), 0) >= _iota2((T, T), 1)).astype(BF16)
    ones_t = jnp.ones((T, RWKV_PAIR), BF16)

    def hsum(x):
        return _mm_split(x, head_ones, a_parts=2)

    def stack(x):
        return jnp.concatenate([x * m0, x * m1], axis=0)

    n_chunks = block // T

    pre = []
    for c in range(n_chunks):
        lo, hi = c * T, (c + 1) * T
        r, k, v = r_all[lo:hi], k_all[lo:hi], v_all[lo:hi]
        lw, lr = logw_all[lo:hi], lr_all[lo:hi]

        kkp = k * k_k
        kk = kkp / jnp.maximum(jnp.sqrt(hsum(kkp * kkp)), 1e-12)
        k2 = k * (1.0 + (lr - 1.0) * k_a)
        b_vec = kk * lr

        lw_parts = _split(lw, 3)
        cum = _dot(tri_t, lw_parts[0]) + _dot(tri_t, lw_parts[1]) + _dot(tri_t, lw_parts[2])
        cum_last = cum[T - 1:T]
        decay = jnp.exp(_dot_tn(lw_parts[0], ones_t) + _dot_tn(lw_parts[1], ones_t)
                        + _dot_tn(lw_parts[2], ones_t))
        w_inv = jnp.exp(-cum)
        w_tail = jnp.exp(cum_last - cum)
        a_s = stack(-kk * jnp.exp(cum - lw)).astype(BF16)
        r_s = stack(r * jnp.exp(cum)).astype(BF16)
        b_s = stack(b_vec * w_inv)
        k_s = stack(k2 * w_inv)
        v_s = stack(v).astype(BF16)
        bk_tail = jnp.concatenate([stack(b_vec * w_tail), stack(k2 * w_tail)],
                                  axis=0).astype(BF16)
        prod = _mm_nt(jnp.concatenate([a_s, r_s], axis=0),
                      jnp.concatenate([b_s, k_s], axis=0))
        pre.append(dict(
            a_s=a_s, r_s=r_s, v_s=v_s, bk_tail=bk_tail, decay=decay,
            a_ab=jnp.where(strict, prod[:2 * T, :2 * T], 0.0),
            a_ak=jnp.where(strict, prod[:2 * T, 2 * T:], 0.0).astype(BF16),
            a_rb=jnp.where(incl, prod[2 * T:, :2 * T], 0.0).astype(BF16),
            a_rk=jnp.where(incl, prod[2 * T:, 2 * T:], 0.0).astype(BF16),
            bonus=hsum(r * k2 * r_k) * v))

    invs = [x.astype(BF16) for x in _unit_lower_inverses([p["a_ab"] for p in pre], rowi, coli)]
    akv = [_dot(p["a_ak"], p["v_s"]).astype(BF16) for p in pre]
    inv_a = [_dot(x, p["a_s"]).astype(BF16) for x, p in zip(invs, pre)]
    u_free = [_dot(x, y) for x, y in zip(invs, akv)]
    y_free = [_dot(p["a_rk"], p["v_s"]) for p in pre]

    st = st_ref[...]
    ys = []
    for c, p in enumerate(pre):
        st_b = st.astype(BF16)
        u = (_dot(inv_a[c], st_b) + u_free[c]).astype(BF16)
        ys.append(_dot(p["r_s"], st_b) + _dot(p["a_rb"], u) + y_free[c])
        uv = jnp.concatenate([u, p["v_s"]], axis=0)
        st = st * p["decay"] + _dot_tn(p["bk_tail"], uv)
    st_ref[...] = st

    for c, p in enumerate(pre):
        lo, hi = c * T, (c + 1) * T
        y = ys[c][:T] + ys[c][T:]
        mu = hsum(y) * (1.0 / RWKV_HEAD)
        d = y - mu
        var = hsum(d * d) * (1.0 / RWKV_HEAD)
        yn = d * lax.rsqrt(var + RWKV_LN_EPS) * ln_w + ln_b
        o_ref[0, pl.ds(lo, T), :] = ((yn + p["bonus"]) * gate_all[lo:hi]).astype(o_ref.dtype)


def rwkv_mixer(p3, vecs, mu_wa, mu_g, w2p, a2p, g2p, *, block):
    B, S, _ = p3.shape
    kr, kk_, kv = COL_RWKV_R // RWKV_PAIR, COL_RWKV_K // RWKV_PAIR, COL_RWKV_V // RWKV_PAIR
    kwa = COL_LOW_WA // LANES
    kg = COL_LOW_G // (2 * LANES)
    return pl.pallas_call(
        functools.partial(_rwkv_kernel, block=block, chunk=RWKV_CHUNK),
        grid=(B, RWKV_PAIRS, S // block),
        in_specs=[
            pl.BlockSpec((1, block, RWKV_PAIR), lambda b, j, s: (b, s, kr + j)),
            pl.BlockSpec((1, block, RWKV_PAIR), lambda b, j, s: (b, s, kk_ + j)),
            pl.BlockSpec((1, block, RWKV_PAIR), lambda b, j, s: (b, s, kv + j)),
            pl.BlockSpec((1, block, LANES), lambda b, j, s: (b, s, kwa)),
            pl.BlockSpec((1, block, 2 * LANES), lambda b, j, s: (b, s, kg)),
            pl.BlockSpec((16, RWKV_PAIR), lambda b, j, s: (0, j)),
            pl.BlockSpec((1, LANES), lambda b, j, s: (0, 0)),
            pl.BlockSpec((1, 2 * LANES), lambda b, j, s: (0, 0)),
            pl.BlockSpec((LANES, RWKV_PAIR), lambda b, j, s: (0, j)),
            pl.BlockSpec((LANES, RWKV_PAIR), lambda b, j, s: (0, j)),
            pl.BlockSpec((2 * LANES, RWKV_PAIR), lambda b, j, s: (0, j)),
        ],
        out_specs=pl.BlockSpec((1, block, RWKV_PAIR), lambda b, j, s: (b, s, j)),
        out_shape=jax.ShapeDtypeStruct((B, S, RWKV_WIDTH), BF16),
        scratch_shapes=[
            pltpu.VMEM((RWKV_PAIR, RWKV_PAIR), F32),
            pltpu.VMEM((8, LANES), F32),
            pltpu.VMEM((8, 2 * LANES), F32),
        ],
        compiler_params=pltpu.CompilerParams(
            dimension_semantics=("parallel", "parallel", "arbitrary"),
            vmem_limit_bytes=VMEM_LIMIT),
    )(p3, p3, p3, p3, p3, vecs, mu_wa, mu_g, w2p, a2p, g2p)


def _moba_kernel(q_ref, k_ref, v_ref, slope_ref, o_ref, kb_ref, vt_ref, kmean_ref,
                 m_ref, l_ref, acc_ref, s_ref, p_ref, *, nb):
    BS = MOBA_BLOCK
    i = pl.program_id(2)

    @pl.when(i == 0)
    def _():
        kmean_ref[...] = jnp.zeros_like(kmean_ref)
        for j in range(nb):
            half = slice((j % 2) * BS, (j % 2 + 1) * BS)
            kj = k_ref[0, pl.ds(j * BS, BS), :]
            kmean_ref[j:j + 1, :] = jnp.mean(kj, axis=0, keepdims=True)
            kb_ref[j // 2, half, :] = kj.astype(BF16)
            vt_ref[j // 2, :, half] = v_ref[0, pl.ds(j * BS, BS), :].T.astype(BF16)

    q_t = q_ref[0].T
    q_tb = q_t.astype(BF16)
    scale = MOBA_HD ** -0.5
    slope = slope_ref[0]
    rel = (_iota2((BS, BS), 1) - _iota2((BS, BS), 0)).astype(F32)

    nbp = kmean_ref.shape[0]
    blk = _iota2((nbp, BS), 0)
    gate = _mm_split(kmean_ref[...], q_t, a_parts=3, b_parts=3)
    gate = jnp.where(blk < i, gate, NEG_INF)
    sel = []
    for r in range(MOBA_TOPK):
        mx = jnp.max(gate, axis=0, keepdims=True)
        idx = jnp.min(jnp.where(gate == mx, blk, nbp), axis=0, keepdims=True)
        sel.append(jnp.where(r < i, idx, -1))
        gate = jnp.where(blk == idx, -jnp.inf, gate)

    srel_a = slope * rel
    srel_b = slope * (rel - float(BS))

    own_pair = i // 2
    own_half = pl.multiple_of((i % 2) * BS, BS)
    other_half = pl.multiple_of((1 - i % 2) * BS, BS)
    s_own = _dot(kb_ref[own_pair, pl.ds(own_half, BS), :], q_tb) * scale - srel_a
    s_own = jnp.where(rel >= 0, s_own, NEG_INF)
    m0 = jnp.max(s_own, axis=0, keepdims=True)
    p0 = jnp.exp(s_own - m0)
    m_ref[...] = m0
    l_ref[...] = jnp.sum(p0, axis=0, keepdims=True)
    p_ref[pl.ds(own_half, BS), :] = p0.astype(BF16)
    p_ref[pl.ds(other_half, BS), :] = jnp.zeros((BS, BS), BF16)
    acc_ref[...] = jnp.zeros_like(acc_ref)
    s_ref[0] = _dot(kb_ref[0], q_tb)

    def past_pair(t, prev):
        s_cur = s_ref[t % 2]
        pv = _dot(vt_ref[prev], p_ref[...])
        s_ref[(t + 1) % 2] = _dot(kb_ref[jnp.minimum(t + 1, nb // 2 - 1)], q_tb)
        ja, jb = 2 * t, 2 * t + 1
        picked_a = (sel[0] == ja) | (sel[1] == ja) | (sel[2] == ja)
        picked_b = (sel[0] == jb) | (sel[1] == jb) | (sel[2] == jb)
        s_a = jnp.where(picked_a, s_cur[:BS] * scale - srel_a, NEG_INF)
        s_b = jnp.where(picked_b, s_cur[BS:] * scale - srel_b, NEG_INF)
        off = slope * ((i - ja) * BS).astype(F32)
        m_old = m_ref[...]
        mx = jnp.maximum(jnp.max(s_a, axis=0, keepdims=True), jnp.max(s_b, axis=0, keepdims=True))
        m_new = jnp.maximum(m_old, mx - off)
        alpha = jnp.exp(m_old - m_new)
        shift = m_new + off
        p_a = jnp.exp(s_a - shift)
        p_b = jnp.exp(s_b - shift)
        l_ref[...] = (alpha * l_ref[...] + jnp.sum(p_a, axis=0, keepdims=True)
                      + jnp.sum(p_b, axis=0, keepdims=True))
        acc_ref[...] = alpha * (acc_ref[...] + pv)
        p_ref[:BS, :] = p_a.astype(BF16)
        p_ref[BS:, :] = p_b.astype(BF16)
        m_ref[...] = m_new
        return t

    last = lax.fori_loop(0, (i + 1) // 2, past_pair, own_pair)
    acc = acc_ref[...] + _dot(vt_ref[last], p_ref[...])
    o_ref[0] = (acc / l_ref[...]).T.astype(o_ref.dtype)


def moba_attention(qkv_h, B, S):
    H, D, BS = MOBA_HEADS, MOBA_HD, MOBA_BLOCK
    nb = S // BS
    assert nb % 2 == 0, "key blocks are stored in pairs"
    nbp = -(-nb // 8) * 8
    slopes = jnp.exp2(-8.0 * jnp.arange(1, H + 1, dtype=F32) / H)
    slopes = jnp.broadcast_to(slopes[:, None, None], (H, 1, BS))
    return pl.pallas_call(
        functools.partial(_moba_kernel, nb=nb),
        grid=(B, H, nb),
        in_specs=[
            pl.BlockSpec((1, BS, D), lambda b, h, i: (h, b * nb + i, 0)),
            pl.BlockSpec((1, S, D), lambda b, h, i: (H + h, b, 0)),
            pl.BlockSpec((1, S, D), lambda b, h, i: (2 * H + h, b, 0)),
            pl.BlockSpec((1, 1, BS), lambda b, h, i: (h, 0, 0)),
        ],
        out_specs=pl.BlockSpec((1, BS, D), lambda b, h, i: (b, i, h)),
        out_shape=jax.ShapeDtypeStruct((B, S, H * D), BF16),
        scratch_shapes=[
            pltpu.VMEM((nb // 2, 2 * BS, D), BF16),
            pltpu.VMEM((nb // 2, D, 2 * BS), BF16),
            pltpu.VMEM((nbp, D), F32),
            pltpu.VMEM((1, BS), F32),
            pltpu.VMEM((1, BS), F32),
            pltpu.VMEM((D, BS), F32),
            pltpu.VMEM((2, 2 * BS, BS), F32),
            pltpu.VMEM((2 * BS, BS), BF16),
        ],
        compiler_params=pltpu.CompilerParams(
            dimension_semantics=("parallel", "parallel", "arbitrary"),
            vmem_limit_bytes=VMEM_LIMIT),
    )(qkv_h, qkv_h, qkv_h, slopes)


def _pad_cols(w, n):
    return jnp.pad(w, ((0, 0), (0, n - w.shape[1])))


def _pad_rows(w, before, total):
    return jnp.pad(w, ((before, total - before - w.shape[0]), (0, 0)))


def _mix_in_layout(w):
    gk, gw = GLA_KEY_WIDTH, GLA_WIDTH
    g_end = 2 * gk + 2 * gw
    gla_main = w[:, :g_end]
    gla_gate = w[:, g_end:g_end + GLA_GATE_RANK]
    r0 = g_end + GLA_GATE_RANK
    rw_main = w[:, r0:r0 + 3 * RWKV_WIDTH]
    l0 = r0 + 3 * RWKV_WIDTH
    wa = w[:, l0:l0 + RWKV_DECAY_LORA + RWKV_AAA_LORA]
    g0 = l0 + RWKV_DECAY_LORA + RWKV_AAA_LORA
    gl = w[:, g0:g0 + RWKV_GATE_LORA]
    return jnp.concatenate(
        [gla_main, rw_main, _pad_cols(gla_gate, LANES), wa, _pad_cols(gl, 2 * LANES)], axis=1)


def mixer_layer0(hn_proj, gla_gate_w2, gla_gate_b, gla_norm, rwkv_mu, rwkv_w0, rwkv_w2, rwkv_a0,
                 rwkv_a2, rwkv_g2, rwkv_k_k, rwkv_k_a, rwkv_r_k, rwkv_ln_w, rwkv_ln_b,
                 *, gla_block, rwkv_block):
    W = RWKV_WIDTH
    o_gla = gla_mixer(hn_proj, _pad_rows(gla_gate_w2, 0, LANES), gla_gate_b, gla_norm,
                      block=gla_block)
    mu_r, mu_k, mu_v = rwkv_mu[:W], rwkv_mu[W:2 * W], rwkv_mu[2 * W:3 * W]
    mu_low = rwkv_mu[3 * W:]
    n_wa = RWKV_DECAY_LORA + RWKV_AAA_LORA
    vecs = jnp.stack([rwkv_w0, rwkv_a0, rwkv_k_k, rwkv_k_a, rwkv_r_k.reshape(-1), rwkv_ln_w,
                      rwkv_ln_b, mu_r, mu_k, mu_v] + [jnp.zeros((W,), F32)] * 6)
    mu_wa = mu_low[:n_wa].reshape(1, -1)
    mu_g = _pad_cols(mu_low[n_wa:].reshape(1, -1), 2 * LANES)
    w2p = _pad_rows(rwkv_w2, 0, LANES)
    a2p = _pad_rows(rwkv_a2, RWKV_DECAY_LORA, LANES)
    g2p = _pad_rows(rwkv_g2, 0, 2 * LANES)
    o_rwkv = rwkv_mixer(hn_proj, vecs, mu_wa, mu_g, w2p, a2p, g2p, block=rwkv_block)
    return o_gla, o_rwkv


def kernel(x, norm_mix, norm_ffn, norm_final, mix_in_w, gla_gate_w2, gla_gate_b, gla_norm, rwkv_mu, rwkv_w0, rwkv_w2, rwkv_a0, rwkv_a2, rwkv_g2, rwkv_k_k, rwkv_k_a, rwkv_r_k, rwkv_ln_w, rwkv_ln_b, mix_out_w, attn_qkv_w, attn_out_w, ffn_gate_w, ffn_up_w, ffn_down_w):
    B, S, D = x.shape
    M = B * S
    tm, tn, tf = 512, 512, 512
    tm_in = min(1024, M)
    h = x.reshape(M, D)

    w_in = _mix_in_layout(mix_in_w[0]).astype(BF16)
    p = norm_matmul(h, norm_mix[0], w_in, tm=tm_in, tn=tn).reshape(B, S, MIX_COLS)
    o_gla, o_rwkv = mixer_layer0(
        p, gla_gate_w2[0], gla_gate_b[0], gla_norm[0], rwkv_mu[0], rwkv_w0[0], rwkv_w2[0],
        rwkv_a0[0], rwkv_a2[0], rwkv_g2[0], rwkv_k_k[0], rwkv_k_a[0], rwkv_r_k[0], rwkv_ln_w[0],
        rwkv_ln_b[0], gla_block=512, rwkv_block=512)
    w_out = mix_out_w[0].astype(BF16)
    h = proj_residual(h, [(o_gla.reshape(M, GLA_WIDTH), w_out[:GLA_WIDTH]),
                          (o_rwkv.reshape(M, RWKV_WIDTH), w_out[GLA_WIDTH:])], tm=tm, tn=D)
    h = ffn_residual(h, norm_ffn[0], ffn_gate_w[0].astype(BF16), ffn_up_w[0].astype(BF16),
                     ffn_down_w[0].astype(BF16), norm_final, tm=tm, tf=tf, final_norm=False)

    qkv_h = norm_matmul(h, norm_mix[1], attn_qkv_w[0].astype(BF16), tm=tm_in, tn=tn,
                        group=MOBA_HD)
    o_attn = moba_attention(qkv_h, B, S)
    h = proj_residual(h, [(o_attn.reshape(M, D), attn_out_w[0].astype(BF16))], tm=tm, tn=D)
    h = ffn_residual(h, norm_ffn[1], ffn_gate_w[1].astype(BF16), ffn_up_w[1].astype(BF16),
                     ffn_down_w[1].astype(BF16), norm_final, tm=tm, tf=tf, final_norm=True)
    return h.reshape(B, S, D)
```

```python
import functools

import jax
import jax.numpy as jnp
from jax import lax
from jax.experimental import pallas as pl
from jax.experimental.pallas import tpu as pltpu

F32 = jnp.float32
BF16 = jnp.bfloat16
HI = lax.Precision.HIGHEST

D_MODEL = 2048
RMS_EPS = 1e-6

GLA_HEADS = 4
GLA_DK = 128
GLA_DV = 256
GLA_KEY_WIDTH = GLA_HEADS * GLA_DK
GLA_WIDTH = GLA_HEADS * GLA_DV
GLA_GATE_RANK = 16
GLA_GATE_TAU = 16.0
GLA_CHUNK = 64

RWKV_WIDTH = 1024
RWKV_HEAD = 64
RWKV_PAIR = 2 * RWKV_HEAD
RWKV_PAIRS = RWKV_WIDTH // RWKV_PAIR
RWKV_DECAY_LORA = 64
RWKV_AAA_LORA = 64
RWKV_GATE_LORA = 160
RWKV_LN_EPS = RWKV_HEAD * 1e-5
RWKV_CHUNK = 64

MOBA_HEADS = 16
MOBA_HD = 128
MOBA_BLOCK = 256
MOBA_TOPK = 3
NEG_INF = -1e30

FFN_HIDDEN = 5632

LANES = 128

COL_GLA_Q = 0
COL_GLA_K = 512
COL_GLA_V = 1024
COL_GLA_OG = 2048
COL_RWKV_R = 3072
COL_RWKV_K = 4096
COL_RWKV_V = 5120
COL_LOW_GATE = 6144
COL_LOW_WA = 6272
COL_LOW_G = 6400
MIX_COLS = 6656

VMEM_LIMIT = 56 * 1024 * 1024


def _mm(a, b):
    return jnp.dot(a.astype(BF16), b.astype(BF16), preferred_element_type=F32)


def _mm_nt(a, b):
    return lax.dot_general(a.astype(BF16), b.astype(BF16), (((1,), (1,)), ((), ())),
                           preferred_element_type=F32)


def _mm_hi(a, b):
    return jnp.dot(a, b, preferred_element_type=F32, precision=HI)


def _mm_nt_hi(a, b):
    return lax.dot_general(a, b, (((1,), (1,)), ((), ())), preferred_element_type=F32,
                           precision=HI)


def _dot(a, b):
    return jnp.dot(a, b, preferred_element_type=F32)


def _dot_tn(a, b):
    return lax.dot_general(a, b, (((0,), (0,)), ((), ())), preferred_element_type=F32)


def _split(x, parts):
    out = []
    for _ in range(parts - 1):
        hi = x.astype(BF16)
        out.append(hi)
        x = x - hi.astype(F32)
    out.append(x.astype(BF16))
    return out


def _mm_split(a, b, *, a_parts=1, b_parts=1):
    a_p = _split(a, a_parts) if a_parts > 1 else [a.astype(BF16)]
    b_p = _split(b, b_parts) if b_parts > 1 else [b.astype(BF16)]
    acc = None
    for i, ai in enumerate(a_p):
        for j, bj in enumerate(b_p):
            if i + j < max(a_parts, b_parts):
                t = _dot(ai, bj)
                acc = t if acc is None else acc + t
    return acc


def _sigmoid(x):
    return 1.0 / (1.0 + jnp.exp(-x))


def _softplus(x):
    return jnp.maximum(x, 0.0) + jnp.log1p(jnp.exp(-jnp.abs(x)))


def _iota2(shape, axis):
    return lax.broadcasted_iota(jnp.int32, shape, axis)


def _norm_matmul_kernel(x_ref, g_ref, w_ref, o_ref, xn_ref, *, group):
    @pl.when(pl.program_id(1) == 0)
    def _():
        x = x_ref[...]
        ms = jnp.mean(x * x, axis=-1, keepdims=True)
        xn_ref[...] = (x * lax.rsqrt(ms + RMS_EPS) * g_ref[...]).astype(BF16)

    res = jnp.dot(xn_ref[...], w_ref[...], preferred_element_type=F32).astype(o_ref.dtype)
    if group is None:
        o_ref[...] = res
    else:
        for c in range(res.shape[1] // group):
            o_ref[c] = res[:, c * group:(c + 1) * group]


def norm_matmul(x, g, w, *, tm, tn, out_dtype=F32, group=None):
    M, D = x.shape
    N = w.shape[1]
    if group is None:
        out_specs = pl.BlockSpec((tm, tn), lambda i, j: (i, j))
        out_shape = jax.ShapeDtypeStruct((M, N), out_dtype)
    else:
        out_specs = pl.BlockSpec((tn // group, tm, group), lambda i, j: (j, i, 0))
        out_shape = jax.ShapeDtypeStruct((N // group, M, group), out_dtype)
    return pl.pallas_call(
        functools.partial(_norm_matmul_kernel, group=group),
        grid=(M // tm, N // tn),
        in_specs=[
            pl.BlockSpec((tm, D), lambda i, j: (i, 0)),
            pl.BlockSpec((1, D), lambda i, j: (0, 0)),
            pl.BlockSpec((D, tn), lambda i, j: (0, j)),
        ],
        out_specs=out_specs,
        out_shape=out_shape,
        scratch_shapes=[pltpu.VMEM((tm, D), BF16)],
        compiler_params=pltpu.CompilerParams(
            dimension_semantics=("parallel", "arbitrary"), vmem_limit_bytes=VMEM_LIMIT),
    )(x, g.reshape(1, D), w)


def _proj_res_kernel(*refs, n_in):
    res_ref = refs[0]
    o_ref = refs[1 + 2 * n_in]
    acc = res_ref[...]
    for i in range(n_in):
        acc = acc + jnp.dot(refs[1 + 2 * i][...], refs[2 + 2 * i][...],
                            preferred_element_type=F32)
    o_ref[...] = acc


def proj_residual(res, pairs, *, tm, tn):
    M, N = res.shape
    in_specs = [pl.BlockSpec((tm, tn), lambda i, j: (i, j))]
    args = [res]
    for a, w in pairs:
        K = a.shape[1]
        in_specs.append(pl.BlockSpec((tm, K), lambda i, j: (i, 0)))
        in_specs.append(pl.BlockSpec((K, tn), lambda i, j: (0, j)))
        args += [a, w]
    return pl.pallas_call(
        functools.partial(_proj_res_kernel, n_in=len(pairs)),
        grid=(M // tm, N // tn),
        in_specs=in_specs,
        out_specs=pl.BlockSpec((tm, tn), lambda i, j: (i, j)),
        out_shape=jax.ShapeDtypeStruct((M, N), F32),
        compiler_params=pltpu.CompilerParams(
            dimension_semantics=("parallel", "arbitrary"), vmem_limit_bytes=VMEM_LIMIT),
    )(*args)


def _ffn_kernel(x_ref, g_ref, wg_ref, wu_ref, wd_ref, gf_ref, o_ref, xn_ref, *, final_norm):
    j = pl.program_id(1)

    @pl.when(j == 0)
    def _():
        x = x_ref[...]
        ms = jnp.mean(x * x, axis=-1, keepdims=True)
        xn_ref[...] = (x * lax.rsqrt(ms + RMS_EPS) * g_ref[...]).astype(BF16)
        o_ref[...] = x

    xn = xn_ref[...]
    gate = jnp.dot(xn, wg_ref[...], preferred_element_type=F32)
    up = jnp.dot(xn, wu_ref[...], preferred_element_type=F32)
    act = (gate * _sigmoid(gate) * up).astype(BF16)
    o_ref[...] += jnp.dot(act, wd_ref[...], preferred_element_type=F32)

    if final_norm:
        @pl.when(j == pl.num_programs(1) - 1)
        def _():
            h = o_ref[...]
            ms = jnp.mean(h * h, axis=-1, keepdims=True)
            o_ref[...] = h * lax.rsqrt(ms + RMS_EPS) * gf_ref[...]


def ffn_residual(x, g, wg, wu, wd, g_final, *, tm, tf, final_norm):
    M, D = x.shape
    F = wg.shape[1]
    return pl.pallas_call(
        functools.partial(_ffn_kernel, final_norm=final_norm),
        grid=(M // tm, F // tf),
        in_specs=[
            pl.BlockSpec((tm, D), lambda i, j: (i, 0)),
            pl.BlockSpec((1, D), lambda i, j: (0, 0)),
            pl.BlockSpec((D, tf), lambda i, j: (0, j)),
            pl.BlockSpec((D, tf), lambda i, j: (0, j)),
            pl.BlockSpec((tf, D), lambda i, j: (j, 0)),
            pl.BlockSpec((1, D), lambda i, j: (0, 0)),
        ],
        out_specs=pl.BlockSpec((tm, D), lambda i, j: (i, 0)),
        out_shape=jax.ShapeDtypeStruct((M, D), F32),
        scratch_shapes=[pltpu.VMEM((tm, D), BF16)],
        compiler_params=pltpu.CompilerParams(
            dimension_semantics=("parallel", "arbitrary"), vmem_limit_bytes=VMEM_LIMIT),
    )(x, g.reshape(1, D), wg, wu, wd, g_final.reshape(1, D))


def _gla_kernel(q_ref, k_ref, v_ref, og_ref, gl_ref, w2_ref, b_ref, gn_ref, o_ref, st_ref,
                *, block, chunk):
    @pl.when(pl.program_id(2) == 0)
    def _():
        st_ref[...] = jnp.zeros_like(st_ref)

    logit = _mm_split(gl_ref[0], w2_ref[...], a_parts=2, b_parts=2) + b_ref[...]
    log_a = -_softplus(-logit) * (1.0 / GLA_GATE_TAU)
    tri = _iota2((chunk, chunk), 0) >= _iota2((chunk, chunk), 1)
    tri_b = tri.astype(BF16)
    scale = GLA_DK ** -0.5
    gn = gn_ref[...]
    n_chunks = block // chunk

    pre = []
    for c in range(n_chunks):
        sl = pl.ds(c * chunk, chunk)
        la = _split(log_a[c * chunk:(c + 1) * chunk], 3)
        b = _dot(tri_b, la[0]) + _dot(tri_b, la[1]) + _dot(tri_b, la[2])
        b_last = b[chunk - 1:chunk]
        q = q_ref[0, sl, :] * scale
        k = k_ref[0, sl, :]
        v = v_ref[0, sl, :]
        q_dec = (q * jnp.exp(b)).astype(BF16)
        k_inv = k * jnp.exp(-b)
        k_dec = (k * jnp.exp(b_last - b)).astype(BF16)
        att = jnp.where(tri, _mm_nt(q_dec, k_inv), 0.0)
        pre.append(dict(q_dec=q_dec, k_dec=k_dec, v_t=v.T.astype(BF16), o_intra=_mm(att, v),
                        decay=jnp.exp(b_last)))

    st = st_ref[...]
    outs = []
    for p in pre:
        outs.append(p["o_intra"] + _mm_nt(p["q_dec"], st))
        st = st * p["decay"] + _dot(p["v_t"], p["k_dec"])
    st_ref[...] = st

    for c, o in enumerate(outs):
        sl = pl.ds(c * chunk, chunk)
        ms = jnp.mean(o * o, axis=-1, keepdims=True)
        og = og_ref[0, sl, :]
        o_ref[0, sl, :] = (o * lax.rsqrt(ms + RMS_EPS) * gn * (og * _sigmoid(og))).astype(o_ref.dtype)


def gla_mixer(p3, gate_w2p, gate_b, gla_norm, *, block):
    B, S, _ = p3.shape
    kq, kk_, kv, kog = (COL_GLA_Q // GLA_DK, COL_GLA_K // GLA_DK, COL_GLA_V // GLA_DV,
                        COL_GLA_OG // GLA_DV)
    klow = COL_LOW_GATE // LANES
    return pl.pallas_call(
        functools.partial(_gla_kernel, block=block, chunk=GLA_CHUNK),
        grid=(B, GLA_HEADS, S // block),
        in_specs=[
            pl.BlockSpec((1, block, GLA_DK), lambda b, h, s: (b, s, kq + h)),
            pl.BlockSpec((1, block, GLA_DK), lambda b, h, s: (b, s, kk_ + h)),
            pl.BlockSpec((1, block, GLA_DV), lambda b, h, s: (b, s, kv + h)),
            pl.BlockSpec((1, block, GLA_DV), lambda b, h, s: (b, s, kog + h)),
            pl.BlockSpec((1, block, LANES), lambda b, h, s: (b, s, klow)),
            pl.BlockSpec((LANES, GLA_DK), lambda b, h, s: (0, h)),
            pl.BlockSpec((1, GLA_DK), lambda b, h, s: (0, h)),
            pl.BlockSpec((1, GLA_DV), lambda b, h, s: (0, 0)),
        ],
        out_specs=pl.BlockSpec((1, block, GLA_DV), lambda b, h, s: (b, s, h)),
        out_shape=jax.ShapeDtypeStruct((B, S, GLA_WIDTH), BF16),
        scratch_shapes=[pltpu.VMEM((GLA_DV, GLA_DK), F32)],
        compiler_params=pltpu.CompilerParams(
            dimension_semantics=("parallel", "parallel", "arbitrary"),
            vmem_limit_bytes=VMEM_LIMIT),
    )(p3, p3, p3, p3, p3, gate_w2p, gate_b.reshape(1, -1), gla_norm.reshape(1, -1))


def _unit_lower_inverses(mats, rowi, coli):
    def same_block(s):
        sh = s.bit_length() - 1
        return (rowi >> sh) == (coli >> sh)

    eye = (rowi == coli).astype(F32)
    a8 = [jnp.where(same_block(8), a, 0.0).astype(BF16) for a in mats]
    inv = [eye + a.astype(F32) for a in a8]
    p = [_dot(a, a) for a in a8]
    inv = [x + _mm(x, y) for x, y in zip(inv, p)]
    p = [_mm(y, y) for y in p]
    inv = [x + _mm(x, y) for x, y in zip(inv, p)]
    s = 8
    while s < RWKV_CHUNK:
        off = same_block(2 * s) & jnp.logical_not(same_block(s))
        e = [jnp.where(off, a, 0.0).astype(BF16) for a in mats]
        inv_b = [x.astype(BF16) for x in inv]
        t = [_dot(x, y).astype(BF16) for x, y in zip(inv_b, e)]
        inv = [x + _dot(y, z) for x, y, z in zip(inv, t, inv_b)]
        s *= 2
    return inv


def _rwkv_kernel(r_ref, k_ref, v_ref, wa_ref, g_ref, vec_ref, muwa_ref, mug_ref, w2_ref, a2_ref,
                 g2_ref, o_ref, st_ref, prev_ref, prevg_ref, *, block, chunk):
    T = chunk
    first = pl.program_id(2) == 0

    @pl.when(first)
    def _():
        st_ref[...] = jnp.zeros_like(st_ref)
        prev_ref[...] = jnp.zeros_like(prev_ref)
        prevg_ref[...] = jnp.zeros_like(prevg_ref)

    row0 = _iota2((block, 1), 0) == 0

    def lerp(x, prev_row, mu):
        shifted = jnp.where(row0, prev_row, pltpu.roll(x, 1, axis=0))
        return x + (shifted - x) * mu

    vec = vec_ref[...]
    w0, a0, k_k, k_a, r_k, ln_w, ln_b = (vec[i:i + 1] for i in range(7))
    mu_r, mu_k, mu_v = (vec[i:i + 1] for i in range(7, 10))

    r_raw, k_raw, v_raw, wa_raw, g_raw = r_ref[0], k_ref[0], v_ref[0], wa_ref[0], g_ref[0]
    prev = prev_ref[...]
    r_all = lerp(r_raw, prev[0:1], mu_r)
    k_all = lerp(k_raw, prev[1:2], mu_k)
    v_all = lerp(v_raw, prev[2:3], mu_v)
    wa = lerp(wa_raw, prev[3:4], muwa_ref[...])
    g_low = lerp(g_raw, prevg_ref[0:1], mug_ref[...])
    prev_ref[0:1] = r_raw[block - 1:block]
    prev_ref[1:2] = k_raw[block - 1:block]
    prev_ref[2:3] = v_raw[block - 1:block]
    prev_ref[3:4] = wa_raw[block - 1:block]
    prevg_ref[0:1] = g_raw[block - 1:block]

    z = w0 + _mm_split(jnp.tanh(wa), w2_ref[...], a_parts=2, b_parts=2)
    logw_all = -jnp.exp(-_softplus(-z) - 0.5)
    lr_all = _sigmoid(a0 + _mm_split(wa, a2_ref[...], a_parts=2, b_parts=2))
    gate_all = _mm_split(_sigmoid(g_low), g2_ref[...], a_parts=2, b_parts=2)

    lane = _iota2((1, RWKV_PAIR), 1)
    m0 = (lane < RWKV_HEAD).astype(F32)
    m1 = 1.0 - m0
    rowi = _iota2((2 * T, 2 * T), 0)
    coli = _iota2((2 * T, 2 * T), 1)
    strict = rowi > coli
    incl = rowi >= coli
    head_ones = ((rowi < RWKV_HEAD) == (coli < RWKV_HEAD)).astype(BF16)
    tri_t = (_iota2((T, T), 0) >= _iota2((T, T), 1)).astype(BF16)
    ones_t = jnp.ones((T, RWKV_PAIR), BF16)

    def hsum(x):
        return _mm_split(x, head_ones, a_parts=2)

    def stack(x):
        return jnp.concatenate([x * m0, x * m1], axis=0)

    n_chunks = block // T

    pre = []
    for c in range(n_chunks):
        lo, hi = c * T, (c + 1) * T
        r, k, v = r_all[lo:hi], k_all[lo:hi], v_all[lo:hi]
        lw, lr = logw_all[lo:hi], lr_all[lo:hi]

        kkp = k * k_k
        kk = kkp / jnp.maximum(jnp.sqrt(hsum(kkp * kkp)), 1e-12)
        k2 = k * (1.0 + (lr - 1.0) * k_a)
        b_vec = kk * lr

        lw_parts = _split(lw, 3)
        cum = _dot(tri_t, lw_parts[0]) + _dot(tri_t, lw_parts[1]) + _dot(tri_t, lw_parts[2])
        cum_last = cum[T - 1:T]
        decay = jnp.exp(_dot_tn(lw_parts[0], ones_t) + _dot_tn(lw_parts[1], ones_t)
                        + _dot_tn(lw_parts[2], ones_t))
        w_inv = jnp.exp(-cum)
        w_tail = jnp.exp(cum_last - cum)
        a_s = stack(-kk * jnp.exp(cum - lw)).astype(BF16)
        r_s = stack(r * jnp.exp(cum)).astype(BF16)
        b_s = stack(b_vec * w_inv)
        k_s = stack(k2 * w_inv)
        v_s = stack(v).astype(BF16)
        bk_tail = jnp.concatenate([stack(b_vec * w_tail), stack(k2 * w_tail)],
                                  axis=0).astype(BF16)
        prod = _mm_nt(jnp.concatenate([a_s, r_s], axis=0),
                      jnp.concatenate([b_s, k_s], axis=0))
        pre.append(dict(
            a_s=a_s, r_s=r_s, v_s=v_s, bk_tail=bk_tail, decay=decay,
            a_ab=jnp.where(strict, prod[:2 * T, :2 * T], 0.0),
            a_ak=jnp.where(strict, prod[:2 * T, 2 * T:], 0.0).astype(BF16),
            a_rb=jnp.where(incl, prod[2 * T:, :2 * T], 0.0).astype(BF16),
            a_rk=jnp.where(incl, prod[2 * T:, 2 * T:], 0.0).astype(BF16),
            bonus=hsum(r * k2 * r_k) * v))

    invs = [x.astype(BF16) for x in _unit_lower_inverses([p["a_ab"] for p in pre], rowi, coli)]
    akv = [_dot(p["a_ak"], p["v_s"]).astype(BF16) for p in pre]
    inv_a = [_dot(x, p["a_s"]).astype(BF16) for x, p in zip(invs, pre)]
    u_free = [_dot(x, y) for x, y in zip(invs, akv)]
    y_free = [_dot(p["a_rk"], p["v_s"]) for p in pre]

    st = st_ref[...]
    ys = []
    for c, p in enumerate(pre):
        st_b = st.astype(BF16)
        u = (_dot(inv_a[c], st_b) + u_free[c]).astype(BF16)
        ys.append(_dot(p["r_s"], st_b) + _dot(p["a_rb"], u) + y_free[c])
        uv = jnp.concatenate([u, p["v_s"]], axis=0)
        st = st * p["decay"] + _dot_tn(p["bk_tail"], uv)
    st_ref[...] = st

    for c, p in enumerate(pre):
        lo, hi = c * T, (c + 1) * T
        y = ys[c][:T] + ys[c][T:]
        mu = hsum(y) * (1.0 / RWKV_HEAD)
        d = y - mu
        var = hsum(d * d) * (1.0 / RWKV_HEAD)
        yn = d * lax.rsqrt(var + RWKV_LN_EPS) * ln_w + ln_b
        o_ref[0, pl.ds(lo, T), :] = ((yn + p["bonus"]) * gate_all[lo:hi]).astype(o_ref.dtype)


def rwkv_mixer(p3, vecs, mu_wa, mu_g, w2p, a2p, g2p, *, block):
    B, S, _ = p3.shape
    kr, kk_, kv = COL_RWKV_R // RWKV_PAIR, COL_RWKV_K // RWKV_PAIR, COL_RWKV_V // RWKV_PAIR
    kwa = COL_LOW_WA // LANES
    kg = COL_LOW_G // (2 * LANES)
    return pl.pallas_call(
        functools.partial(_rwkv_kernel, block=block, chunk=RWKV_CHUNK),
        grid=(B, RWKV_PAIRS, S // block),
        in_specs=[
            pl.BlockSpec((1, block, RWKV_PAIR), lambda b, j, s: (b, s, kr + j)),
            pl.BlockSpec((1, block, RWKV_PAIR), lambda b, j, s: (b, s, kk_ + j)),
            pl.BlockSpec((1, block, RWKV_PAIR), lambda b, j, s: (b, s, kv + j)),
            pl.BlockSpec((1, block, LANES), lambda b, j, s: (b, s, kwa)),
            pl.BlockSpec((1, block, 2 * LANES), lambda b, j, s: (b, s, kg)),
            pl.BlockSpec((16, RWKV_PAIR), lambda b, j, s: (0, j)),
            pl.BlockSpec((1, LANES), lambda b, j, s: (0, 0)),
            pl.BlockSpec((1, 2 * LANES), lambda b, j, s: (0, 0)),
            pl.BlockSpec((LANES, RWKV_PAIR), lambda b, j, s: (0, j)),
            pl.BlockSpec((LANES, RWKV_PAIR), lambda b, j, s: (0, j)),
            pl.BlockSpec((2 * LANES, RWKV_PAIR), lambda b, j, s: (0, j)),
        ],
        out_specs=pl.BlockSpec((1, block, RWKV_PAIR), lambda b, j, s: (b, s, j)),
        out_shape=jax.ShapeDtypeStruct((B, S, RWKV_WIDTH), BF16),
        scratch_shapes=[
            pltpu.VMEM((RWKV_PAIR, RWKV_PAIR), F32),
            pltpu.VMEM((8, LANES), F32),
            pltpu.VMEM((8, 2 * LANES), F32),
        ],
        compiler_params=pltpu.CompilerParams(
            dimension_semantics=("parallel", "parallel", "arbitrary"),
            vmem_limit_bytes=VMEM_LIMIT),
    )(p3, p3, p3, p3, p3, vecs, mu_wa, mu_g, w2p, a2p, g2p)


def _moba_kernel(q_ref, k_ref, v_ref, slope_ref, o_ref, kb_ref, vt_ref, kmean_ref,
                 m_ref, l_ref, acc_ref, s_ref, p_ref, *, nb):
    BS, D = MOBA_BLOCK, MOBA_HD
    i = pl.program_id(2)
    log2e = 1.4426950408889634
    c1 = MOBA_HD ** -0.5 * log2e
    slope2 = slope_ref[0] * log2e
    s_parts = _split(slope2, 3)

    @pl.when(i == 0)
    def _():
        kmean_ref[...] = jnp.zeros_like(kmean_ref)
        lane = _iota2((2 * BS, LANES), 1)
        kpos = _iota2((2 * BS, LANES), 0)
        kpos_lo = (kpos & (BS - 1)).astype(F32)
        kpos_hi = (kpos & BS).astype(F32)
        sk = [x[:, :LANES].astype(F32) for x in s_parts]
        feat = jnp.where(lane == 0, sk[0],
                         jnp.where(lane == 1, sk[1],
                                   jnp.where(lane == 2, sk[2],
                                             jnp.where(lane < 6, kpos_lo,
                                                       jnp.where(lane < 9, kpos_hi, 0.0)))))
        feat = feat.astype(BF16)
        for j in range(nb):
            half = slice((j % 2) * BS, (j % 2 + 1) * BS)
            kj = k_ref[0, pl.ds(j * BS, BS), :]
            kmean_ref[j:j + 1, :] = jnp.mean(kj, axis=0, keepdims=True)
            kb_ref[j // 2, half, :D] = kj.astype(BF16)
            vt_ref[j // 2, :, half] = v_ref[0, pl.ds(j * BS, BS), :].T.astype(BF16)
        for t in range(nb // 2):
            kb_ref[t, :, D:] = feat

    q_t = q_ref[0].T

    nbp = kmean_ref.shape[0]
    blk = _iota2((nbp, BS), 0)
    gate = _mm_split(kmean_ref[...], q_t, a_parts=3, b_parts=3)
    gate = jnp.where(blk < i, gate, NEG_INF)
    sel = []
    for r in range(MOBA_TOPK):
        mx = jnp.max(gate, axis=0, keepdims=True)
        idx = jnp.min(jnp.where(gate == mx, blk, nbp), axis=0, keepdims=True)
        sel.append(jnp.where(r < i, idx, -1))
        gate = jnp.where(blk == idx, -jnp.inf, gate)

    qpos = _iota2((LANES, BS), 1).astype(F32)
    arow = _iota2((LANES, BS), 0)
    sq = [x.astype(F32) for x in s_parts]
    aug = jnp.where(arow < 3, -qpos,
                    jnp.where((arow == 3) | (arow == 6), sq[0],
                              jnp.where((arow == 4) | (arow == 7), sq[1],
                                        jnp.where((arow == 5) | (arow == 8), sq[2], 0.0))))
    q_aug = jnp.concatenate([(q_t * c1).astype(BF16), aug.astype(BF16)], axis=0)

    own_pair = i // 2
    own_half = pl.multiple_of((i % 2) * BS, BS)
    other_half = pl.multiple_of((1 - i % 2) * BS, BS)
    causal = _iota2((BS, BS), 1) >= _iota2((BS, BS), 0)
    s_own = _dot(kb_ref[own_pair, pl.ds(own_half, BS), :], q_aug)
    s_own = jnp.where(causal, s_own - slope2 * own_half.astype(F32), NEG_INF)
    m0 = jnp.max(s_own, axis=0, keepdims=True)
    p0 = jnp.exp2(s_own - m0)
    m_ref[...] = m0
    l_ref[...] = jnp.sum(p0, axis=0, keepdims=True)
    p_ref[pl.ds(own_half, BS), :] = p0.astype(BF16)
    p_ref[pl.ds(other_half, BS), :] = jnp.zeros((BS, BS), BF16)
    acc_ref[...] = jnp.zeros_like(acc_ref)
    s_ref[0] = _dot(kb_ref[0], q_aug)

    def past_pair(t, prev):
        s_cur = s_ref[t % 2]
        pv = _dot(vt_ref[prev], p_ref[...])
        s_ref[(t + 1) % 2] = _dot(kb_ref[jnp.minimum(t + 1, nb // 2 - 1)], q_aug)
        ja, jb = 2 * t, 2 * t + 1
        picked_a = (sel[0] == ja) | (sel[1] == ja) | (sel[2] == ja)
        picked_b = (sel[0] == jb) | (sel[1] == jb) | (sel[2] == jb)
        s_a = jnp.where(picked_a, s_cur[:BS], NEG_INF)
        s_b = jnp.where(picked_b, s_cur[BS:], NEG_INF)
        off = slope2 * ((i - ja) * BS).astype(F32)
        m_old = m_ref[...]
        mx = jnp.maximum(jnp.max(s_a, axis=0, keepdims=True), jnp.max(s_b, axis=0, keepdims=True))
        m_new = jnp.maximum(m_old, mx - off)
        alpha = jnp.exp2(m_old - m_new)
        shift = m_new + off
        p_a = jnp.exp2(s_a - shift)
        p_b = jnp.exp2(s_b - shift)
        l_ref[...] = (alpha * l_ref[...] + jnp.sum(p_a, axis=0, keepdims=True)
                      + jnp.sum(p_b, axis=0, keepdims=True))
        acc_ref[...] = alpha * (acc_ref[...] + pv)
        p_ref[:BS, :] = p_a.astype(BF16)
        p_ref[BS:, :] = p_b.astype(BF16)
        m_ref[...] = m_new
        return t

    last = lax.fori_loop(0, (i + 1) // 2, past_pair, own_pair)
    acc = acc_ref[...] + _dot(vt_ref[last], p_ref[...])
    o_ref[0] = (acc / l_ref[...]).T.astype(o_ref.dtype)


def moba_attention(qkv_h, B, S):
    H, D, BS = MOBA_HEADS, MOBA_HD, MOBA_BLOCK
    nb = S // BS
    assert nb % 2 == 0, "key blocks are stored in pairs"
    nbp = -(-nb // 8) * 8
    slopes = jnp.exp2(-8.0 * jnp.arange(1, H + 1, dtype=F32) / H)
    slopes = jnp.broadcast_to(slopes[:, None, None], (H, 1, BS))
    return pl.pallas_call(
        functools.partial(_moba_kernel, nb=nb),
        grid=(B, H, nb),
        in_specs=[
            pl.BlockSpec((1, BS, D), lambda b, h, i: (h, b * nb + i, 0)),
            pl.BlockSpec((1, S, D), lambda b, h, i: (H + h, b, 0)),
            pl.BlockSpec((1, S, D), lambda b, h, i: (2 * H + h, b, 0)),
            pl.BlockSpec((1, 1, BS), lambda b, h, i: (h, 0, 0)),
        ],
        out_specs=pl.BlockSpec((1, BS, D), lambda b, h, i: (b, i, h)),
        out_shape=jax.ShapeDtypeStruct((B, S, H * D), BF16),
        scratch_shapes=[
            pltpu.VMEM((nb // 2, 2 * BS, D + LANES), BF16),
            pltpu.VMEM((nb // 2, D, 2 * BS), BF16),
            pltpu.VMEM((nbp, D), F32),
            pltpu.VMEM((1, BS), F32),
            pltpu.VMEM((1, BS), F32),
            pltpu.VMEM((D, BS), F32),
            pltpu.VMEM((2, 2 * BS, BS), F32),
            pltpu.VMEM((2 * BS, BS), BF16),
        ],
        compiler_params=pltpu.CompilerParams(
            dimension_semantics=("parallel", "parallel", "arbitrary"),
            vmem_limit_bytes=VMEM_LIMIT),
    )(qkv_h, qkv_h, qkv_h, slopes)


def _pad_cols(w, n):
    return jnp.pad(w, ((0, 0), (0, n - w.shape[1])))


def _pad_rows(w, before, total):
    return jnp.pad(w, ((before, total - before - w.shape[0]), (0, 0)))


def _mix_in_layout(w):
    gk, gw = GLA_KEY_WIDTH, GLA_WIDTH
    g_end = 2 * gk + 2 * gw
    gla_main = w[:, :g_end]
    gla_gate = w[:, g_end:g_end + GLA_GATE_RANK]
    r0 = g_end + GLA_GATE_RANK
    rw_main = w[:, r0:r0 + 3 * RWKV_WIDTH]
    l0 = r0 + 3 * RWKV_WIDTH
    wa = w[:, l0:l0 + RWKV_DECAY_LORA + RWKV_AAA_LORA]
    g0 = l0 + RWKV_DECAY_LORA + RWKV_AAA_LORA
    gl = w[:, g0:g0 + RWKV_GATE_LORA]
    return jnp.concatenate(
        [gla_main, rw_main, _pad_cols(gla_gate, LANES), wa, _pad_cols(gl, 2 * LANES)], axis=1)


def mixer_layer0(hn_proj, gla_gate_w2, gla_gate_b, gla_norm, rwkv_mu, rwkv_w0, rwkv_w2, rwkv_a0,
                 rwkv_a2, rwkv_g2, rwkv_k_k, rwkv_k_a, rwkv_r_k, rwkv_ln_w, rwkv_ln_b,
                 *, gla_block, rwkv_block):
    W = RWKV_WIDTH
    o_gla = gla_mixer(hn_proj, _pad_rows(gla_gate_w2, 0, LANES), gla_gate_b, gla_norm,
                      block=gla_block)
    mu_r, mu_k, mu_v = rwkv_mu[:W], rwkv_mu[W:2 * W], rwkv_mu[2 * W:3 * W]
    mu_low = rwkv_mu[3 * W:]
    n_wa = RWKV_DECAY_LORA + RWKV_AAA_LORA
    vecs = jnp.stack([rwkv_w0, rwkv_a0, rwkv_k_k, rwkv_k_a, rwkv_r_k.reshape(-1), rwkv_ln_w,
                      rwkv_ln_b, mu_r, mu_k, mu_v] + [jnp.zeros((W,), F32)] * 6)
    mu_wa = mu_low[:n_wa].reshape(1, -1)
    mu_g = _pad_cols(mu_low[n_wa:].reshape(1, -1), 2 * LANES)
    w2p = _pad_rows(rwkv_w2, 0, LANES)
    a2p = _pad_rows(rwkv_a2, RWKV_DECAY_LORA, LANES)
    g2p = _pad_rows(rwkv_g2, 0, 2 * LANES)
    o_rwkv = rwkv_mixer(hn_proj, vecs, mu_wa, mu_g, w2p, a2p, g2p, block=rwkv_block)
    return o_gla, o_rwkv


def kernel(x, norm_mix, norm_ffn, norm_final, mix_in_w, gla_gate_w2, gla_gate_b, gla_norm, rwkv_mu, rwkv_w0, rwkv_w2, rwkv_a0, rwkv_a2, rwkv_g2, rwkv_k_k, rwkv_k_a, rwkv_r_k, rwkv_ln_w, rwkv_ln_b, mix_out_w, attn_qkv_w, attn_out_w, ffn_gate_w, ffn_up_w, ffn_down_w):
    B, S, D = x.shape
    M = B * S
    tm, tn, tf = 512, 512, 512
    tm_in = min(1024, M)
    h = x.reshape(M, D)

    w_in = _mix_in_layout(mix_in_w[0]).astype(BF16)
    p = norm_matmul(h, norm_mix[0], w_in, tm=tm_in, tn=tn).reshape(B, S, MIX_COLS)
    o_gla, o_rwkv = mixer_layer0(
        p, gla_gate_w2[0], gla_gate_b[0], gla_norm[0], rwkv_mu[0], rwkv_w0[0], rwkv_w2[0],
        rwkv_a0[0], rwkv_a2[0], rwkv_g2[0], rwkv_k_k[0], rwkv_k_a[0], rwkv_r_k[0], rwkv_ln_w[0],
        rwkv_ln_b[0], gla_block=512, rwkv_block=512)
    w_out = mix_out_w[0].astype(BF16)
    h = proj_residual(h, [(o_gla.reshape(M, GLA_WIDTH), w_out[:GLA_WIDTH]),
                          (o_rwkv.reshape(M, RWKV_WIDTH), w_out[GLA_WIDTH:])], tm=tm, tn=D)
    h = ffn_residual(h, norm_ffn[0], ffn_gate_w[0].astype(BF16), ffn_up_w[0].astype(BF16),
                     ffn_down_w[0].astype(BF16), norm_final, tm=tm, tf=tf, final_norm=False)

    qkv_h = norm_matmul(h, norm_mix[1], attn_qkv_w[0].astype(BF16), tm=tm_in, tn=tn,
                        group=MOBA_HD)
    o_attn = moba_attention(qkv_h, B, S)
    h = proj_residual(h, [(o_attn.reshape(M, D), attn_out_w[0].astype(BF16))], tm=tm, tn=D)
    h = ffn_residual(h, norm_ffn[1], ffn_gate_w[1].astype(BF16), ffn_up_w[1].astype(BF16),
                     ffn_down_w[1].astype(BF16), norm_final, tm=tm, tf=tf, final_norm=True)
    return h.reshape(B, S, D)
```

```python
import functools

import jax
import jax.numpy as jnp
from jax import lax
from jax.experimental import pallas as pl
from jax.experimental.pallas import tpu as pltpu

F32 = jnp.float32
BF16 = jnp.bfloat16
HI = lax.Precision.HIGHEST

D_MODEL = 2048
RMS_EPS = 1e-6

GLA_HEADS = 4
GLA_DK = 128
GLA_DV = 256
GLA_KEY_WIDTH = GLA_HEADS * GLA_DK
GLA_WIDTH = GLA_HEADS * GLA_DV
GLA_GATE_RANK = 16
GLA_GATE_TAU = 16.0
GLA_CHUNK = 64

RWKV_WIDTH = 1024
RWKV_HEAD = 64
RWKV_PAIR = 2 * RWKV_HEAD
RWKV_PAIRS = RWKV_WIDTH // RWKV_PAIR
RWKV_DECAY_LORA = 64
RWKV_AAA_LORA = 64
RWKV_GATE_LORA = 160
RWKV_LN_EPS = RWKV_HEAD * 1e-5
RWKV_CHUNK = 64

MOBA_HEADS = 16
MOBA_HD = 128
MOBA_BLOCK = 256
MOBA_TOPK = 3
NEG_INF = -1e30

FFN_HIDDEN = 5632

LANES = 128

COL_GLA_Q = 0
COL_GLA_K = 512
COL_GLA_V = 1024
COL_GLA_OG = 2048
COL_RWKV_R = 3072
COL_RWKV_K = 4096
COL_RWKV_V = 5120
COL_LOW_GATE = 6144
COL_LOW_WA = 6272
COL_LOW_G = 6400
MIX_COLS = 6656

VMEM_LIMIT = 56 * 1024 * 1024


def _mm(a, b):
    return jnp.dot(a.astype(BF16), b.astype(BF16), preferred_element_type=F32)


def _mm_nt(a, b):
    return lax.dot_general(a.astype(BF16), b.astype(BF16), (((1,), (1,)), ((), ())),
                           preferred_element_type=F32)


def _mm_hi(a, b):
    return jnp.dot(a, b, preferred_element_type=F32, precision=HI)


def _mm_nt_hi(a, b):
    return lax.dot_general(a, b, (((1,), (1,)), ((), ())), preferred_element_type=F32,
                           precision=HI)


def _dot(a, b):
    return jnp.dot(a, b, preferred_element_type=F32)


def _dot_tn(a, b):
    return lax.dot_general(a, b, (((0,), (0,)), ((), ())), preferred_element_type=F32)


def _split(x, parts):
    out = []
    for _ in range(parts - 1):
        hi = x.astype(BF16)
        out.append(hi)
        x = x - hi.astype(F32)
    out.append(x.astype(BF16))
    return out


def _mm_split(a, b, *, a_parts=1, b_parts=1):
    a_p = _split(a, a_parts) if a_parts > 1 else [a.astype(BF16)]
    b_p = _split(b, b_parts) if b_parts > 1 else [b.astype(BF16)]
    acc = None
    for i, ai in enumerate(a_p):
        for j, bj in enumerate(b_p):
            if i + j < max(a_parts, b_parts):
                t = _dot(ai, bj)
                acc = t if acc is None else acc + t
    return acc


def _sigmoid(x):
    return 1.0 / (1.0 + jnp.exp(-x))


def _softplus(x):
    return jnp.maximum(x, 0.0) + jnp.log1p(jnp.exp(-jnp.abs(x)))


def _iota2(shape, axis):
    return lax.broadcasted_iota(jnp.int32, shape, axis)


def _norm_matmul_kernel(x_ref, g_ref, w_ref, o_ref, xn_ref, *, group):
    @pl.when(pl.program_id(1) == 0)
    def _():
        x = x_ref[...]
        ms = jnp.mean(x * x, axis=-1, keepdims=True)
        xn_ref[...] = (x * lax.rsqrt(ms + RMS_EPS) * g_ref[...]).astype(BF16)

    res = jnp.dot(xn_ref[...], w_ref[...], preferred_element_type=F32).astype(o_ref.dtype)
    if group is None:
        o_ref[...] = res
    else:
        for c in range(res.shape[1] // group):
            o_ref[c] = res[:, c * group:(c + 1) * group]


def norm_matmul(x, g, w, *, tm, tn, out_dtype=F32, group=None):
    M, D = x.shape
    N = w.shape[1]
    if group is None:
        out_specs = pl.BlockSpec((tm, tn), lambda i, j: (i, j))
        out_shape = jax.ShapeDtypeStruct((M, N), out_dtype)
    else:
        out_specs = pl.BlockSpec((tn // group, tm, group), lambda i, j: (j, i, 0))
        out_shape = jax.ShapeDtypeStruct((N // group, M, group), out_dtype)
    return pl.pallas_call(
        functools.partial(_norm_matmul_kernel, group=group),
        grid=(M // tm, N // tn),
        in_specs=[
            pl.BlockSpec((tm, D), lambda i, j: (i, 0)),
            pl.BlockSpec((1, D), lambda i, j: (0, 0)),
            pl.BlockSpec((D, tn), lambda i, j: (0, j)),
        ],
        out_specs=out_specs,
        out_shape=out_shape,
        scratch_shapes=[pltpu.VMEM((tm, D), BF16)],
        compiler_params=pltpu.CompilerParams(
            dimension_semantics=("parallel", "arbitrary"), vmem_limit_bytes=VMEM_LIMIT),
    )(x, g.reshape(1, D), w)


def _proj_res_kernel(*refs, n_in):
    res_ref = refs[0]
    o_ref = refs[1 + 2 * n_in]
    acc = res_ref[...]
    for i in range(n_in):
        acc = acc + jnp.dot(refs[1 + 2 * i][...], refs[2 + 2 * i][...],
                            preferred_element_type=F32)
    o_ref[...] = acc


def proj_residual(res, pairs, *, tm, tn):
    M, N = res.shape
    in_specs = [pl.BlockSpec((tm, tn), lambda i, j: (i, j))]
    args = [res]
    for a, w in pairs:
        K = a.shape[1]
        in_specs.append(pl.BlockSpec((tm, K), lambda i, j: (i, 0)))
        in_specs.append(pl.BlockSpec((K, tn), lambda i, j: (0, j)))
        args += [a, w]
    return pl.pallas_call(
        functools.partial(_proj_res_kernel, n_in=len(pairs)),
        grid=(M // tm, N // tn),
        in_specs=in_specs,
        out_specs=pl.BlockSpec((tm, tn), lambda i, j: (i, j)),
        out_shape=jax.ShapeDtypeStruct((M, N), F32),
        compiler_params=pltpu.CompilerParams(
            dimension_semantics=("parallel", "arbitrary"), vmem_limit_bytes=VMEM_LIMIT),
    )(*args)


def _ffn_kernel(x_ref, g_ref, wg_ref, wu_ref, wd_ref, gf_ref, o_ref, xn_ref, *, final_norm):
    j = pl.program_id(1)

    @pl.when(j == 0)
    def _():
        x = x_ref[...]
        ms = jnp.mean(x * x, axis=-1, keepdims=True)
        xn_ref[...] = (x * lax.rsqrt(ms + RMS_EPS) * g_ref[...]).astype(BF16)
        o_ref[...] = x

    xn = xn_ref[...]
    gate = jnp.dot(xn, wg_ref[...].astype(BF16), preferred_element_type=F32)
    up = jnp.dot(xn, wu_ref[...].astype(BF16), preferred_element_type=F32)
    act = (gate * _sigmoid(gate) * up).astype(BF16)
    o_ref[...] += jnp.dot(act, wd_ref[...].astype(BF16), preferred_element_type=F32)

    if final_norm:
        @pl.when(j == pl.num_programs(1) - 1)
        def _():
            h = o_ref[...]
            ms = jnp.mean(h * h, axis=-1, keepdims=True)
            o_ref[...] = h * lax.rsqrt(ms + RMS_EPS) * gf_ref[...]


def ffn_residual(x, g, wg, wu, wd, g_final, *, tm, tf, final_norm):
    M, D = x.shape
    F = wg.shape[1]
    return pl.pallas_call(
        functools.partial(_ffn_kernel, final_norm=final_norm),
        grid=(M // tm, F // tf),
        in_specs=[
            pl.BlockSpec((tm, D), lambda i, j: (i, 0)),
            pl.BlockSpec((1, D), lambda i, j: (0, 0)),
            pl.BlockSpec((D, tf), lambda i, j: (0, j)),
            pl.BlockSpec((D, tf), lambda i, j: (0, j)),
            pl.BlockSpec((tf, D), lambda i, j: (j, 0)),
            pl.BlockSpec((1, D), lambda i, j: (0, 0)),
        ],
        out_specs=pl.BlockSpec((tm, D), lambda i, j: (i, 0)),
        out_shape=jax.ShapeDtypeStruct((M, D), F32),
        scratch_shapes=[pltpu.VMEM((tm, D), BF16)],
        compiler_params=pltpu.CompilerParams(
            dimension_semantics=("parallel", "arbitrary"), vmem_limit_bytes=VMEM_LIMIT),
    )(x, g.reshape(1, D), wg, wu, wd, g_final.reshape(1, D))


def _gla_kernel(q_ref, k_ref, v_ref, og_ref, gl_ref, w2_ref, b_ref, gn_ref, o_ref, st_ref,
                *, block, chunk):
    @pl.when(pl.program_id(2) == 0)
    def _():
        st_ref[...] = jnp.zeros_like(st_ref)

    logit = _mm_split(gl_ref[0], w2_ref[...], a_parts=2, b_parts=2) + b_ref[...]
    log_a = -_softplus(-logit) * (1.0 / GLA_GATE_TAU)
    tri = _iota2((chunk, chunk), 0) >= _iota2((chunk, chunk), 1)
    tri_b = tri.astype(BF16)
    scale = GLA_DK ** -0.5
    gn = gn_ref[...]
    n_chunks = block // chunk

    pre = []
    for c in range(n_chunks):
        sl = pl.ds(c * chunk, chunk)
        la = _split(log_a[c * chunk:(c + 1) * chunk], 3)
        b = _dot(tri_b, la[0]) + _dot(tri_b, la[1]) + _dot(tri_b, la[2])
        b_last = b[chunk - 1:chunk]
        q = q_ref[0, sl, :] * scale
        k = k_ref[0, sl, :]
        v = v_ref[0, sl, :]
        q_dec = (q * jnp.exp(b)).astype(BF16)
        k_inv = k * jnp.exp(-b)
        k_dec = (k * jnp.exp(b_last - b)).astype(BF16)
        att = jnp.where(tri, _mm_nt(q_dec, k_inv), 0.0)
        pre.append(dict(q_dec=q_dec, k_dec=k_dec, v_t=v.T.astype(BF16), o_intra=_mm(att, v),
                        decay=jnp.exp(b_last)))

    st = st_ref[...]
    outs = []
    for p in pre:
        outs.append(p["o_intra"] + _mm_nt(p["q_dec"], st))
        st = st * p["decay"] + _dot(p["v_t"], p["k_dec"])
    st_ref[...] = st

    for c, o in enumerate(outs):
        sl = pl.ds(c * chunk, chunk)
        ms = jnp.mean(o * o, axis=-1, keepdims=True)
        og = og_ref[0, sl, :]
        o_ref[0, sl, :] = (o * lax.rsqrt(ms + RMS_EPS) * gn * (og * _sigmoid(og))).astype(o_ref.dtype)


def gla_mixer(p3, gate_w2p, gate_b, gla_norm, *, block):
    B, S, _ = p3.shape
    kq, kk_, kv, kog = (COL_GLA_Q // GLA_DK, COL_GLA_K // GLA_DK, COL_GLA_V // GLA_DV,
                        COL_GLA_OG // GLA_DV)
    klow = COL_LOW_GATE // LANES
    return pl.pallas_call(
        functools.partial(_gla_kernel, block=block, chunk=GLA_CHUNK),
        grid=(B, GLA_HEADS, S // block),
        in_specs=[
            pl.BlockSpec((1, block, GLA_DK), lambda b, h, s: (b, s, kq + h)),
            pl.BlockSpec((1, block, GLA_DK), lambda b, h, s: (b, s, kk_ + h)),
            pl.BlockSpec((1, block, GLA_DV), lambda b, h, s: (b, s, kv + h)),
            pl.BlockSpec((1, block, GLA_DV), lambda b, h, s: (b, s, kog + h)),
            pl.BlockSpec((1, block, LANES), lambda b, h, s: (b, s, klow)),
            pl.BlockSpec((LANES, GLA_DK), lambda b, h, s: (0, h)),
            pl.BlockSpec((1, GLA_DK), lambda b, h, s: (0, h)),
            pl.BlockSpec((1, GLA_DV), lambda b, h, s: (0, 0)),
        ],
        out_specs=pl.BlockSpec((1, block, GLA_DV), lambda b, h, s: (b, s, h)),
        out_shape=jax.ShapeDtypeStruct((B, S, GLA_WIDTH), BF16),
        scratch_shapes=[pltpu.VMEM((GLA_DV, GLA_DK), F32)],
        compiler_params=pltpu.CompilerParams(
            dimension_semantics=("parallel", "parallel", "arbitrary"),
            vmem_limit_bytes=VMEM_LIMIT),
    )(p3, p3, p3, p3, p3, gate_w2p, gate_b.reshape(1, -1), gla_norm.reshape(1, -1))


def _unit_lower_inverses(mats, rowi, coli):
    def same_block(s):
        sh = s.bit_length() - 1
        return (rowi >> sh) == (coli >> sh)

    eye = (rowi == coli).astype(F32)
    a8 = [jnp.where(same_block(8), a, 0.0).astype(BF16) for a in mats]
    inv = [eye + a.astype(F32) for a in a8]
    p = [_dot(a, a) for a in a8]
    inv = [x + _mm(x, y) for x, y in zip(inv, p)]
    p = [_mm(y, y) for y in p]
    inv = [x + _mm(x, y) for x, y in zip(inv, p)]
    s = 8
    while s < RWKV_CHUNK:
        off = same_block(2 * s) & jnp.logical_not(same_block(s))
        e = [jnp.where(off, a, 0.0).astype(BF16) for a in mats]
        inv_b = [x.astype(BF16) for x in inv]
        t = [_dot(x, y).astype(BF16) for x, y in zip(inv_b, e)]
        inv = [x + _dot(y, z) for x, y, z in zip(inv, t, inv_b)]
        s *= 2
    return inv


def _rwkv_kernel(r_ref, k_ref, v_ref, wa_ref, g_ref, vec_ref, muwa_ref, mug_ref, w2_ref, a2_ref,
                 g2_ref, o_ref, st_ref, prev_ref, prevg_ref, *, block, chunk):
    T = chunk
    first = pl.program_id(2) == 0

    @pl.when(first)
    def _():
        st_ref[...] = jnp.zeros_like(st_ref)
        prev_ref[...] = jnp.zeros_like(prev_ref)
        prevg_ref[...] = jnp.zeros_like(prevg_ref)

    row0 = _iota2((block, 1), 0) == 0

    def lerp(x, prev_row, mu):
        shifted = jnp.where(row0, prev_row, pltpu.roll(x, 1, axis=0))
        return x + (shifted - x) * mu

    vec = vec_ref[...]
    w0, a0, k_k, k_a, r_k, ln_w, ln_b = (vec[i:i + 1] for i in range(7))
    mu_r, mu_k, mu_v = (vec[i:i + 1] for i in range(7, 10))

    r_raw, k_raw, v_raw, wa_raw, g_raw = r_ref[0], k_ref[0], v_ref[0], wa_ref[0], g_ref[0]
    prev = prev_ref[...]
    r_all = lerp(r_raw, prev[0:1], mu_r)
    k_all = lerp(k_raw, prev[1:2], mu_k)
    v_all = lerp(v_raw, prev[2:3], mu_v)
    wa = lerp(wa_raw, prev[3:4], muwa_ref[...])
    g_low = lerp(g_raw, prevg_ref[0:1], mug_ref[...])
    prev_ref[0:1] = r_raw[block - 1:block]
    prev_ref[1:2] = k_raw[block - 1:block]
    prev_ref[2:3] = v_raw[block - 1:block]
    prev_ref[3:4] = wa_raw[block - 1:block]
    prevg_ref[0:1] = g_raw[block - 1:block]

    z = w0 + _mm_split(jnp.tanh(wa), w2_ref[...], a_parts=2, b_parts=2)
    logw_all = -jnp.exp(-_softplus(-z) - 0.5)
    lr_all = _sigmoid(a0 + _mm_split(wa, a2_ref[...], a_parts=2, b_parts=2))
    gate_all = _mm_split(_sigmoid(g_low), g2_ref[...], a_parts=2, b_parts=2)

    lane = _iota2((1, RWKV_PAIR), 1)
    m0 = (lane < RWKV_HEAD).astype(F32)
    m1 = 1.0 - m0
    rowi = _iota2((2 * T, 2 * T), 0)
    coli = _iota2((2 * T, 2 * T), 1)
    strict = rowi > coli
    incl = rowi >= coli
    head_ones = ((rowi < RWKV_HEAD) == (coli < RWKV_HEAD)).astype(BF16)
    tri_t = (_iota2((T, T), 0) >= _iota2((T, T), 1)).astype(BF16)
    ones_t = jnp.ones((T, RWKV_PAIR), BF16)

    def hsum(x):
        return _mm_split(x, head_ones, a_parts=2)

    def stack(x):
        return jnp.concatenate([x * m0, x * m1], axis=0)

    n_chunks = block // T

    pre = []
    for c in range(n_chunks):
        lo, hi = c * T, (c + 1) * T
        r, k, v = r_all[lo:hi], k_all[lo:hi], v_all[lo:hi]
        lw, lr = logw_all[lo:hi], lr_all[lo:hi]

        kkp = k * k_k
        kk = kkp / jnp.maximum(jnp.sqrt(hsum(kkp * kkp)), 1e-12)
        k2 = k * (1.0 + (lr - 1.0) * k_a)
        b_vec = kk * lr

        lw_parts = _split(lw, 3)
        cum = _dot(tri_t, lw_parts[0]) + _dot(tri_t, lw_parts[1]) + _dot(tri_t, lw_parts[2])
        cum_last = cum[T - 1:T]
        decay = jnp.exp(_dot_tn(lw_parts[0], ones_t) + _dot_tn(lw_parts[1], ones_t)
                        + _dot_tn(lw_parts[2], ones_t))
        w_inv = jnp.exp(-cum)
        w_tail = jnp.exp(cum_last - cum)
        a_s = stack(-kk * jnp.exp(cum - lw)).astype(BF16)
        r_s = stack(r * jnp.exp(cum)).astype(BF16)
        b_s = stack(b_vec * w_inv)
        k_s = stack(k2 * w_inv)
        v_s = stack(v).astype(BF16)
        bk_tail = jnp.concatenate([stack(b_vec * w_tail), stack(k2 * w_tail)],
                                  axis=0).astype(BF16)
        prod = _mm_nt(jnp.concatenate([a_s, r_s], axis=0),
                      jnp.concatenate([b_s, k_s], axis=0))
        pre.append(dict(
            a_s=a_s, r_s=r_s, v_s=v_s, bk_tail=bk_tail, decay=decay,
            a_ab=jnp.where(strict, prod[:2 * T, :2 * T], 0.0),
            a_ak=jnp.where(strict, prod[:2 * T, 2 * T:], 0.0).astype(BF16),
            a_rb=jnp.where(incl, prod[2 * T:, :2 * T], 0.0).astype(BF16),
            a_rk=jnp.where(incl, prod[2 * T:, 2 * T:], 0.0).astype(BF16),
            bonus=hsum(r * k2 * r_k) * v))

    invs = [x.astype(BF16) for x in _unit_lower_inverses([p["a_ab"] for p in pre], rowi, coli)]
    akv = [_dot(p["a_ak"], p["v_s"]).astype(BF16) for p in pre]
    inv_a = [_dot(x, p["a_s"]).astype(BF16) for x, p in zip(invs, pre)]
    u_free = [_dot(x, y) for x, y in zip(invs, akv)]
    y_free = [_dot(p["a_rk"], p["v_s"]) for p in pre]

    st = st_ref[...]
    ys = []
    for c, p in enumerate(pre):
        st_b = st.astype(BF16)
        u = (_dot(inv_a[c], st_b) + u_free[c]).astype(BF16)
        ys.append(_dot(p["r_s"], st_b) + _dot(p["a_rb"], u) + y_free[c])
        uv = jnp.concatenate([u, p["v_s"]], axis=0)
        st = st * p["decay"] + _dot_tn(p["bk_tail"], uv)
    st_ref[...] = st

    for c, p in enumerate(pre):
        lo, hi = c * T, (c + 1) * T
        y = ys[c][:T] + ys[c][T:]
        mu = hsum(y) * (1.0 / RWKV_HEAD)
        d = y - mu
        var = hsum(d * d) * (1.0 / RWKV_HEAD)
        yn = d * lax.rsqrt(var + RWKV_LN_EPS) * ln_w + ln_b
        o_ref[0, pl.ds(lo, T), :] = ((yn + p["bonus"]) * gate_all[lo:hi]).astype(o_ref.dtype)


def rwkv_mixer(p3, vecs, mu_wa, mu_g, w2p, a2p, g2p, *, block):
    B, S, _ = p3.shape
    kr, kk_, kv = COL_RWKV_R // RWKV_PAIR, COL_RWKV_K // RWKV_PAIR, COL_RWKV_V // RWKV_PAIR
    kwa = COL_LOW_WA // LANES
    kg = COL_LOW_G // (2 * LANES)
    return pl.pallas_call(
        functools.partial(_rwkv_kernel, block=block, chunk=RWKV_CHUNK),
        grid=(B, RWKV_PAIRS, S // block),
        in_specs=[
            pl.BlockSpec((1, block, RWKV_PAIR), lambda b, j, s: (b, s, kr + j)),
            pl.BlockSpec((1, block, RWKV_PAIR), lambda b, j, s: (b, s, kk_ + j)),
            pl.BlockSpec((1, block, RWKV_PAIR), lambda b, j, s: (b, s, kv + j)),
            pl.BlockSpec((1, block, LANES), lambda b, j, s: (b, s, kwa)),
            pl.BlockSpec((1, block, 2 * LANES), lambda b, j, s: (b, s, kg)),
            pl.BlockSpec((16, RWKV_PAIR), lambda b, j, s: (0, j)),
            pl.BlockSpec((1, LANES), lambda b, j, s: (0, 0)),
            pl.BlockSpec((1, 2 * LANES), lambda b, j, s: (0, 0)),
            pl.BlockSpec((LANES, RWKV_PAIR), lambda b, j, s: (0, j)),
            pl.BlockSpec((LANES, RWKV_PAIR), lambda b, j, s: (0, j)),
            pl.BlockSpec((2 * LANES, RWKV_PAIR), lambda b, j, s: (0, j)),
        ],
        out_specs=pl.BlockSpec((1, block, RWKV_PAIR), lambda b, j, s: (b, s, j)),
        out_shape=jax.ShapeDtypeStruct((B, S, RWKV_WIDTH), BF16),
        scratch_shapes=[
            pltpu.VMEM((RWKV_PAIR, RWKV_PAIR), F32),
            pltpu.VMEM((8, LANES), F32),
            pltpu.VMEM((8, 2 * LANES), F32),
        ],
        compiler_params=pltpu.CompilerParams(
            dimension_semantics=("parallel", "parallel", "arbitrary"),
            vmem_limit_bytes=VMEM_LIMIT),
    )(p3, p3, p3, p3, p3, vecs, mu_wa, mu_g, w2p, a2p, g2p)


def _moba_kernel(q_ref, k_ref, v_ref, slope_ref, o_ref, kb_ref, vt_ref, kmean_ref,
                 m_ref, l_ref, acc_ref, s_ref, p_ref, *, nb):
    BS, D = MOBA_BLOCK, MOBA_HD
    i = pl.program_id(2)
    log2e = 1.4426950408889634
    c1 = MOBA_HD ** -0.5 * log2e
    slope2 = slope_ref[0] * log2e
    s_parts = _split(slope2, 3)

    @pl.when(i == 0)
    def _():
        kmean_ref[...] = jnp.zeros_like(kmean_ref)
        lane = _iota2((2 * BS, LANES), 1)
        kpos = _iota2((2 * BS, LANES), 0)
        kpos_lo = (kpos & (BS - 1)).astype(F32)
        kpos_hi = (kpos & BS).astype(F32)
        sk = [x[:, :LANES].astype(F32) for x in s_parts]
        feat = jnp.where(lane == 0, sk[0],
                         jnp.where(lane == 1, sk[1],
                                   jnp.where(lane == 2, sk[2],
                                             jnp.where(lane < 6, kpos_lo,
                                                       jnp.where(lane < 9, kpos_hi, 0.0)))))
        feat = feat.astype(BF16)
        for j in range(nb):
            half = slice((j % 2) * BS, (j % 2 + 1) * BS)
            kj = k_ref[0, pl.ds(j * BS, BS), :]
            kmean_ref[j:j + 1, :] = jnp.mean(kj, axis=0, keepdims=True)
            kb_ref[j // 2, half, :D] = kj.astype(BF16)
            vt_ref[j // 2, :, half] = v_ref[0, pl.ds(j * BS, BS), :].T.astype(BF16)
        for t in range(nb // 2):
            kb_ref[t, :, D:] = feat

    q_t = q_ref[0].T

    nbp = kmean_ref.shape[0]
    blk = _iota2((nbp, BS), 0)
    gate = _mm_split(kmean_ref[...], q_t, a_parts=3, b_parts=3)
    gate = jnp.where(blk < i, gate, NEG_INF)
    sel = []
    for r in range(MOBA_TOPK):
        mx = jnp.max(gate, axis=0, keepdims=True)
        idx = jnp.min(jnp.where(gate == mx, blk, nbp), axis=0, keepdims=True)
        sel.append(jnp.where(r < i, idx, -1))
        gate = jnp.where(blk == idx, -jnp.inf, gate)

    qpos = _iota2((LANES, BS), 1).astype(F32)
    arow = _iota2((LANES, BS), 0)
    sq = [x.astype(F32) for x in s_parts]
    aug = jnp.where(arow < 3, -qpos,
                    jnp.where((arow == 3) | (arow == 6), sq[0],
                              jnp.where((arow == 4) | (arow == 7), sq[1],
                                        jnp.where((arow == 5) | (arow == 8), sq[2], 0.0))))
    q_aug = jnp.concatenate([(q_t * c1).astype(BF16), aug.astype(BF16)], axis=0)

    own_pair = i // 2
    own_half = pl.multiple_of((i % 2) * BS, BS)
    other_half = pl.multiple_of((1 - i % 2) * BS, BS)
    causal = _iota2((BS, BS), 1) >= _iota2((BS, BS), 0)
    s_own = _dot(kb_ref[own_pair, pl.ds(own_half, BS), :], q_aug)
    s_own = jnp.where(causal, s_own - slope2 * own_half.astype(F32), NEG_INF)
    m0 = jnp.max(s_own, axis=0, keepdims=True)
    p0 = jnp.exp2(s_own - m0)
    m_ref[...] = m0
    l_ref[...] = jnp.sum(p0, axis=0, keepdims=True)
    p_ref[pl.ds(own_half, BS), :] = p0.astype(BF16)
    p_ref[pl.ds(other_half, BS), :] = jnp.zeros((BS, BS), BF16)
    acc_ref[...] = jnp.zeros_like(acc_ref)
    s_ref[0] = _dot(kb_ref[0], q_aug)

    def past_pair(t, prev):
        s_cur = s_ref[t % 2]
        pv = _dot(vt_ref[prev], p_ref[...])
        s_ref[(t + 1) % 2] = _dot(kb_ref[jnp.minimum(t + 1, nb // 2 - 1)], q_aug)
        ja, jb = 2 * t, 2 * t + 1
        picked_a = (sel[0] == ja) | (sel[1] == ja) | (sel[2] == ja)
        picked_b = (sel[0] == jb) | (sel[1] == jb) | (sel[2] == jb)
        s_a = jnp.where(picked_a, s_cur[:BS], NEG_INF)
        s_b = jnp.where(picked_b, s_cur[BS:], NEG_INF)
        off = slope2 * ((i - ja) * BS).astype(F32)
        m_old = m_ref[...]
        mx = jnp.maximum(jnp.max(s_a, axis=0, keepdims=True), jnp.max(s_b, axis=0, keepdims=True))
        m_new = jnp.maximum(m_old, mx - off)
        alpha = jnp.exp2(m_old - m_new)
        shift = m_new + off
        p_a = jnp.exp2(s_a - shift)
        p_b = jnp.exp2(s_b - shift)
        l_ref[...] = (alpha * l_ref[...] + jnp.sum(p_a, axis=0, keepdims=True)
                      + jnp.sum(p_b, axis=0, keepdims=True))
        acc_ref[...] = alpha * (acc_ref[...] + pv)
        p_ref[:BS, :] = p_a.astype(BF16)
        p_ref[BS:, :] = p_b.astype(BF16)
        m_ref[...] = m_new
        return t

    last = lax.fori_loop(0, (i + 1) // 2, past_pair, own_pair)
    acc = acc_ref[...] + _dot(vt_ref[last], p_ref[...])
    o_ref[0] = (acc / l_ref[...]).T.astype(o_ref.dtype)


def moba_attention(qkv_h, B, S):
    H, D, BS = MOBA_HEADS, MOBA_HD, MOBA_BLOCK
    nb = S // BS
    assert nb % 2 == 0, "key blocks are stored in pairs"
    nbp = -(-nb // 8) * 8
    slopes = jnp.exp2(-8.0 * jnp.arange(1, H + 1, dtype=F32) / H)
    slopes = jnp.broadcast_to(slopes[:, None, None], (H, 1, BS))
    return pl.pallas_call(
        functools.partial(_moba_kernel, nb=nb),
        grid=(B, H, nb),
        in_specs=[
            pl.BlockSpec((1, BS, D), lambda b, h, i: (h, b * nb + i, 0)),
            pl.BlockSpec((1, S, D), lambda b, h, i: (H + h, b, 0)),
            pl.BlockSpec((1, S, D), lambda b, h, i: (2 * H + h, b, 0)),
            pl.BlockSpec((1, 1, BS), lambda b, h, i: (h, 0, 0)),
        ],
        out_specs=pl.BlockSpec((1, BS, D), lambda b, h, i: (b, i, h)),
        out_shape=jax.ShapeDtypeStruct((B, S, H * D), BF16),
        scratch_shapes=[
            pltpu.VMEM((nb // 2, 2 * BS, D + LANES), BF16),
            pltpu.VMEM((nb // 2, D, 2 * BS), BF16),
            pltpu.VMEM((nbp, D), F32),
            pltpu.VMEM((1, BS), F32),
            pltpu.VMEM((1, BS), F32),
            pltpu.VMEM((D, BS), F32),
            pltpu.VMEM((2, 2 * BS, BS), F32),
            pltpu.VMEM((2 * BS, BS), BF16),
        ],
        compiler_params=pltpu.CompilerParams(
            dimension_semantics=("parallel", "parallel", "arbitrary"),
            vmem_limit_bytes=VMEM_LIMIT),
    )(qkv_h, qkv_h, qkv_h, slopes)


def _pad_cols(w, n):
    return jnp.pad(w, ((0, 0), (0, n - w.shape[1])))


def _pad_rows(w, before, total):
    return jnp.pad(w, ((before, total - before - w.shape[0]), (0, 0)))


def _mix_in_layout(w):
    gk, gw = GLA_KEY_WIDTH, GLA_WIDTH
    g_end = 2 * gk + 2 * gw
    gla_main = w[:, :g_end]
    gla_gate = w[:, g_end:g_end + GLA_GATE_RANK]
    r0 = g_end + GLA_GATE_RANK
    rw_main = w[:, r0:r0 + 3 * RWKV_WIDTH]
    l0 = r0 + 3 * RWKV_WIDTH
    wa = w[:, l0:l0 + RWKV_DECAY_LORA + RWKV_AAA_LORA]
    g0 = l0 + RWKV_DECAY_LORA + RWKV_AAA_LORA
    gl = w[:, g0:g0 + RWKV_GATE_LORA]
    return jnp.concatenate(
        [gla_main, rw_main, _pad_cols(gla_gate, LANES), wa, _pad_cols(gl, 2 * LANES)], axis=1)


def mixer_layer0(hn_proj, gla_gate_w2, gla_gate_b, gla_norm, rwkv_mu, rwkv_w0, rwkv_w2, rwkv_a0,
                 rwkv_a2, rwkv_g2, rwkv_k_k, rwkv_k_a, rwkv_r_k, rwkv_ln_w, rwkv_ln_b,
                 *, gla_block, rwkv_block):
    W = RWKV_WIDTH
    o_gla = gla_mixer(hn_proj, _pad_rows(gla_gate_w2, 0, LANES), gla_gate_b, gla_norm,
                      block=gla_block)
    mu_r, mu_k, mu_v = rwkv_mu[:W], rwkv_mu[W:2 * W], rwkv_mu[2 * W:3 * W]
    mu_low = rwkv_mu[3 * W:]
    n_wa = RWKV_DECAY_LORA + RWKV_AAA_LORA
    vecs = jnp.stack([rwkv_w0, rwkv_a0, rwkv_k_k, rwkv_k_a, rwkv_r_k.reshape(-1), rwkv_ln_w,
                      rwkv_ln_b, mu_r, mu_k, mu_v] + [jnp.zeros((W,), F32)] * 6)
    mu_wa = mu_low[:n_wa].reshape(1, -1)
    mu_g = _pad_cols(mu_low[n_wa:].reshape(1, -1), 2 * LANES)
    w2p = _pad_rows(rwkv_w2, 0, LANES)
    a2p = _pad_rows(rwkv_a2, RWKV_DECAY_LORA, LANES)
    g2p = _pad_rows(rwkv_g2, 0, 2 * LANES)
    o_rwkv = rwkv_mixer(hn_proj, vecs, mu_wa, mu_g, w2p, a2p, g2p, block=rwkv_block)
    return o_gla, o_rwkv


def kernel(x, norm_mix, norm_ffn, norm_final, mix_in_w, gla_gate_w2, gla_gate_b, gla_norm, rwkv_mu, rwkv_w0, rwkv_w2, rwkv_a0, rwkv_a2, rwkv_g2, rwkv_k_k, rwkv_k_a, rwkv_r_k, rwkv_ln_w, rwkv_ln_b, mix_out_w, attn_qkv_w, attn_out_w, ffn_gate_w, ffn_up_w, ffn_down_w):
    B, S, D = x.shape
    M = B * S
    tm, tn, tf = 512, 512, 256
    tm_in = min(1024, M)
    tm_ffn = min(1024, M)
    h = x.reshape(M, D)

    w_in = _mix_in_layout(mix_in_w[0]).astype(BF16)
    p = norm_matmul(h, norm_mix[0], w_in, tm=tm_in, tn=tn).reshape(B, S, MIX_COLS)
    o_gla, o_rwkv = mixer_layer0(
        p, gla_gate_w2[0], gla_gate_b[0], gla_norm[0], rwkv_mu[0], rwkv_w0[0], rwkv_w2[0],
        rwkv_a0[0], rwkv_a2[0], rwkv_g2[0], rwkv_k_k[0], rwkv_k_a[0], rwkv_r_k[0], rwkv_ln_w[0],
        rwkv_ln_b[0], gla_block=512, rwkv_block=512)
    w_out = mix_out_w[0].astype(BF16)
    h = proj_residual(h, [(o_gla.reshape(M, GLA_WIDTH), w_out[:GLA_WIDTH]),
                          (o_rwkv.reshape(M, RWKV_WIDTH), w_out[GLA_WIDTH:])], tm=tm, tn=D)
    h = ffn_residual(h, norm_ffn[0], ffn_gate_w[0], ffn_up_w[0], ffn_down_w[0], norm_final,
                     tm=tm_ffn, tf=tf, final_norm=False)

    qkv_h = norm_matmul(h, norm_mix[1], attn_qkv_w[0].astype(BF16), tm=tm_in, tn=tn,
                        group=MOBA_HD)
    o_attn = moba_attention(qkv_h, B, S)
    h = proj_residual(h, [(o_attn.reshape(M, D), attn_out_w[0].astype(BF16))], tm=tm, tn=D)
    h = ffn_residual(h, norm_ffn[1], ffn_gate_w[1], ffn_up_w[1], ffn_down_w[1], norm_final,
                     tm=tm_ffn, tf=tf, final_norm=True)
    return h.reshape(B, S, D)
```

```python
import functools

import jax
import jax.numpy as jnp
from jax import lax
from jax.experimental import pallas as pl
from jax.experimental.pallas import tpu as pltpu

F32 = jnp.float32
BF16 = jnp.bfloat16
HI = lax.Precision.HIGHEST

D_MODEL = 2048
RMS_EPS = 1e-6

GLA_HEADS = 4
GLA_DK = 128
GLA_DV = 256
GLA_KEY_WIDTH = GLA_HEADS * GLA_DK
GLA_WIDTH = GLA_HEADS * GLA_DV
GLA_GATE_RANK = 16
GLA_GATE_TAU = 16.0
GLA_CHUNK = 64

RWKV_WIDTH = 1024
RWKV_HEAD = 64
RWKV_PAIR = 2 * RWKV_HEAD
RWKV_PAIRS = RWKV_WIDTH // RWKV_PAIR
RWKV_DECAY_LORA = 64
RWKV_AAA_LORA = 64
RWKV_GATE_LORA = 160
RWKV_LN_EPS = RWKV_HEAD * 1e-5
RWKV_CHUNK = 64

MOBA_HEADS = 16
MOBA_HD = 128
MOBA_BLOCK = 256
MOBA_TOPK = 3
NEG_INF = -1e30

FFN_HIDDEN = 5632

LANES = 128

COL_GLA_Q = 0
COL_GLA_K = 512
COL_GLA_V = 1024
COL_GLA_OG = 2048
COL_RWKV_R = 3072
COL_RWKV_K = 4096
COL_RWKV_V = 5120
COL_LOW_GATE = 6144
COL_LOW_WA = 6272
COL_LOW_G = 6400
MIX_COLS = 6656

VMEM_LIMIT = 56 * 1024 * 1024


def _mm(a, b):
    return jnp.dot(a.astype(BF16), b.astype(BF16), preferred_element_type=F32)


def _mm_nt(a, b):
    return lax.dot_general(a.astype(BF16), b.astype(BF16), (((1,), (1,)), ((), ())),
                           preferred_element_type=F32)


def _mm_hi(a, b):
    return jnp.dot(a, b, preferred_element_type=F32, precision=HI)


def _mm_nt_hi(a, b):
    return lax.dot_general(a, b, (((1,), (1,)), ((), ())), preferred_element_type=F32,
                           precision=HI)


def _dot(a, b):
    return jnp.dot(a, b, preferred_element_type=F32)


def _dot_tn(a, b):
    return lax.dot_general(a, b, (((0,), (0,)), ((), ())), preferred_element_type=F32)


def _split(x, parts):
    out = []
    for _ in range(parts - 1):
        hi = x.astype(BF16)
        out.append(hi)
        x = x - hi.astype(F32)
    out.append(x.astype(BF16))
    return out


def _mm_split(a, b, *, a_parts=1, b_parts=1):
    a_p = _split(a, a_parts) if a_parts > 1 else [a.astype(BF16)]
    b_p = _split(b, b_parts) if b_parts > 1 else [b.astype(BF16)]
    acc = None
    for i, ai in enumerate(a_p):
        for j, bj in enumerate(b_p):
            if i + j < max(a_parts, b_parts):
                t = _dot(ai, bj)
                acc = t if acc is None else acc + t
    return acc


def _sigmoid(x):
    return 1.0 / (1.0 + jnp.exp(-x))


def _softplus(x):
    return jnp.maximum(x, 0.0) + jnp.log1p(jnp.exp(-jnp.abs(x)))


def _iota2(shape, axis):
    return lax.broadcasted_iota(jnp.int32, shape, axis)


def _norm_matmul_kernel(x_ref, g_ref, w_ref, o_ref, xn_ref, *, group):
    @pl.when(pl.program_id(1) == 0)
    def _():
        x = x_ref[...]
        ms = jnp.mean(x * x, axis=-1, keepdims=True)
        xn_ref[...] = (x * lax.rsqrt(ms + RMS_EPS) * g_ref[...]).astype(BF16)

    res = jnp.dot(xn_ref[...], w_ref[...], preferred_element_type=F32).astype(o_ref.dtype)
    if group is None:
        o_ref[...] = res
    else:
        for c in range(res.shape[1] // group):
            o_ref[c] = res[:, c * group:(c + 1) * group]


def norm_matmul(x, g, w, *, tm, tn, out_dtype=F32, group=None):
    M, D = x.shape
    N = w.shape[1]
    if group is None:
        out_specs = pl.BlockSpec((tm, tn), lambda i, j: (i, j))
        out_shape = jax.ShapeDtypeStruct((M, N), out_dtype)
    else:
        out_specs = pl.BlockSpec((tn // group, tm, group), lambda i, j: (j, i, 0))
        out_shape = jax.ShapeDtypeStruct((N // group, M, group), out_dtype)
    return pl.pallas_call(
        functools.partial(_norm_matmul_kernel, group=group),
        grid=(M // tm, N // tn),
        in_specs=[
            pl.BlockSpec((tm, D), lambda i, j: (i, 0)),
            pl.BlockSpec((1, D), lambda i, j: (0, 0)),
            pl.BlockSpec((D, tn), lambda i, j: (0, j)),
        ],
        out_specs=out_specs,
        out_shape=out_shape,
        scratch_shapes=[pltpu.VMEM((tm, D), BF16)],
        compiler_params=pltpu.CompilerParams(
            dimension_semantics=("parallel", "arbitrary"), vmem_limit_bytes=VMEM_LIMIT),
    )(x, g.reshape(1, D), w)


def _proj_res_kernel(*refs, n_in):
    res_ref = refs[0]
    o_ref = refs[1 + 2 * n_in]
    acc = res_ref[...]
    for i in range(n_in):
        acc = acc + jnp.dot(refs[1 + 2 * i][...], refs[2 + 2 * i][...],
                            preferred_element_type=F32)
    o_ref[...] = acc


def proj_residual(res, pairs, *, tm, tn):
    M, N = res.shape
    in_specs = [pl.BlockSpec((tm, tn), lambda i, j: (i, j))]
    args = [res]
    for a, w in pairs:
        K = a.shape[1]
        in_specs.append(pl.BlockSpec((tm, K), lambda i, j: (i, 0)))
        in_specs.append(pl.BlockSpec((K, tn), lambda i, j: (0, j)))
        args += [a, w]
    return pl.pallas_call(
        functools.partial(_proj_res_kernel, n_in=len(pairs)),
        grid=(M // tm, N // tn),
        in_specs=in_specs,
        out_specs=pl.BlockSpec((tm, tn), lambda i, j: (i, j)),
        out_shape=jax.ShapeDtypeStruct((M, N), F32),
        compiler_params=pltpu.CompilerParams(
            dimension_semantics=("parallel", "arbitrary"), vmem_limit_bytes=VMEM_LIMIT),
    )(*args)


def _ffn_kernel(x_ref, g_ref, wg_ref, wu_ref, wd_ref, gf_ref, o_ref, xn_ref, *, final_norm):
    j = pl.program_id(1)

    @pl.when(j == 0)
    def _():
        x = x_ref[...]
        ms = jnp.mean(x * x, axis=-1, keepdims=True)
        xn_ref[...] = (x * lax.rsqrt(ms + RMS_EPS) * g_ref[...]).astype(BF16)
        o_ref[...] = x

    xn = xn_ref[...]
    gate = jnp.dot(xn, wg_ref[...].astype(BF16), preferred_element_type=F32)
    up = jnp.dot(xn, wu_ref[...].astype(BF16), preferred_element_type=F32)
    act = (gate * _sigmoid(gate) * up).astype(BF16)
    o_ref[...] += jnp.dot(act, wd_ref[...].astype(BF16), preferred_element_type=F32)

    if final_norm:
        @pl.when(j == pl.num_programs(1) - 1)
        def _():
            h = o_ref[...]
            ms = jnp.mean(h * h, axis=-1, keepdims=True)
            o_ref[...] = h * lax.rsqrt(ms + RMS_EPS) * gf_ref[...]


def ffn_residual(x, g, wg, wu, wd, g_final, *, layer, tm, tf, final_norm):
    M, D = x.shape
    F = wg.shape[2]
    return pl.pallas_call(
        functools.partial(_ffn_kernel, final_norm=final_norm),
        grid=(M // tm, F // tf),
        in_specs=[
            pl.BlockSpec((tm, D), lambda i, j: (i, 0)),
            pl.BlockSpec((1, D), lambda i, j: (0, 0)),
            pl.BlockSpec((None, D, tf), lambda i, j: (layer, 0, j)),
            pl.BlockSpec((None, D, tf), lambda i, j: (layer, 0, j)),
            pl.BlockSpec((None, tf, D), lambda i, j: (layer, j, 0)),
            pl.BlockSpec((1, D), lambda i, j: (0, 0)),
        ],
        out_specs=pl.BlockSpec((tm, D), lambda i, j: (i, 0)),
        out_shape=jax.ShapeDtypeStruct((M, D), F32),
        scratch_shapes=[pltpu.VMEM((tm, D), BF16)],
        compiler_params=pltpu.CompilerParams(
            dimension_semantics=("parallel", "arbitrary"), vmem_limit_bytes=VMEM_LIMIT),
    )(x, g.reshape(1, D), wg, wu, wd, g_final.reshape(1, D))


def _gla_kernel(q_ref, k_ref, v_ref, og_ref, gl_ref, w2_ref, b_ref, gn_ref, o_ref, st_ref,
                *, block, chunk):
    @pl.when(pl.program_id(2) == 0)
    def _():
        st_ref[...] = jnp.zeros_like(st_ref)

    logit = _mm_split(gl_ref[0], w2_ref[...], a_parts=2, b_parts=2) + b_ref[...]
    log_a = -_softplus(-logit) * (1.0 / GLA_GATE_TAU)
    tri = _iota2((chunk, chunk), 0) >= _iota2((chunk, chunk), 1)
    tri_b = tri.astype(BF16)
    scale = GLA_DK ** -0.5
    gn = gn_ref[...]
    n_chunks = block // chunk

    pre = []
    for c in range(n_chunks):
        sl = pl.ds(c * chunk, chunk)
        la = _split(log_a[c * chunk:(c + 1) * chunk], 3)
        b = _dot(tri_b, la[0]) + _dot(tri_b, la[1]) + _dot(tri_b, la[2])
        b_last = b[chunk - 1:chunk]
        q = q_ref[0, sl, :] * scale
        k = k_ref[0, sl, :]
        v = v_ref[0, sl, :]
        q_dec = (q * jnp.exp(b)).astype(BF16)
        k_inv = k * jnp.exp(-b)
        k_dec = (k * jnp.exp(b_last - b)).astype(BF16)
        att = jnp.where(tri, _mm_nt(q_dec, k_inv), 0.0)
        pre.append(dict(q_dec=q_dec, k_dec=k_dec, v_t=v.T.astype(BF16), o_intra=_mm(att, v),
                        decay=jnp.exp(b_last)))

    st = st_ref[...]
    outs = []
    for p in pre:
        outs.append(p["o_intra"] + _mm_nt(p["q_dec"], st))
        st = st * p["decay"] + _dot(p["v_t"], p["k_dec"])
    st_ref[...] = st

    for c, o in enumerate(outs):
        sl = pl.ds(c * chunk, chunk)
        ms = jnp.mean(o * o, axis=-1, keepdims=True)
        og = og_ref[0, sl, :]
        o_ref[0, sl, :] = (o * lax.rsqrt(ms + RMS_EPS) * gn * (og * _sigmoid(og))).astype(o_ref.dtype)


def gla_mixer(p3, gate_w2p, gate_b, gla_norm, *, block):
    B, S, _ = p3.shape
    kq, kk_, kv, kog = (COL_GLA_Q // GLA_DK, COL_GLA_K // GLA_DK, COL_GLA_V // GLA_DV,
                        COL_GLA_OG // GLA_DV)
    klow = COL_LOW_GATE // LANES
    return pl.pallas_call(
        functools.partial(_gla_kernel, block=block, chunk=GLA_CHUNK),
        grid=(B, GLA_HEADS, S // block),
        in_specs=[
            pl.BlockSpec((1, block, GLA_DK), lambda b, h, s: (b, s, kq + h)),
            pl.BlockSpec((1, block, GLA_DK), lambda b, h, s: (b, s, kk_ + h)),
            pl.BlockSpec((1, block, GLA_DV), lambda b, h, s: (b, s, kv + h)),
            pl.BlockSpec((1, block, GLA_DV), lambda b, h, s: (b, s, kog + h)),
            pl.BlockSpec((1, block, LANES), lambda b, h, s: (b, s, klow)),
            pl.BlockSpec((LANES, GLA_DK), lambda b, h, s: (0, h)),
            pl.BlockSpec((1, GLA_DK), lambda b, h, s: (0, h)),
            pl.BlockSpec((1, GLA_DV), lambda b, h, s: (0, 0)),
        ],
        out_specs=pl.BlockSpec((1, block, GLA_DV), lambda b, h, s: (b, s, h)),
        out_shape=jax.ShapeDtypeStruct((B, S, GLA_WIDTH), BF16),
        scratch_shapes=[pltpu.VMEM((GLA_DV, GLA_DK), F32)],
        compiler_params=pltpu.CompilerParams(
            dimension_semantics=("parallel", "parallel", "arbitrary"),
            vmem_limit_bytes=VMEM_LIMIT),
    )(p3, p3, p3, p3, p3, gate_w2p, gate_b.reshape(1, -1), gla_norm.reshape(1, -1))


def _unit_lower_inverses(mats, rowi, coli):
    def same_block(s):
        sh = s.bit_length() - 1
        return (rowi >> sh) == (coli >> sh)

    eye = (rowi == coli).astype(F32)
    a8 = [jnp.where(same_block(8), a, 0.0).astype(BF16) for a in mats]
    inv = [eye + a.astype(F32) for a in a8]
    p = [_dot(a, a) for a in a8]
    inv = [x + _mm(x, y) for x, y in zip(inv, p)]
    p = [_mm(y, y) for y in p]
    inv = [x + _mm(x, y) for x, y in zip(inv, p)]
    s = 8
    while s < RWKV_CHUNK:
        off = same_block(2 * s) & jnp.logical_not(same_block(s))
        e = [jnp.where(off, a, 0.0).astype(BF16) for a in mats]
        inv_b = [x.astype(BF16) for x in inv]
        t = [_dot(x, y).astype(BF16) for x, y in zip(inv_b, e)]
        inv = [x + _dot(y, z) for x, y, z in zip(inv, t, inv_b)]
        s *= 2
    return inv


def _rwkv_kernel(r_ref, k_ref, v_ref, wa_ref, g_ref, vec_ref, muwa_ref, mug_ref, w2_ref, a2_ref,
                 g2_ref, o_ref, st_ref, prev_ref, prevg_ref, *, block, chunk):
    T = chunk
    first = pl.program_id(2) == 0

    @pl.when(first)
    def _():
        st_ref[...] = jnp.zeros_like(st_ref)
        prev_ref[...] = jnp.zeros_like(prev_ref)
        prevg_ref[...] = jnp.zeros_like(prevg_ref)

    row0 = _iota2((block, 1), 0) == 0

    def lerp(x, prev_row, mu):
        shifted = jnp.where(row0, prev_row, pltpu.roll(x, 1, axis=0))
        return x + (shifted - x) * mu

    vec = vec_ref[...]
    w0, a0, k_k, k_a, r_k, ln_w, ln_b = (vec[i:i + 1] for i in range(7))
    mu_r, mu_k, mu_v = (vec[i:i + 1] for i in range(7, 10))

    r_raw, k_raw, v_raw, wa_raw, g_raw = r_ref[0], k_ref[0], v_ref[0], wa_ref[0], g_ref[0]
    prev = prev_ref[...]
    r_all = lerp(r_raw, prev[0:1], mu_r)
    k_all = lerp(k_raw, prev[1:2], mu_k)
    v_all = lerp(v_raw, prev[2:3], mu_v)
    wa = lerp(wa_raw, prev[3:4], muwa_ref[...])
    g_low = lerp(g_raw, prevg_ref[0:1], mug_ref[...])
    prev_ref[0:1] = r_raw[block - 1:block]
    prev_ref[1:2] = k_raw[block - 1:block]
    prev_ref[2:3] = v_raw[block - 1:block]
    prev_ref[3:4] = wa_raw[block - 1:block]
    prevg_ref[0:1] = g_raw[block - 1:block]

    z = w0 + _mm_split(jnp.tanh(wa), w2_ref[...], a_parts=2, b_parts=2)
    logw_all = -jnp.exp(-_softplus(-z) - 0.5)
    lr_all = _sigmoid(a0 + _mm_split(wa, a2_ref[...], a_parts=2, b_parts=2))
    gate_all = _mm_split(_sigmoid(g_low), g2_ref[...], a_parts=2, b_parts=2)

    lane = _iota2((1, RWKV_PAIR), 1)
    m0 = (lane < RWKV_HEAD).astype(F32)
    m1 = 1.0 - m0
    rowi = _iota2((2 * T, 2 * T), 0)
    coli = _iota2((2 * T, 2 * T), 1)
    strict = rowi > coli
    incl = rowi >= coli
    head_ones = ((rowi < RWKV_HEAD) == (coli < RWKV_HEAD)).astype(BF16)
    head_ones2 = jnp.concatenate([head_ones, head_ones], axis=0)
    tri_t = (_iota2((T, T), 0) >= _iota2((T, T), 1)).astype(BF16)
    tri_t3 = jnp.concatenate([tri_t, tri_t, tri_t], axis=1)
    ones_t3 = jnp.ones((3 * T, RWKV_PAIR), BF16)

    def hsum(x):
        return _dot(jnp.concatenate(_split(x, 2), axis=1), head_ones2)

    def stack(x):
        return jnp.concatenate([x * m0, x * m1], axis=0)

    n_chunks = block // T

    kkp = k_all * k_k
    kk_all = kkp / jnp.maximum(jnp.sqrt(hsum(kkp * kkp)), 1e-12)
    k2_all = k_all * (1.0 + (lr_all - 1.0) * k_a)
    b_all = kk_all * lr_all
    bonus_all = hsum(r_all * k2_all * r_k) * v_all

    pre = []
    for c in range(n_chunks):
        lo, hi = c * T, (c + 1) * T
        r, v, lw = r_all[lo:hi], v_all[lo:hi], logw_all[lo:hi]
        kk, k2, b_vec = kk_all[lo:hi], k2_all[lo:hi], b_all[lo:hi]

        lw3 = jnp.concatenate(_split(lw, 3), axis=0)
        cum = _dot(tri_t3, lw3)
        cum_last = cum[T - 1:T]
        decay = jnp.exp(_dot_tn(lw3, ones_t3))
        w_inv = jnp.exp(-cum)
        w_tail = jnp.exp(cum_last - cum)
        a_s = stack(-kk * jnp.exp(cum - lw)).astype(BF16)
        r_s = stack(r * jnp.exp(cum)).astype(BF16)
        b_s = stack(b_vec * w_inv)
        k_s = stack(k2 * w_inv)
        v_s = stack(v).astype(BF16)
        prod = _mm_nt(jnp.concatenate([a_s, r_s], axis=0),
                      jnp.concatenate([b_s, k_s], axis=0))
        pre.append(dict(
            a_s=a_s, r_s=r_s, v_s=v_s, decay=decay,
            b_tail=stack(b_vec * w_tail).astype(BF16), k_tail=stack(k2 * w_tail).astype(BF16),
            a_ab=jnp.where(strict, prod[:2 * T, :2 * T], 0.0),
            a_ak=jnp.where(strict, prod[:2 * T, 2 * T:], 0.0).astype(BF16),
            a_rb=jnp.where(incl, prod[2 * T:, :2 * T], 0.0).astype(BF16),
            a_rk=jnp.where(incl, prod[2 * T:, 2 * T:], 0.0).astype(BF16)))

    invs = [x.astype(BF16) for x in _unit_lower_inverses([p["a_ab"] for p in pre], rowi, coli)]
    akv = [_dot(p["a_ak"], p["v_s"]).astype(BF16) for p in pre]
    x = [_dot(i, jnp.concatenate([p["a_s"], y], axis=1)).astype(BF16)
         for i, p, y in zip(invs, pre, akv)]
    bx = [_dot_tn(p["b_tail"], y) for p, y in zip(pre, x)]
    rx = [_dot(p["a_rb"], y) for p, y in zip(pre, x)]
    trans = [y[:, :RWKV_PAIR].astype(BF16) for y in bx]
    gain = [y[:, RWKV_PAIR:] + _dot_tn(p["k_tail"], p["v_s"]) for p, y in zip(pre, bx)]
    read = [(p["r_s"].astype(F32) + y[:, :RWKV_PAIR]).astype(BF16) for p, y in zip(pre, rx)]
    y_free = [y[:, RWKV_PAIR:] + _dot(p["a_rk"], p["v_s"]) for p, y in zip(pre, rx)]

    st = st_ref[...]
    ys = []
    for c, p in enumerate(pre):
        st_b = st.astype(BF16)
        y_s = _dot(read[c], st_b) + y_free[c]
        ys.append(y_s[:T] + y_s[T:])
        st = st * p["decay"] + _dot(trans[c], st_b) + gain[c]
    st_ref[...] = st

    y = jnp.concatenate(ys, axis=0)
    mu = hsum(y) * (1.0 / RWKV_HEAD)
    d = y - mu
    var = hsum(d * d) * (1.0 / RWKV_HEAD)
    yn = d * lax.rsqrt(var + RWKV_LN_EPS) * ln_w + ln_b
    o_ref[0] = ((yn + bonus_all) * gate_all).astype(o_ref.dtype)


def rwkv_mixer(p3, vecs, mu_wa, mu_g, w2p, a2p, g2p, *, block):
    B, S, _ = p3.shape
    kr, kk_, kv = COL_RWKV_R // RWKV_PAIR, COL_RWKV_K // RWKV_PAIR, COL_RWKV_V // RWKV_PAIR
    kwa = COL_LOW_WA // LANES
    kg = COL_LOW_G // (2 * LANES)
    return pl.pallas_call(
        functools.partial(_rwkv_kernel, block=block, chunk=RWKV_CHUNK),
        grid=(B, RWKV_PAIRS, S // block),
        in_specs=[
            pl.BlockSpec((1, block, RWKV_PAIR), lambda b, j, s: (b, s, kr + j)),
            pl.BlockSpec((1, block, RWKV_PAIR), lambda b, j, s: (b, s, kk_ + j)),
            pl.BlockSpec((1, block, RWKV_PAIR), lambda b, j, s: (b, s, kv + j)),
            pl.BlockSpec((1, block, LANES), lambda b, j, s: (b, s, kwa)),
            pl.BlockSpec((1, block, 2 * LANES), lambda b, j, s: (b, s, kg)),
            pl.BlockSpec((16, RWKV_PAIR), lambda b, j, s: (0, j)),
            pl.BlockSpec((1, LANES), lambda b, j, s: (0, 0)),
            pl.BlockSpec((1, 2 * LANES), lambda b, j, s: (0, 0)),
            pl.BlockSpec((LANES, RWKV_PAIR), lambda b, j, s: (0, j)),
            pl.BlockSpec((LANES, RWKV_PAIR), lambda b, j, s: (0, j)),
            pl.BlockSpec((2 * LANES, RWKV_PAIR), lambda b, j, s: (0, j)),
        ],
        out_specs=pl.BlockSpec((1, block, RWKV_PAIR), lambda b, j, s: (b, s, j)),
        out_shape=jax.ShapeDtypeStruct((B, S, RWKV_WIDTH), BF16),
        scratch_shapes=[
            pltpu.VMEM((RWKV_PAIR, RWKV_PAIR), F32),
            pltpu.VMEM((8, LANES), F32),
            pltpu.VMEM((8, 2 * LANES), F32),
        ],
        compiler_params=pltpu.CompilerParams(
            dimension_semantics=("parallel", "parallel", "arbitrary"),
            vmem_limit_bytes=VMEM_LIMIT),
    )(p3, p3, p3, p3, p3, vecs, mu_wa, mu_g, w2p, a2p, g2p)


def _moba_kernel(q_ref, k_ref, v_ref, slope_ref, o_ref, kb_ref, vt_ref, kmean_ref,
                 m_ref, l_ref, acc_ref, s_ref, p_ref, *, nb):
    BS, D = MOBA_BLOCK, MOBA_HD
    i = pl.program_id(2)
    log2e = 1.4426950408889634
    c1 = MOBA_HD ** -0.5 * log2e
    slope2 = slope_ref[0] * log2e
    s_parts = _split(slope2, 3)

    @pl.when(i == 0)
    def _():
        kmean_ref[...] = jnp.zeros_like(kmean_ref)
        lane = _iota2((2 * BS, LANES), 1)
        kpos = _iota2((2 * BS, LANES), 0)
        kpos_lo = (kpos & (BS - 1)).astype(F32)
        kpos_hi = (kpos & BS).astype(F32)
        sk = [x[:, :LANES].astype(F32) for x in s_parts]
        feat = jnp.where(lane == 0, sk[0],
                         jnp.where(lane == 1, sk[1],
                                   jnp.where(lane == 2, sk[2],
                                             jnp.where(lane < 6, kpos_lo,
                                                       jnp.where(lane < 9, kpos_hi, 0.0)))))
        feat = feat.astype(BF16)
        for j in range(nb):
            half = slice((j % 2) * BS, (j % 2 + 1) * BS)
            kj = k_ref[0, pl.ds(j * BS, BS), :]
            kmean_ref[j:j + 1, :] = jnp.mean(kj, axis=0, keepdims=True)
            kb_ref[j // 2, half, :D] = kj.astype(BF16)
            vt_ref[j // 2, :, half] = v_ref[0, pl.ds(j * BS, BS), :].T.astype(BF16)
        for t in range(nb // 2):
            kb_ref[t, :, D:] = feat

    q_t = q_ref[0].T

    nbp = kmean_ref.shape[0]
    blk = _iota2((nbp, BS), 0)
    gate = _mm_split(kmean_ref[...], q_t, a_parts=3, b_parts=3)
    gate = jnp.where(blk < i, gate, NEG_INF)
    sel = []
    for r in range(MOBA_TOPK):
        mx = jnp.max(gate, axis=0, keepdims=True)
        idx = jnp.min(jnp.where(gate == mx, blk, nbp), axis=0, keepdims=True)
        sel.append(jnp.where(r < i, idx, -1))
        gate = jnp.where(blk == idx, -jnp.inf, gate)

    qpos = _iota2((LANES, BS), 1).astype(F32)
    arow = _iota2((LANES, BS), 0)
    sq = [x.astype(F32) for x in s_parts]
    aug = jnp.where(arow < 3, -qpos,
                    jnp.where((arow == 3) | (arow == 6), sq[0],
                              jnp.where((arow == 4) | (arow == 7), sq[1],
                                        jnp.where((arow == 5) | (arow == 8), sq[2], 0.0))))
    q_aug = jnp.concatenate([(q_t * c1).astype(BF16), aug.astype(BF16)], axis=0)

    own_pair = i // 2
    own_half = pl.multiple_of((i % 2) * BS, BS)
    other_half = pl.multiple_of((1 - i % 2) * BS, BS)
    causal = _iota2((BS, BS), 1) >= _iota2((BS, BS), 0)
    s_own = _dot(kb_ref[own_pair, pl.ds(own_half, BS), :], q_aug)
    s_own = jnp.where(causal, s_own - slope2 * own_half.astype(F32), NEG_INF)
    m0 = jnp.max(s_own, axis=0, keepdims=True)
    p0 = jnp.exp2(s_own - m0)
    m_ref[...] = m0
    l_ref[...] = jnp.sum(p0, axis=0, keepdims=True)
    p_ref[pl.ds(own_half, BS), :] = p0.astype(BF16)
    p_ref[pl.ds(other_half, BS), :] = jnp.zeros((BS, BS), BF16)
    acc_ref[...] = jnp.zeros_like(acc_ref)
    s_ref[0] = _dot(kb_ref[0], q_aug)

    def past_pair(t, prev):
        s_cur = s_ref[t % 2]
        pv = _dot(vt_ref[prev], p_ref[...])
        s_ref[(t + 1) % 2] = _dot(kb_ref[jnp.minimum(t + 1, nb // 2 - 1)], q_aug)
        ja, jb = 2 * t, 2 * t + 1
        picked_a = (sel[0] == ja) | (sel[1] == ja) | (sel[2] == ja)
        picked_b = (sel[0] == jb) | (sel[1] == jb) | (sel[2] == jb)
        s_a = jnp.where(picked_a, s_cur[:BS], NEG_INF)
        s_b = jnp.where(picked_b, s_cur[BS:], NEG_INF)
        off = slope2 * ((i - ja) * BS).astype(F32)
        m_old = m_ref[...]
        mx = jnp.maximum(jnp.max(s_a, axis=0, keepdims=True), jnp.max(s_b, axis=0, keepdims=True))
        m_new = jnp.maximum(m_old, mx - off)
        alpha = jnp.exp2(m_old - m_new)
        shift = m_new + off
        p_a = jnp.exp2(s_a - shift)
        p_b = jnp.exp2(s_b - shift)
        l_ref[...] = (alpha * l_ref[...] + jnp.sum(p_a, axis=0, keepdims=True)
                      + jnp.sum(p_b, axis=0, keepdims=True))
        acc_ref[...] = alpha * (acc_ref[...] + pv)
        p_ref[:BS, :] = p_a.astype(BF16)
        p_ref[BS:, :] = p_b.astype(BF16)
        m_ref[...] = m_new
        return t

    last = lax.fori_loop(0, (i + 1) // 2, past_pair, own_pair)
    acc = acc_ref[...] + _dot(vt_ref[last], p_ref[...])
    o_ref[0] = (acc / l_ref[...]).T.astype(o_ref.dtype)


def moba_attention(qkv_h, B, S):
    H, D, BS = MOBA_HEADS, MOBA_HD, MOBA_BLOCK
    nb = S // BS
    assert nb % 2 == 0, "key blocks are stored in pairs"
    nbp = -(-nb // 8) * 8
    slopes = jnp.exp2(-8.0 * jnp.arange(1, H + 1, dtype=F32) / H)
    slopes = jnp.broadcast_to(slopes[:, None, None], (H, 1, BS))
    return pl.pallas_call(
        functools.partial(_moba_kernel, nb=nb),
        grid=(B, H, nb),
        in_specs=[
            pl.BlockSpec((1, BS, D), lambda b, h, i: (h, b * nb + i, 0)),
            pl.BlockSpec((1, S, D), lambda b, h, i: (H + h, b, 0)),
            pl.BlockSpec((1, S, D), lambda b, h, i: (2 * H + h, b, 0)),
            pl.BlockSpec((1, 1, BS), lambda b, h, i: (h, 0, 0)),
        ],
        out_specs=pl.BlockSpec((1, BS, D), lambda b, h, i: (b, i, h)),
        out_shape=jax.ShapeDtypeStruct((B, S, H * D), BF16),
        scratch_shapes=[
            pltpu.VMEM((nb // 2, 2 * BS, D + LANES), BF16),
            pltpu.VMEM((nb // 2, D, 2 * BS), BF16),
            pltpu.VMEM((nbp, D), F32),
            pltpu.VMEM((1, BS), F32),
            pltpu.VMEM((1, BS), F32),
            pltpu.VMEM((D, BS), F32),
            pltpu.VMEM((2, 2 * BS, BS), F32),
            pltpu.VMEM((2 * BS, BS), BF16),
        ],
        compiler_params=pltpu.CompilerParams(
            dimension_semantics=("parallel", "parallel", "arbitrary"),
            vmem_limit_bytes=VMEM_LIMIT),
    )(qkv_h, qkv_h, qkv_h, slopes)


def _pad_cols(w, n):
    return jnp.pad(w, ((0, 0), (0, n - w.shape[1])))


def _pad_rows(w, before, total):
    return jnp.pad(w, ((before, total - before - w.shape[0]), (0, 0)))


def _mix_in_layout(w):
    gk, gw = GLA_KEY_WIDTH, GLA_WIDTH
    g_end = 2 * gk + 2 * gw
    gla_main = w[:, :g_end]
    gla_gate = w[:, g_end:g_end + GLA_GATE_RANK]
    r0 = g_end + GLA_GATE_RANK
    rw_main = w[:, r0:r0 + 3 * RWKV_WIDTH]
    l0 = r0 + 3 * RWKV_WIDTH
    wa = w[:, l0:l0 + RWKV_DECAY_LORA + RWKV_AAA_LORA]
    g0 = l0 + RWKV_DECAY_LORA + RWKV_AAA_LORA
    gl = w[:, g0:g0 + RWKV_GATE_LORA]
    return jnp.concatenate(
        [gla_main, rw_main, _pad_cols(gla_gate, LANES), wa, _pad_cols(gl, 2 * LANES)], axis=1)


def mixer_layer0(hn_proj, gla_gate_w2, gla_gate_b, gla_norm, rwkv_mu, rwkv_w0, rwkv_w2, rwkv_a0,
                 rwkv_a2, rwkv_g2, rwkv_k_k, rwkv_k_a, rwkv_r_k, rwkv_ln_w, rwkv_ln_b,
                 *, gla_block, rwkv_block):
    W = RWKV_WIDTH
    o_gla = gla_mixer(hn_proj, _pad_rows(gla_gate_w2, 0, LANES), gla_gate_b, gla_norm,
                      block=gla_block)
    mu_r, mu_k, mu_v = rwkv_mu[:W], rwkv_mu[W:2 * W], rwkv_mu[2 * W:3 * W]
    mu_low = rwkv_mu[3 * W:]
    n_wa = RWKV_DECAY_LORA + RWKV_AAA_LORA
    vecs = jnp.stack([rwkv_w0, rwkv_a0, rwkv_k_k, rwkv_k_a, rwkv_r_k.reshape(-1), rwkv_ln_w,
                      rwkv_ln_b, mu_r, mu_k, mu_v] + [jnp.zeros((W,), F32)] * 6)
    mu_wa = mu_low[:n_wa].reshape(1, -1)
    mu_g = _pad_cols(mu_low[n_wa:].reshape(1, -1), 2 * LANES)
    w2p = _pad_rows(rwkv_w2, 0, LANES)
    a2p = _pad_rows(rwkv_a2, RWKV_DECAY_LORA, LANES)
    g2p = _pad_rows(rwkv_g2, 0, 2 * LANES)
    o_rwkv = rwkv_mixer(hn_proj, vecs, mu_wa, mu_g, w2p, a2p, g2p, block=rwkv_block)
    return o_gla, o_rwkv


def kernel(x, norm_mix, norm_ffn, norm_final, mix_in_w, gla_gate_w2, gla_gate_b, gla_norm, rwkv_mu, rwkv_w0, rwkv_w2, rwkv_a0, rwkv_a2, rwkv_g2, rwkv_k_k, rwkv_k_a, rwkv_r_k, rwkv_ln_w, rwkv_ln_b, mix_out_w, attn_qkv_w, attn_out_w, ffn_gate_w, ffn_up_w, ffn_down_w):
    B, S, D = x.shape
    M = B * S
    tm, tn, tf = 512, 512, 256
    tm_in = min(1024, M)
    tm_ffn = min(1024, M)
    h = x.reshape(M, D)

    w_in = _mix_in_layout(mix_in_w[0]).astype(BF16)
    p = norm_matmul(h, norm_mix[0], w_in, tm=tm_in, tn=tn).reshape(B, S, MIX_COLS)
    o_gla, o_rwkv = mixer_layer0(
        p, gla_gate_w2[0], gla_gate_b[0], gla_norm[0], rwkv_mu[0], rwkv_w0[0], rwkv_w2[0],
        rwkv_a0[0], rwkv_a2[0], rwkv_g2[0], rwkv_k_k[0], rwkv_k_a[0], rwkv_r_k[0], rwkv_ln_w[0],
        rwkv_ln_b[0], gla_block=512, rwkv_block=512)
    w_out = mix_out_w[0].astype(BF16)
    h = proj_residual(h, [(o_gla.reshape(M, GLA_WIDTH), w_out[:GLA_WIDTH]),
                          (o_rwkv.reshape(M, RWKV_WIDTH), w_out[GLA_WIDTH:])], tm=tm, tn=D)
    h = ffn_residual(h, norm_ffn[0], ffn_gate_w, ffn_up_w, ffn_down_w, norm_final, layer=0,
                     tm=tm_ffn, tf=tf, final_norm=False)

    qkv_h = norm_matmul(h, norm_mix[1], attn_qkv_w[0].astype(BF16), tm=tm_in, tn=tn,
                        group=MOBA_HD)
    o_attn = moba_attention(qkv_h, B, S)
    h = proj_residual(h, [(o_attn.reshape(M, D), attn_out_w[0].astype(BF16))], tm=tm, tn=D)
    h = ffn_residual(h, norm_ffn[1], ffn_gate_w, ffn_up_w, ffn_down_w, norm_final, layer=1,
                     tm=tm_ffn, tf=tf, final_norm=True)
    return h.reshape(B, S, D)
```

```python
import functools

import jax
import jax.numpy as jnp
from jax import lax
from jax.experimental import pallas as pl
from jax.experimental.pallas import tpu as pltpu

F32 = jnp.float32
BF16 = jnp.bfloat16
HI = lax.Precision.HIGHEST

D_MODEL = 2048
RMS_EPS = 1e-6

GLA_HEADS = 4
GLA_DK = 128
GLA_DV = 256
GLA_KEY_WIDTH = GLA_HEADS * GLA_DK
GLA_WIDTH = GLA_HEADS * GLA_DV
GLA_GATE_RANK = 16
GLA_GATE_TAU = 16.0
GLA_CHUNK = 64

RWKV_WIDTH = 1024
RWKV_HEAD = 64
RWKV_PAIR = 2 * RWKV_HEAD
RWKV_PAIRS = RWKV_WIDTH // RWKV_PAIR
RWKV_DECAY_LORA = 64
RWKV_AAA_LORA = 64
RWKV_GATE_LORA = 160
RWKV_LN_EPS = RWKV_HEAD * 1e-5
RWKV_CHUNK = 64

MOBA_HEADS = 16
MOBA_HD = 128
MOBA_BLOCK = 256
MOBA_TOPK = 3
NEG_INF = -1e30

FFN_HIDDEN = 5632

LANES = 128

COL_GLA_Q = 0
COL_GLA_K = 512
COL_GLA_V = 1024
COL_GLA_OG = 2048
COL_RWKV_R = 3072
COL_RWKV_K = 4096
COL_RWKV_V = 5120
COL_LOW_GATE = 6144
COL_LOW_WA = 6272
COL_LOW_G = 6400
MIX_COLS = 6656

VMEM_LIMIT = 56 * 1024 * 1024


def _mm(a, b):
    return jnp.dot(a.astype(BF16), b.astype(BF16), preferred_element_type=F32)


def _mm_nt(a, b):
    return lax.dot_general(a.astype(BF16), b.astype(BF16), (((1,), (1,)), ((), ())),
                           preferred_element_type=F32)


def _mm_hi(a, b):
    return jnp.dot(a, b, preferred_element_type=F32, precision=HI)


def _mm_nt_hi(a, b):
    return lax.dot_general(a, b, (((1,), (1,)), ((), ())), preferred_element_type=F32,
                           precision=HI)


def _dot(a, b):
    return jnp.dot(a, b, preferred_element_type=F32)


def _dot_tn(a, b):
    return lax.dot_general(a, b, (((0,), (0,)), ((), ())), preferred_element_type=F32)


def _split(x, parts):
    out = []
    for _ in range(parts - 1):
        hi = x.astype(BF16)
        out.append(hi)
        x = x - hi.astype(F32)
    out.append(x.astype(BF16))
    return out


def _mm_split(a, b, *, a_parts=1, b_parts=1):
    a_p = _split(a, a_parts) if a_parts > 1 else [a.astype(BF16)]
    b_p = _split(b, b_parts) if b_parts > 1 else [b.astype(BF16)]
    acc = None
    for i, ai in enumerate(a_p):
        for j, bj in enumerate(b_p):
            if i + j < max(a_parts, b_parts):
                t = _dot(ai, bj)
                acc = t if acc is None else acc + t
    return acc


def _sigmoid(x):
    return 1.0 / (1.0 + jnp.exp(-x))


def _softplus(x):
    return jnp.maximum(x, 0.0) + jnp.log1p(jnp.exp(-jnp.abs(x)))


def _iota2(shape, axis):
    return lax.broadcasted_iota(jnp.int32, shape, axis)


def _norm_matmul_kernel(x_ref, g_ref, w_ref, o_ref, xn_ref, *, group):
    @pl.when(pl.program_id(1) == 0)
    def _():
        x = x_ref[...]
        ms = jnp.mean(x * x, axis=-1, keepdims=True)
        xn_ref[...] = (x * lax.rsqrt(ms + RMS_EPS) * g_ref[...]).astype(BF16)

    res = jnp.dot(xn_ref[...], w_ref[...], preferred_element_type=F32).astype(o_ref.dtype)
    if group is None:
        o_ref[...] = res
    else:
        for c in range(res.shape[1] // group):
            o_ref[c] = res[:, c * group:(c + 1) * group]


def norm_matmul(x, g, w, *, tm, tn, out_dtype=F32, group=None):
    M, D = x.shape
    N = w.shape[1]
    if group is None:
        out_specs = pl.BlockSpec((tm, tn), lambda i, j: (i, j))
        out_shape = jax.ShapeDtypeStruct((M, N), out_dtype)
    else:
        out_specs = pl.BlockSpec((tn // group, tm, group), lambda i, j: (j, i, 0))
        out_shape = jax.ShapeDtypeStruct((N // group, M, group), out_dtype)
    return pl.pallas_call(
        functools.partial(_norm_matmul_kernel, group=group),
        grid=(M // tm, N // tn),
        in_specs=[
            pl.BlockSpec((tm, D), lambda i, j: (i, 0)),
            pl.BlockSpec((1, D), lambda i, j: (0, 0)),
            pl.BlockSpec((D, tn), lambda i, j: (0, j)),
        ],
        out_specs=out_specs,
        out_shape=out_shape,
        scratch_shapes=[pltpu.VMEM((tm, D), BF16)],
        compiler_params=pltpu.CompilerParams(
            dimension_semantics=("parallel", "arbitrary"), vmem_limit_bytes=VMEM_LIMIT),
    )(x, g.reshape(1, D), w)


def _proj_res_kernel(*refs, n_in):
    res_ref = refs[0]
    o_ref = refs[1 + 2 * n_in]
    acc = res_ref[...]
    for i in range(n_in):
        acc = acc + jnp.dot(refs[1 + 2 * i][...], refs[2 + 2 * i][...],
                            preferred_element_type=F32)
    o_ref[...] = acc


def proj_residual(res, pairs, *, tm, tn):
    M, N = res.shape
    in_specs = [pl.BlockSpec((tm, tn), lambda i, j: (i, j))]
    args = [res]
    for a, w in pairs:
        K = a.shape[1]
        in_specs.append(pl.BlockSpec((tm, K), lambda i, j: (i, 0)))
        in_specs.append(pl.BlockSpec((K, tn), lambda i, j: (0, j)))
        args += [a, w]
    return pl.pallas_call(
        functools.partial(_proj_res_kernel, n_in=len(pairs)),
        grid=(M // tm, N // tn),
        in_specs=in_specs,
        out_specs=pl.BlockSpec((tm, tn), lambda i, j: (i, j)),
        out_shape=jax.ShapeDtypeStruct((M, N), F32),
        compiler_params=pltpu.CompilerParams(
            dimension_semantics=("parallel", "arbitrary"), vmem_limit_bytes=VMEM_LIMIT),
    )(*args)


def _ffn_kernel(x_ref, g_ref, wg_ref, wu_ref, wd_ref, gf_ref, o_ref, xn_ref, *, final_norm):
    j = pl.program_id(1)

    @pl.when(j == 0)
    def _():
        x = x_ref[...]
        ms = jnp.mean(x * x, axis=-1, keepdims=True)
        xn_ref[...] = (x * lax.rsqrt(ms + RMS_EPS) * g_ref[...]).astype(BF16)
        o_ref[...] = x

    xn = xn_ref[...]
    gate = jnp.dot(xn, wg_ref[...].astype(BF16), preferred_element_type=F32)
    up = jnp.dot(xn, wu_ref[...].astype(BF16), preferred_element_type=F32)
    act = (gate * _sigmoid(gate) * up).astype(BF16)
    o_ref[...] += jnp.dot(act, wd_ref[...].astype(BF16), preferred_element_type=F32)

    if final_norm:
        @pl.when(j == pl.num_programs(1) - 1)
        def _():
            h = o_ref[...]
            ms = jnp.mean(h * h, axis=-1, keepdims=True)
            o_ref[...] = h * lax.rsqrt(ms + RMS_EPS) * gf_ref[...]


def ffn_residual(x, g, wg, wu, wd, g_final, *, layer, tm, tf, final_norm):
    M, D = x.shape
    F = wg.shape[2]
    return pl.pallas_call(
        functools.partial(_ffn_kernel, final_norm=final_norm),
        grid=(M // tm, F // tf),
        in_specs=[
            pl.BlockSpec((tm, D), lambda i, j: (i, 0)),
            pl.BlockSpec((1, D), lambda i, j: (0, 0)),
            pl.BlockSpec((None, D, tf), lambda i, j: (layer, 0, j)),
            pl.BlockSpec((None, D, tf), lambda i, j: (layer, 0, j)),
            pl.BlockSpec((None, tf, D), lambda i, j: (layer, j, 0)),
            pl.BlockSpec((1, D), lambda i, j: (0, 0)),
        ],
        out_specs=pl.BlockSpec((tm, D), lambda i, j: (i, 0)),
        out_shape=jax.ShapeDtypeStruct((M, D), F32),
        scratch_shapes=[pltpu.VMEM((tm, D), BF16)],
        compiler_params=pltpu.CompilerParams(
            dimension_semantics=("parallel", "arbitrary"), vmem_limit_bytes=VMEM_LIMIT),
    )(x, g.reshape(1, D), wg, wu, wd, g_final.reshape(1, D))


def _gla_kernel(q_ref, k_ref, v_ref, og_ref, gl_ref, w2_ref, b_ref, gn_ref, o_ref, st_ref,
                *, block, chunk):
    @pl.when(pl.program_id(2) == 0)
    def _():
        st_ref[...] = jnp.zeros_like(st_ref)

    logit = _mm_split(gl_ref[0], w2_ref[...], a_parts=2, b_parts=2) + b_ref[...]
    log_a = -_softplus(-logit) * (1.0 / GLA_GATE_TAU)
    tri = _iota2((chunk, chunk), 0) >= _iota2((chunk, chunk), 1)
    tri_b = tri.astype(BF16)
    scale = GLA_DK ** -0.5
    gn = gn_ref[...]
    n_chunks = block // chunk

    pre = []
    for c in range(n_chunks):
        sl = pl.ds(c * chunk, chunk)
        la = _split(log_a[c * chunk:(c + 1) * chunk], 3)
        b = _dot(tri_b, la[0]) + _dot(tri_b, la[1]) + _dot(tri_b, la[2])
        b_last = b[chunk - 1:chunk]
        q = q_ref[0, sl, :] * scale
        k = k_ref[0, sl, :]
        v = v_ref[0, sl, :]
        q_dec = (q * jnp.exp(b)).astype(BF16)
        k_inv = k * jnp.exp(-b)
        k_dec = (k * jnp.exp(b_last - b)).astype(BF16)
        att = jnp.where(tri, _mm_nt(q_dec, k_inv), 0.0)
        pre.append(dict(q_dec=q_dec, k_dec=k_dec, v_t=v.T.astype(BF16), o_intra=_mm(att, v),
                        decay=jnp.exp(b_last)))

    st = st_ref[...]
    outs = []
    for p in pre:
        outs.append(p["o_intra"] + _mm_nt(p["q_dec"], st))
        st = st * p["decay"] + _dot(p["v_t"], p["k_dec"])
    st_ref[...] = st

    for c, o in enumerate(outs):
        sl = pl.ds(c * chunk, chunk)
        ms = jnp.mean(o * o, axis=-1, keepdims=True)
        og = og_ref[0, sl, :]
        o_ref[0, sl, :] = (o * lax.rsqrt(ms + RMS_EPS) * gn * (og * _sigmoid(og))).astype(o_ref.dtype)


def gla_mixer(p3, gate_w2p, gate_b, gla_norm, *, block):
    B, S, _ = p3.shape
    kq, kk_, kv, kog = (COL_GLA_Q // GLA_DK, COL_GLA_K // GLA_DK, COL_GLA_V // GLA_DV,
                        COL_GLA_OG // GLA_DV)
    klow = COL_LOW_GATE // LANES
    return pl.pallas_call(
        functools.partial(_gla_kernel, block=block, chunk=GLA_CHUNK),
        grid=(B, GLA_HEADS, S // block),
        in_specs=[
            pl.BlockSpec((1, block, GLA_DK), lambda b, h, s: (b, s, kq + h)),
            pl.BlockSpec((1, block, GLA_DK), lambda b, h, s: (b, s, kk_ + h)),
            pl.BlockSpec((1, block, GLA_DV), lambda b, h, s: (b, s, kv + h)),
            pl.BlockSpec((1, block, GLA_DV), lambda b, h, s: (b, s, kog + h)),
            pl.BlockSpec((1, block, LANES), lambda b, h, s: (b, s, klow)),
            pl.BlockSpec((LANES, GLA_DK), lambda b, h, s: (0, h)),
            pl.BlockSpec((1, GLA_DK), lambda b, h, s: (0, h)),
            pl.BlockSpec((1, GLA_DV), lambda b, h, s: (0, 0)),
        ],
        out_specs=pl.BlockSpec((1, block, GLA_DV), lambda b, h, s: (b, s, h)),
        out_shape=jax.ShapeDtypeStruct((B, S, GLA_WIDTH), BF16),
        scratch_shapes=[pltpu.VMEM((GLA_DV, GLA_DK), F32)],
        compiler_params=pltpu.CompilerParams(
            dimension_semantics=("parallel", "parallel", "arbitrary"),
            vmem_limit_bytes=VMEM_LIMIT),
    )(p3, p3, p3, p3, p3, gate_w2p, gate_b.reshape(1, -1), gla_norm.reshape(1, -1))


def _unit_lower_inverses(mats, rowi, coli):
    def same_block(s):
        sh = s.bit_length() - 1
        return (rowi >> sh) == (coli >> sh)

    eye = (rowi == coli).astype(F32)
    a8 = [jnp.where(same_block(8), a, 0.0).astype(BF16) for a in mats]
    inv = [eye + a.astype(F32) for a in a8]
    p = [_dot(a, a) for a in a8]
    inv = [x + _mm(x, y) for x, y in zip(inv, p)]
    p = [_mm(y, y) for y in p]
    inv = [x + _mm(x, y) for x, y in zip(inv, p)]
    s = 8
    while s < RWKV_CHUNK:
        off = same_block(2 * s) & jnp.logical_not(same_block(s))
        e = [jnp.where(off, a, 0.0).astype(BF16) for a in mats]
        inv_b = [x.astype(BF16) for x in inv]
        t = [_dot(x, y).astype(BF16) for x, y in zip(inv_b, e)]
        inv = [x + _dot(y, z) for x, y, z in zip(inv, t, inv_b)]
        s *= 2
    return inv


def _rwkv_kernel(r_ref, k_ref, v_ref, wa_ref, g_ref, vec_ref, muwa_ref, mug_ref, w2_ref, a2_ref,
                 g2_ref, o_ref, st_ref, prev_ref, prevg_ref, *, block, chunk):
    T = chunk
    first = pl.program_id(2) == 0

    @pl.when(first)
    def _():
        st_ref[...] = jnp.zeros_like(st_ref)
        prev_ref[...] = jnp.zeros_like(prev_ref)
        prevg_ref[...] = jnp.zeros_like(prevg_ref)

    row0 = _iota2((block, 1), 0) == 0

    def lerp(x, prev_row, mu):
        shifted = jnp.where(row0, prev_row, pltpu.roll(x, 1, axis=0))
        return x + (shifted - x) * mu

    vec = vec_ref[...]
    w0, a0, k_k, k_a, r_k, ln_w, ln_b = (vec[i:i + 1] for i in range(7))
    mu_r, mu_k, mu_v = (vec[i:i + 1] for i in range(7, 10))

    r_raw, k_raw, v_raw, wa_raw, g_raw = r_ref[0], k_ref[0], v_ref[0], wa_ref[0], g_ref[0]
    prev = prev_ref[...]
    r_all = lerp(r_raw, prev[0:1], mu_r)
    k_all = lerp(k_raw, prev[1:2], mu_k)
    v_all = lerp(v_raw, prev[2:3], mu_v)
    wa = lerp(wa_raw, prev[3:4], muwa_ref[...])
    g_low = lerp(g_raw, prevg_ref[0:1], mug_ref[...])
    prev_ref[0:1] = r_raw[block - 1:block]
    prev_ref[1:2] = k_raw[block - 1:block]
    prev_ref[2:3] = v_raw[block - 1:block]
    prev_ref[3:4] = wa_raw[block - 1:block]
    prevg_ref[0:1] = g_raw[block - 1:block]

    z = w0 + _mm_split(jnp.tanh(wa), w2_ref[...], a_parts=2, b_parts=2)
    logw_all = -jnp.exp(-_softplus(-z) - 0.5)
    lr_all = _sigmoid(a0 + _mm_split(wa, a2_ref[...], a_parts=2, b_parts=2))
    gate_all = _mm_split(_sigmoid(g_low), g2_ref[...], a_parts=2, b_parts=2)

    lane = _iota2((1, RWKV_PAIR), 1)
    m0 = (lane < RWKV_HEAD).astype(F32)
    m1 = 1.0 - m0
    rowi = _iota2((2 * T, 2 * T), 0)
    coli = _iota2((2 * T, 2 * T), 1)
    strict = rowi > coli
    incl = rowi >= coli
    head_ones = ((rowi < RWKV_HEAD) == (coli < RWKV_HEAD)).astype(BF16)
    head_ones2 = jnp.concatenate([head_ones, head_ones], axis=0)
    tri_t = (_iota2((T, T), 0) >= _iota2((T, T), 1)).astype(BF16)
    tri_t3 = jnp.concatenate([tri_t, tri_t, tri_t], axis=1)
    ones_t3 = jnp.ones((3 * T, RWKV_PAIR), BF16)

    def hsum(x):
        return _dot(jnp.concatenate(_split(x, 2), axis=1), head_ones2)

    def stack(x):
        return jnp.concatenate([x * m0, x * m1], axis=0)

    n_chunks = block // T

    kkp = k_all * k_k
    kk_all = kkp / jnp.maximum(jnp.sqrt(hsum(kkp * kkp)), 1e-12)
    k2_all = k_all * (1.0 + (lr_all - 1.0) * k_a)
    b_all = kk_all * lr_all
    bonus_all = hsum(r_all * k2_all * r_k) * v_all

    pre = []
    for c in range(n_chunks):
        lo, hi = c * T, (c + 1) * T
        r, v, lw = r_all[lo:hi], v_all[lo:hi], logw_all[lo:hi]
        kk, k2, b_vec = kk_all[lo:hi], k2_all[lo:hi], b_all[lo:hi]

        lw3 = jnp.concatenate(_split(lw, 3), axis=0)
        cum = _dot(tri_t3, lw3)
        cum_last = cum[T - 1:T]
        decay = jnp.exp(_dot_tn(lw3, ones_t3))
        w_inv = jnp.exp(-cum)
        w_tail = jnp.exp(cum_last - cum)
        a_s = stack(-kk * jnp.exp(cum - lw)).astype(BF16)
        r_s = stack(r * jnp.exp(cum)).astype(BF16)
        b_s = stack(b_vec * w_inv)
        k_s = stack(k2 * w_inv)
        v_s = stack(v).astype(BF16)
        prod = _mm_nt(jnp.concatenate([a_s, r_s], axis=0),
                      jnp.concatenate([b_s, k_s], axis=0))
        pre.append(dict(
            a_s=a_s, r_s=r_s, v_s=v_s, decay=decay,
            b_tail=stack(b_vec * w_tail).astype(BF16), k_tail=stack(k2 * w_tail).astype(BF16),
            a_ab=jnp.where(strict, prod[:2 * T, :2 * T], 0.0),
            a_ak=jnp.where(strict, prod[:2 * T, 2 * T:], 0.0).astype(BF16),
            a_rb=jnp.where(incl, prod[2 * T:, :2 * T], 0.0).astype(BF16),
            a_rk=jnp.where(incl, prod[2 * T:, 2 * T:], 0.0).astype(BF16)))

    invs = [x.astype(BF16) for x in _unit_lower_inverses([p["a_ab"] for p in pre], rowi, coli)]
    akv = [_dot(p["a_ak"], p["v_s"]).astype(BF16) for p in pre]
    x = [_dot(i, jnp.concatenate([p["a_s"], y], axis=1)).astype(BF16)
         for i, p, y in zip(invs, pre, akv)]
    bx = [_dot_tn(p["b_tail"], y) for p, y in zip(pre, x)]
    rx = [_dot(p["a_rb"], y) for p, y in zip(pre, x)]
    trans = [y[:, :RWKV_PAIR].astype(BF16) for y in bx]
    gain = [y[:, RWKV_PAIR:] + _dot_tn(p["k_tail"], p["v_s"]) for p, y in zip(pre, bx)]
    read = [(p["r_s"].astype(F32) + y[:, :RWKV_PAIR]).astype(BF16) for p, y in zip(pre, rx)]
    y_free = [y[:, RWKV_PAIR:] + _dot(p["a_rk"], p["v_s"]) for p, y in zip(pre, rx)]

    st = st_ref[...]
    ys = []
    for c, p in enumerate(pre):
        st_b = st.astype(BF16)
        y_s = _dot(read[c], st_b) + y_free[c]
        ys.append(y_s[:T] + y_s[T:])
        st = st * p["decay"] + _dot(trans[c], st_b) + gain[c]
    st_ref[...] = st

    y = jnp.concatenate(ys, axis=0)
    mu = hsum(y) * (1.0 / RWKV_HEAD)
    d = y - mu
    var = hsum(d * d) * (1.0 / RWKV_HEAD)
    yn = d * lax.rsqrt(var + RWKV_LN_EPS) * ln_w + ln_b
    o_ref[0] = ((yn + bonus_all) * gate_all).astype(o_ref.dtype)


def rwkv_mixer(p3, vecs, mu_wa, mu_g, w2p, a2p, g2p, *, block):
    B, S, _ = p3.shape
    kr, kk_, kv = COL_RWKV_R // RWKV_PAIR, COL_RWKV_K // RWKV_PAIR, COL_RWKV_V // RWKV_PAIR
    kwa = COL_LOW_WA // LANES
    kg = COL_LOW_G // (2 * LANES)
    return pl.pallas_call(
        functools.partial(_rwkv_kernel, block=block, chunk=RWKV_CHUNK),
        grid=(B, RWKV_PAIRS, S // block),
        in_specs=[
            pl.BlockSpec((1, block, RWKV_PAIR), lambda b, j, s: (b, s, kr + j)),
            pl.BlockSpec((1, block, RWKV_PAIR), lambda b, j, s: (b, s, kk_ + j)),
            pl.BlockSpec((1, block, RWKV_PAIR), lambda b, j, s: (b, s, kv + j)),
            pl.BlockSpec((1, block, LANES), lambda b, j, s: (b, s, kwa)),
            pl.BlockSpec((1, block, 2 * LANES), lambda b, j, s: (b, s, kg)),
            pl.BlockSpec((16, RWKV_PAIR), lambda b, j, s: (0, j)),
            pl.BlockSpec((1, LANES), lambda b, j, s: (0, 0)),
            pl.BlockSpec((1, 2 * LANES), lambda b, j, s: (0, 0)),
            pl.BlockSpec((LANES, RWKV_PAIR), lambda b, j, s: (0, j)),
            pl.BlockSpec((LANES, RWKV_PAIR), lambda b, j, s: (0, j)),
            pl.BlockSpec((2 * LANES, RWKV_PAIR), lambda b, j, s: (0, j)),
        ],
        out_specs=pl.BlockSpec((1, block, RWKV_PAIR), lambda b, j, s: (b, s, j)),
        out_shape=jax.ShapeDtypeStruct((B, S, RWKV_WIDTH), BF16),
        scratch_shapes=[
            pltpu.VMEM((RWKV_PAIR, RWKV_PAIR), F32),
            pltpu.VMEM((8, LANES), F32),
            pltpu.VMEM((8, 2 * LANES), F32),
        ],
        compiler_params=pltpu.CompilerParams(
            dimension_semantics=("parallel", "parallel", "arbitrary"),
            vmem_limit_bytes=VMEM_LIMIT),
    )(p3, p3, p3, p3, p3, vecs, mu_wa, mu_g, w2p, a2p, g2p)


def _moba_kernel(q_ref, k_ref, v_ref, slope_ref, o_ref, kb_ref, vt_ref, kmean_ref,
                 m_ref, l_ref, acc_ref, s_ref, p_ref, *, nb):
    BS, D = MOBA_BLOCK, MOBA_HD
    QT = 2 * BS
    g = pl.program_id(2)
    log2e = 1.4426950408889634
    c1 = MOBA_HD ** -0.5 * log2e
    slope2 = slope_ref[0] * log2e
    s_parts = _split(slope2, 3)

    @pl.when(g == 0)
    def _():
        kmean_ref[...] = jnp.zeros_like(kmean_ref)
        lane = _iota2((QT, LANES), 1)
        kpos = _iota2((QT, LANES), 0)
        kpos_lo = (kpos & (BS - 1)).astype(F32)
        kpos_hi = (kpos & BS).astype(F32)
        sk = [x[:, :LANES].astype(F32) for x in s_parts]
        piece = jnp.where((lane == 0) | (lane == 3), sk[0],
                          jnp.where((lane == 1) | (lane == 4), sk[1], sk[2]))
        feat = jnp.where(lane < 6, piece,
                         jnp.where(lane < 9, kpos_lo, jnp.where(lane < 12, kpos_hi, 0.0)))
        feat = feat.astype(BF16)
        for j in range(nb):
            half = slice((j % 2) * BS, (j % 2 + 1) * BS)
            kj = k_ref[0, pl.ds(j * BS, BS), :]
            kmean_ref[j:j + 1, :] = jnp.mean(kj, axis=0, keepdims=True)
            kb_ref[j // 2, half, :D] = kj.astype(BF16)
            vt_ref[j // 2, :, half] = v_ref[0, pl.ds(j * BS, BS), :].T.astype(BF16)
        for t in range(nb // 2):
            kb_ref[t, :, D:] = feat

    q_t = q_ref[0].T
    qpos = _iota2((1, QT), 1)
    blk_q = 2 * g + (qpos >= BS).astype(jnp.int32)

    nbp = kmean_ref.shape[0]
    blk = _iota2((nbp, QT), 0)
    gate = _mm_split(kmean_ref[...], q_t, a_parts=3, b_parts=3)
    gate = jnp.where(blk < blk_q, gate, NEG_INF)
    sel = []
    for r in range(MOBA_TOPK):
        mx = jnp.max(gate, axis=0, keepdims=True)
        idx = jnp.min(jnp.where(gate == mx, blk, nbp), axis=0, keepdims=True)
        sel.append(jnp.where(r < blk_q, idx, -1))
        gate = jnp.where(blk == idx, -jnp.inf, gate)

    def picked(j):
        return (sel[0] == j) | (sel[1] == j) | (sel[2] == j)

    arow = _iota2((LANES, QT), 0)
    qpos_a = _iota2((LANES, QT), 1)
    qpos_lo = (qpos_a & (BS - 1)).astype(F32)
    qpos_hi = (qpos_a & BS).astype(F32)
    sq = [x.astype(F32) for x in s_parts]
    piece = jnp.where((arow == 6) | (arow == 9), sq[0],
                      jnp.where((arow == 7) | (arow == 10), sq[1], sq[2]))
    aug = jnp.where(arow < 3, -qpos_lo,
                    jnp.where(arow < 6, -qpos_hi, jnp.where(arow < 12, piece, 0.0)))
    q_aug = jnp.concatenate([(q_t * c1).astype(BF16), aug.astype(BF16)], axis=0)

    kp = _iota2((QT, QT), 0)
    qp = _iota2((QT, QT), 1)
    allowed = (kp <= qp) & (((kp >= BS) == (qp >= BS)) | picked(2 * g))
    s_own = jnp.where(allowed, _dot(kb_ref[g], q_aug), NEG_INF)
    m0 = jnp.max(s_own, axis=0, keepdims=True)
    p0 = jnp.exp2(s_own - m0)
    m_ref[...] = m0
    l_ref[...] = jnp.sum(p0, axis=0, keepdims=True)
    p_ref[...] = p0.astype(BF16)
    acc_ref[...] = jnp.zeros_like(acc_ref)
    s_ref[0] = _dot(kb_ref[0], q_aug)

    def past_pair(t, prev):
        s_cur = s_ref[t % 2]
        pv = _dot(vt_ref[prev], p_ref[...])
        s_ref[(t + 1) % 2] = _dot(kb_ref[jnp.minimum(t + 1, nb // 2 - 1)], q_aug)
        s_a = jnp.where(picked(2 * t), s_cur[:BS], NEG_INF)
        s_b = jnp.where(picked(2 * t + 1), s_cur[BS:], NEG_INF)
        off = slope2 * ((g - t) * QT).astype(F32)
        m_old = m_ref[...]
        mx = jnp.maximum(jnp.max(s_a, axis=0, keepdims=True), jnp.max(s_b, axis=0, keepdims=True))
        m_new = jnp.maximum(m_old, mx - off)
        alpha = jnp.exp2(m_old - m_new)
        shift = m_new + off
        p_a = jnp.exp2(s_a - shift)
        p_b = jnp.exp2(s_b - shift)
        l_ref[...] = (alpha * l_ref[...] + jnp.sum(p_a, axis=0, keepdims=True)
                      + jnp.sum(p_b, axis=0, keepdims=True))
        acc_ref[...] = alpha * (acc_ref[...] + pv)
        p_ref[:BS, :] = p_a.astype(BF16)
        p_ref[BS:, :] = p_b.astype(BF16)
        m_ref[...] = m_new
        return t

    last = lax.fori_loop(0, g, past_pair, g)
    acc = acc_ref[...] + _dot(vt_ref[last], p_ref[...])
    o_ref[0] = (acc / l_ref[...]).T.astype(o_ref.dtype)


def moba_attention(qkv_h, B, S):
    H, D, BS = MOBA_HEADS, MOBA_HD, MOBA_BLOCK
    QT = 2 * BS
    nb = S // BS
    assert nb % 2 == 0, "key blocks are stored in pairs"
    nt = nb // 2
    nbp = -(-nb // 8) * 8
    slopes = jnp.exp2(-8.0 * jnp.arange(1, H + 1, dtype=F32) / H)
    slopes = jnp.broadcast_to(slopes[:, None, None], (H, 1, QT))
    return pl.pallas_call(
        functools.partial(_moba_kernel, nb=nb),
        grid=(B, H, nt),
        in_specs=[
            pl.BlockSpec((1, QT, D), lambda b, h, g: (h, b * nt + g, 0)),
            pl.BlockSpec((1, S, D), lambda b, h, g: (H + h, b, 0)),
            pl.BlockSpec((1, S, D), lambda b, h, g: (2 * H + h, b, 0)),
            pl.BlockSpec((1, 1, QT), lambda b, h, g: (h, 0, 0)),
        ],
        out_specs=pl.BlockSpec((1, QT, D), lambda b, h, g: (b, g, h)),
        out_shape=jax.ShapeDtypeStruct((B, S, H * D), BF16),
        scratch_shapes=[
            pltpu.VMEM((nt, QT, D + LANES), BF16),
            pltpu.VMEM((nt, D, QT), BF16),
            pltpu.VMEM((nbp, D), F32),
            pltpu.VMEM((1, QT), F32),
            pltpu.VMEM((1, QT), F32),
            pltpu.VMEM((D, QT), F32),
            pltpu.VMEM((2, QT, QT), F32),
            pltpu.VMEM((QT, QT), BF16),
        ],
        compiler_params=pltpu.CompilerParams(
            dimension_semantics=("parallel", "parallel", "arbitrary"),
            vmem_limit_bytes=VMEM_LIMIT),
    )(qkv_h, qkv_h, qkv_h, slopes)


def _pad_cols(w, n):
    return jnp.pad(w, ((0, 0), (0, n - w.shape[1])))


def _pad_rows(w, before, total):
    return jnp.pad(w, ((before, total - before - w.shape[0]), (0, 0)))


def _mix_in_layout(w):
    gk, gw = GLA_KEY_WIDTH, GLA_WIDTH
    g_end = 2 * gk + 2 * gw
    gla_main = w[:, :g_end]
    gla_gate = w[:, g_end:g_end + GLA_GATE_RANK]
    r0 = g_end + GLA_GATE_RANK
    rw_main = w[:, r0:r0 + 3 * RWKV_WIDTH]
    l0 = r0 + 3 * RWKV_WIDTH
    wa = w[:, l0:l0 + RWKV_DECAY_LORA + RWKV_AAA_LORA]
    g0 = l0 + RWKV_DECAY_LORA + RWKV_AAA_LORA
    gl = w[:, g0:g0 + RWKV_GATE_LORA]
    return jnp.concatenate(
        [gla_main, rw_main, _pad_cols(gla_gate, LANES), wa, _pad_cols(gl, 2 * LANES)], axis=1)


def mixer_layer0(hn_proj, gla_gate_w2, gla_gate_b, gla_norm, rwkv_mu, rwkv_w0, rwkv_w2, rwkv_a0,
                 rwkv_a2, rwkv_g2, rwkv_k_k, rwkv_k_a, rwkv_r_k, rwkv_ln_w, rwkv_ln_b,
                 *, gla_block, rwkv_block):
    W = RWKV_WIDTH
    o_gla = gla_mixer(hn_proj, _pad_rows(gla_gate_w2, 0, LANES), gla_gate_b, gla_norm,
                      block=gla_block)
    mu_r, mu_k, mu_v = rwkv_mu[:W], rwkv_mu[W:2 * W], rwkv_mu[2 * W:3 * W]
    mu_low = rwkv_mu[3 * W:]
    n_wa = RWKV_DECAY_LORA + RWKV_AAA_LORA
    vecs = jnp.stack([rwkv_w0, rwkv_a0, rwkv_k_k, rwkv_k_a, rwkv_r_k.reshape(-1), rwkv_ln_w,
                      rwkv_ln_b, mu_r, mu_k, mu_v] + [jnp.zeros((W,), F32)] * 6)
    mu_wa = mu_low[:n_wa].reshape(1, -1)
    mu_g = _pad_cols(mu_low[n_wa:].reshape(1, -1), 2 * LANES)
    w2p = _pad_rows(rwkv_w2, 0, LANES)
    a2p = _pad_rows(rwkv_a2, RWKV_DECAY_LORA, LANES)
    g2p = _pad_rows(rwkv_g2, 0, 2 * LANES)
    o_rwkv = rwkv_mixer(hn_proj, vecs, mu_wa, mu_g, w2p, a2p, g2p, block=rwkv_block)
    return o_gla, o_rwkv


def kernel(x, norm_mix, norm_ffn, norm_final, mix_in_w, gla_gate_w2, gla_gate_b, gla_norm, rwkv_mu, rwkv_w0, rwkv_w2, rwkv_a0, rwkv_a2, rwkv_g2, rwkv_k_k, rwkv_k_a, rwkv_r_k, rwkv_ln_w, rwkv_ln_b, mix_out_w, attn_qkv_w, attn_out_w, ffn_gate_w, ffn_up_w, ffn_down_w):
    B, S, D = x.shape
    M = B * S
    tm, tn, tf = 512, 512, 256
    tm_in = min(1024, M)
    tm_ffn = min(1024, M)
    h = x.reshape(M, D)

    w_in = _mix_in_layout(mix_in_w[0]).astype(BF16)
    p = norm_matmul(h, norm_mix[0], w_in, tm=tm_in, tn=tn).reshape(B, S, MIX_COLS)
    o_gla, o_rwkv = mixer_layer0(
        p, gla_gate_w2[0], gla_gate_b[0], gla_norm[0], rwkv_mu[0], rwkv_w0[0], rwkv_w2[0],
        rwkv_a0[0], rwkv_a2[0], rwkv_g2[0], rwkv_k_k[0], rwkv_k_a[0], rwkv_r_k[0], rwkv_ln_w[0],
        rwkv_ln_b[0], gla_block=512, rwkv_block=512)
    w_out = mix_out_w[0].astype(BF16)
    h = proj_residual(h, [(o_gla.reshape(M, GLA_WIDTH), w_out[:GLA_WIDTH]),
                          (o_rwkv.reshape(M, RWKV_WIDTH), w_out[GLA_WIDTH:])], tm=tm, tn=D)
    h = ffn_residual(h, norm_ffn[0], ffn_gate_w, ffn_up_w, ffn_down_w, norm_final, layer=0,
                     tm=tm_ffn, tf=tf, final_norm=False)

    qkv_h = norm_matmul(h, norm_mix[1], attn_qkv_w[0].astype(BF16), tm=tm_in, tn=tn,
                        group=MOBA_HD)
    o_attn = moba_attention(qkv_h, B, S)
    h = proj_residual(h, [(o_attn.reshape(M, D), attn_out_w[0].astype(BF16))], tm=tm, tn=D)
    h = ffn_residual(h, norm_ffn[1], ffn_gate_w, ffn_up_w, ffn_down_w, norm_final, layer=1,
                     tm=tm_ffn, tf=tf, final_norm=True)
    return h.reshape(B, S, D)
```

```python
import functools

import jax
import jax.numpy as jnp
from jax import lax
from jax.experimental import pallas as pl
from jax.experimental.pallas import tpu as pltpu

F32 = jnp.float32
BF16 = jnp.bfloat16
HI = lax.Precision.HIGHEST

D_MODEL = 2048
RMS_EPS = 1e-6

GLA_HEADS = 4
GLA_DK = 128
GLA_DV = 256
GLA_KEY_WIDTH = GLA_HEADS * GLA_DK
GLA_WIDTH = GLA_HEADS * GLA_DV
GLA_GATE_RANK = 16
GLA_GATE_TAU = 16.0
GLA_CHUNK = 64

RWKV_WIDTH = 1024
RWKV_HEAD = 64
RWKV_PAIR = 2 * RWKV_HEAD
RWKV_PAIRS = RWKV_WIDTH // RWKV_PAIR
RWKV_DECAY_LORA = 64
RWKV_AAA_LORA = 64
RWKV_GATE_LORA = 160
RWKV_LN_EPS = RWKV_HEAD * 1e-5
RWKV_CHUNK = 64

MOBA_HEADS = 16
MOBA_HD = 128
MOBA_BLOCK = 256
MOBA_TOPK = 3
NEG_INF = -1e30

FFN_HIDDEN = 5632

LANES = 128

COL_GLA_Q = 0
COL_GLA_K = 512
COL_GLA_V = 1024
COL_GLA_OG = 2048
COL_RWKV_R = 3072
COL_RWKV_K = 4096
COL_RWKV_V = 5120
COL_LOW_GATE = 6144
COL_LOW_WA = 6272
COL_LOW_G = 6400
MIX_COLS = 6656

VMEM_LIMIT = 56 * 1024 * 1024


def _mm(a, b):
    return jnp.dot(a.astype(BF16), b.astype(BF16), preferred_element_type=F32)


def _mm_nt(a, b):
    return lax.dot_general(a.astype(BF16), b.astype(BF16), (((1,), (1,)), ((), ())),
                           preferred_element_type=F32)


def _mm_hi(a, b):
    return jnp.dot(a, b, preferred_element_type=F32, precision=HI)


def _mm_nt_hi(a, b):
    return lax.dot_general(a, b, (((1,), (1,)), ((), ())), preferred_element_type=F32,
                           precision=HI)


def _dot(a, b):
    return jnp.dot(a, b, preferred_element_type=F32)


def _dot_tn(a, b):
    return lax.dot_general(a, b, (((0,), (0,)), ((), ())), preferred_element_type=F32)


def _split(x, parts):
    out = []
    for _ in range(parts - 1):
        hi = x.astype(BF16)
        out.append(hi)
        x = x - hi.astype(F32)
    out.append(x.astype(BF16))
    return out


def _mm_split(a, b, *, a_parts=1, b_parts=1):
    a_p = _split(a, a_parts) if a_parts > 1 else [a.astype(BF16)]
    b_p = _split(b, b_parts) if b_parts > 1 else [b.astype(BF16)]
    acc = None
    for i, ai in enumerate(a_p):
        for j, bj in enumerate(b_p):
            if i + j < max(a_parts, b_parts):
                t = _dot(ai, bj)
                acc = t if acc is None else acc + t
    return acc


def _sigmoid(x):
    return 1.0 / (1.0 + jnp.exp(-x))


def _softplus(x):
    return jnp.maximum(x, 0.0) + jnp.log1p(jnp.exp(-jnp.abs(x)))


def _iota2(shape, axis):
    return lax.broadcasted_iota(jnp.int32, shape, axis)


def _norm_matmul_kernel(x_ref, g_ref, w_ref, o_ref, xn_ref, *, group):
    @pl.when(pl.program_id(1) == 0)
    def _():
        x = x_ref[...]
        ms = jnp.mean(x * x, axis=-1, keepdims=True)
        xn_ref[...] = (x * lax.rsqrt(ms + RMS_EPS) * g_ref[...]).astype(BF16)

    res = jnp.dot(xn_ref[...], w_ref[...], preferred_element_type=F32).astype(o_ref.dtype)
    if group is None:
        o_ref[...] = res
    else:
        for c in range(res.shape[1] // group):
            o_ref[c] = res[:, c * group:(c + 1) * group]


def norm_matmul(x, g, w, *, tm, tn, out_dtype=F32, group=None):
    M, D = x.shape
    N = w.shape[1]
    if group is None:
        out_specs = pl.BlockSpec((tm, tn), lambda i, j: (i, j))
        out_shape = jax.ShapeDtypeStruct((M, N), out_dtype)
    else:
        out_specs = pl.BlockSpec((tn // group, tm, group), lambda i, j: (j, i, 0))
        out_shape = jax.ShapeDtypeStruct((N // group, M, group), out_dtype)
    return pl.pallas_call(
        functools.partial(_norm_matmul_kernel, group=group),
        grid=(M // tm, N // tn),
        in_specs=[
            pl.BlockSpec((tm, D), lambda i, j: (i, 0)),
            pl.BlockSpec((1, D), lambda i, j: (0, 0)),
            pl.BlockSpec((D, tn), lambda i, j: (0, j)),
        ],
        out_specs=out_specs,
        out_shape=out_shape,
        scratch_shapes=[pltpu.VMEM((tm, D), BF16)],
        compiler_params=pltpu.CompilerParams(
            dimension_semantics=("parallel", "arbitrary"), vmem_limit_bytes=VMEM_LIMIT),
    )(x, g.reshape(1, D), w)


def _proj_res_kernel(*refs, n_in):
    res_ref = refs[0]
    o_ref = refs[1 + 2 * n_in]
    acc = res_ref[...]
    for i in range(n_in):
        acc = acc + jnp.dot(refs[1 + 2 * i][...], refs[2 + 2 * i][...],
                            preferred_element_type=F32)
    o_ref[...] = acc


def proj_residual(res, pairs, *, tm, tn):
    M, N = res.shape
    in_specs = [pl.BlockSpec((tm, tn), lambda i, j: (i, j))]
    args = [res]
    for a, w in pairs:
        K = a.shape[1]
        in_specs.append(pl.BlockSpec((tm, K), lambda i, j: (i, 0)))
        in_specs.append(pl.BlockSpec((K, tn), lambda i, j: (0, j)))
        args += [a, w]
    return pl.pallas_call(
        functools.partial(_proj_res_kernel, n_in=len(pairs)),
        grid=(M // tm, N // tn),
        in_specs=in_specs,
        out_specs=pl.BlockSpec((tm, tn), lambda i, j: (i, j)),
        out_shape=jax.ShapeDtypeStruct((M, N), F32),
        compiler_params=pltpu.CompilerParams(
            dimension_semantics=("parallel", "arbitrary"), vmem_limit_bytes=VMEM_LIMIT),
    )(*args)


def _ffn_kernel(x_ref, g_ref, wg_ref, wu_ref, wd_ref, gf_ref, o_ref, xn_ref, *, final_norm):
    j = pl.program_id(1)

    @pl.when(j == 0)
    def _():
        x = x_ref[...]
        ms = jnp.mean(x * x, axis=-1, keepdims=True)
        xn_ref[...] = (x * lax.rsqrt(ms + RMS_EPS) * g_ref[...]).astype(BF16)
        o_ref[...] = x

    xn = xn_ref[...]
    gate = jnp.dot(xn, wg_ref[...].astype(BF16), preferred_element_type=F32)
    up = jnp.dot(xn, wu_ref[...].astype(BF16), preferred_element_type=F32)
    act = (gate * _sigmoid(gate) * up).astype(BF16)
    o_ref[...] += jnp.dot(act, wd_ref[...].astype(BF16), preferred_element_type=F32)

    if final_norm:
        @pl.when(j == pl.num_programs(1) - 1)
        def _():
            h = o_ref[...]
            ms = jnp.mean(h * h, axis=-1, keepdims=True)
            o_ref[...] = h * lax.rsqrt(ms + RMS_EPS) * gf_ref[...]


def ffn_residual(x, g, wg, wu, wd, g_final, *, layer, tm, tf, final_norm):
    M, D = x.shape
    F = wg.shape[2]
    return pl.pallas_call(
        functools.partial(_ffn_kernel, final_norm=final_norm),
        grid=(M // tm, F // tf),
        in_specs=[
            pl.BlockSpec((tm, D), lambda i, j: (i, 0)),
            pl.BlockSpec((1, D), lambda i, j: (0, 0)),
            pl.BlockSpec((None, D, tf), lambda i, j: (layer, 0, j)),
            pl.BlockSpec((None, D, tf), lambda i, j: (layer, 0, j)),
            pl.BlockSpec((None, tf, D), lambda i, j: (layer, j, 0)),
            pl.BlockSpec((1, D), lambda i, j: (0, 0)),
        ],
        out_specs=pl.BlockSpec((tm, D), lambda i, j: (i, 0)),
        out_shape=jax.ShapeDtypeStruct((M, D), F32),
        scratch_shapes=[pltpu.VMEM((tm, D), BF16)],
        compiler_params=pltpu.CompilerParams(
            dimension_semantics=("parallel", "arbitrary"), vmem_limit_bytes=VMEM_LIMIT),
    )(x, g.reshape(1, D), wg, wu, wd, g_final.reshape(1, D))


def _gla_kernel(q_ref, k_ref, v_ref, og_ref, gl_ref, w2_ref, b_ref, gn_ref, o_ref, st_ref,
                *, block, chunk):
    @pl.when(pl.program_id(2) == 0)
    def _():
        st_ref[...] = jnp.zeros_like(st_ref)

    logit = _mm_split(gl_ref[0], w2_ref[...], a_parts=2, b_parts=2) + b_ref[...]
    log_a = -_softplus(-logit) * (1.0 / GLA_GATE_TAU)
    tri = _iota2((chunk, chunk), 0) >= _iota2((chunk, chunk), 1)
    tri_b = tri.astype(BF16)
    scale = GLA_DK ** -0.5
    gn = gn_ref[...]
    n_chunks = block // chunk

    pre = []
    for c in range(n_chunks):
        sl = pl.ds(c * chunk, chunk)
        la = _split(log_a[c * chunk:(c + 1) * chunk], 3)
        b = _dot(tri_b, la[0]) + _dot(tri_b, la[1]) + _dot(tri_b, la[2])
        b_last = b[chunk - 1:chunk]
        q = q_ref[0, sl, :] * scale
        k = k_ref[0, sl, :]
        v = v_ref[0, sl, :]
        q_dec = (q * jnp.exp(b)).astype(BF16)
        k_inv = k * jnp.exp(-b)
        k_dec = (k * jnp.exp(b_last - b)).astype(BF16)
        att = jnp.where(tri, _mm_nt(q_dec, k_inv), 0.0)
        pre.append(dict(q_dec=q_dec, k_dec=k_dec, v_t=v.T.astype(BF16), o_intra=_mm(att, v),
                        decay=jnp.exp(b_last)))

    st = st_ref[...]
    outs = []
    for p in pre:
        outs.append(p["o_intra"] + _mm_nt(p["q_dec"], st))
        st = st * p["decay"] + _dot(p["v_t"], p["k_dec"])
    st_ref[...] = st

    for c, o in enumerate(outs):
        sl = pl.ds(c * chunk, chunk)
        ms = jnp.mean(o * o, axis=-1, keepdims=True)
        og = og_ref[0, sl, :]
        o_ref[0, sl, :] = (o * lax.rsqrt(ms + RMS_EPS) * gn * (og * _sigmoid(og))).astype(o_ref.dtype)


def gla_mixer(p3, gate_w2p, gate_b, gla_norm, *, block):
    B, S, _ = p3.shape
    kq, kk_, kv, kog = (COL_GLA_Q // GLA_DK, COL_GLA_K // GLA_DK, COL_GLA_V // GLA_DV,
                        COL_GLA_OG // GLA_DV)
    klow = COL_LOW_GATE // LANES
    return pl.pallas_call(
        functools.partial(_gla_kernel, block=block, chunk=GLA_CHUNK),
        grid=(B, GLA_HEADS, S // block),
        in_specs=[
            pl.BlockSpec((1, block, GLA_DK), lambda b, h, s: (b, s, kq + h)),
            pl.BlockSpec((1, block, GLA_DK), lambda b, h, s: (b, s, kk_ + h)),
            pl.BlockSpec((1, block, GLA_DV), lambda b, h, s: (b, s, kv + h)),
            pl.BlockSpec((1, block, GLA_DV), lambda b, h, s: (b, s, kog + h)),
            pl.BlockSpec((1, block, LANES), lambda b, h, s: (b, s, klow)),
            pl.BlockSpec((LANES, GLA_DK), lambda b, h, s: (0, h)),
            pl.BlockSpec((1, GLA_DK), lambda b, h, s: (0, h)),
            pl.BlockSpec((1, GLA_DV), lambda b, h, s: (0, 0)),
        ],
        out_specs=pl.BlockSpec((1, block, GLA_DV), lambda b, h, s: (b, s, h)),
        out_shape=jax.ShapeDtypeStruct((B, S, GLA_WIDTH), BF16),
        scratch_shapes=[pltpu.VMEM((GLA_DV, GLA_DK), F32)],
        compiler_params=pltpu.CompilerParams(
            dimension_semantics=("parallel", "parallel", "arbitrary"),
            vmem_limit_bytes=VMEM_LIMIT),
    )(p3, p3, p3, p3, p3, gate_w2p, gate_b.reshape(1, -1), gla_norm.reshape(1, -1))


def _unit_lower_inverses(mats, rowi, coli):
    def same_block(s):
        sh = s.bit_length() - 1
        return (rowi >> sh) == (coli >> sh)

    eye = (rowi == coli).astype(F32)
    a8 = [jnp.where(same_block(8), a, 0.0).astype(BF16) for a in mats]
    inv = [eye + a.astype(F32) for a in a8]
    p = [_dot(a, a) for a in a8]
    inv = [x + _mm(x, y) for x, y in zip(inv, p)]
    p = [_mm(y, y) for y in p]
    inv = [x + _mm(x, y) for x, y in zip(inv, p)]
    s = 8
    while s < RWKV_CHUNK:
        off = same_block(2 * s) & jnp.logical_not(same_block(s))
        e = [jnp.where(off, a, 0.0).astype(BF16) for a in mats]
        inv_b = [x.astype(BF16) for x in inv]
        t = [_dot(x, y).astype(BF16) for x, y in zip(inv_b, e)]
        inv = [x + _dot(y, z) for x, y, z in zip(inv, t, inv_b)]
        s *= 2
    return inv


def _rwkv_kernel(r_ref, k_ref, v_ref, wa_ref, g_ref, vec_ref, muwa_ref, mug_ref, w2_ref, a2_ref,
                 g2_ref, o_ref, st_ref, prev_ref, prevwa_ref, prevg_ref, *, block, chunk):
    T = chunk
    n_pairs = r_ref.shape[2] // RWKV_PAIR
    first = pl.program_id(2) == 0

    @pl.when(first)
    def _():
        st_ref[...] = jnp.zeros_like(st_ref)
        prev_ref[...] = jnp.zeros_like(prev_ref)
        prevwa_ref[...] = jnp.zeros_like(prevwa_ref)
        prevg_ref[...] = jnp.zeros_like(prevg_ref)

    row0 = _iota2((block, 1), 0) == 0

    def lerp(x, prev_row, mu):
        shifted = jnp.where(row0, prev_row, pltpu.roll(x, 1, axis=0))
        return x + (shifted - x) * mu

    vec = vec_ref[...]
    w0, a0, k_k, k_a, r_k, ln_w, ln_b = (vec[i:i + 1] for i in range(7))
    mu_r, mu_k, mu_v = (vec[i:i + 1] for i in range(7, 10))

    r_raw, k_raw, v_raw, wa_raw, g_raw = r_ref[0], k_ref[0], v_ref[0], wa_ref[0], g_ref[0]
    prev = prev_ref[...]
    r_all = lerp(r_raw, prev[0:1], mu_r)
    k_all = lerp(k_raw, prev[1:2], mu_k)
    v_all = lerp(v_raw, prev[2:3], mu_v)
    wa = lerp(wa_raw, prevwa_ref[0:1], muwa_ref[...])
    g_low = lerp(g_raw, prevg_ref[0:1], mug_ref[...])
    prev_ref[0:1] = r_raw[block - 1:block]
    prev_ref[1:2] = k_raw[block - 1:block]
    prev_ref[2:3] = v_raw[block - 1:block]
    prevwa_ref[0:1] = wa_raw[block - 1:block]
    prevg_ref[0:1] = g_raw[block - 1:block]

    z = w0 + _mm_split(jnp.tanh(wa), w2_ref[...], a_parts=2, b_parts=2)
    logw_all = -jnp.exp(-_softplus(-z) - 0.5)
    lr_all = _sigmoid(a0 + _mm_split(wa, a2_ref[...], a_parts=2, b_parts=2))
    gate_all = _mm_split(_sigmoid(g_low), g2_ref[...], a_parts=2, b_parts=2)

    lane = _iota2((1, RWKV_PAIR), 1)
    m0 = (lane < RWKV_HEAD).astype(F32)
    m1 = 1.0 - m0
    rowi = _iota2((2 * T, 2 * T), 0)
    coli = _iota2((2 * T, 2 * T), 1)
    strict = rowi > coli
    incl = rowi >= coli
    head_ones = ((rowi < RWKV_HEAD) == (coli < RWKV_HEAD)).astype(BF16)
    head_ones2 = jnp.concatenate([head_ones, head_ones], axis=0)
    tri_t = (_iota2((T, T), 0) >= _iota2((T, T), 1)).astype(BF16)
    tri_t3 = jnp.concatenate([tri_t, tri_t, tri_t], axis=1)
    ones_t3 = jnp.ones((3 * T, RWKV_PAIR), BF16)

    def hsum(x):
        return _dot(jnp.concatenate(_split(x, 2), axis=1), head_ones2)

    def stack(x):
        return jnp.concatenate([x * m0, x * m1], axis=0)

    n_chunks = block // T

    pre = []
    bonus = []
    for pair in range(n_pairs):
        ln = slice(pair * RWKV_PAIR, (pair + 1) * RWKV_PAIR)
        r_p, k_p, v_p, lr_p = r_all[:, ln], k_all[:, ln], v_all[:, ln], lr_all[:, ln]
        kkp = k_p * k_k[:, ln]
        kk_p = kkp / jnp.maximum(jnp.sqrt(hsum(kkp * kkp)), 1e-12)
        k2_p = k_p * (1.0 + (lr_p - 1.0) * k_a[:, ln])
        b_p = kk_p * lr_p
        bonus.append(hsum(r_p * k2_p * r_k[:, ln]) * v_p)
        for c in range(n_chunks):
            lo, hi = c * T, (c + 1) * T
            r, v, lw = r_p[lo:hi], v_p[lo:hi], logw_all[lo:hi, ln]
            kk, k2, b_vec = kk_p[lo:hi], k2_p[lo:hi], b_p[lo:hi]

            lw3 = jnp.concatenate(_split(lw, 3), axis=0)
            cum = _dot(tri_t3, lw3)
            cum_last = cum[T - 1:T]
            decay = jnp.exp(_dot_tn(lw3, ones_t3))
            w_inv = jnp.exp(-cum)
            w_tail = jnp.exp(cum_last - cum)
            a_s = stack(-kk * jnp.exp(cum - lw)).astype(BF16)
            r_s = stack(r * jnp.exp(cum)).astype(BF16)
            b_s = stack(b_vec * w_inv)
            k_s = stack(k2 * w_inv)
            v_s = stack(v).astype(BF16)
            prod = _mm_nt(jnp.concatenate([a_s, r_s], axis=0),
                          jnp.concatenate([b_s, k_s], axis=0))
            pre.append(dict(
                a_s=a_s, r_s=r_s, v_s=v_s, decay=decay,
                b_tail=stack(b_vec * w_tail).astype(BF16),
                k_tail=stack(k2 * w_tail).astype(BF16),
                a_ab=jnp.where(strict, prod[:2 * T, :2 * T], 0.0),
                a_ak=jnp.where(strict, prod[:2 * T, 2 * T:], 0.0).astype(BF16),
                a_rb=jnp.where(incl, prod[2 * T:, :2 * T], 0.0).astype(BF16),
                a_rk=jnp.where(incl, prod[2 * T:, 2 * T:], 0.0).astype(BF16)))

    invs = [x.astype(BF16) for x in _unit_lower_inverses([p["a_ab"] for p in pre], rowi, coli)]
    akv = [_dot(p["a_ak"], p["v_s"]).astype(BF16) for p in pre]
    x = [_dot(i, jnp.concatenate([p["a_s"], y], axis=1)).astype(BF16)
         for i, p, y in zip(invs, pre, akv)]
    bx = [_dot_tn(p["b_tail"], y) for p, y in zip(pre, x)]
    rx = [_dot(p["a_rb"], y) for p, y in zip(pre, x)]
    trans = [y[:, :RWKV_PAIR].astype(BF16) for y in bx]
    gain = [y[:, RWKV_PAIR:] + _dot_tn(p["k_tail"], p["v_s"]) for p, y in zip(pre, bx)]
    read = [(p["r_s"].astype(F32) + y[:, :RWKV_PAIR]).astype(BF16) for p, y in zip(pre, rx)]
    y_free = [y[:, RWKV_PAIR:] + _dot(p["a_rk"], p["v_s"]) for p, y in zip(pre, rx)]

    st = [st_ref[pair] for pair in range(n_pairs)]
    ys = [[] for _ in range(n_pairs)]
    for c in range(n_chunks):
        for pair in range(n_pairs):
            i = pair * n_chunks + c
            st_b = st[pair].astype(BF16)
            y_s = _dot(read[i], st_b) + y_free[i]
            ys[pair].append(y_s[:T] + y_s[T:])
            st[pair] = st[pair] * pre[i]["decay"] + _dot(trans[i], st_b) + gain[i]
    for pair in range(n_pairs):
        st_ref[pair] = st[pair]

    for pair in range(n_pairs):
        ln = slice(pair * RWKV_PAIR, (pair + 1) * RWKV_PAIR)
        y = jnp.concatenate(ys[pair], axis=0)
        mu = hsum(y) * (1.0 / RWKV_HEAD)
        d = y - mu
        var = hsum(d * d) * (1.0 / RWKV_HEAD)
        yn = d * lax.rsqrt(var + RWKV_LN_EPS) * ln_w[:, ln] + ln_b[:, ln]
        o_ref[0, :, ln] = ((yn + bonus[pair]) * gate_all[:, ln]).astype(o_ref.dtype)


def rwkv_mixer(p3, vecs, mu_wa, mu_g, w2p, a2p, g2p, *, block, pairs):
    B, S, _ = p3.shape
    W = pairs * RWKV_PAIR
    kr, kk_, kv = COL_RWKV_R // W, COL_RWKV_K // W, COL_RWKV_V // W
    kwa = COL_LOW_WA // LANES
    kg = COL_LOW_G // (2 * LANES)
    return pl.pallas_call(
        functools.partial(_rwkv_kernel, block=block, chunk=RWKV_CHUNK),
        grid=(B, RWKV_PAIRS // pairs, S // block),
        in_specs=[
            pl.BlockSpec((1, block, W), lambda b, j, s: (b, s, kr + j)),
            pl.BlockSpec((1, block, W), lambda b, j, s: (b, s, kk_ + j)),
            pl.BlockSpec((1, block, W), lambda b, j, s: (b, s, kv + j)),
            pl.BlockSpec((1, block, LANES), lambda b, j, s: (b, s, kwa)),
            pl.BlockSpec((1, block, 2 * LANES), lambda b, j, s: (b, s, kg)),
            pl.BlockSpec((16, W), lambda b, j, s: (0, j)),
            pl.BlockSpec((1, LANES), lambda b, j, s: (0, 0)),
            pl.BlockSpec((1, 2 * LANES), lambda b, j, s: (0, 0)),
            pl.BlockSpec((LANES, W), lambda b, j, s: (0, j)),
            pl.BlockSpec((LANES, W), lambda b, j, s: (0, j)),
            pl.BlockSpec((2 * LANES, W), lambda b, j, s: (0, j)),
        ],
        out_specs=pl.BlockSpec((1, block, W), lambda b, j, s: (b, s, j)),
        out_shape=jax.ShapeDtypeStruct((B, S, RWKV_WIDTH), BF16),
        scratch_shapes=[
            pltpu.VMEM((pairs, RWKV_PAIR, RWKV_PAIR), F32),
            pltpu.VMEM((8, W), F32),
            pltpu.VMEM((8, LANES), F32),
            pltpu.VMEM((8, 2 * LANES), F32),
        ],
        compiler_params=pltpu.CompilerParams(
            dimension_semantics=("parallel", "parallel", "arbitrary"),
            vmem_limit_bytes=VMEM_LIMIT),
    )(p3, p3, p3, p3, p3, vecs, mu_wa, mu_g, w2p, a2p, g2p)


def _moba_kernel(q_ref, k_ref, v_ref, slope_ref, o_ref, kb_ref, vt_ref, kmean_ref,
                 m_ref, l_ref, acc_ref, s_ref, p_ref, *, nb):
    BS, D = MOBA_BLOCK, MOBA_HD
    QT = 2 * BS
    g = pl.program_id(2)
    log2e = 1.4426950408889634
    c1 = MOBA_HD ** -0.5 * log2e
    slope2 = slope_ref[0] * log2e
    s_parts = _split(slope2, 3)

    @pl.when(g == 0)
    def _():
        kmean_ref[...] = jnp.zeros_like(kmean_ref)
        lane = _iota2((QT, LANES), 1)
        kpos = _iota2((QT, LANES), 0)
        kpos_lo = (kpos & (BS - 1)).astype(F32)
        kpos_hi = (kpos & BS).astype(F32)
        sk = [x[:, :LANES].astype(F32) for x in s_parts]
        piece = jnp.where((lane == 0) | (lane == 3), sk[0],
                          jnp.where((lane == 1) | (lane == 4), sk[1], sk[2]))
        feat = jnp.where(lane < 6, piece,
                         jnp.where(lane < 9, kpos_lo, jnp.where(lane < 12, kpos_hi, 0.0)))
        feat = feat.astype(BF16)
        for j in range(nb):
            half = slice((j % 2) * BS, (j % 2 + 1) * BS)
            kj = k_ref[0, pl.ds(j * BS, BS), :]
            kmean_ref[j:j + 1, :] = jnp.mean(kj, axis=0, keepdims=True)
            kb_ref[j // 2, half, :D] = kj.astype(BF16)
            vt_ref[j // 2, :, half] = v_ref[0, pl.ds(j * BS, BS), :].T.astype(BF16)
        for t in range(nb // 2):
            kb_ref[t, :, D:] = feat

    q_t = q_ref[0].T
    qpos = _iota2((1, QT), 1)
    blk_q = 2 * g + (qpos >= BS).astype(jnp.int32)

    nbp = kmean_ref.shape[0]
    blk = _iota2((nbp, QT), 0)
    gate = _mm_split(kmean_ref[...], q_t, a_parts=3, b_parts=3)
    gate = jnp.where(blk < blk_q, gate, NEG_INF)
    sel = []
    for r in range(MOBA_TOPK):
        mx = jnp.max(gate, axis=0, keepdims=True)
        idx = jnp.min(jnp.where(gate == mx, blk, nbp), axis=0, keepdims=True)
        sel.append(jnp.where(r < blk_q, idx, -1))
        gate = jnp.where(blk == idx, -jnp.inf, gate)

    def picked(j):
        return (sel[0] == j) | (sel[1] == j) | (sel[2] == j)

    arow = _iota2((LANES, QT), 0)
    qpos_a = _iota2((LANES, QT), 1)
    qpos_lo = (qpos_a & (BS - 1)).astype(F32)
    qpos_hi = (qpos_a & BS).astype(F32)
    sq = [x.astype(F32) for x in s_parts]
    piece = jnp.where((arow == 6) | (arow == 9), sq[0],
                      jnp.where((arow == 7) | (arow == 10), sq[1], sq[2]))
    aug = jnp.where(arow < 3, -qpos_lo,
                    jnp.where(arow < 6, -qpos_hi, jnp.where(arow < 12, piece, 0.0)))
    q_aug = jnp.concatenate([(q_t * c1).astype(BF16), aug.astype(BF16)], axis=0)

    kp = _iota2((QT, QT), 0)
    qp = _iota2((QT, QT), 1)
    allowed = (kp <= qp) & (((kp >= BS) == (qp >= BS)) | picked(2 * g))
    s_own = jnp.where(allowed, _dot(kb_ref[g], q_aug), NEG_INF)
    m0 = jnp.max(s_own, axis=0, keepdims=True)
    p0 = jnp.exp2(s_own - m0)
    m_ref[...] = m0
    l_ref[...] = jnp.sum(p0, axis=0, keepdims=True)
    p_ref[...] = p0.astype(BF16)
    acc_ref[...] = jnp.zeros_like(acc_ref)
    s_ref[0] = _dot(kb_ref[0], q_aug)

    def past_pair(t, prev):
        s_cur = s_ref[t % 2]
        pv = _dot(vt_ref[prev], p_ref[...])
        s_ref[(t + 1) % 2] = _dot(kb_ref[jnp.minimum(t + 1, nb // 2 - 1)], q_aug)
        s_a = jnp.where(picked(2 * t), s_cur[:BS], NEG_INF)
        s_b = jnp.where(picked(2 * t + 1), s_cur[BS:], NEG_INF)
        off = slope2 * ((g - t) * QT).astype(F32)
        m_old = m_ref[...]
        mx = jnp.maximum(jnp.max(s_a, axis=0, keepdims=True), jnp.max(s_b, axis=0, keepdims=True))
        m_new = jnp.maximum(m_old, mx - off)
        alpha = jnp.exp2(m_old - m_new)
        shift = m_new + off
        p_a = jnp.exp2(s_a - shift)
        p_b = jnp.exp2(s_b - shift)
        l_ref[...] = (alpha * l_ref[...] + jnp.sum(p_a, axis=0, keepdims=True)
                      + jnp.sum(p_b, axis=0, keepdims=True))
        acc_ref[...] = alpha * (acc_ref[...] + pv)
        p_ref[:BS, :] = p_a.astype(BF16)
        p_ref[BS:, :] = p_b.astype(BF16)
        m_ref[...] = m_new
        return t

    last = lax.fori_loop(0, g, past_pair, g)
    acc = acc_ref[...] + _dot(vt_ref[last], p_ref[...])
    o_ref[0] = (acc / l_ref[...]).T.astype(o_ref.dtype)


def moba_attention(qkv_h, B, S):
    H, D, BS = MOBA_HEADS, MOBA_HD, MOBA_BLOCK
    QT = 2 * BS
    nb = S // BS
    assert nb % 2 == 0, "key blocks are stored in pairs"
    nt = nb // 2
    nbp = -(-nb // 8) * 8
    slopes = jnp.exp2(-8.0 * jnp.arange(1, H + 1, dtype=F32) / H)
    slopes = jnp.broadcast_to(slopes[:, None, None], (H, 1, QT))
    return pl.pallas_call(
        functools.partial(_moba_kernel, nb=nb),
        grid=(B, H, nt),
        in_specs=[
            pl.BlockSpec((1, QT, D), lambda b, h, g: (h, b * nt + g, 0)),
            pl.BlockSpec((1, S, D), lambda b, h, g: (H + h, b, 0)),
            pl.BlockSpec((1, S, D), lambda b, h, g: (2 * H + h, b, 0)),
            pl.BlockSpec((1, 1, QT), lambda b, h, g: (h, 0, 0)),
        ],
        out_specs=pl.BlockSpec((1, QT, D), lambda b, h, g: (b, g, h)),
        out_shape=jax.ShapeDtypeStruct((B, S, H * D), BF16),
        scratch_shapes=[
            pltpu.VMEM((nt, QT, D + LANES), BF16),
            pltpu.VMEM((nt, D, QT), BF16),
            pltpu.VMEM((nbp, D), F32),
            pltpu.VMEM((1, QT), F32),
            pltpu.VMEM((1, QT), F32),
            pltpu.VMEM((D, QT), F32),
            pltpu.VMEM((2, QT, QT), F32),
            pltpu.VMEM((QT, QT), BF16),
        ],
        compiler_params=pltpu.CompilerParams(
            dimension_semantics=("parallel", "parallel", "arbitrary"),
            vmem_limit_bytes=VMEM_LIMIT),
    )(qkv_h, qkv_h, qkv_h, slopes)


def _pad_cols(w, n):
    return jnp.pad(w, ((0, 0), (0, n - w.shape[1])))


def _pad_rows(w, before, total):
    return jnp.pad(w, ((before, total - before - w.shape[0]), (0, 0)))


def _mix_in_layout(w):
    gk, gw = GLA_KEY_WIDTH, GLA_WIDTH
    g_end = 2 * gk + 2 * gw
    gla_main = w[:, :g_end]
    gla_gate = w[:, g_end:g_end + GLA_GATE_RANK]
    r0 = g_end + GLA_GATE_RANK
    rw_main = w[:, r0:r0 + 3 * RWKV_WIDTH]
    l0 = r0 + 3 * RWKV_WIDTH
    wa = w[:, l0:l0 + RWKV_DECAY_LORA + RWKV_AAA_LORA]
    g0 = l0 + RWKV_DECAY_LORA + RWKV_AAA_LORA
    gl = w[:, g0:g0 + RWKV_GATE_LORA]
    return jnp.concatenate(
        [gla_main, rw_main, _pad_cols(gla_gate, LANES), wa, _pad_cols(gl, 2 * LANES)], axis=1)


def mixer_layer0(hn_proj, gla_gate_w2, gla_gate_b, gla_norm, rwkv_mu, rwkv_w0, rwkv_w2, rwkv_a0,
                 rwkv_a2, rwkv_g2, rwkv_k_k, rwkv_k_a, rwkv_r_k, rwkv_ln_w, rwkv_ln_b,
                 *, gla_block, rwkv_block):
    W = RWKV_WIDTH
    o_gla = gla_mixer(hn_proj, _pad_rows(gla_gate_w2, 0, LANES), gla_gate_b, gla_norm,
                      block=gla_block)
    mu_r, mu_k, mu_v = rwkv_mu[:W], rwkv_mu[W:2 * W], rwkv_mu[2 * W:3 * W]
    mu_low = rwkv_mu[3 * W:]
    n_wa = RWKV_DECAY_LORA + RWKV_AAA_LORA
    vecs = jnp.stack([rwkv_w0, rwkv_a0, rwkv_k_k, rwkv_k_a, rwkv_r_k.reshape(-1), rwkv_ln_w,
                      rwkv_ln_b, mu_r, mu_k, mu_v] + [jnp.zeros((W,), F32)] * 6)
    mu_wa = mu_low[:n_wa].reshape(1, -1)
    mu_g = _pad_cols(mu_low[n_wa:].reshape(1, -1), 2 * LANES)
    w2p = _pad_rows(rwkv_w2, 0, LANES)
    a2p = _pad_rows(rwkv_a2, RWKV_DECAY_LORA, LANES)
    g2p = _pad_rows(rwkv_g2, 0, 2 * LANES)
    o_rwkv = rwkv_mixer(hn_proj, vecs, mu_wa, mu_g, w2p, a2p, g2p, block=rwkv_block, pairs=2)
    return o_gla, o_rwkv


def kernel(x, norm_mix, norm_ffn, norm_final, mix_in_w, gla_gate_w2, gla_gate_b, gla_norm, rwkv_mu, rwkv_w0, rwkv_w2, rwkv_a0, rwkv_a2, rwkv_g2, rwkv_k_k, rwkv_k_a, rwkv_r_k, rwkv_ln_w, rwkv_ln_b, mix_out_w, attn_qkv_w, attn_out_w, ffn_gate_w, ffn_up_w, ffn_down_w):
    B, S, D = x.shape
    M = B * S
    tm, tn, tf = 512, 512, 256
    tm_in = min(1024, M)
    tm_ffn = min(1024, M)
    h = x.reshape(M, D)

    w_in = _mix_in_layout(mix_in_w[0]).astype(BF16)
    p = norm_matmul(h, norm_mix[0], w_in, tm=tm_in, tn=tn).reshape(B, S, MIX_COLS)
    o_gla, o_rwkv = mixer_layer0(
        p, gla_gate_w2[0], gla_gate_b[0], gla_norm[0], rwkv_mu[0], rwkv_w0[0], rwkv_w2[0],
        rwkv_a0[0], rwkv_a2[0], rwkv_g2[0], rwkv_k_k[0], rwkv_k_a[0], rwkv_r_k[0], rwkv_ln_w[0],
        rwkv_ln_b[0], gla_block=512, rwkv_block=512)
    w_out = mix_out_w[0].astype(BF16)
    h = proj_residual(h, [(o_gla.reshape(M, GLA_WIDTH), w_out[:GLA_WIDTH]),
                          (o_rwkv.reshape(M, RWKV_WIDTH), w_out[GLA_WIDTH:])], tm=tm, tn=D)
    h = ffn_residual(h, norm_ffn[0], ffn_gate_w, ffn_up_w, ffn_down_w, norm_final, layer=0,
                     tm=tm_ffn, tf=tf, final_norm=False)

    qkv_h = norm_matmul(h, norm_mix[1], attn_qkv_w[0].astype(BF16), tm=tm_in, tn=tn,
                        group=MOBA_HD)
    o_attn = moba_attention(qkv_h, B, S)
    h = proj_residual(h, [(o_attn.reshape(M, D), attn_out_w[0].astype(BF16))], tm=tm, tn=D)
    h = ffn_residual(h, norm_ffn[1], ffn_gate_w, ffn_up_w, ffn_down_w, norm_final, layer=1,
                     tm=tm_ffn, tf=tf, final_norm=True)
    return h.reshape(B, S, D)
```

```python
import functools

import jax
import jax.numpy as jnp
from jax import lax
from jax.experimental import pallas as pl
from jax.experimental.pallas import tpu as pltpu

F32 = jnp.float32
BF16 = jnp.bfloat16
HI = lax.Precision.HIGHEST

D_MODEL = 2048
RMS_EPS = 1e-6

GLA_HEADS = 4
GLA_DK = 128
GLA_DV = 256
GLA_KEY_WIDTH = GLA_HEADS * GLA_DK
GLA_WIDTH = GLA_HEADS * GLA_DV
GLA_GATE_RANK = 16
GLA_GATE_TAU = 16.0
GLA_CHUNK = 64

RWKV_WIDTH = 1024
RWKV_HEAD = 64
RWKV_PAIR = 2 * RWKV_HEAD
RWKV_PAIRS = RWKV_WIDTH // RWKV_PAIR
RWKV_DECAY_LORA = 64
RWKV_AAA_LORA = 64
RWKV_GATE_LORA = 160
RWKV_LN_EPS = RWKV_HEAD * 1e-5
RWKV_CHUNK = 64

MOBA_HEADS = 16
MOBA_HD = 128
MOBA_BLOCK = 256
MOBA_TOPK = 3
NEG_INF = -1e30

FFN_HIDDEN = 5632

LANES = 128

COL_GLA_Q = 0
COL_GLA_K = 512
COL_GLA_V = 1024
COL_GLA_OG = 2048
COL_RWKV_R = 3072
COL_RWKV_K = 4096
COL_RWKV_V = 5120
COL_LOW_GATE = 6144
COL_LOW_WA = 6272
COL_LOW_G = 6400
MIX_COLS = 6656

VMEM_LIMIT = 56 * 1024 * 1024


def _mm(a, b):
    return jnp.dot(a.astype(BF16), b.astype(BF16), preferred_element_type=F32)


def _mm_nt(a, b):
    return lax.dot_general(a.astype(BF16), b.astype(BF16), (((1,), (1,)), ((), ())),
                           preferred_element_type=F32)


def _mm_hi(a, b):
    return jnp.dot(a, b, preferred_element_type=F32, precision=HI)


def _mm_nt_hi(a, b):
    return lax.dot_general(a, b, (((1,), (1,)), ((), ())), preferred_element_type=F32,
                           precision=HI)


def _dot(a, b):
    return jnp.dot(a, b, preferred_element_type=F32)


def _dot_tn(a, b):
    return lax.dot_general(a, b, (((0,), (0,)), ((), ())), preferred_element_type=F32)


def _split(x, parts):
    out = []
    for _ in range(parts - 1):
        hi = x.astype(BF16)
        out.append(hi)
        x = x - hi.astype(F32)
    out.append(x.astype(BF16))
    return out


def _mm_split(a, b, *, a_parts=1, b_parts=1):
    a_p = _split(a, a_parts) if a_parts > 1 else [a.astype(BF16)]
    b_p = _split(b, b_parts) if b_parts > 1 else [b.astype(BF16)]
    acc = None
    for i, ai in enumerate(a_p):
        for j, bj in enumerate(b_p):
            if i + j < max(a_parts, b_parts):
                t = _dot(ai, bj)
                acc = t if acc is None else acc + t
    return acc


def _sigmoid(x):
    return 1.0 / (1.0 + jnp.exp(-x))


def _softplus(x):
    return jnp.maximum(x, 0.0) + jnp.log1p(jnp.exp(-jnp.abs(x)))


def _iota2(shape, axis):
    return lax.broadcasted_iota(jnp.int32, shape, axis)


def _norm_matmul_kernel(x_ref, g_ref, w_ref, o_ref, xn_ref, *, group):
    @pl.when(pl.program_id(1) == 0)
    def _():
        x = x_ref[...]
        ms = jnp.mean(x * x, axis=-1, keepdims=True)
        xn_ref[...] = (x * lax.rsqrt(ms + RMS_EPS) * g_ref[...]).astype(BF16)

    res = jnp.dot(xn_ref[...], w_ref[...], preferred_element_type=F32).astype(o_ref.dtype)
    if group is None:
        o_ref[...] = res
    else:
        for c in range(res.shape[1] // group):
            o_ref[c] = res[:, c * group:(c + 1) * group]


def norm_matmul(x, g, w, *, tm, tn, out_dtype=F32, group=None):
    M, D = x.shape
    N = w.shape[1]
    if group is None:
        out_specs = pl.BlockSpec((tm, tn), lambda i, j: (i, j))
        out_shape = jax.ShapeDtypeStruct((M, N), out_dtype)
    else:
        out_specs = pl.BlockSpec((tn // group, tm, group), lambda i, j: (j, i, 0))
        out_shape = jax.ShapeDtypeStruct((N // group, M, group), out_dtype)
    return pl.pallas_call(
        functools.partial(_norm_matmul_kernel, group=group),
        grid=(M // tm, N // tn),
        in_specs=[
            pl.BlockSpec((tm, D), lambda i, j: (i, 0)),
            pl.BlockSpec((1, D), lambda i, j: (0, 0)),
            pl.BlockSpec((D, tn), lambda i, j: (0, j)),
        ],
        out_specs=out_specs,
        out_shape=out_shape,
        scratch_shapes=[pltpu.VMEM((tm, D), BF16)],
        compiler_params=pltpu.CompilerParams(
            dimension_semantics=("parallel", "arbitrary"), vmem_limit_bytes=VMEM_LIMIT),
    )(x, g.reshape(1, D), w)


def _proj_res_kernel(*refs, n_in):
    res_ref = refs[0]
    o_ref = refs[1 + 2 * n_in]
    acc = res_ref[...]
    for i in range(n_in):
        acc = acc + jnp.dot(refs[1 + 2 * i][...], refs[2 + 2 * i][...],
                            preferred_element_type=F32)
    o_ref[...] = acc


def proj_residual(res, pairs, *, tm, tn):
    M, N = res.shape
    in_specs = [pl.BlockSpec((tm, tn), lambda i, j: (i, j))]
    args = [res]
    for a, w in pairs:
        K = a.shape[1]
        in_specs.append(pl.BlockSpec((tm, K), lambda i, j: (i, 0)))
        in_specs.append(pl.BlockSpec((K, tn), lambda i, j: (0, j)))
        args += [a, w]
    return pl.pallas_call(
        functools.partial(_proj_res_kernel, n_in=len(pairs)),
        grid=(M // tm, N // tn),
        in_specs=in_specs,
        out_specs=pl.BlockSpec((tm, tn), lambda i, j: (i, j)),
        out_shape=jax.ShapeDtypeStruct((M, N), F32),
        compiler_params=pltpu.CompilerParams(
            dimension_semantics=("parallel", "arbitrary"), vmem_limit_bytes=VMEM_LIMIT),
    )(*args)


def _ffn_kernel(x_ref, g_ref, wg_ref, wu_ref, wd_ref, gf_ref, o_ref, xn_ref, *, final_norm):
    j = pl.program_id(1)

    @pl.when(j == 0)
    def _():
        x = x_ref[...]
        ms = jnp.mean(x * x, axis=-1, keepdims=True)
        xn_ref[...] = (x * lax.rsqrt(ms + RMS_EPS) * g_ref[...]).astype(BF16)
        o_ref[...] = x

    xn = xn_ref[...]
    gate = jnp.dot(xn, wg_ref[...].astype(BF16), preferred_element_type=F32)
    up = jnp.dot(xn, wu_ref[...].astype(BF16), preferred_element_type=F32)
    act = (gate * _sigmoid(gate) * up).astype(BF16)
    o_ref[...] += jnp.dot(act, wd_ref[...].astype(BF16), preferred_element_type=F32)

    if final_norm:
        @pl.when(j == pl.num_programs(1) - 1)
        def _():
            h = o_ref[...]
            ms = jnp.mean(h * h, axis=-1, keepdims=True)
            o_ref[...] = h * lax.rsqrt(ms + RMS_EPS) * gf_ref[...]


def ffn_residual(x, g, wg, wu, wd, g_final, *, layer, tm, tf, final_norm):
    M, D = x.shape
    F = wg.shape[2]
    return pl.pallas_call(
        functools.partial(_ffn_kernel, final_norm=final_norm),
        grid=(M // tm, F // tf),
        in_specs=[
            pl.BlockSpec((tm, D), lambda i, j: (i, 0)),
            pl.BlockSpec((1, D), lambda i, j: (0, 0)),
            pl.BlockSpec((None, D, tf), lambda i, j: (layer, 0, j)),
            pl.BlockSpec((None, D, tf), lambda i, j: (layer, 0, j)),
            pl.BlockSpec((None, tf, D), lambda i, j: (layer, j, 0)),
            pl.BlockSpec((1, D), lambda i, j: (0, 0)),
        ],
        out_specs=pl.BlockSpec((tm, D), lambda i, j: (i, 0)),
        out_shape=jax.ShapeDtypeStruct((M, D), F32),
        scratch_shapes=[pltpu.VMEM((tm, D), BF16)],
        compiler_params=pltpu.CompilerParams(
            dimension_semantics=("parallel", "arbitrary"), vmem_limit_bytes=VMEM_LIMIT),
    )(x, g.reshape(1, D), wg, wu, wd, g_final.reshape(1, D))


def _gla_kernel(q_ref, k_ref, v_ref, og_ref, gl_ref, w2_ref, b_ref, gn_ref, o_ref, st_ref,
                *, block, chunk):
    n_heads = q_ref.shape[2] // GLA_DK

    @pl.when(pl.program_id(2) == 0)
    def _():
        st_ref[...] = jnp.zeros_like(st_ref)

    logit = _mm_split(gl_ref[0], w2_ref[...], a_parts=2, b_parts=2) + b_ref[...]
    log_a = -_softplus(-logit) * (1.0 / GLA_GATE_TAU)
    tri = _iota2((chunk, chunk), 0) >= _iota2((chunk, chunk), 1)
    tri_b = tri.astype(BF16)
    tri_b3 = jnp.concatenate([tri_b, tri_b, tri_b], axis=1)
    scale = GLA_DK ** -0.5
    gn = gn_ref[...]
    n_chunks = block // chunk

    streams = [(h, c) for h in range(n_heads) for c in range(n_chunks)]

    def rows(c):
        return pl.ds(c * chunk, chunk)

    def k_lanes(h):
        return slice(h * GLA_DK, (h + 1) * GLA_DK)

    def v_lanes(h):
        return slice(h * GLA_DV, (h + 1) * GLA_DV)

    la3 = [jnp.concatenate(_split(log_a[c * chunk:(c + 1) * chunk, k_lanes(h)], 3), axis=0)
           for h, c in streams]
    cum = [_dot(tri_b3, x) for x in la3]
    q_dec = [(q_ref[0, rows(c), k_lanes(h)] * scale * jnp.exp(b)).astype(BF16)
             for (h, c), b in zip(streams, cum)]
    k_inv = [(k_ref[0, rows(c), k_lanes(h)] * jnp.exp(-b)).astype(BF16)
             for (h, c), b in zip(streams, cum)]
    k_dec = [(k_ref[0, rows(c), k_lanes(h)] * jnp.exp(b[chunk - 1:chunk] - b)).astype(BF16)
             for (h, c), b in zip(streams, cum)]
    v_b = [v_ref[0, rows(c), v_lanes(h)].astype(BF16) for h, c in streams]
    v_t = [v_ref[0, rows(c), v_lanes(h)].T.astype(BF16) for h, c in streams]
    att = [jnp.where(tri, _mm_nt(x, y), 0.0).astype(BF16) for x, y in zip(q_dec, k_inv)]
    o_intra = [_dot(x, y) for x, y in zip(att, v_b)]
    gain = [_dot(x, y) for x, y in zip(v_t, k_dec)]
    pre = [dict(q_dec=q_dec[i], o_intra=o_intra[i], gain=gain[i],
                decay=jnp.exp(cum[i][chunk - 1:chunk])) for i in range(len(streams))]

    st = [st_ref[h] for h in range(n_heads)]
    outs = [[] for _ in range(n_heads)]
    for c in range(n_chunks):
        for h in range(n_heads):
            p = pre[h * n_chunks + c]
            outs[h].append(p["o_intra"] + _mm_nt(p["q_dec"], st[h]))
            st[h] = st[h] * p["decay"] + p["gain"]
    for h in range(n_heads):
        st_ref[h] = st[h]

    for h in range(n_heads):
        lv = slice(h * GLA_DV, (h + 1) * GLA_DV)
        o = jnp.concatenate(outs[h], axis=0)
        ms = jnp.mean(o * o, axis=-1, keepdims=True)
        og = og_ref[0, :, lv]
        o_ref[0, :, lv] = (o * lax.rsqrt(ms + RMS_EPS) * gn * (og * _sigmoid(og))).astype(o_ref.dtype)


def gla_mixer(p3, gate_w2p, gate_b, gla_norm, *, block, heads):
    B, S, _ = p3.shape
    wk, wv = heads * GLA_DK, heads * GLA_DV
    kq, kk_, kv, kog = COL_GLA_Q // wk, COL_GLA_K // wk, COL_GLA_V // wv, COL_GLA_OG // wv
    klow = COL_LOW_GATE // LANES
    return pl.pallas_call(
        functools.partial(_gla_kernel, block=block, chunk=GLA_CHUNK),
        grid=(B, GLA_HEADS // heads, S // block),
        in_specs=[
            pl.BlockSpec((1, block, wk), lambda b, h, s: (b, s, kq + h)),
            pl.BlockSpec((1, block, wk), lambda b, h, s: (b, s, kk_ + h)),
            pl.BlockSpec((1, block, wv), lambda b, h, s: (b, s, kv + h)),
            pl.BlockSpec((1, block, wv), lambda b, h, s: (b, s, kog + h)),
            pl.BlockSpec((1, block, LANES), lambda b, h, s: (b, s, klow)),
            pl.BlockSpec((LANES, wk), lambda b, h, s: (0, h)),
            pl.BlockSpec((1, wk), lambda b, h, s: (0, h)),
            pl.BlockSpec((1, GLA_DV), lambda b, h, s: (0, 0)),
        ],
        out_specs=pl.BlockSpec((1, block, wv), lambda b, h, s: (b, s, h)),
        out_shape=jax.ShapeDtypeStruct((B, S, GLA_WIDTH), BF16),
        scratch_shapes=[pltpu.VMEM((heads, GLA_DV, GLA_DK), F32)],
        compiler_params=pltpu.CompilerParams(
            dimension_semantics=("parallel", "parallel", "arbitrary"),
            vmem_limit_bytes=VMEM_LIMIT),
    )(p3, p3, p3, p3, p3, gate_w2p, gate_b.reshape(1, -1), gla_norm.reshape(1, -1))


def _unit_lower_inverses(mats, rowi, coli):
    def same_block(s):
        sh = s.bit_length() - 1
        return (rowi >> sh) == (coli >> sh)

    eye = (rowi == coli).astype(F32)
    a8 = [jnp.where(same_block(8), a, 0.0).astype(BF16) for a in mats]
    inv = [eye + a.astype(F32) for a in a8]
    p = [_dot(a, a) for a in a8]
    inv = [x + _mm(x, y) for x, y in zip(inv, p)]
    p = [_mm(y, y) for y in p]
    inv = [x + _mm(x, y) for x, y in zip(inv, p)]
    s = 8
    while s < RWKV_CHUNK:
        off = same_block(2 * s) & jnp.logical_not(same_block(s))
        e = [jnp.where(off, a, 0.0).astype(BF16) for a in mats]
        inv_b = [x.astype(BF16) for x in inv]
        t = [_dot(x, y).astype(BF16) for x, y in zip(inv_b, e)]
        inv = [x + _dot(y, z) for x, y, z in zip(inv, t, inv_b)]
        s *= 2
    return inv


def _rwkv_kernel(r_ref, k_ref, v_ref, wa_ref, g_ref, vec_ref, muwa_ref, mug_ref, w2_ref, a2_ref,
                 g2_ref, o_ref, st_ref, prev_ref, prevwa_ref, prevg_ref, *, block, chunk):
    T = chunk
    n_pairs = r_ref.shape[2] // RWKV_PAIR
    first = pl.program_id(2) == 0

    @pl.when(first)
    def _():
        st_ref[...] = jnp.zeros_like(st_ref)
        prev_ref[...] = jnp.zeros_like(prev_ref)
        prevwa_ref[...] = jnp.zeros_like(prevwa_ref)
        prevg_ref[...] = jnp.zeros_like(prevg_ref)

    row0 = _iota2((block, 1), 0) == 0

    def lerp(x, prev_row, mu):
        shifted = jnp.where(row0, prev_row, pltpu.roll(x, 1, axis=0))
        return x + (shifted - x) * mu

    vec = vec_ref[...]
    w0, a0, k_k, k_a, r_k, ln_w, ln_b = (vec[i:i + 1] for i in range(7))
    mu_r, mu_k, mu_v = (vec[i:i + 1] for i in range(7, 10))

    r_raw, k_raw, v_raw, wa_raw, g_raw = r_ref[0], k_ref[0], v_ref[0], wa_ref[0], g_ref[0]
    prev = prev_ref[...]
    r_all = lerp(r_raw, prev[0:1], mu_r)
    k_all = lerp(k_raw, prev[1:2], mu_k)
    v_all = lerp(v_raw, prev[2:3], mu_v)
    wa = lerp(wa_raw, prevwa_ref[0:1], muwa_ref[...])
    g_low = lerp(g_raw, prevg_ref[0:1], mug_ref[...])
    prev_ref[0:1] = r_raw[block - 1:block]
    prev_ref[1:2] = k_raw[block - 1:block]
    prev_ref[2:3] = v_raw[block - 1:block]
    prevwa_ref[0:1] = wa_raw[block - 1:block]
    prevg_ref[0:1] = g_raw[block - 1:block]

    z = w0 + _mm_split(jnp.tanh(wa), w2_ref[...], a_parts=2, b_parts=2)
    logw_all = -jnp.exp(-_softplus(-z) - 0.5)
    lr_all = _sigmoid(a0 + _mm_split(wa, a2_ref[...], a_parts=2, b_parts=2))
    gate_all = _mm_split(_sigmoid(g_low), g2_ref[...], a_parts=2, b_parts=2)

    lane = _iota2((1, RWKV_PAIR), 1)
    m0 = (lane < RWKV_HEAD).astype(F32)
    m1 = 1.0 - m0
    rowi = _iota2((2 * T, 2 * T), 0)
    coli = _iota2((2 * T, 2 * T), 1)
    strict = rowi > coli
    incl = rowi >= coli
    head_ones = ((rowi < RWKV_HEAD) == (coli < RWKV_HEAD)).astype(BF16)
    head_ones2 = jnp.concatenate([head_ones, head_ones], axis=0)
    tri_t = (_iota2((T, T), 0) >= _iota2((T, T), 1)).astype(BF16)
    tri_t3 = jnp.concatenate([tri_t, tri_t, tri_t], axis=1)
    ones_t3 = jnp.ones((3 * T, RWKV_PAIR), BF16)

    def hsum(x):
        return _dot(jnp.concatenate(_split(x, 2), axis=1), head_ones2)

    def stack(x):
        return jnp.concatenate([x * m0, x * m1], axis=0)

    n_chunks = block // T

    kk_p, k2_p, b_p, bonus = [], [], [], []
    for pair in range(n_pairs):
        ln = slice(pair * RWKV_PAIR, (pair + 1) * RWKV_PAIR)
        kkp = k_all[:, ln] * k_k[:, ln]
        kk_p.append(kkp / jnp.maximum(jnp.sqrt(hsum(kkp * kkp)), 1e-12))
        k2_p.append(k_all[:, ln] * (1.0 + (lr_all[:, ln] - 1.0) * k_a[:, ln]))
        b_p.append(kk_p[pair] * lr_all[:, ln])
        bonus.append(hsum(r_all[:, ln] * k2_p[pair] * r_k[:, ln]) * v_all[:, ln])

    streams = [(pair, c) for pair in range(n_pairs) for c in range(n_chunks)]

    def tile(x, pair, c):
        return x[c * T:(c + 1) * T, pair * RWKV_PAIR:(pair + 1) * RWKV_PAIR]

    lw = [tile(logw_all, p, c) for p, c in streams]
    lw3 = [jnp.concatenate(_split(x, 3), axis=0) for x in lw]
    cum = [_dot(tri_t3, x) for x in lw3]
    decay = [jnp.exp(_dot_tn(x, ones_t3)) for x in lw3]
    w_inv = [jnp.exp(-x) for x in cum]
    w_tail = [jnp.exp(x[T - 1:T] - x) for x in cum]
    a_s = [stack(-kk_p[p][c * T:(c + 1) * T] * jnp.exp(x - y)).astype(BF16)
           for (p, c), x, y in zip(streams, cum, lw)]
    r_s = [stack(tile(r_all, p, c) * jnp.exp(x)).astype(BF16) for (p, c), x in zip(streams, cum)]
    b_s = [stack(b_p[p][c * T:(c + 1) * T] * x).astype(BF16) for (p, c), x in zip(streams, w_inv)]
    k_s = [stack(k2_p[p][c * T:(c + 1) * T] * x).astype(BF16) for (p, c), x in zip(streams, w_inv)]
    v_s = [stack(tile(v_all, p, c)).astype(BF16) for p, c in streams]
    b_tail = [stack(b_p[p][c * T:(c + 1) * T] * x).astype(BF16)
              for (p, c), x in zip(streams, w_tail)]
    k_tail = [stack(k2_p[p][c * T:(c + 1) * T] * x).astype(BF16)
              for (p, c), x in zip(streams, w_tail)]
    prod = [_mm_nt(jnp.concatenate([a, r], axis=0), jnp.concatenate([b, k], axis=0))
            for a, r, b, k in zip(a_s, r_s, b_s, k_s)]
    pre = [dict(a_s=a_s[i], r_s=r_s[i], v_s=v_s[i], decay=decay[i], b_tail=b_tail[i],
                k_tail=k_tail[i],
                a_ab=jnp.where(strict, prod[i][:2 * T, :2 * T], 0.0),
                a_ak=jnp.where(strict, prod[i][:2 * T, 2 * T:], 0.0).astype(BF16),
                a_rb=jnp.where(incl, prod[i][2 * T:, :2 * T], 0.0).astype(BF16),
                a_rk=jnp.where(incl, prod[i][2 * T:, 2 * T:], 0.0).astype(BF16))
           for i in range(len(streams))]

    invs = [x.astype(BF16) for x in _unit_lower_inverses([p["a_ab"] for p in pre], rowi, coli)]
    akv = [_dot(p["a_ak"], p["v_s"]).astype(BF16) for p in pre]
    x = [_dot(i, jnp.concatenate([p["a_s"], y], axis=1)).astype(BF16)
         for i, p, y in zip(invs, pre, akv)]
    bx = [_dot_tn(p["b_tail"], y) for p, y in zip(pre, x)]
    rx = [_dot(p["a_rb"], y) for p, y in zip(pre, x)]
    trans = [y[:, :RWKV_PAIR].astype(BF16) for y in bx]
    gain = [y[:, RWKV_PAIR:] + _dot_tn(p["k_tail"], p["v_s"]) for p, y in zip(pre, bx)]
    read = [(p["r_s"].astype(F32) + y[:, :RWKV_PAIR]).astype(BF16) for p, y in zip(pre, rx)]
    y_free = [y[:, RWKV_PAIR:] + _dot(p["a_rk"], p["v_s"]) for p, y in zip(pre, rx)]

    st = [st_ref[pair] for pair in range(n_pairs)]
    ys = [[] for _ in range(n_pairs)]
    for c in range(n_chunks):
        for pair in range(n_pairs):
            i = pair * n_chunks + c
            st_b = st[pair].astype(BF16)
            y_s = _dot(read[i], st_b) + y_free[i]
            ys[pair].append(y_s[:T] + y_s[T:])
            st[pair] = st[pair] * pre[i]["decay"] + _dot(trans[i], st_b) + gain[i]
    for pair in range(n_pairs):
        st_ref[pair] = st[pair]

    for pair in range(n_pairs):
        ln = slice(pair * RWKV_PAIR, (pair + 1) * RWKV_PAIR)
        y = jnp.concatenate(ys[pair], axis=0)
        mu = hsum(y) * (1.0 / RWKV_HEAD)
        d = y - mu
        var = hsum(d * d) * (1.0 / RWKV_HEAD)
        yn = d * lax.rsqrt(var + RWKV_LN_EPS) * ln_w[:, ln] + ln_b[:, ln]
        o_ref[0, :, ln] = ((yn + bonus[pair]) * gate_all[:, ln]).astype(o_ref.dtype)


def rwkv_mixer(p3, vecs, mu_wa, mu_g, w2p, a2p, g2p, *, block, pairs):
    B, S, _ = p3.shape
    W = pairs * RWKV_PAIR
    kr, kk_, kv = COL_RWKV_R // W, COL_RWKV_K // W, COL_RWKV_V // W
    kwa = COL_LOW_WA // LANES
    kg = COL_LOW_G // (2 * LANES)
    return pl.pallas_call(
        functools.partial(_rwkv_kernel, block=block, chunk=RWKV_CHUNK),
        grid=(B, RWKV_PAIRS // pairs, S // block),
        in_specs=[
            pl.BlockSpec((1, block, W), lambda b, j, s: (b, s, kr + j)),
            pl.BlockSpec((1, block, W), lambda b, j, s: (b, s, kk_ + j)),
            pl.BlockSpec((1, block, W), lambda b, j, s: (b, s, kv + j)),
            pl.BlockSpec((1, block, LANES), lambda b, j, s: (b, s, kwa)),
            pl.BlockSpec((1, block, 2 * LANES), lambda b, j, s: (b, s, kg)),
            pl.BlockSpec((16, W), lambda b, j, s: (0, j)),
            pl.BlockSpec((1, LANES), lambda b, j, s: (0, 0)),
            pl.BlockSpec((1, 2 * LANES), lambda b, j, s: (0, 0)),
            pl.BlockSpec((LANES, W), lambda b, j, s: (0, j)),
            pl.BlockSpec((LANES, W), lambda b, j, s: (0, j)),
            pl.BlockSpec((2 * LANES, W), lambda b, j, s: (0, j)),
        ],
        out_specs=pl.BlockSpec((1, block, W), lambda b, j, s: (b, s, j)),
        out_shape=jax.ShapeDtypeStruct((B, S, RWKV_WIDTH), BF16),
        scratch_shapes=[
            pltpu.VMEM((pairs, RWKV_PAIR, RWKV_PAIR), F32),
            pltpu.VMEM((8, W), F32),
            pltpu.VMEM((8, LANES), F32),
            pltpu.VMEM((8, 2 * LANES), F32),
        ],
        compiler_params=pltpu.CompilerParams(
            dimension_semantics=("parallel", "parallel", "arbitrary"),
            vmem_limit_bytes=VMEM_LIMIT),
    )(p3, p3, p3, p3, p3, vecs, mu_wa, mu_g, w2p, a2p, g2p)


def _moba_kernel(q_ref, k_ref, v_ref, slope_ref, o_ref, kb_ref, vt_ref, kmean_ref,
                 m_ref, l_ref, acc_ref, s_ref, p_ref, *, nb):
    BS, D = MOBA_BLOCK, MOBA_HD
    QT = 2 * BS
    g = pl.program_id(2)
    log2e = 1.4426950408889634
    c1 = MOBA_HD ** -0.5 * log2e
    slope2 = slope_ref[0] * log2e
    s_parts = _split(slope2, 3)

    @pl.when(g == 0)
    def _():
        kmean_ref[...] = jnp.zeros_like(kmean_ref)
        lane = _iota2((QT, LANES), 1)
        kpos = _iota2((QT, LANES), 0)
        kpos_lo = (kpos & (BS - 1)).astype(F32)
        kpos_hi = (kpos & BS).astype(F32)
        sk = [x[:, :LANES].astype(F32) for x in s_parts]
        piece = jnp.where((lane == 0) | (lane == 3), sk[0],
                          jnp.where((lane == 1) | (lane == 4), sk[1], sk[2]))
        feat = jnp.where(lane < 6, piece,
                         jnp.where(lane < 9, kpos_lo, jnp.where(lane < 12, kpos_hi, 0.0)))
        feat = feat.astype(BF16)
        for j in range(nb):
            half = slice((j % 2) * BS, (j % 2 + 1) * BS)
            kj = k_ref[0, pl.ds(j * BS, BS), :]
            kmean_ref[j:j + 1, :] = jnp.mean(kj, axis=0, keepdims=True)
            kb_ref[j // 2, half, :D] = kj.astype(BF16)
            vt_ref[j // 2, :, half] = v_ref[0, pl.ds(j * BS, BS), :].T.astype(BF16)
        for t in range(nb // 2):
            kb_ref[t, :, D:] = feat

    q_t = q_ref[0].T
    qpos = _iota2((1, QT), 1)
    blk_q = 2 * g + (qpos >= BS).astype(jnp.int32)

    nbp = kmean_ref.shape[0]
    blk = _iota2((nbp, QT), 0)
    gate = _mm_split(kmean_ref[...], q_t, a_parts=3, b_parts=3)
    gate = jnp.where(blk < blk_q, gate, NEG_INF)
    sel = []
    for r in range(MOBA_TOPK):
        mx = jnp.max(gate, axis=0, keepdims=True)
        idx = jnp.min(jnp.where(gate == mx, blk, nbp), axis=0, keepdims=True)
        sel.append(jnp.where(r < blk_q, idx, -1))
        gate = jnp.where(blk == idx, -jnp.inf, gate)

    def picked(j):
        return (sel[0] == j) | (sel[1] == j) | (sel[2] == j)

    arow = _iota2((LANES, QT), 0)
    qpos_a = _iota2((LANES, QT), 1)
    qpos_lo = (qpos_a & (BS - 1)).astype(F32)
    qpos_hi = (qpos_a & BS).astype(F32)
    sq = [x.astype(F32) for x in s_parts]
    piece = jnp.where((arow == 6) | (arow == 9), sq[0],
                      jnp.where((arow == 7) | (arow == 10), sq[1], sq[2]))
    aug = jnp.where(arow < 3, -qpos_lo,
                    jnp.where(arow < 6, -qpos_hi, jnp.where(arow < 12, piece, 0.0)))
    q_aug = jnp.concatenate([(q_t * c1).astype(BF16), aug.astype(BF16)], axis=0)

    kp = _iota2((QT, QT), 0)
    qp = _iota2((QT, QT), 1)
    allowed = (kp <= qp) & (((kp >= BS) == (qp >= BS)) | picked(2 * g))
    s_own = jnp.where(allowed, _dot(kb_ref[g], q_aug), NEG_INF)
    m0 = jnp.max(s_own, axis=0, keepdims=True)
    p0 = jnp.exp2(s_own - m0)
    m_ref[...] = m0
    l_ref[...] = jnp.sum(p0, axis=0, keepdims=True)
    p_ref[...] = p0.astype(BF16)
    acc_ref[...] = jnp.zeros_like(acc_ref)
    s_ref[0] = _dot(kb_ref[0], q_aug)

    def past_pair(t, prev):
        s_cur = s_ref[t % 2]
        pv = _dot(vt_ref[prev], p_ref[...])
        s_ref[(t + 1) % 2] = _dot(kb_ref[jnp.minimum(t + 1, nb // 2 - 1)], q_aug)
        s_a = jnp.where(picked(2 * t), s_cur[:BS], NEG_INF)
        s_b = jnp.where(picked(2 * t + 1), s_cur[BS:], NEG_INF)
        off = slope2 * ((g - t) * QT).astype(F32)
        m_old = m_ref[...]
        mx = jnp.maximum(jnp.max(s_a, axis=0, keepdims=True), jnp.max(s_b, axis=0, keepdims=True))
        m_new = jnp.maximum(m_old, mx - off)
        alpha = jnp.exp2(m_old - m_new)
        shift = m_new + off
        p_a = jnp.exp2(s_a - shift)
        p_b = jnp.exp2(s_b - shift)
        l_ref[...] = (alpha * l_ref[...] + jnp.sum(p_a, axis=0, keepdims=True)
                      + jnp.sum(p_b, axis=0, keepdims=True))
        acc_ref[...] = alpha * (acc_ref[...] + pv)
        p_ref[:BS, :] = p_a.astype(BF16)
        p_ref[BS:, :] = p_b.astype(BF16)
        m_ref[...] = m_new
        return t

    last = lax.fori_loop(0, g, past_pair, g)
    acc = acc_ref[...] + _dot(vt_ref[last], p_ref[...])
    o_ref[0] = (acc / l_ref[...]).T.astype(o_ref.dtype)


def moba_attention(qkv_h, B, S):
    H, D, BS = MOBA_HEADS, MOBA_HD, MOBA_BLOCK
    QT = 2 * BS
    nb = S // BS
    assert nb % 2 == 0, "key blocks are stored in pairs"
    nt = nb // 2
    nbp = -(-nb // 8) * 8
    slopes = jnp.exp2(-8.0 * jnp.arange(1, H + 1, dtype=F32) / H)
    slopes = jnp.broadcast_to(slopes[:, None, None], (H, 1, QT))
    return pl.pallas_call(
        functools.partial(_moba_kernel, nb=nb),
        grid=(B, H, nt),
        in_specs=[
            pl.BlockSpec((1, QT, D), lambda b, h, g: (h, b * nt + g, 0)),
            pl.BlockSpec((1, S, D), lambda b, h, g: (H + h, b, 0)),
            pl.BlockSpec((1, S, D), lambda b, h, g: (2 * H + h, b, 0)),
            pl.BlockSpec((1, 1, QT), lambda b, h, g: (h, 0, 0)),
        ],
        out_specs=pl.BlockSpec((1, QT, D), lambda b, h, g: (b, g, h)),
        out_shape=jax.ShapeDtypeStruct((B, S, H * D), BF16),
        scratch_shapes=[
            pltpu.VMEM((nt, QT, D + LANES), BF16),
            pltpu.VMEM((nt, D, QT), BF16),
            pltpu.VMEM((nbp, D), F32),
            pltpu.VMEM((1, QT), F32),
            pltpu.VMEM((1, QT), F32),
            pltpu.VMEM((D, QT), F32),
            pltpu.VMEM((2, QT, QT), F32),
            pltpu.VMEM((QT, QT), BF16),
        ],
        compiler_params=pltpu.CompilerParams(
            dimension_semantics=("parallel", "parallel", "arbitrary"),
            vmem_limit_bytes=VMEM_LIMIT),
    )(qkv_h, qkv_h, qkv_h, slopes)


def _pad_cols(w, n):
    return jnp.pad(w, ((0, 0), (0, n - w.shape[1])))


def _pad_rows(w, before, total):
    return jnp.pad(w, ((before, total - before - w.shape[0]), (0, 0)))


def _mix_in_layout(w):
    gk, gw = GLA_KEY_WIDTH, GLA_WIDTH
    g_end = 2 * gk + 2 * gw
    gla_main = w[:, :g_end]
    gla_gate = w[:, g_end:g_end + GLA_GATE_RANK]
    r0 = g_end + GLA_GATE_RANK
    rw_main = w[:, r0:r0 + 3 * RWKV_WIDTH]
    l0 = r0 + 3 * RWKV_WIDTH
    wa = w[:, l0:l0 + RWKV_DECAY_LORA + RWKV_AAA_LORA]
    g0 = l0 + RWKV_DECAY_LORA + RWKV_AAA_LORA
    gl = w[:, g0:g0 + RWKV_GATE_LORA]
    return jnp.concatenate(
        [gla_main, rw_main, _pad_cols(gla_gate, LANES), wa, _pad_cols(gl, 2 * LANES)], axis=1)


def mixer_layer0(hn_proj, gla_gate_w2, gla_gate_b, gla_norm, rwkv_mu, rwkv_w0, rwkv_w2, rwkv_a0,
                 rwkv_a2, rwkv_g2, rwkv_k_k, rwkv_k_a, rwkv_r_k, rwkv_ln_w, rwkv_ln_b,
                 *, gla_block, rwkv_block):
    W = RWKV_WIDTH
    o_gla = gla_mixer(hn_proj, _pad_rows(gla_gate_w2, 0, LANES), gla_gate_b, gla_norm,
                      block=gla_block, heads=4)
    mu_r, mu_k, mu_v = rwkv_mu[:W], rwkv_mu[W:2 * W], rwkv_mu[2 * W:3 * W]
    mu_low = rwkv_mu[3 * W:]
    n_wa = RWKV_DECAY_LORA + RWKV_AAA_LORA
    vecs = jnp.stack([rwkv_w0, rwkv_a0, rwkv_k_k, rwkv_k_a, rwkv_r_k.reshape(-1), rwkv_ln_w,
                      rwkv_ln_b, mu_r, mu_k, mu_v] + [jnp.zeros((W,), F32)] * 6)
    mu_wa = mu_low[:n_wa].reshape(1, -1)
    mu_g = _pad_cols(mu_low[n_wa:].reshape(1, -1), 2 * LANES)
    w2p = _pad_rows(rwkv_w2, 0, LANES)
    a2p = _pad_rows(rwkv_a2, RWKV_DECAY_LORA, LANES)
    g2p = _pad_rows(rwkv_g2, 0, 2 * LANES)
    o_rwkv = rwkv_mixer(hn_proj, vecs, mu_wa, mu_g, w2p, a2p, g2p, block=rwkv_block, pairs=4)
    return o_gla, o_rwkv


def kernel(x, norm_mix, norm_ffn, norm_final, mix_in_w, gla_gate_w2, gla_gate_b, gla_norm, rwkv_mu, rwkv_w0, rwkv_w2, rwkv_a0, rwkv_a2, rwkv_g2, rwkv_k_k, rwkv_k_a, rwkv_r_k, rwkv_ln_w, rwkv_ln_b, mix_out_w, attn_qkv_w, attn_out_w, ffn_gate_w, ffn_up_w, ffn_down_w):
    B, S, D = x.shape
    M = B * S
    tm, tn, tf = 512, 512, 256
    tm_in = min(1024, M)
    tm_ffn = min(1024, M)
    h = x.reshape(M, D)

    w_in = _mix_in_layout(mix_in_w[0]).astype(BF16)
    p = norm_matmul(h, norm_mix[0], w_in, tm=tm_in, tn=tn).reshape(B, S, MIX_COLS)
    o_gla, o_rwkv = mixer_layer0(
        p, gla_gate_w2[0], gla_gate_b[0], gla_norm[0], rwkv_mu[0], rwkv_w0[0], rwkv_w2[0],
        rwkv_a0[0], rwkv_a2[0], rwkv_g2[0], rwkv_k_k[0], rwkv_k_a[0], rwkv_r_k[0], rwkv_ln_w[0],
        rwkv_ln_b[0], gla_block=512, rwkv_block=512)
    w_out = mix_out_w[0].astype(BF16)
    h = proj_residual(h, [(o_gla.reshape(M, GLA_WIDTH), w_out[:GLA_WIDTH]),
                          (o_rwkv.reshape(M, RWKV_WIDTH), w_out[GLA_WIDTH:])], tm=tm, tn=D)
    h = ffn_residual(h, norm_ffn[0], ffn_gate_w, ffn_up_w, ffn_down_w, norm_final, layer=0,
                     tm=tm_ffn, tf=tf, final_norm=False)

    qkv_h = norm_matmul(h, norm_mix[1], attn_qkv_w[0].astype(BF16), tm=tm_in, tn=tn,
                        group=MOBA_HD)
    o_attn = moba_attention(qkv_h, B, S)
    h = proj_residual(h, [(o_attn.reshape(M, D), attn_out_w[0].astype(BF16))], tm=tm, tn=D)
    h = ffn_residual(h, norm_ffn[1], ffn_gate_w, ffn_up_w, ffn_down_w, norm_final, layer=1,
                     tm=tm_ffn, tf=tf, final_norm=True)
    return h.reshape(B, S, D)
```

```python
import functools

import jax
import jax.numpy as jnp
from jax import lax
from jax.experimental import pallas as pl
from jax.experimental.pallas import tpu as pltpu

F32 = jnp.float32
BF16 = jnp.bfloat16
HI = lax.Precision.HIGHEST

D_MODEL = 2048
RMS_EPS = 1e-6

GLA_HEADS = 4
GLA_DK = 128
GLA_DV = 256
GLA_KEY_WIDTH = GLA_HEADS * GLA_DK
GLA_WIDTH = GLA_HEADS * GLA_DV
GLA_GATE_RANK = 16
GLA_GATE_TAU = 16.0
GLA_CHUNK = 64

RWKV_WIDTH = 1024
RWKV_HEAD = 64
RWKV_PAIR = 2 * RWKV_HEAD
RWKV_PAIRS = RWKV_WIDTH // RWKV_PAIR
RWKV_DECAY_LORA = 64
RWKV_AAA_LORA = 64
RWKV_GATE_LORA = 160
RWKV_LN_EPS = RWKV_HEAD * 1e-5
RWKV_CHUNK = 64

MOBA_HEADS = 16
MOBA_HD = 128
MOBA_BLOCK = 256
MOBA_TOPK = 3
NEG_INF = -1e30

FFN_HIDDEN = 5632

LANES = 128

COL_GLA_Q = 0
COL_GLA_K = 512
COL_GLA_V = 1024
COL_GLA_OG = 2048
COL_RWKV_R = 3072
COL_RWKV_K = 4096
COL_RWKV_V = 5120
COL_LOW_GATE = 6144
COL_LOW_WA = 6272
COL_LOW_G = 6400
MIX_COLS = 6656

VMEM_LIMIT = 56 * 1024 * 1024


def _mm(a, b):
    return jnp.dot(a.astype(BF16), b.astype(BF16), preferred_element_type=F32)


def _mm_nt(a, b):
    return lax.dot_general(a.astype(BF16), b.astype(BF16), (((1,), (1,)), ((), ())),
                           preferred_element_type=F32)


def _mm_hi(a, b):
    return jnp.dot(a, b, preferred_element_type=F32, precision=HI)


def _mm_nt_hi(a, b):
    return lax.dot_general(a, b, (((1,), (1,)), ((), ())), preferred_element_type=F32,
                           precision=HI)


def _dot(a, b):
    return jnp.dot(a, b, preferred_element_type=F32)


def _dot_tn(a, b):
    return lax.dot_general(a, b, (((0,), (0,)), ((), ())), preferred_element_type=F32)


def _split(x, parts):
    out = []
    for _ in range(parts - 1):
        hi = x.astype(BF16)
        out.append(hi)
        x = x - hi.astype(F32)
    out.append(x.astype(BF16))
    return out


def _mm_split(a, b, *, a_parts=1, b_parts=1):
    a_p = _split(a, a_parts) if a_parts > 1 else [a.astype(BF16)]
    b_p = _split(b, b_parts) if b_parts > 1 else [b.astype(BF16)]
    acc = None
    for i, ai in enumerate(a_p):
        for j, bj in enumerate(b_p):
            if i + j < max(a_parts, b_parts):
                t = _dot(ai, bj)
                acc = t if acc is None else acc + t
    return acc


def _sigmoid(x):
    return 1.0 / (1.0 + jnp.exp(-x))


def _softplus(x):
    return jnp.maximum(x, 0.0) + jnp.log1p(jnp.exp(-jnp.abs(x)))


def _iota2(shape, axis):
    return lax.broadcasted_iota(jnp.int32, shape, axis)


def _norm_matmul_kernel(x_ref, g_ref, w_ref, o_ref, xn_ref, *, group):
    @pl.when(pl.program_id(1) == 0)
    def _():
        x = x_ref[...]
        ms = jnp.mean(x * x, axis=-1, keepdims=True)
        xn_ref[...] = (x * lax.rsqrt(ms + RMS_EPS) * g_ref[...]).astype(BF16)

    res = jnp.dot(xn_ref[...], w_ref[...], preferred_element_type=F32).astype(o_ref.dtype)
    if group is None:
        o_ref[...] = res
    else:
        for c in range(res.shape[1] // group):
            o_ref[c] = res[:, c * group:(c + 1) * group]


def norm_matmul(x, g, w, *, tm, tn, out_dtype=F32, group=None):
    M, D = x.shape
    N = w.shape[1]
    if group is None:
        out_specs = pl.BlockSpec((tm, tn), lambda i, j: (i, j))
        out_shape = jax.ShapeDtypeStruct((M, N), out_dtype)
    else:
        out_specs = pl.BlockSpec((tn // group, tm, group), lambda i, j: (j, i, 0))
        out_shape = jax.ShapeDtypeStruct((N // group, M, group), out_dtype)
    return pl.pallas_call(
        functools.partial(_norm_matmul_kernel, group=group),
        grid=(M // tm, N // tn),
        in_specs=[
            pl.BlockSpec((tm, D), lambda i, j: (i, 0)),
            pl.BlockSpec((1, D), lambda i, j: (0, 0)),
            pl.BlockSpec((D, tn), lambda i, j: (0, j)),
        ],
        out_specs=out_specs,
        out_shape=out_shape,
        scratch_shapes=[pltpu.VMEM((tm, D), BF16)],
        compiler_params=pltpu.CompilerParams(
            dimension_semantics=("parallel", "arbitrary"), vmem_limit_bytes=VMEM_LIMIT),
    )(x, g.reshape(1, D), w)


def _proj_res_kernel(*refs, n_in):
    res_ref = refs[0]
    o_ref = refs[1 + 2 * n_in]
    acc = res_ref[...]
    for i in range(n_in):
        acc = acc + jnp.dot(refs[1 + 2 * i][...], refs[2 + 2 * i][...],
                            preferred_element_type=F32)
    o_ref[...] = acc


def proj_residual(res, pairs, *, tm, tn):
    M, N = res.shape
    in_specs = [pl.BlockSpec((tm, tn), lambda i, j: (i, j))]
    args = [res]
    for a, w in pairs:
        K = a.shape[1]
        in_specs.append(pl.BlockSpec((tm, K), lambda i, j: (i, 0)))
        in_specs.append(pl.BlockSpec((K, tn), lambda i, j: (0, j)))
        args += [a, w]
    return pl.pallas_call(
        functools.partial(_proj_res_kernel, n_in=len(pairs)),
        grid=(M // tm, N // tn),
        in_specs=in_specs,
        out_specs=pl.BlockSpec((tm, tn), lambda i, j: (i, j)),
        out_shape=jax.ShapeDtypeStruct((M, N), F32),
        compiler_params=pltpu.CompilerParams(
            dimension_semantics=("parallel", "arbitrary"), vmem_limit_bytes=VMEM_LIMIT),
    )(*args)


def _ffn_kernel(x_ref, g_ref, wg_ref, wu_ref, wd_ref, gf_ref, o_ref, xn_ref, *, final_norm):
    j = pl.program_id(1)

    @pl.when(j == 0)
    def _():
        x = x_ref[...]
        ms = jnp.mean(x * x, axis=-1, keepdims=True)
        xn_ref[...] = (x * lax.rsqrt(ms + RMS_EPS) * g_ref[...]).astype(BF16)
        o_ref[...] = x

    xn = xn_ref[...]
    gate = jnp.dot(xn, wg_ref[...].astype(BF16), preferred_element_type=F32)
    up = jnp.dot(xn, wu_ref[...].astype(BF16), preferred_element_type=F32)
    act = (gate * _sigmoid(gate) * up).astype(BF16)
    o_ref[...] += jnp.dot(act, wd_ref[...].astype(BF16), preferred_element_type=F32)

    if final_norm:
        @pl.when(j == pl.num_programs(1) - 1)
        def _():
            h = o_ref[...]
            ms = jnp.mean(h * h, axis=-1, keepdims=True)
            o_ref[...] = h * lax.rsqrt(ms + RMS_EPS) * gf_ref[...]


def ffn_residual(x, g, wg, wu, wd, g_final, *, layer, tm, tf, final_norm):
    M, D = x.shape
    F = wg.shape[2]
    return pl.pallas_call(
        functools.partial(_ffn_kernel, final_norm=final_norm),
        grid=(M // tm, F // tf),
        in_specs=[
            pl.BlockSpec((tm, D), lambda i, j: (i, 0)),
            pl.BlockSpec((1, D), lambda i, j: (0, 0)),
            pl.BlockSpec((None, D, tf), lambda i, j: (layer, 0, j)),
            pl.BlockSpec((None, D, tf), lambda i, j: (layer, 0, j)),
            pl.BlockSpec((None, tf, D), lambda i, j: (layer, j, 0)),
            pl.BlockSpec((1, D), lambda i, j: (0, 0)),
        ],
        out_specs=pl.BlockSpec((tm, D), lambda i, j: (i, 0)),
        out_shape=jax.ShapeDtypeStruct((M, D), F32),
        scratch_shapes=[pltpu.VMEM((tm, D), BF16)],
        compiler_params=pltpu.CompilerParams(
            dimension_semantics=("parallel", "arbitrary"), vmem_limit_bytes=VMEM_LIMIT),
    )(x, g.reshape(1, D), wg, wu, wd, g_final.reshape(1, D))


def _gla_kernel(q_ref, k_ref, v_ref, og_ref, gl_ref, w2_ref, b_ref, gn_ref, o_ref, st_ref,
                *, block, chunk):
    n_heads = q_ref.shape[2] // GLA_DK

    @pl.when(pl.program_id(2) == 0)
    def _():
        st_ref[...] = jnp.zeros_like(st_ref)

    logit = _mm_split(gl_ref[0], w2_ref[...], a_parts=2, b_parts=2) + b_ref[...]
    log_a = -_softplus(-logit) * (1.0 / GLA_GATE_TAU)
    tri = _iota2((chunk, chunk), 0) >= _iota2((chunk, chunk), 1)
    tri_b = tri.astype(BF16)
    tri_b3 = jnp.concatenate([tri_b, tri_b, tri_b], axis=1)
    scale = GLA_DK ** -0.5
    gn = gn_ref[...]
    n_chunks = block // chunk

    streams = [(h, c) for h in range(n_heads) for c in range(n_chunks)]

    def rows(c):
        return pl.ds(c * chunk, chunk)

    def k_lanes(h):
        return slice(h * GLA_DK, (h + 1) * GLA_DK)

    def v_lanes(h):
        return slice(h * GLA_DV, (h + 1) * GLA_DV)

    la3 = [jnp.concatenate(_split(log_a[c * chunk:(c + 1) * chunk, k_lanes(h)], 3), axis=0)
           for h, c in streams]
    cum = [_dot(tri_b3, x) for x in la3]
    q_dec = [(q_ref[0, rows(c), k_lanes(h)] * scale * jnp.exp(b)).astype(BF16)
             for (h, c), b in zip(streams, cum)]
    k_inv = [(k_ref[0, rows(c), k_lanes(h)] * jnp.exp(-b)).astype(BF16)
             for (h, c), b in zip(streams, cum)]
    k_dec = [(k_ref[0, rows(c), k_lanes(h)] * jnp.exp(b[chunk - 1:chunk] - b)).astype(BF16)
             for (h, c), b in zip(streams, cum)]
    v_b = [v_ref[0, rows(c), v_lanes(h)].astype(BF16) for h, c in streams]
    v_t = [v_ref[0, rows(c), v_lanes(h)].T.astype(BF16) for h, c in streams]
    att = [jnp.where(tri, _mm_nt(x, y), 0.0).astype(BF16) for x, y in zip(q_dec, k_inv)]
    o_intra = [_dot(x, y) for x, y in zip(att, v_b)]
    gain = [_dot(x, y) for x, y in zip(v_t, k_dec)]
    pre = [dict(q_dec=q_dec[i], o_intra=o_intra[i], gain=gain[i],
                decay=jnp.exp(cum[i][chunk - 1:chunk])) for i in range(len(streams))]

    st = [st_ref[h] for h in range(n_heads)]
    outs = [[] for _ in range(n_heads)]
    for c in range(n_chunks):
        for h in range(n_heads):
            p = pre[h * n_chunks + c]
            outs[h].append(p["o_intra"] + _mm_nt(p["q_dec"], st[h]))
            st[h] = st[h] * p["decay"] + p["gain"]
    for h in range(n_heads):
        st_ref[h] = st[h]

    for h in range(n_heads):
        lv = slice(h * GLA_DV, (h + 1) * GLA_DV)
        o = jnp.concatenate(outs[h], axis=0)
        ms = jnp.mean(o * o, axis=-1, keepdims=True)
        og = og_ref[0, :, lv]
        o_ref[0, :, lv] = (o * lax.rsqrt(ms + RMS_EPS) * gn * (og * _sigmoid(og))).astype(o_ref.dtype)


def gla_mixer(p3, gate_w2p, gate_b, gla_norm, *, block, heads):
    B, S, _ = p3.shape
    wk, wv = heads * GLA_DK, heads * GLA_DV
    kq, kk_, kv, kog = COL_GLA_Q // wk, COL_GLA_K // wk, COL_GLA_V // wv, COL_GLA_OG // wv
    klow = COL_LOW_GATE // LANES
    return pl.pallas_call(
        functools.partial(_gla_kernel, block=block, chunk=GLA_CHUNK),
        grid=(B, GLA_HEADS // heads, S // block),
        in_specs=[
            pl.BlockSpec((1, block, wk), lambda b, h, s: (b, s, kq + h)),
            pl.BlockSpec((1, block, wk), lambda b, h, s: (b, s, kk_ + h)),
            pl.BlockSpec((1, block, wv), lambda b, h, s: (b, s, kv + h)),
            pl.BlockSpec((1, block, wv), lambda b, h, s: (b, s, kog + h)),
            pl.BlockSpec((1, block, LANES), lambda b, h, s: (b, s, klow)),
            pl.BlockSpec((LANES, wk), lambda b, h, s: (0, h)),
            pl.BlockSpec((1, wk), lambda b, h, s: (0, h)),
            pl.BlockSpec((1, GLA_DV), lambda b, h, s: (0, 0)),
        ],
        out_specs=pl.BlockSpec((1, block, wv), lambda b, h, s: (b, s, h)),
        out_shape=jax.ShapeDtypeStruct((B, S, GLA_WIDTH), BF16),
        scratch_shapes=[pltpu.VMEM((heads, GLA_DV, GLA_DK), F32)],
        compiler_params=pltpu.CompilerParams(
            dimension_semantics=("parallel", "parallel", "arbitrary"),
            vmem_limit_bytes=VMEM_LIMIT),
    )(p3, p3, p3, p3, p3, gate_w2p, gate_b.reshape(1, -1), gla_norm.reshape(1, -1))


def _unit_lower_inverses(mats, rowi, coli):
    def same_block(s):
        sh = s.bit_length() - 1
        return (rowi >> sh) == (coli >> sh)

    eye = (rowi == coli).astype(F32)
    a8 = [jnp.where(same_block(8), a, 0.0).astype(BF16) for a in mats]
    inv = [eye + a.astype(F32) for a in a8]
    p = [_dot(a, a) for a in a8]
    inv = [x + _mm(x, y) for x, y in zip(inv, p)]
    p = [_mm(y, y) for y in p]
    inv = [x + _mm(x, y) for x, y in zip(inv, p)]
    s = 8
    while s < RWKV_CHUNK:
        off = same_block(2 * s) & jnp.logical_not(same_block(s))
        e = [jnp.where(off, a, 0.0).astype(BF16) for a in mats]
        inv_b = [x.astype(BF16) for x in inv]
        t = [_dot(x, y).astype(BF16) for x, y in zip(inv_b, e)]
        inv = [x + _dot(y, z) for x, y, z in zip(inv, t, inv_b)]
        s *= 2
    return inv


def _rwkv_kernel(r_ref, k_ref, v_ref, wa_ref, g_ref, vec_ref, muwa_ref, mug_ref, w2_ref, a2_ref,
                 g2_ref, o_ref, st_ref, prev_ref, prevwa_ref, prevg_ref, *, block, chunk):
    T = chunk
    n_pairs = r_ref.shape[2] // RWKV_PAIR
    first = pl.program_id(2) == 0

    @pl.when(first)
    def _():
        st_ref[...] = jnp.zeros_like(st_ref)
        prev_ref[...] = jnp.zeros_like(prev_ref)
        prevwa_ref[...] = jnp.zeros_like(prevwa_ref)
        prevg_ref[...] = jnp.zeros_like(prevg_ref)

    row0 = _iota2((block, 1), 0) == 0

    def lerp(x, prev_row, mu):
        shifted = jnp.where(row0, prev_row, pltpu.roll(x, 1, axis=0))
        return x + (shifted - x) * mu

    vec = vec_ref[...]
    w0, a0, k_k, k_a, r_k, ln_w, ln_b = (vec[i:i + 1] for i in range(7))
    mu_r, mu_k, mu_v = (vec[i:i + 1] for i in range(7, 10))

    r_raw, k_raw, v_raw, wa_raw, g_raw = r_ref[0], k_ref[0], v_ref[0], wa_ref[0], g_ref[0]
    prev = prev_ref[...]
    r_all = lerp(r_raw, prev[0:1], mu_r)
    k_all = lerp(k_raw, prev[1:2], mu_k)
    v_all = lerp(v_raw, prev[2:3], mu_v)
    wa = lerp(wa_raw, prevwa_ref[0:1], muwa_ref[...])
    g_low = lerp(g_raw, prevg_ref[0:1], mug_ref[...])
    prev_ref[0:1] = r_raw[block - 1:block]
    prev_ref[1:2] = k_raw[block - 1:block]
    prev_ref[2:3] = v_raw[block - 1:block]
    prevwa_ref[0:1] = wa_raw[block - 1:block]
    prevg_ref[0:1] = g_raw[block - 1:block]

    z = w0 + _mm_split(jnp.tanh(wa), w2_ref[...], a_parts=2, b_parts=2)
    logw_all = -jnp.exp(-_softplus(-z) - 0.5)
    lr_all = _sigmoid(a0 + _mm_split(wa, a2_ref[...], a_parts=2, b_parts=2))
    gate_all = _mm_split(_sigmoid(g_low), g2_ref[...], a_parts=2, b_parts=2)

    lane = _iota2((1, RWKV_PAIR), 1)
    m0 = (lane < RWKV_HEAD).astype(F32)
    m1 = 1.0 - m0
    rowi = _iota2((2 * T, 2 * T), 0)
    coli = _iota2((2 * T, 2 * T), 1)
    strict = rowi > coli
    incl = rowi >= coli
    head_ones = ((rowi < RWKV_HEAD) == (coli < RWKV_HEAD)).astype(BF16)
    head_ones2 = jnp.concatenate([head_ones, head_ones], axis=0)
    tri_t = (_iota2((T, T), 0) >= _iota2((T, T), 1)).astype(BF16)
    tri_t3 = jnp.concatenate([tri_t, tri_t, tri_t], axis=1)
    ones_t3 = jnp.ones((3 * T, RWKV_PAIR), BF16)

    def hsum(x):
        return _dot(jnp.concatenate(_split(x, 2), axis=1), head_ones2)

    def stack(x):
        return jnp.concatenate([x * m0, x * m1], axis=0)

    n_chunks = block // T

    kk_p, k2_p, b_p, bonus = [], [], [], []
    for pair in range(n_pairs):
        ln = slice(pair * RWKV_PAIR, (pair + 1) * RWKV_PAIR)
        kkp = k_all[:, ln] * k_k[:, ln]
        kk_p.append(kkp / jnp.maximum(jnp.sqrt(hsum(kkp * kkp)), 1e-12))
        k2_p.append(k_all[:, ln] * (1.0 + (lr_all[:, ln] - 1.0) * k_a[:, ln]))
        b_p.append(kk_p[pair] * lr_all[:, ln])
        bonus.append(hsum(r_all[:, ln] * k2_p[pair] * r_k[:, ln]) * v_all[:, ln])

    streams = [(pair, c) for pair in range(n_pairs) for c in range(n_chunks)]

    def tile(x, pair, c):
        return x[c * T:(c + 1) * T, pair * RWKV_PAIR:(pair + 1) * RWKV_PAIR]

    lw = [tile(logw_all, p, c) for p, c in streams]
    lw3 = [jnp.concatenate(_split(x, 3), axis=0) for x in lw]
    cum = [_dot(tri_t3, x) for x in lw3]
    decay = [jnp.exp(_dot_tn(x, ones_t3)) for x in lw3]
    w_inv = [jnp.exp(-x) for x in cum]
    w_tail = [jnp.exp(x[T - 1:T] - x) for x in cum]
    a_s = [stack(-kk_p[p][c * T:(c + 1) * T] * jnp.exp(x - y)).astype(BF16)
           for (p, c), x, y in zip(streams, cum, lw)]
    r_s = [stack(tile(r_all, p, c) * jnp.exp(x)).astype(BF16) for (p, c), x in zip(streams, cum)]
    b_s = [stack(b_p[p][c * T:(c + 1) * T] * x).astype(BF16) for (p, c), x in zip(streams, w_inv)]
    k_s = [stack(k2_p[p][c * T:(c + 1) * T] * x).astype(BF16) for (p, c), x in zip(streams, w_inv)]
    v_s = [stack(tile(v_all, p, c)).astype(BF16) for p, c in streams]
    b_tail = [stack(b_p[p][c * T:(c + 1) * T] * x).astype(BF16)
              for (p, c), x in zip(streams, w_tail)]
    k_tail = [stack(k2_p[p][c * T:(c + 1) * T] * x).astype(BF16)
              for (p, c), x in zip(streams, w_tail)]
    prod = [_mm_nt(jnp.concatenate([a, r], axis=0), jnp.concatenate([b, k], axis=0))
            for a, r, b, k in zip(a_s, r_s, b_s, k_s)]
    pre = [dict(a_s=a_s[i], r_s=r_s[i], v_s=v_s[i], decay=decay[i], b_tail=b_tail[i],
                k_tail=k_tail[i],
                a_ab=jnp.where(strict, prod[i][:2 * T, :2 * T], 0.0),
                a_ak=jnp.where(strict, prod[i][:2 * T, 2 * T:], 0.0).astype(BF16),
                a_rb=jnp.where(incl, prod[i][2 * T:, :2 * T], 0.0).astype(BF16),
                a_rk=jnp.where(incl, prod[i][2 * T:, 2 * T:], 0.0).astype(BF16))
           for i in range(len(streams))]

    invs = [x.astype(BF16) for x in _unit_lower_inverses([p["a_ab"] for p in pre], rowi, coli)]
    akv = [_dot(p["a_ak"], p["v_s"]).astype(BF16) for p in pre]
    x = [_dot(i, jnp.concatenate([p["a_s"], y], axis=1)).astype(BF16)
         for i, p, y in zip(invs, pre, akv)]
    bx = [_dot_tn(p["b_tail"], y) for p, y in zip(pre, x)]
    rx = [_dot(p["a_rb"], y) for p, y in zip(pre, x)]
    trans = [y[:, :RWKV_PAIR].astype(BF16) for y in bx]
    gain = [y[:, RWKV_PAIR:] + _dot_tn(p["k_tail"], p["v_s"]) for p, y in zip(pre, bx)]
    read = [(p["r_s"].astype(F32) + y[:, :RWKV_PAIR]).astype(BF16) for p, y in zip(pre, rx)]
    y_free = [y[:, RWKV_PAIR:] + _dot(p["a_rk"], p["v_s"]) for p, y in zip(pre, rx)]

    st = [st_ref[pair] for pair in range(n_pairs)]
    ys = [[] for _ in range(n_pairs)]
    for c in range(n_chunks):
        for pair in range(n_pairs):
            i = pair * n_chunks + c
            st_b = st[pair].astype(BF16)
            y_s = _dot(read[i], st_b) + y_free[i]
            ys[pair].append(y_s[:T] + y_s[T:])
            st[pair] = st[pair] * pre[i]["decay"] + _dot(trans[i], st_b) + gain[i]
    for pair in range(n_pairs):
        st_ref[pair] = st[pair]

    for pair in range(n_pairs):
        ln = slice(pair * RWKV_PAIR, (pair + 1) * RWKV_PAIR)
        y = jnp.concatenate(ys[pair], axis=0)
        mu = hsum(y) * (1.0 / RWKV_HEAD)
        d = y - mu
        var = hsum(d * d) * (1.0 / RWKV_HEAD)
        yn = d * lax.rsqrt(var + RWKV_LN_EPS) * ln_w[:, ln] + ln_b[:, ln]
        o_ref[0, :, ln] = ((yn + bonus[pair]) * gate_all[:, ln]).astype(o_ref.dtype)


def rwkv_mixer(p3, vecs, mu_wa, mu_g, w2p, a2p, g2p, *, block, pairs):
    B, S, _ = p3.shape
    W = pairs * RWKV_PAIR
    kr, kk_, kv = COL_RWKV_R // W, COL_RWKV_K // W, COL_RWKV_V // W
    kwa = COL_LOW_WA // LANES
    kg = COL_LOW_G // (2 * LANES)
    return pl.pallas_call(
        functools.partial(_rwkv_kernel, block=block, chunk=RWKV_CHUNK),
        grid=(B, RWKV_PAIRS // pairs, S // block),
        in_specs=[
            pl.BlockSpec((1, block, W), lambda b, j, s: (b, s, kr + j)),
            pl.BlockSpec((1, block, W), lambda b, j, s: (b, s, kk_ + j)),
            pl.BlockSpec((1, block, W), lambda b, j, s: (b, s, kv + j)),
            pl.BlockSpec((1, block, LANES), lambda b, j, s: (b, s, kwa)),
            pl.BlockSpec((1, block, 2 * LANES), lambda b, j, s: (b, s, kg)),
            pl.BlockSpec((16, W), lambda b, j, s: (0, j)),
            pl.BlockSpec((1, LANES), lambda b, j, s: (0, 0)),
            pl.BlockSpec((1, 2 * LANES), lambda b, j, s: (0, 0)),
            pl.BlockSpec((LANES, W), lambda b, j, s: (0, j)),
            pl.BlockSpec((LANES, W), lambda b, j, s: (0, j)),
            pl.BlockSpec((2 * LANES, W), lambda b, j, s: (0, j)),
        ],
        out_specs=pl.BlockSpec((1, block, W), lambda b, j, s: (b, s, j)),
        out_shape=jax.ShapeDtypeStruct((B, S, RWKV_WIDTH), BF16),
        scratch_shapes=[
            pltpu.VMEM((pairs, RWKV_PAIR, RWKV_PAIR), F32),
            pltpu.VMEM((8, W), F32),
            pltpu.VMEM((8, LANES), F32),
            pltpu.VMEM((8, 2 * LANES), F32),
        ],
        compiler_params=pltpu.CompilerParams(
            dimension_semantics=("parallel", "parallel", "arbitrary"),
            vmem_limit_bytes=VMEM_LIMIT),
    )(p3, p3, p3, p3, p3, vecs, mu_wa, mu_g, w2p, a2p, g2p)


def _moba_kernel(q_ref, k_ref, v_ref, slope_ref, o_ref, kb_ref, vt_ref, kmean_ref,
                 m_ref, l_ref, acc_ref, s_ref, p_ref, *, nb):
    BS, D = MOBA_BLOCK, MOBA_HD
    QT = 2 * BS
    heads = q_ref.shape[0]
    g = pl.program_id(2)
    log2e = 1.4426950408889634
    c1 = MOBA_HD ** -0.5 * log2e
    slope2 = [slope_ref[h] * log2e for h in range(heads)]
    s_parts = [_split(x, 3) for x in slope2]

    @pl.when(g == 0)
    def _():
        kmean_ref[...] = jnp.zeros_like(kmean_ref)
        lane = _iota2((QT, LANES), 1)
        kpos = _iota2((QT, LANES), 0)
        kpos_lo = (kpos & (BS - 1)).astype(F32)
        kpos_hi = (kpos & BS).astype(F32)
        for h in range(heads):
            sk = [x[:, :LANES].astype(F32) for x in s_parts[h]]
            piece = jnp.where((lane == 0) | (lane == 3), sk[0],
                              jnp.where((lane == 1) | (lane == 4), sk[1], sk[2]))
            feat = jnp.where(lane < 6, piece,
                             jnp.where(lane < 9, kpos_lo, jnp.where(lane < 12, kpos_hi, 0.0)))
            feat = feat.astype(BF16)
            for j in range(nb):
                half = slice((j % 2) * BS, (j % 2 + 1) * BS)
                kj = k_ref[h, pl.ds(j * BS, BS), :]
                kmean_ref[h, j:j + 1, :] = jnp.mean(kj, axis=0, keepdims=True)
                kb_ref[h, j // 2, half, :D] = kj.astype(BF16)
                vt_ref[h, j // 2, :, half] = v_ref[h, pl.ds(j * BS, BS), :].T.astype(BF16)
            for t in range(nb // 2):
                kb_ref[h, t, :, D:] = feat

    qpos = _iota2((1, QT), 1)
    blk_q = 2 * g + (qpos >= BS).astype(jnp.int32)
    nbp = kmean_ref.shape[1]
    blk = _iota2((nbp, QT), 0)
    arow = _iota2((LANES, QT), 0)
    qpos_a = _iota2((LANES, QT), 1)
    qpos_lo = (qpos_a & (BS - 1)).astype(F32)
    qpos_hi = (qpos_a & BS).astype(F32)
    kp = _iota2((QT, QT), 0)
    qp = _iota2((QT, QT), 1)
    own = (kp <= qp) & ((kp >= BS) == (qp >= BS))
    first_block = (kp <= qp) & (kp < BS)

    def select(h):
        q_t = q_ref[h].T
        gate = _mm_split(kmean_ref[h], q_t, a_parts=3, b_parts=3)
        gate = jnp.where(blk < blk_q, gate, NEG_INF)
        sel = []
        for r in range(MOBA_TOPK):
            mx = jnp.max(gate, axis=0, keepdims=True)
            idx = jnp.min(jnp.where(gate == mx, blk, nbp), axis=0, keepdims=True)
            sel.append(jnp.where(r < blk_q, idx, -1))
            gate = jnp.where(blk == idx, -jnp.inf, gate)
        sq = [x.astype(F32) for x in s_parts[h]]
        piece = jnp.where((arow == 6) | (arow == 9), sq[0],
                          jnp.where((arow == 7) | (arow == 10), sq[1], sq[2]))
        aug = jnp.where(arow < 3, -qpos_lo,
                        jnp.where(arow < 6, -qpos_hi, jnp.where(arow < 12, piece, 0.0)))
        q_aug = jnp.concatenate([(q_t * c1).astype(BF16), aug.astype(BF16)], axis=0)
        return sel, q_aug

    sel, q_aug = zip(*[select(h) for h in range(heads)])

    def picked(h, j):
        return (sel[h][0] == j) | (sel[h][1] == j) | (sel[h][2] == j)

    for h in range(heads):
        allowed = own | (first_block & picked(h, 2 * g))
        s_own = jnp.where(allowed, _dot(kb_ref[h, g], q_aug[h]), NEG_INF)
        m0 = jnp.max(s_own, axis=0, keepdims=True)
        p0 = jnp.exp2(s_own - m0)
        m_ref[h] = m0
        l_ref[h] = jnp.sum(p0, axis=0, keepdims=True)
        p_ref[h] = p0.astype(BF16)
        acc_ref[h] = jnp.zeros((D, QT), F32)
        s_ref[h, 0] = _dot(kb_ref[h, 0], q_aug[h])

    def past_pair(h, t, prev):
        s_cur = s_ref[h, t % 2]
        pv = _dot(vt_ref[h, prev], p_ref[h])
        s_ref[h, (t + 1) % 2] = _dot(kb_ref[h, jnp.minimum(t + 1, nb // 2 - 1)], q_aug[h])
        s_a = jnp.where(picked(h, 2 * t), s_cur[:BS], NEG_INF)
        s_b = jnp.where(picked(h, 2 * t + 1), s_cur[BS:], NEG_INF)
        off = slope2[h] * ((g - t) * QT).astype(F32)
        m_old = m_ref[h]
        mx = jnp.maximum(jnp.max(s_a, axis=0, keepdims=True), jnp.max(s_b, axis=0, keepdims=True))
        m_new = jnp.maximum(m_old, mx - off)
        alpha = jnp.exp2(m_old - m_new)
        shift = m_new + off
        p_a = jnp.exp2(s_a - shift)
        p_b = jnp.exp2(s_b - shift)
        l_ref[h] = (alpha * l_ref[h] + jnp.sum(p_a, axis=0, keepdims=True)
                    + jnp.sum(p_b, axis=0, keepdims=True))
        acc_ref[h] = alpha * (acc_ref[h] + pv)
        p_ref[h, :BS, :] = p_a.astype(BF16)
        p_ref[h, BS:, :] = p_b.astype(BF16)
        m_ref[h] = m_new
        return t

    last = [lax.fori_loop(0, g, functools.partial(past_pair, h), g) for h in range(heads)]
    for h in range(heads):
        acc = acc_ref[h] + _dot(vt_ref[h, last[h]], p_ref[h])
        o_ref[0, :, h * D:(h + 1) * D] = (acc / l_ref[h]).T.astype(o_ref.dtype)


def moba_attention(qkv_h, B, S, *, heads):
    H, D, BS = MOBA_HEADS, MOBA_HD, MOBA_BLOCK
    QT = 2 * BS
    nb = S // BS
    assert nb % 2 == 0, "key blocks are stored in pairs"
    nt = nb // 2
    nbp = -(-nb // 8) * 8
    hg = H // heads
    slopes = jnp.exp2(-8.0 * jnp.arange(1, H + 1, dtype=F32) / H)
    slopes = jnp.broadcast_to(slopes[:, None, None], (H, 1, QT))
    return pl.pallas_call(
        functools.partial(_moba_kernel, nb=nb),
        grid=(B, hg, nt),
        in_specs=[
            pl.BlockSpec((heads, QT, D), lambda b, h, g: (h, b * nt + g, 0)),
            pl.BlockSpec((heads, S, D), lambda b, h, g: (hg + h, b, 0)),
            pl.BlockSpec((heads, S, D), lambda b, h, g: (2 * hg + h, b, 0)),
            pl.BlockSpec((heads, 1, QT), lambda b, h, g: (h, 0, 0)),
        ],
        out_specs=pl.BlockSpec((1, QT, heads * D), lambda b, h, g: (b, g, h)),
        out_shape=jax.ShapeDtypeStruct((B, S, H * D), BF16),
        scratch_shapes=[
            pltpu.VMEM((heads, nt, QT, D + LANES), BF16),
            pltpu.VMEM((heads, nt, D, QT), BF16),
            pltpu.VMEM((heads, nbp, D), F32),
            pltpu.VMEM((heads, 1, QT), F32),
            pltpu.VMEM((heads, 1, QT), F32),
            pltpu.VMEM((heads, D, QT), F32),
            pltpu.VMEM((heads, 2, QT, QT), F32),
            pltpu.VMEM((heads, QT, QT), BF16),
        ],
        compiler_params=pltpu.CompilerParams(
            dimension_semantics=("parallel", "parallel", "arbitrary"),
            vmem_limit_bytes=VMEM_LIMIT),
    )(qkv_h, qkv_h, qkv_h, slopes)


def _pad_cols(w, n):
    return jnp.pad(w, ((0, 0), (0, n - w.shape[1])))


def _pad_rows(w, before, total):
    return jnp.pad(w, ((before, total - before - w.shape[0]), (0, 0)))


def _mix_in_layout(w):
    gk, gw = GLA_KEY_WIDTH, GLA_WIDTH
    g_end = 2 * gk + 2 * gw
    gla_main = w[:, :g_end]
    gla_gate = w[:, g_end:g_end + GLA_GATE_RANK]
    r0 = g_end + GLA_GATE_RANK
    rw_main = w[:, r0:r0 + 3 * RWKV_WIDTH]
    l0 = r0 + 3 * RWKV_WIDTH
    wa = w[:, l0:l0 + RWKV_DECAY_LORA + RWKV_AAA_LORA]
    g0 = l0 + RWKV_DECAY_LORA + RWKV_AAA_LORA
    gl = w[:, g0:g0 + RWKV_GATE_LORA]
    return jnp.concatenate(
        [gla_main, rw_main, _pad_cols(gla_gate, LANES), wa, _pad_cols(gl, 2 * LANES)], axis=1)


def mixer_layer0(hn_proj, gla_gate_w2, gla_gate_b, gla_norm, rwkv_mu, rwkv_w0, rwkv_w2, rwkv_a0,
                 rwkv_a2, rwkv_g2, rwkv_k_k, rwkv_k_a, rwkv_r_k, rwkv_ln_w, rwkv_ln_b,
                 *, gla_block, rwkv_block):
    W = RWKV_WIDTH
    o_gla = gla_mixer(hn_proj, _pad_rows(gla_gate_w2, 0, LANES), gla_gate_b, gla_norm,
                      block=gla_block, heads=4)
    mu_r, mu_k, mu_v = rwkv_mu[:W], rwkv_mu[W:2 * W], rwkv_mu[2 * W:3 * W]
    mu_low = rwkv_mu[3 * W:]
    n_wa = RWKV_DECAY_LORA + RWKV_AAA_LORA
    vecs = jnp.stack([rwkv_w0, rwkv_a0, rwkv_k_k, rwkv_k_a, rwkv_r_k.reshape(-1), rwkv_ln_w,
                      rwkv_ln_b, mu_r, mu_k, mu_v] + [jnp.zeros((W,), F32)] * 6)
    mu_wa = mu_low[:n_wa].reshape(1, -1)
    mu_g = _pad_cols(mu_low[n_wa:].reshape(1, -1), 2 * LANES)
    w2p = _pad_rows(rwkv_w2, 0, LANES)
    a2p = _pad_rows(rwkv_a2, RWKV_DECAY_LORA, LANES)
    g2p = _pad_rows(rwkv_g2, 0, 2 * LANES)
    o_rwkv = rwkv_mixer(hn_proj, vecs, mu_wa, mu_g, w2p, a2p, g2p, block=rwkv_block, pairs=4)
    return o_gla, o_rwkv


def kernel(x, norm_mix, norm_ffn, norm_final, mix_in_w, gla_gate_w2, gla_gate_b, gla_norm, rwkv_mu, rwkv_w0, rwkv_w2, rwkv_a0, rwkv_a2, rwkv_g2, rwkv_k_k, rwkv_k_a, rwkv_r_k, rwkv_ln_w, rwkv_ln_b, mix_out_w, attn_qkv_w, attn_out_w, ffn_gate_w, ffn_up_w, ffn_down_w):
    B, S, D = x.shape
    M = B * S
    tm, tn, tf = 512, 512, 256
    tm_in = min(1024, M)
    tm_ffn = min(1024, M)
    h = x.reshape(M, D)

    w_in = _mix_in_layout(mix_in_w[0]).astype(BF16)
    p = norm_matmul(h, norm_mix[0], w_in, tm=tm_in, tn=tn).reshape(B, S, MIX_COLS)
    o_gla, o_rwkv = mixer_layer0(
        p, gla_gate_w2[0], gla_gate_b[0], gla_norm[0], rwkv_mu[0], rwkv_w0[0], rwkv_w2[0],
        rwkv_a0[0], rwkv_a2[0], rwkv_g2[0], rwkv_k_k[0], rwkv_k_a[0], rwkv_r_k[0], rwkv_ln_w[0],
        rwkv_ln_b[0], gla_block=512, rwkv_block=512)
    w_out = mix_out_w[0].astype(BF16)
    h = proj_residual(h, [(o_gla.reshape(M, GLA_WIDTH), w_out[:GLA_WIDTH]),
                          (o_rwkv.reshape(M, RWKV_WIDTH), w_out[GLA_WIDTH:])], tm=tm, tn=D)
    h = ffn_residual(h, norm_ffn[0], ffn_gate_w, ffn_up_w, ffn_down_w, norm_final, layer=0,
                     tm=tm_ffn, tf=tf, final_norm=False)

    qkv_h = norm_matmul(h, norm_mix[1], attn_qkv_w[0].astype(BF16), tm=tm_in, tn=tn,
                        group=MOBA_HD)
    o_attn = moba_attention(qkv_h, B, S, heads=2)
    h = proj_residual(h, [(o_attn.reshape(M, D), attn_out_w[0].astype(BF16))], tm=tm, tn=D)
    h = ffn_residual(h, norm_ffn[1], ffn_gate_w, ffn_up_w, ffn_down_w, norm_final, layer=1,
                     tm=tm_ffn, tf=tf, final_norm=True)
    return h.reshape(B, S, D)
```

```python
import functools

import jax
import jax.numpy as jnp
from jax import lax
from jax.experimental import pallas as pl
from jax.experimental.pallas import tpu as pltpu

F32 = jnp.float32
BF16 = jnp.bfloat16
HI = lax.Precision.HIGHEST

D_MODEL = 2048
RMS_EPS = 1e-6

GLA_HEADS = 4
GLA_DK = 128
GLA_DV = 256
GLA_KEY_WIDTH = GLA_HEADS * GLA_DK
GLA_WIDTH = GLA_HEADS * GLA_DV
GLA_GATE_RANK = 16
GLA_GATE_TAU = 16.0
GLA_CHUNK = 64

RWKV_WIDTH = 1024
RWKV_HEAD = 64
RWKV_PAIR = 2 * RWKV_HEAD
RWKV_PAIRS = RWKV_WIDTH // RWKV_PAIR
RWKV_DECAY_LORA = 64
RWKV_AAA_LORA = 64
RWKV_GATE_LORA = 160
RWKV_LN_EPS = RWKV_HEAD * 1e-5
RWKV_CHUNK = 64

MOBA_HEADS = 16
MOBA_HD = 128
MOBA_BLOCK = 256
MOBA_TOPK = 3
NEG_INF = -1e30

FFN_HIDDEN = 5632

LANES = 128

COL_GLA_Q = 0
COL_GLA_K = 512
COL_GLA_V = 1024
COL_GLA_OG = 2048
COL_RWKV_R = 3072
COL_RWKV_K = 4096
COL_RWKV_V = 5120
COL_LOW_GATE = 6144
COL_LOW_WA = 6272
COL_LOW_G = 6400
MIX_COLS = 6656

VMEM_LIMIT = 56 * 1024 * 1024


def _mm(a, b):
    return jnp.dot(a.astype(BF16), b.astype(BF16), preferred_element_type=F32)


def _mm_nt(a, b):
    return lax.dot_general(a.astype(BF16), b.astype(BF16), (((1,), (1,)), ((), ())),
                           preferred_element_type=F32)


def _mm_hi(a, b):
    return jnp.dot(a, b, preferred_element_type=F32, precision=HI)


def _mm_nt_hi(a, b):
    return lax.dot_general(a, b, (((1,), (1,)), ((), ())), preferred_element_type=F32,
                           precision=HI)


def _dot(a, b):
    return jnp.dot(a, b, preferred_element_type=F32)


def _dot_tn(a, b):
    return lax.dot_general(a, b, (((0,), (0,)), ((), ())), preferred_element_type=F32)


def _split(x, parts):
    out = []
    for _ in range(parts - 1):
        hi = x.astype(BF16)
        out.append(hi)
        x = x - hi.astype(F32)
    out.append(x.astype(BF16))
    return out


def _mm_split(a, b, *, a_parts=1, b_parts=1):
    a_p = _split(a, a_parts) if a_parts > 1 else [a.astype(BF16)]
    b_p = _split(b, b_parts) if b_parts > 1 else [b.astype(BF16)]
    acc = None
    for i, ai in enumerate(a_p):
        for j, bj in enumerate(b_p):
            if i + j < max(a_parts, b_parts):
                t = _dot(ai, bj)
                acc = t if acc is None else acc + t
    return acc


def _sigmoid(x):
    return 1.0 / (1.0 + jnp.exp(-x))


def _softplus(x):
    return jnp.maximum(x, 0.0) + jnp.log1p(jnp.exp(-jnp.abs(x)))


def _iota2(shape, axis):
    return lax.broadcasted_iota(jnp.int32, shape, axis)


def _norm_matmul_kernel(x_ref, g_ref, w_ref, o_ref, xn_ref, *, group):
    @pl.when(pl.program_id(1) == 0)
    def _():
        x = x_ref[...]
        ms = jnp.mean(x * x, axis=-1, keepdims=True)
        xn_ref[...] = (x * lax.rsqrt(ms + RMS_EPS) * g_ref[...]).astype(BF16)

    res = jnp.dot(xn_ref[...], w_ref[...], preferred_element_type=F32).astype(o_ref.dtype)
    if group is None:
        o_ref[...] = res
    else:
        for c in range(res.shape[1] // group):
            o_ref[c] = res[:, c * group:(c + 1) * group]


def norm_matmul(x, g, w, *, tm, tn, out_dtype=F32, group=None):
    M, D = x.shape
    N = w.shape[1]
    if group is None:
        out_specs = pl.BlockSpec((tm, tn), lambda i, j: (i, j))
        out_shape = jax.ShapeDtypeStruct((M, N), out_dtype)
    else:
        out_specs = pl.BlockSpec((tn // group, tm, group), lambda i, j: (j, i, 0))
        out_shape = jax.ShapeDtypeStruct((N // group, M, group), out_dtype)
    return pl.pallas_call(
        functools.partial(_norm_matmul_kernel, group=group),
        grid=(M // tm, N // tn),
        in_specs=[
            pl.BlockSpec((tm, D), lambda i, j: (i, 0)),
            pl.BlockSpec((1, D), lambda i, j: (0, 0)),
            pl.BlockSpec((D, tn), lambda i, j: (0, j)),
        ],
        out_specs=out_specs,
        out_shape=out_shape,
        scratch_shapes=[pltpu.VMEM((tm, D), BF16)],
        compiler_params=pltpu.CompilerParams(
            dimension_semantics=("parallel", "arbitrary"), vmem_limit_bytes=VMEM_LIMIT),
    )(x, g.reshape(1, D), w)


def _proj_res_kernel(*refs, n_in):
    res_ref = refs[0]
    o_ref = refs[1 + 2 * n_in]
    acc = res_ref[...]
    for i in range(n_in):
        acc = acc + jnp.dot(refs[1 + 2 * i][...], refs[2 + 2 * i][...],
                            preferred_element_type=F32)
    o_ref[...] = acc


def proj_residual(res, pairs, *, tm, tn):
    M, N = res.shape
    in_specs = [pl.BlockSpec((tm, tn), lambda i, j: (i, j))]
    args = [res]
    for a, w in pairs:
        K = a.shape[1]
        in_specs.append(pl.BlockSpec((tm, K), lambda i, j: (i, 0)))
        in_specs.append(pl.BlockSpec((K, tn), lambda i, j: (0, j)))
        args += [a, w]
    return pl.pallas_call(
        functools.partial(_proj_res_kernel, n_in=len(pairs)),
        grid=(M // tm, N // tn),
        in_specs=in_specs,
        out_specs=pl.BlockSpec((tm, tn), lambda i, j: (i, j)),
        out_shape=jax.ShapeDtypeStruct((M, N), F32),
        compiler_params=pltpu.CompilerParams(
            dimension_semantics=("parallel", "arbitrary"), vmem_limit_bytes=VMEM_LIMIT),
    )(*args)


def _ffn_kernel(x_ref, g_ref, wg_ref, wu_ref, wd_ref, gf_ref, o_ref, xn_ref, *, final_norm):
    j = pl.program_id(1)

    @pl.when(j == 0)
    def _():
        x = x_ref[...]
        ms = jnp.mean(x * x, axis=-1, keepdims=True)
        xn_ref[...] = (x * lax.rsqrt(ms + RMS_EPS) * g_ref[...]).astype(BF16)
        o_ref[...] = x

    xn = xn_ref[...]
    gate = jnp.dot(xn, wg_ref[...].astype(BF16), preferred_element_type=F32)
    up = jnp.dot(xn, wu_ref[...].astype(BF16), preferred_element_type=F32)
    act = (gate * _sigmoid(gate) * up).astype(BF16)
    o_ref[...] += jnp.dot(act, wd_ref[...].astype(BF16), preferred_element_type=F32)

    if final_norm:
        @pl.when(j == pl.num_programs(1) - 1)
        def _():
            h = o_ref[...]
            ms = jnp.mean(h * h, axis=-1, keepdims=True)
            o_ref[...] = h * lax.rsqrt(ms + RMS_EPS) * gf_ref[...]


def ffn_residual(x, g, wg, wu, wd, g_final, *, layer, tm, tf, final_norm):
    M, D = x.shape
    F = wg.shape[2]
    return pl.pallas_call(
        functools.partial(_ffn_kernel, final_norm=final_norm),
        grid=(M // tm, F // tf),
        in_specs=[
            pl.BlockSpec((tm, D), lambda i, j: (i, 0)),
            pl.BlockSpec((1, D), lambda i, j: (0, 0)),
            pl.BlockSpec((None, D, tf), lambda i, j: (layer, 0, j)),
            pl.BlockSpec((None, D, tf), lambda i, j: (layer, 0, j)),
            pl.BlockSpec((None, tf, D), lambda i, j: (layer, j, 0)),
            pl.BlockSpec((1, D), lambda i, j: (0, 0)),
        ],
        out_specs=pl.BlockSpec((tm, D), lambda i, j: (i, 0)),
        out_shape=jax.ShapeDtypeStruct((M, D), F32),
        scratch_shapes=[pltpu.VMEM((tm, D), BF16)],
        compiler_params=pltpu.CompilerParams(
            dimension_semantics=("parallel", "arbitrary"), vmem_limit_bytes=VMEM_LIMIT),
    )(x, g.reshape(1, D), wg, wu, wd, g_final.reshape(1, D))


def _gla_kernel(q_ref, k_ref, v_ref, og_ref, gl_ref, w2_ref, b_ref, gn_ref, o_ref, st_ref,
                *, block, chunk):
    n_heads = q_ref.shape[2] // GLA_DK

    @pl.when(pl.program_id(2) == 0)
    def _():
        st_ref[...] = jnp.zeros_like(st_ref)

    logit = _mm_split(gl_ref[0], w2_ref[...], a_parts=2, b_parts=2) + b_ref[...]
    log_a = -_softplus(-logit) * (1.0 / GLA_GATE_TAU)
    tri = _iota2((chunk, chunk), 0) >= _iota2((chunk, chunk), 1)
    tri_b = tri.astype(BF16)
    tri_b3 = jnp.concatenate([tri_b, tri_b, tri_b], axis=1)
    scale = GLA_DK ** -0.5
    gn = gn_ref[...]
    n_chunks = block // chunk

    streams = [(h, c) for h in range(n_heads) for c in range(n_chunks)]

    def rows(c):
        return pl.ds(c * chunk, chunk)

    def k_lanes(h):
        return slice(h * GLA_DK, (h + 1) * GLA_DK)

    def v_lanes(h):
        return slice(h * GLA_DV, (h + 1) * GLA_DV)

    la3 = [jnp.concatenate(_split(log_a[c * chunk:(c + 1) * chunk, k_lanes(h)], 3), axis=0)
           for h, c in streams]
    cum = [_dot(tri_b3, x) for x in la3]
    q_dec = [(q_ref[0, rows(c), k_lanes(h)] * scale * jnp.exp(b)).astype(BF16)
             for (h, c), b in zip(streams, cum)]
    k_inv = [(k_ref[0, rows(c), k_lanes(h)] * jnp.exp(-b)).astype(BF16)
             for (h, c), b in zip(streams, cum)]
    k_dec = [(k_ref[0, rows(c), k_lanes(h)] * jnp.exp(b[chunk - 1:chunk] - b)).astype(BF16)
             for (h, c), b in zip(streams, cum)]
    v_b = [v_ref[0, rows(c), v_lanes(h)].astype(BF16) for h, c in streams]
    v_t = [v_ref[0, rows(c), v_lanes(h)].T.astype(BF16) for h, c in streams]
    att = [jnp.where(tri, _mm_nt(x, y), 0.0).astype(BF16) for x, y in zip(q_dec, k_inv)]
    o_intra = [_dot(x, y) for x, y in zip(att, v_b)]
    gain = [_dot(x, y) for x, y in zip(v_t, k_dec)]
    pre = [dict(q_dec=q_dec[i], o_intra=o_intra[i], gain=gain[i],
                decay=jnp.exp(cum[i][chunk - 1:chunk])) for i in range(len(streams))]

    st = [st_ref[h] for h in range(n_heads)]
    outs = [[] for _ in range(n_heads)]
    for c in range(n_chunks):
        for h in range(n_heads):
            p = pre[h * n_chunks + c]
            outs[h].append(p["o_intra"] + _mm_nt(p["q_dec"], st[h]))
            st[h] = st[h] * p["decay"] + p["gain"]
    for h in range(n_heads):
        st_ref[h] = st[h]

    for h in range(n_heads):
        lv = slice(h * GLA_DV, (h + 1) * GLA_DV)
        o = jnp.concatenate(outs[h], axis=0)
        ms = jnp.mean(o * o, axis=-1, keepdims=True)
        og = og_ref[0, :, lv]
        o_ref[0, :, lv] = (o * lax.rsqrt(ms + RMS_EPS) * gn * (og * _sigmoid(og))).astype(o_ref.dtype)


def gla_mixer(p3, gate_w2p, gate_b, gla_norm, *, block, heads):
    B, S, _ = p3.shape
    wk, wv = heads * GLA_DK, heads * GLA_DV
    kq, kk_, kv, kog = COL_GLA_Q // wk, COL_GLA_K // wk, COL_GLA_V // wv, COL_GLA_OG // wv
    klow = COL_LOW_GATE // LANES
    return pl.pallas_call(
        functools.partial(_gla_kernel, block=block, chunk=GLA_CHUNK),
        grid=(B, GLA_HEADS // heads, S // block),
        in_specs=[
            pl.BlockSpec((1, block, wk), lambda b, h, s: (b, s, kq + h)),
            pl.BlockSpec((1, block, wk), lambda b, h, s: (b, s, kk_ + h)),
            pl.BlockSpec((1, block, wv), lambda b, h, s: (b, s, kv + h)),
            pl.BlockSpec((1, block, wv), lambda b, h, s: (b, s, kog + h)),
            pl.BlockSpec((1, block, LANES), lambda b, h, s: (b, s, klow)),
            pl.BlockSpec((LANES, wk), lambda b, h, s: (0, h)),
            pl.BlockSpec((1, wk), lambda b, h, s: (0, h)),
            pl.BlockSpec((1, GLA_DV), lambda b, h, s: (0, 0)),
        ],
        out_specs=pl.BlockSpec((1, block, wv), lambda b, h, s: (b, s, h)),
        out_shape=jax.ShapeDtypeStruct((B, S, GLA_WIDTH), BF16),
        scratch_shapes=[pltpu.VMEM((heads, GLA_DV, GLA_DK), F32)],
        compiler_params=pltpu.CompilerParams(
            dimension_semantics=("parallel", "parallel", "arbitrary"),
            vmem_limit_bytes=VMEM_LIMIT),
    )(p3, p3, p3, p3, p3, gate_w2p, gate_b.reshape(1, -1), gla_norm.reshape(1, -1))


def _unit_lower_inverses(mats, rowi, coli):
    def same_block(s):
        sh = s.bit_length() - 1
        return (rowi >> sh) == (coli >> sh)

    eye = (rowi == coli).astype(F32)
    a8 = [jnp.where(same_block(8), a, 0.0).astype(BF16) for a in mats]
    inv = [eye + a.astype(F32) for a in a8]
    p = [_dot(a, a) for a in a8]
    inv = [x + _mm(x, y) for x, y in zip(inv, p)]
    p = [_mm(y, y) for y in p]
    inv = [x + _mm(x, y) for x, y in zip(inv, p)]
    s = 8
    while s < RWKV_CHUNK:
        off = same_block(2 * s) & jnp.logical_not(same_block(s))
        e = [jnp.where(off, a, 0.0).astype(BF16) for a in mats]
        inv_b = [x.astype(BF16) for x in inv]
        t = [_dot(x, y).astype(BF16) for x, y in zip(inv_b, e)]
        inv = [x + _dot(y, z) for x, y, z in zip(inv, t, inv_b)]
        s *= 2
    return inv


def _rwkv_kernel(r_ref, k_ref, v_ref, wa_ref, g_ref, vec_ref, muwa_ref, mug_ref, w2_ref, a2_ref,
                 g2_ref, o_ref, st_ref, prev_ref, prevwa_ref, prevg_ref, *, block, chunk):
    T = chunk
    n_pairs = r_ref.shape[2] // RWKV_PAIR
    first = pl.program_id(2) == 0

    @pl.when(first)
    def _():
        st_ref[...] = jnp.zeros_like(st_ref)
        prev_ref[...] = jnp.zeros_like(prev_ref)
        prevwa_ref[...] = jnp.zeros_like(prevwa_ref)
        prevg_ref[...] = jnp.zeros_like(prevg_ref)

    row0 = _iota2((block, 1), 0) == 0

    def lerp(x, prev_row, mu):
        shifted = jnp.where(row0, prev_row, pltpu.roll(x, 1, axis=0))
        return x + (shifted - x) * mu

    vec = vec_ref[...]
    w0, a0, k_k, k_a, r_k, ln_w, ln_b = (vec[i:i + 1] for i in range(7))
    mu_r, mu_k, mu_v = (vec[i:i + 1] for i in range(7, 10))

    r_raw, k_raw, v_raw, wa_raw, g_raw = r_ref[0], k_ref[0], v_ref[0], wa_ref[0], g_ref[0]
    prev = prev_ref[...]
    r_all = lerp(r_raw, prev[0:1], mu_r)
    k_all = lerp(k_raw, prev[1:2], mu_k)
    v_all = lerp(v_raw, prev[2:3], mu_v)
    wa = lerp(wa_raw, prevwa_ref[0:1], muwa_ref[...])
    g_low = lerp(g_raw, prevg_ref[0:1], mug_ref[...])
    prev_ref[0:1] = r_raw[block - 1:block]
    prev_ref[1:2] = k_raw[block - 1:block]
    prev_ref[2:3] = v_raw[block - 1:block]
    prevwa_ref[0:1] = wa_raw[block - 1:block]
    prevg_ref[0:1] = g_raw[block - 1:block]

    z = w0 + _mm_split(jnp.tanh(wa), w2_ref[...], a_parts=2, b_parts=2)
    logw_all = -jnp.exp(-_softplus(-z) - 0.5)
    lr_all = _sigmoid(a0 + _mm_split(wa, a2_ref[...], a_parts=2, b_parts=2))
    gate_all = _mm_split(_sigmoid(g_low), g2_ref[...], a_parts=2, b_parts=2)

    lane = _iota2((1, RWKV_PAIR), 1)
    m0 = (lane < RWKV_HEAD).astype(F32)
    m1 = 1.0 - m0
    rowi = _iota2((2 * T, 2 * T), 0)
    coli = _iota2((2 * T, 2 * T), 1)
    strict = rowi > coli
    incl = rowi >= coli
    head_ones = ((rowi < RWKV_HEAD) == (coli < RWKV_HEAD)).astype(BF16)
    head_ones2 = jnp.concatenate([head_ones, head_ones], axis=0)
    tri_t = (_iota2((T, T), 0) >= _iota2((T, T), 1)).astype(BF16)
    tri_t3 = jnp.concatenate([tri_t, tri_t, tri_t], axis=1)
    ones_t3 = jnp.ones((3 * T, RWKV_PAIR), BF16)

    def hsum(x):
        return _dot(jnp.concatenate(_split(x, 2), axis=1), head_ones2)

    def stack(x):
        return jnp.concatenate([x * m0, x * m1], axis=0)

    n_chunks = block // T

    kk_p, k2_p, b_p, bonus = [], [], [], []
    for pair in range(n_pairs):
        ln = slice(pair * RWKV_PAIR, (pair + 1) * RWKV_PAIR)
        kkp = k_all[:, ln] * k_k[:, ln]
        kk_p.append(kkp / jnp.maximum(jnp.sqrt(hsum(kkp * kkp)), 1e-12))
        k2_p.append(k_all[:, ln] * (1.0 + (lr_all[:, ln] - 1.0) * k_a[:, ln]))
        b_p.append(kk_p[pair] * lr_all[:, ln])
        bonus.append(hsum(r_all[:, ln] * k2_p[pair] * r_k[:, ln]) * v_all[:, ln])

    streams = [(pair, c) for pair in range(n_pairs) for c in range(n_chunks)]

    def tile(x, pair, c):
        return x[c * T:(c + 1) * T, pair * RWKV_PAIR:(pair + 1) * RWKV_PAIR]

    lw = [tile(logw_all, p, c) for p, c in streams]
    lw3 = [jnp.concatenate(_split(x, 3), axis=0) for x in lw]
    cum = [_dot(tri_t3, x) for x in lw3]
    decay = [jnp.exp(_dot_tn(x, ones_t3)) for x in lw3]
    w_inv = [jnp.exp(-x) for x in cum]
    w_tail = [jnp.exp(x[T - 1:T] - x) for x in cum]
    a_s = [stack(-kk_p[p][c * T:(c + 1) * T] * jnp.exp(x - y)).astype(BF16)
           for (p, c), x, y in zip(streams, cum, lw)]
    r_s = [stack(tile(r_all, p, c) * jnp.exp(x)).astype(BF16) for (p, c), x in zip(streams, cum)]
    b_s = [stack(b_p[p][c * T:(c + 1) * T] * x).astype(BF16) for (p, c), x in zip(streams, w_inv)]
    k_s = [stack(k2_p[p][c * T:(c + 1) * T] * x).astype(BF16) for (p, c), x in zip(streams, w_inv)]
    v_s = [stack(tile(v_all, p, c)).astype(BF16) for p, c in streams]
    b_tail = [stack(b_p[p][c * T:(c + 1) * T] * x).astype(BF16)
              for (p, c), x in zip(streams, w_tail)]
    k_tail = [stack(k2_p[p][c * T:(c + 1) * T] * x).astype(BF16)
              for (p, c), x in zip(streams, w_tail)]
    prod = [_mm_nt(jnp.concatenate([a, r], axis=0), jnp.concatenate([b, k], axis=0))
            for a, r, b, k in zip(a_s, r_s, b_s, k_s)]
    pre = [dict(a_s=a_s[i], r_s=r_s[i], v_s=v_s[i], decay=decay[i], b_tail=b_tail[i],
                k_tail=k_tail[i],
                a_ab=jnp.where(strict, prod[i][:2 * T, :2 * T], 0.0),
                a_ak=jnp.where(strict, prod[i][:2 * T, 2 * T:], 0.0).astype(BF16),
                a_rb=jnp.where(incl, prod[i][2 * T:, :2 * T], 0.0).astype(BF16),
                a_rk=jnp.where(incl, prod[i][2 * T:, 2 * T:], 0.0).astype(BF16))
           for i in range(len(streams))]

    invs = [x.astype(BF16) for x in _unit_lower_inverses([p["a_ab"] for p in pre], rowi, coli)]
    akv = [_dot(p["a_ak"], p["v_s"]).astype(BF16) for p in pre]
    x = [_dot(i, jnp.concatenate([p["a_s"], y], axis=1)).astype(BF16)
         for i, p, y in zip(invs, pre, akv)]
    bx = [_dot_tn(p["b_tail"], y) for p, y in zip(pre, x)]
    rx = [_dot(p["a_rb"], y) for p, y in zip(pre, x)]
    trans = [y[:, :RWKV_PAIR].astype(BF16) for y in bx]
    gain = [y[:, RWKV_PAIR:] + _dot_tn(p["k_tail"], p["v_s"]) for p, y in zip(pre, bx)]
    read = [(p["r_s"].astype(F32) + y[:, :RWKV_PAIR]).astype(BF16) for p, y in zip(pre, rx)]
    y_free = [y[:, RWKV_PAIR:] + _dot(p["a_rk"], p["v_s"]) for p, y in zip(pre, rx)]

    st = [st_ref[pair] for pair in range(n_pairs)]
    ys = [[] for _ in range(n_pairs)]
    for c in range(n_chunks):
        for pair in range(n_pairs):
            i = pair * n_chunks + c
            st_b = st[pair].astype(BF16)
            y_s = _dot(read[i], st_b) + y_free[i]
            ys[pair].append(y_s[:T] + y_s[T:])
            st[pair] = st[pair] * pre[i]["decay"] + _dot(trans[i], st_b) + gain[i]
    for pair in range(n_pairs):
        st_ref[pair] = st[pair]

    for pair in range(n_pairs):
        ln = slice(pair * RWKV_PAIR, (pair + 1) * RWKV_PAIR)
        y = jnp.concatenate(ys[pair], axis=0)
        mu = hsum(y) * (1.0 / RWKV_HEAD)
        d = y - mu
        var = hsum(d * d) * (1.0 / RWKV_HEAD)
        yn = d * lax.rsqrt(var + RWKV_LN_EPS) * ln_w[:, ln] + ln_b[:, ln]
        o_ref[0, :, ln] = ((yn + bonus[pair]) * gate_all[:, ln]).astype(o_ref.dtype)


def rwkv_mixer(p3, vecs, mu_wa, mu_g, w2p, a2p, g2p, *, block, pairs):
    B, S, _ = p3.shape
    W = pairs * RWKV_PAIR
    kr, kk_, kv = COL_RWKV_R // W, COL_RWKV_K // W, COL_RWKV_V // W
    kwa = COL_LOW_WA // LANES
    kg = COL_LOW_G // (2 * LANES)
    return pl.pallas_call(
        functools.partial(_rwkv_kernel, block=block, chunk=RWKV_CHUNK),
        grid=(B, RWKV_PAIRS // pairs, S // block),
        in_specs=[
            pl.BlockSpec((1, block, W), lambda b, j, s: (b, s, kr + j)),
            pl.BlockSpec((1, block, W), lambda b, j, s: (b, s, kk_ + j)),
            pl.BlockSpec((1, block, W), lambda b, j, s: (b, s, kv + j)),
            pl.BlockSpec((1, block, LANES), lambda b, j, s: (b, s, kwa)),
            pl.BlockSpec((1, block, 2 * LANES), lambda b, j, s: (b, s, kg)),
            pl.BlockSpec((16, W), lambda b, j, s: (0, j)),
            pl.BlockSpec((1, LANES), lambda b, j, s: (0, 0)),
            pl.BlockSpec((1, 2 * LANES), lambda b, j, s: (0, 0)),
            pl.BlockSpec((LANES, W), lambda b, j, s: (0, j)),
            pl.BlockSpec((LANES, W), lambda b, j, s: (0, j)),
            pl.BlockSpec((2 * LANES, W), lambda b, j, s: (0, j)),
        ],
        out_specs=pl.BlockSpec((1, block, W), lambda b, j, s: (b, s, j)),
        out_shape=jax.ShapeDtypeStruct((B, S, RWKV_WIDTH), BF16),
        scratch_shapes=[
            pltpu.VMEM((pairs, RWKV_PAIR, RWKV_PAIR), F32),
            pltpu.VMEM((8, W), F32),
            pltpu.VMEM((8, LANES), F32),
            pltpu.VMEM((8, 2 * LANES), F32),
        ],
        compiler_params=pltpu.CompilerParams(
            dimension_semantics=("parallel", "parallel", "arbitrary"),
            vmem_limit_bytes=VMEM_LIMIT),
    )(p3, p3, p3, p3, p3, vecs, mu_wa, mu_g, w2p, a2p, g2p)


def _moba_kernel(q_ref, k_ref, v_ref, slope_ref, o_ref, kb_ref, vt_ref, kmean_ref,
                 m_ref, l_ref, acc_ref, s_ref, p_ref, *, nb):
    BS, D = MOBA_BLOCK, MOBA_HD
    QT = 2 * BS
    heads = q_ref.shape[0]
    g = pl.program_id(2)
    log2e = 1.4426950408889634
    c1 = MOBA_HD ** -0.5 * log2e
    slope2 = [slope_ref[h] * log2e for h in range(heads)]
    s_parts = [_split(x, 3) for x in slope2]

    @pl.when(g == 0)
    def _():
        kmean_ref[...] = jnp.zeros_like(kmean_ref)
        lane = _iota2((QT, LANES), 1)
        kpos = _iota2((QT, LANES), 0)
        kpos_lo = (kpos & (BS - 1)).astype(F32)
        kpos_hi = (kpos & BS).astype(F32)
        for h in range(heads):
            sk = [x[:, :LANES].astype(F32) for x in s_parts[h]]
            piece = jnp.where((lane == 0) | (lane == 3), sk[0],
                              jnp.where((lane == 1) | (lane == 4), sk[1], sk[2]))
            feat = jnp.where(lane < 6, piece,
                             jnp.where(lane < 9, kpos_lo, jnp.where(lane < 12, kpos_hi, 0.0)))
            feat = feat.astype(BF16)
            for j in range(nb):
                half = slice((j % 2) * BS, (j % 2 + 1) * BS)
                kj = k_ref[h, pl.ds(j * BS, BS), :]
                kmean_ref[h, j:j + 1, :] = jnp.mean(kj, axis=0, keepdims=True)
                kb_ref[h, j // 2, half, :D] = kj.astype(BF16)
                vt_ref[h, j // 2, :, half] = v_ref[h, pl.ds(j * BS, BS), :].T.astype(BF16)
            for t in range(nb // 2):
                kb_ref[h, t, :, D:] = feat

    qpos = _iota2((1, QT), 1)
    blk_q = 2 * g + (qpos >= BS).astype(jnp.int32)
    nbp = kmean_ref.shape[1]
    blk = _iota2((nbp, QT), 0)
    arow = _iota2((LANES, QT), 0)
    qpos_a = _iota2((LANES, QT), 1)
    qpos_lo = (qpos_a & (BS - 1)).astype(F32)
    qpos_hi = (qpos_a & BS).astype(F32)
    kp = _iota2((QT, QT), 0)
    qp = _iota2((QT, QT), 1)
    own = (kp <= qp) & ((kp >= BS) == (qp >= BS))
    first_block = (kp <= qp) & (kp < BS)

    def select(h):
        q_t = q_ref[h].T
        gate = _mm_split(kmean_ref[h], q_t, a_parts=3, b_parts=3)
        gate = jnp.where(blk < blk_q, gate, NEG_INF)
        sel = []
        for r in range(MOBA_TOPK):
            mx = jnp.max(gate, axis=0, keepdims=True)
            idx = jnp.min(jnp.where(gate == mx, blk, nbp), axis=0, keepdims=True)
            sel.append(jnp.where(r < blk_q, idx, -1))
            gate = jnp.where(blk == idx, -jnp.inf, gate)
        sq = [x.astype(F32) for x in s_parts[h]]
        piece = jnp.where((arow == 6) | (arow == 9), sq[0],
                          jnp.where((arow == 7) | (arow == 10), sq[1], sq[2]))
        aug = jnp.where(arow < 3, -qpos_lo,
                        jnp.where(arow < 6, -qpos_hi, jnp.where(arow < 12, piece, 0.0)))
        q_aug = jnp.concatenate([(q_t * c1).astype(BF16), aug.astype(BF16)], axis=0)
        return sel, q_aug

    sel, q_aug = zip(*[select(h) for h in range(heads)])

    def picked(h, j):
        return (sel[h][0] == j) | (sel[h][1] == j) | (sel[h][2] == j)

    for h in range(heads):
        s_own = _dot(kb_ref[h, g], q_aug[h])
        s_ref[h, 0] = _dot(kb_ref[h, 0], q_aug[h])
        allowed = own | (first_block & picked(h, 2 * g))
        s_own = jnp.where(allowed, s_own, NEG_INF)
        m0 = jnp.max(s_own, axis=0, keepdims=True)
        p0 = jnp.exp2(s_own - m0)
        m_ref[h] = m0
        l_ref[h] = jnp.sum(p0, axis=0, keepdims=True)
        p_ref[h] = p0.astype(BF16)
        acc_ref[h] = jnp.zeros((D, QT), F32)

    def past_pair(h, t, prev):
        s_cur = s_ref[h, t % 2]
        nxt = jnp.minimum(t + 1, nb // 2 - 1)
        s_ref[h, (t + 1) % 2, :BS, :] = _dot(kb_ref[h, nxt, :BS, :], q_aug[h])
        s_ref[h, (t + 1) % 2, BS:, :] = _dot(kb_ref[h, nxt, BS:, :], q_aug[h])
        pv = _dot(vt_ref[h, prev], p_ref[h])
        s_a = jnp.where(picked(h, 2 * t), s_cur[:BS], NEG_INF)
        s_b = jnp.where(picked(h, 2 * t + 1), s_cur[BS:], NEG_INF)
        off = slope2[h] * ((g - t) * QT).astype(F32)
        m_old = m_ref[h]
        mx = jnp.maximum(jnp.max(s_a, axis=0, keepdims=True), jnp.max(s_b, axis=0, keepdims=True))
        m_new = jnp.maximum(m_old, mx - off)
        alpha = jnp.exp2(m_old - m_new)
        shift = m_new + off
        p_a = jnp.exp2(s_a - shift)
        p_b = jnp.exp2(s_b - shift)
        l_ref[h] = (alpha * l_ref[h] + jnp.sum(p_a, axis=0, keepdims=True)
                    + jnp.sum(p_b, axis=0, keepdims=True))
        acc_ref[h] = alpha * (acc_ref[h] + pv)
        p_ref[h, :BS, :] = p_a.astype(BF16)
        p_ref[h, BS:, :] = p_b.astype(BF16)
        m_ref[h] = m_new
        return t

    last = [lax.fori_loop(0, g, functools.partial(past_pair, h), g) for h in range(heads)]
    for h in range(heads):
        acc = acc_ref[h] + _dot(vt_ref[h, last[h]], p_ref[h])
        o_ref[0, :, h * D:(h + 1) * D] = (acc / l_ref[h]).T.astype(o_ref.dtype)


def moba_attention(qkv_h, B, S, *, heads):
    H, D, BS = MOBA_HEADS, MOBA_HD, MOBA_BLOCK
    QT = 2 * BS
    nb = S // BS
    assert nb % 2 == 0, "key blocks are stored in pairs"
    nt = nb // 2
    nbp = -(-nb // 8) * 8
    hg = H // heads
    slopes = jnp.exp2(-8.0 * jnp.arange(1, H + 1, dtype=F32) / H)
    slopes = jnp.broadcast_to(slopes[:, None, None], (H, 1, QT))
    return pl.pallas_call(
        functools.partial(_moba_kernel, nb=nb),
        grid=(B, hg, nt),
        in_specs=[
            pl.BlockSpec((heads, QT, D), lambda b, h, g: (h, b * nt + g, 0)),
            pl.BlockSpec((heads, S, D), lambda b, h, g: (hg + h, b, 0)),
            pl.BlockSpec((heads, S, D), lambda b, h, g: (2 * hg + h, b, 0)),
            pl.BlockSpec((heads, 1, QT), lambda b, h, g: (h, 0, 0)),
        ],
        out_specs=pl.BlockSpec((1, QT, heads * D), lambda b, h, g: (b, g, h)),
        out_shape=jax.ShapeDtypeStruct((B, S, H * D), BF16),
        scratch_shapes=[
            pltpu.VMEM((heads, nt, QT, D + LANES), BF16),
            pltpu.VMEM((heads, nt, D, QT), BF16),
            pltpu.VMEM((heads, nbp, D), F32),
            pltpu.VMEM((heads, 1, QT), F32),
            pltpu.VMEM((heads, 1, QT), F32),
            pltpu.VMEM((heads, D, QT), F32),
            pltpu.VMEM((heads, 2, QT, QT), F32),
            pltpu.VMEM((heads, QT, QT), BF16),
        ],
        compiler_params=pltpu.CompilerParams(
            dimension_semantics=("parallel", "parallel", "arbitrary"),
            vmem_limit_bytes=VMEM_LIMIT),
    )(qkv_h, qkv_h, qkv_h, slopes)


def _pad_cols(w, n):
    return jnp.pad(w, ((0, 0), (0, n - w.shape[1])))


def _pad_rows(w, before, total):
    return jnp.pad(w, ((before, total - before - w.shape[0]), (0, 0)))


def _mix_in_layout(w):
    gk, gw = GLA_KEY_WIDTH, GLA_WIDTH
    g_end = 2 * gk + 2 * gw
    gla_main = w[:, :g_end]
    gla_gate = w[:, g_end:g_end + GLA_GATE_RANK]
    r0 = g_end + GLA_GATE_RANK
    rw_main = w[:, r0:r0 + 3 * RWKV_WIDTH]
    l0 = r0 + 3 * RWKV_WIDTH
    wa = w[:, l0:l0 + RWKV_DECAY_LORA + RWKV_AAA_LORA]
    g0 = l0 + RWKV_DECAY_LORA + RWKV_AAA_LORA
    gl = w[:, g0:g0 + RWKV_GATE_LORA]
    return jnp.concatenate(
        [gla_main, rw_main, _pad_cols(gla_gate, LANES), wa, _pad_cols(gl, 2 * LANES)], axis=1)


def mixer_layer0(hn_proj, gla_gate_w2, gla_gate_b, gla_norm, rwkv_mu, rwkv_w0, rwkv_w2, rwkv_a0,
                 rwkv_a2, rwkv_g2, rwkv_k_k, rwkv_k_a, rwkv_r_k, rwkv_ln_w, rwkv_ln_b,
                 *, gla_block, rwkv_block):
    W = RWKV_WIDTH
    o_gla = gla_mixer(hn_proj, _pad_rows(gla_gate_w2, 0, LANES), gla_gate_b, gla_norm,
                      block=gla_block, heads=4)
    mu_r, mu_k, mu_v = rwkv_mu[:W], rwkv_mu[W:2 * W], rwkv_mu[2 * W:3 * W]
    mu_low = rwkv_mu[3 * W:]
    n_wa = RWKV_DECAY_LORA + RWKV_AAA_LORA
    vecs = jnp.stack([rwkv_w0, rwkv_a0, rwkv_k_k, rwkv_k_a, rwkv_r_k.reshape(-1), rwkv_ln_w,
                      rwkv_ln_b, mu_r, mu_k, mu_v] + [jnp.zeros((W,), F32)] * 6)
    mu_wa = mu_low[:n_wa].reshape(1, -1)
    mu_g = _pad_cols(mu_low[n_wa:].reshape(1, -1), 2 * LANES)
    w2p = _pad_rows(rwkv_w2, 0, LANES)
    a2p = _pad_rows(rwkv_a2, RWKV_DECAY_LORA, LANES)
    g2p = _pad_rows(rwkv_g2, 0, 2 * LANES)
    o_rwkv = rwkv_mixer(hn_proj, vecs, mu_wa, mu_g, w2p, a2p, g2p, block=rwkv_block, pairs=4)
    return o_gla, o_rwkv


def kernel(x, norm_mix, norm_ffn, norm_final, mix_in_w, gla_gate_w2, gla_gate_b, gla_norm, rwkv_mu, rwkv_w0, rwkv_w2, rwkv_a0, rwkv_a2, rwkv_g2, rwkv_k_k, rwkv_k_a, rwkv_r_k, rwkv_ln_w, rwkv_ln_b, mix_out_w, attn_qkv_w, attn_out_w, ffn_gate_w, ffn_up_w, ffn_down_w):
    B, S, D = x.shape
    M = B * S
    tm, tn, tf = 512, 512, 256
    tm_in = min(1024, M)
    tm_ffn = min(1024, M)
    h = x.reshape(M, D)

    w_in = _mix_in_layout(mix_in_w[0]).astype(BF16)
    p = norm_matmul(h, norm_mix[0], w_in, tm=tm_in, tn=tn).reshape(B, S, MIX_COLS)
    o_gla, o_rwkv = mixer_layer0(
        p, gla_gate_w2[0], gla_gate_b[0], gla_norm[0], rwkv_mu[0], rwkv_w0[0], rwkv_w2[0],
        rwkv_a0[0], rwkv_a2[0], rwkv_g2[0], rwkv_k_k[0], rwkv_k_a[0], rwkv_r_k[0], rwkv_ln_w[0],
        rwkv_ln_b[0], gla_block=512, rwkv_block=512)
    w_out = mix_out_w[0].astype(BF16)
    h = proj_residual(h, [(o_gla.reshape(M, GLA_WIDTH), w_out[:GLA_WIDTH]),
                          (o_rwkv.reshape(M, RWKV_WIDTH), w_out[GLA_WIDTH:])], tm=tm, tn=D)
    h = ffn_residual(h, norm_ffn[0], ffn_gate_w, ffn_up_w, ffn_down_w, norm_final, layer=0,
                     tm=tm_ffn, tf=tf, final_norm=False)

    qkv_h = norm_matmul(h, norm_mix[1], attn_qkv_w[0].astype(BF16), tm=tm_in, tn=tn,
                        group=MOBA_HD)
    o_attn = moba_attention(qkv_h, B, S, heads=2)
    h = proj_residual(h, [(o_attn.reshape(M, D), attn_out_w[0].astype(BF16))], tm=tm, tn=D)
    h = ffn_residual(h, norm_ffn[1], ffn_gate_w, ffn_up_w, ffn_down_w, norm_final, layer=1,
                     tm=tm_ffn, tf=tf, final_norm=True)
    return h.reshape(B, S, D)
```

```python
import functools

import jax
import jax.numpy as jnp
from jax import lax
from jax.experimental import pallas as pl
from jax.experimental.pallas import tpu as pltpu

F32 = jnp.float32
BF16 = jnp.bfloat16
HI = lax.Precision.HIGHEST

D_MODEL = 2048
RMS_EPS = 1e-6

GLA_HEADS = 4
GLA_DK = 128
GLA_DV = 256
GLA_KEY_WIDTH = GLA_HEADS * GLA_DK
GLA_WIDTH = GLA_HEADS * GLA_DV
GLA_GATE_RANK = 16
GLA_GATE_TAU = 16.0
GLA_CHUNK = 64

RWKV_WIDTH = 1024
RWKV_HEAD = 64
RWKV_PAIR = 2 * RWKV_HEAD
RWKV_PAIRS = RWKV_WIDTH // RWKV_PAIR
RWKV_DECAY_LORA = 64
RWKV_AAA_LORA = 64
RWKV_GATE_LORA = 160
RWKV_LN_EPS = RWKV_HEAD * 1e-5
RWKV_CHUNK = 64

MOBA_HEADS = 16
MOBA_HD = 128
MOBA_BLOCK = 256
MOBA_TOPK = 3
NEG_INF = -1e30

FFN_HIDDEN = 5632

LANES = 128

COL_GLA_Q = 0
COL_GLA_K = 512
COL_GLA_V = 1024
COL_GLA_OG = 2048
COL_RWKV_R = 3072
COL_RWKV_K = 4096
COL_RWKV_V = 5120
COL_LOW_GATE = 6144
COL_LOW_WA = 6272
COL_LOW_G = 6400
MIX_COLS = 6656

VMEM_LIMIT = 56 * 1024 * 1024


def _mm(a, b):
    return jnp.dot(a.astype(BF16), b.astype(BF16), preferred_element_type=F32)


def _mm_nt(a, b):
    return lax.dot_general(a.astype(BF16), b.astype(BF16), (((1,), (1,)), ((), ())),
                           preferred_element_type=F32)


def _mm_hi(a, b):
    return jnp.dot(a, b, preferred_element_type=F32, precision=HI)


def _mm_nt_hi(a, b):
    return lax.dot_general(a, b, (((1,), (1,)), ((), ())), preferred_element_type=F32,
                           precision=HI)


def _dot(a, b):
    return jnp.dot(a, b, preferred_element_type=F32)


def _dot_tn(a, b):
    return lax.dot_general(a, b, (((0,), (0,)), ((), ())), preferred_element_type=F32)


def _split(x, parts):
    out = []
    for _ in range(parts - 1):
        hi = x.astype(BF16)
        out.append(hi)
        x = x - hi.astype(F32)
    out.append(x.astype(BF16))
    return out


def _mm_split(a, b, *, a_parts=1, b_parts=1):
    a_p = _split(a, a_parts) if a_parts > 1 else [a.astype(BF16)]
    b_p = _split(b, b_parts) if b_parts > 1 else [b.astype(BF16)]
    acc = None
    for i, ai in enumerate(a_p):
        for j, bj in enumerate(b_p):
            if i + j < max(a_parts, b_parts):
                t = _dot(ai, bj)
                acc = t if acc is None else acc + t
    return acc


def _sigmoid(x):
    return 1.0 / (1.0 + jnp.exp(-x))


def _softplus(x):
    return jnp.maximum(x, 0.0) + jnp.log1p(jnp.exp(-jnp.abs(x)))


def _iota2(shape, axis):
    return lax.broadcasted_iota(jnp.int32, shape, axis)


def _norm_matmul_kernel(x_ref, g_ref, w_ref, o_ref, xn_ref, *, group):
    @pl.when(pl.program_id(1) == 0)
    def _():
        x = x_ref[...]
        ms = jnp.mean(x * x, axis=-1, keepdims=True)
        xn_ref[...] = (x * lax.rsqrt(ms + RMS_EPS) * g_ref[...]).astype(BF16)

    res = jnp.dot(xn_ref[...], w_ref[...], preferred_element_type=F32).astype(o_ref.dtype)
    if group is None:
        o_ref[...] = res
    else:
        for c in range(res.shape[1] // group):
            o_ref[c] = res[:, c * group:(c + 1) * group]


def norm_matmul(x, g, w, *, tm, tn, out_dtype=F32, group=None):
    M, D = x.shape
    N = w.shape[1]
    if group is None:
        out_specs = pl.BlockSpec((tm, tn), lambda i, j: (i, j))
        out_shape = jax.ShapeDtypeStruct((M, N), out_dtype)
    else:
        out_specs = pl.BlockSpec((tn // group, tm, group), lambda i, j: (j, i, 0))
        out_shape = jax.ShapeDtypeStruct((N // group, M, group), out_dtype)
    return pl.pallas_call(
        functools.partial(_norm_matmul_kernel, group=group),
        grid=(M // tm, N // tn),
        in_specs=[
            pl.BlockSpec((tm, D), lambda i, j: (i, 0)),
            pl.BlockSpec((1, D), lambda i, j: (0, 0)),
            pl.BlockSpec((D, tn), lambda i, j: (0, j)),
        ],
        out_specs=out_specs,
        out_shape=out_shape,
        scratch_shapes=[pltpu.VMEM((tm, D), BF16)],
        compiler_params=pltpu.CompilerParams(
            dimension_semantics=("parallel", "arbitrary"), vmem_limit_bytes=VMEM_LIMIT),
    )(x, g.reshape(1, D), w)


def _proj_res_kernel(*refs, n_in):
    res_ref = refs[0]
    o_ref = refs[1 + 2 * n_in]
    acc = res_ref[...]
    for i in range(n_in):
        acc = acc + jnp.dot(refs[1 + 2 * i][...], refs[2 + 2 * i][...],
                            preferred_element_type=F32)
    o_ref[...] = acc


def proj_residual(res, pairs, *, tm, tn):
    M, N = res.shape
    in_specs = [pl.BlockSpec((tm, tn), lambda i, j: (i, j))]
    args = [res]
    for a, w in pairs:
        K = a.shape[1]
        in_specs.append(pl.BlockSpec((tm, K), lambda i, j: (i, 0)))
        in_specs.append(pl.BlockSpec((K, tn), lambda i, j: (0, j)))
        args += [a, w]
    return pl.pallas_call(
        functools.partial(_proj_res_kernel, n_in=len(pairs)),
        grid=(M // tm, N // tn),
        in_specs=in_specs,
        out_specs=pl.BlockSpec((tm, tn), lambda i, j: (i, j)),
        out_shape=jax.ShapeDtypeStruct((M, N), F32),
        compiler_params=pltpu.CompilerParams(
            dimension_semantics=("parallel", "arbitrary"), vmem_limit_bytes=VMEM_LIMIT),
    )(*args)


def _ffn_kernel(x_ref, g_ref, wg_ref, wu_ref, wd_ref, gf_ref, o_ref, xn_ref, *, final_norm):
    j = pl.program_id(1)

    @pl.when(j == 0)
    def _():
        x = x_ref[...]
        ms = jnp.mean(x * x, axis=-1, keepdims=True)
        xn_ref[...] = (x * lax.rsqrt(ms + RMS_EPS) * g_ref[...]).astype(BF16)
        o_ref[...] = x

    xn = xn_ref[...]
    gate = jnp.dot(xn, wg_ref[...].astype(BF16), preferred_element_type=F32)
    up = jnp.dot(xn, wu_ref[...].astype(BF16), preferred_element_type=F32)
    act = (gate * _sigmoid(gate) * up).astype(BF16)
    o_ref[...] += jnp.dot(act, wd_ref[...].astype(BF16), preferred_element_type=F32)

    if final_norm:
        @pl.when(j == pl.num_programs(1) - 1)
        def _():
            h = o_ref[...]
            ms = jnp.mean(h * h, axis=-1, keepdims=True)
            o_ref[...] = h * lax.rsqrt(ms + RMS_EPS) * gf_ref[...]


def ffn_residual(x, g, wg, wu, wd, g_final, *, layer, tm, tf, final_norm):
    M, D = x.shape
    F = wg.shape[2]
    return pl.pallas_call(
        functools.partial(_ffn_kernel, final_norm=final_norm),
        grid=(M // tm, F // tf),
        in_specs=[
            pl.BlockSpec((tm, D), lambda i, j: (i, 0)),
            pl.BlockSpec((1, D), lambda i, j: (0, 0)),
            pl.BlockSpec((None, D, tf), lambda i, j: (layer, 0, j)),
            pl.BlockSpec((None, D, tf), lambda i, j: (layer, 0, j)),
            pl.BlockSpec((None, tf, D), lambda i, j: (layer, j, 0)),
            pl.BlockSpec((1, D), lambda i, j: (0, 0)),
        ],
        out_specs=pl.BlockSpec((tm, D), lambda i, j: (i, 0)),
        out_shape=jax.ShapeDtypeStruct((M, D), F32),
        scratch_shapes=[pltpu.VMEM((tm, D), BF16)],
        compiler_params=pltpu.CompilerParams(
            dimension_semantics=("parallel", "arbitrary"), vmem_limit_bytes=VMEM_LIMIT),
    )(x, g.reshape(1, D), wg, wu, wd, g_final.reshape(1, D))


def _gla_kernel(q_ref, k_ref, v_ref, og_ref, gl_ref, w2_ref, b_ref, gn_ref, o_ref, st_ref,
                *, block, chunk):
    n_heads = q_ref.shape[2] // GLA_DK

    @pl.when(pl.program_id(2) == 0)
    def _():
        st_ref[...] = jnp.zeros_like(st_ref)

    logit = _mm_split(gl_ref[0], w2_ref[...], a_parts=2, b_parts=2) + b_ref[...]
    log_a = -_softplus(-logit) * (1.0 / GLA_GATE_TAU)
    tri = _iota2((chunk, chunk), 0) >= _iota2((chunk, chunk), 1)
    tri_b = tri.astype(BF16)
    tri_b3 = jnp.concatenate([tri_b, tri_b, tri_b], axis=1)
    scale = GLA_DK ** -0.5
    gn = gn_ref[...]
    n_chunks = block // chunk

    streams = [(h, c) for h in range(n_heads) for c in range(n_chunks)]

    def rows(c):
        return pl.ds(c * chunk, chunk)

    def k_lanes(h):
        return slice(h * GLA_DK, (h + 1) * GLA_DK)

    def v_lanes(h):
        return slice(h * GLA_DV, (h + 1) * GLA_DV)

    la3 = [jnp.concatenate(_split(log_a[c * chunk:(c + 1) * chunk, k_lanes(h)], 3), axis=0)
           for h, c in streams]
    cum = [_dot(tri_b3, x) for x in la3]
    q_dec = [(q_ref[0, rows(c), k_lanes(h)] * scale * jnp.exp(b)).astype(BF16)
             for (h, c), b in zip(streams, cum)]
    k_inv = [(k_ref[0, rows(c), k_lanes(h)] * jnp.exp(-b)).astype(BF16)
             for (h, c), b in zip(streams, cum)]
    k_dec = [(k_ref[0, rows(c), k_lanes(h)] * jnp.exp(b[chunk - 1:chunk] - b)).astype(BF16)
             for (h, c), b in zip(streams, cum)]
    v_b = [v_ref[0, rows(c), v_lanes(h)].astype(BF16) for h, c in streams]
    v_t = [v_ref[0, rows(c), v_lanes(h)].T.astype(BF16) for h, c in streams]
    att = [jnp.where(tri, _mm_nt(x, y), 0.0).astype(BF16) for x, y in zip(q_dec, k_inv)]
    o_intra = [_dot(x, y) for x, y in zip(att, v_b)]
    gain = [_dot(x, y) for x, y in zip(v_t, k_dec)]
    pre = [dict(q_dec=q_dec[i], o_intra=o_intra[i], gain=gain[i],
                decay=jnp.exp(cum[i][chunk - 1:chunk])) for i in range(len(streams))]

    st = [st_ref[h] for h in range(n_heads)]
    outs = [[] for _ in range(n_heads)]
    for c in range(n_chunks):
        for h in range(n_heads):
            p = pre[h * n_chunks + c]
            outs[h].append(p["o_intra"] + _mm_nt(p["q_dec"], st[h]))
            st[h] = st[h] * p["decay"] + p["gain"]
    for h in range(n_heads):
        st_ref[h] = st[h]

    for h in range(n_heads):
        lv = slice(h * GLA_DV, (h + 1) * GLA_DV)
        o = jnp.concatenate(outs[h], axis=0)
        ms = jnp.mean(o * o, axis=-1, keepdims=True)
        og = og_ref[0, :, lv]
        o_ref[0, :, lv] = (o * lax.rsqrt(ms + RMS_EPS) * gn * (og * _sigmoid(og))).astype(o_ref.dtype)


def gla_mixer(p3, gate_w2p, gate_b, gla_norm, *, block, heads):
    B, S, _ = p3.shape
    wk, wv = heads * GLA_DK, heads * GLA_DV
    kq, kk_, kv, kog = COL_GLA_Q // wk, COL_GLA_K // wk, COL_GLA_V // wv, COL_GLA_OG // wv
    klow = COL_LOW_GATE // LANES
    return pl.pallas_call(
        functools.partial(_gla_kernel, block=block, chunk=GLA_CHUNK),
        grid=(B, GLA_HEADS // heads, S // block),
        in_specs=[
            pl.BlockSpec((1, block, wk), lambda b, h, s: (b, s, kq + h)),
            pl.BlockSpec((1, block, wk), lambda b, h, s: (b, s, kk_ + h)),
            pl.BlockSpec((1, block, wv), lambda b, h, s: (b, s, kv + h)),
            pl.BlockSpec((1, block, wv), lambda b, h, s: (b, s, kog + h)),
            pl.BlockSpec((1, block, LANES), lambda b, h, s: (b, s, klow)),
            pl.BlockSpec((LANES, wk), lambda b, h, s: (0, h)),
            pl.BlockSpec((1, wk), lambda b, h, s: (0, h)),
            pl.BlockSpec((1, GLA_DV), lambda b, h, s: (0, 0)),
        ],
        out_specs=pl.BlockSpec((1, block, wv), lambda b, h, s: (b, s, h)),
        out_shape=jax.ShapeDtypeStruct((B, S, GLA_WIDTH), BF16),
        scratch_shapes=[pltpu.VMEM((heads, GLA_DV, GLA_DK), F32)],
        compiler_params=pltpu.CompilerParams(
            dimension_semantics=("parallel", "parallel", "arbitrary"),
            vmem_limit_bytes=VMEM_LIMIT),
    )(p3, p3, p3, p3, p3, gate_w2p, gate_b.reshape(1, -1), gla_norm.reshape(1, -1))


def _unit_lower_inverses(mats, rowi, coli):
    def same_block(s):
        sh = s.bit_length() - 1
        return (rowi >> sh) == (coli >> sh)

    eye = (rowi == coli).astype(F32)
    a8 = [jnp.where(same_block(8), a, 0.0).astype(BF16) for a in mats]
    inv = [eye + a.astype(F32) for a in a8]
    p = [_dot(a, a) for a in a8]
    inv = [x + _mm(x, y) for x, y in zip(inv, p)]
    p = [_mm(y, y) for y in p]
    inv = [x + _mm(x, y) for x, y in zip(inv, p)]
    s = 8
    while s < RWKV_CHUNK:
        off = same_block(2 * s) & jnp.logical_not(same_block(s))
        e = [jnp.where(off, a, 0.0).astype(BF16) for a in mats]
        inv_b = [x.astype(BF16) for x in inv]
        t = [_dot(x, y).astype(BF16) for x, y in zip(inv_b, e)]
        inv = [x + _dot(y, z) for x, y, z in zip(inv, t, inv_b)]
        s *= 2
    return inv


def _rwkv_kernel(r_ref, k_ref, v_ref, wa_ref, g_ref, vec_ref, muwa_ref, mug_ref, w2_ref, a2_ref,
                 g2_ref, o_ref, st_ref, prev_ref, prevwa_ref, prevg_ref, *, block, chunk):
    T = chunk
    n_pairs = r_ref.shape[2] // RWKV_PAIR
    first = pl.program_id(2) == 0

    @pl.when(first)
    def _():
        st_ref[...] = jnp.zeros_like(st_ref)
        prev_ref[...] = jnp.zeros_like(prev_ref)
        prevwa_ref[...] = jnp.zeros_like(prevwa_ref)
        prevg_ref[...] = jnp.zeros_like(prevg_ref)

    row0 = _iota2((block, 1), 0) == 0

    def lerp(x, prev_row, mu):
        shifted = jnp.where(row0, prev_row, pltpu.roll(x, 1, axis=0))
        return x + (shifted - x) * mu

    vec = vec_ref[...]
    w0, a0, k_k, k_a, r_k, ln_w, ln_b = (vec[i:i + 1] for i in range(7))
    mu_r, mu_k, mu_v = (vec[i:i + 1] for i in range(7, 10))

    r_raw, k_raw, v_raw, wa_raw, g_raw = r_ref[0], k_ref[0], v_ref[0], wa_ref[0], g_ref[0]
    prev = prev_ref[...]
    r_all = lerp(r_raw, prev[0:1], mu_r)
    k_all = lerp(k_raw, prev[1:2], mu_k)
    v_all = lerp(v_raw, prev[2:3], mu_v)
    wa = lerp(wa_raw, prevwa_ref[0:1], muwa_ref[...])
    g_low = lerp(g_raw, prevg_ref[0:1], mug_ref[...])
    prev_ref[0:1] = r_raw[block - 1:block]
    prev_ref[1:2] = k_raw[block - 1:block]
    prev_ref[2:3] = v_raw[block - 1:block]
    prevwa_ref[0:1] = wa_raw[block - 1:block]
    prevg_ref[0:1] = g_raw[block - 1:block]

    z = w0 + _mm_split(jnp.tanh(wa), w2_ref[...], a_parts=2, b_parts=2)
    logw_all = -jnp.exp(-_softplus(-z) - 0.5)
    lr_all = _sigmoid(a0 + _mm_split(wa, a2_ref[...], a_parts=2, b_parts=2))
    gate_all = _mm_split(_sigmoid(g_low), g2_ref[...], a_parts=2, b_parts=2)

    lane = _iota2((1, RWKV_PAIR), 1)
    m0 = (lane < RWKV_HEAD).astype(F32)
    m1 = 1.0 - m0
    rowi = _iota2((2 * T, 2 * T), 0)
    coli = _iota2((2 * T, 2 * T), 1)
    strict = rowi > coli
    incl = rowi >= coli
    head_ones = ((rowi < RWKV_HEAD) == (coli < RWKV_HEAD)).astype(BF16)
    head_ones2 = jnp.concatenate([head_ones, head_ones], axis=0)
    tri_t = (_iota2((T, T), 0) >= _iota2((T, T), 1)).astype(BF16)
    tri_t3 = jnp.concatenate([tri_t, tri_t, tri_t], axis=1)
    ones_t3 = jnp.ones((3 * T, RWKV_PAIR), BF16)

    def hsum(x):
        return _dot(jnp.concatenate(_split(x, 2), axis=1), head_ones2)

    def stack(x):
        return jnp.concatenate([x * m0, x * m1], axis=0)

    n_chunks = block // T

    kk_p, k2_p, b_p, bonus = [], [], [], []
    for pair in range(n_pairs):
        ln = slice(pair * RWKV_PAIR, (pair + 1) * RWKV_PAIR)
        kkp = k_all[:, ln] * k_k[:, ln]
        kk_p.append(kkp / jnp.maximum(jnp.sqrt(hsum(kkp * kkp)), 1e-12))
        k2_p.append(k_all[:, ln] * (1.0 + (lr_all[:, ln] - 1.0) * k_a[:, ln]))
        b_p.append(kk_p[pair] * lr_all[:, ln])
        bonus.append(hsum(r_all[:, ln] * k2_p[pair] * r_k[:, ln]) * v_all[:, ln])

    streams = [(pair, c) for pair in range(n_pairs) for c in range(n_chunks)]

    def tile(x, pair, c):
        return x[c * T:(c + 1) * T, pair * RWKV_PAIR:(pair + 1) * RWKV_PAIR]

    lw = [tile(logw_all, p, c) for p, c in streams]
    lw3 = [jnp.concatenate(_split(x, 3), axis=0) for x in lw]
    cum = [_dot(tri_t3, x) for x in lw3]
    decay = [jnp.exp(_dot_tn(x, ones_t3)) for x in lw3]
    w_inv = [jnp.exp(-x) for x in cum]
    w_tail = [jnp.exp(x[T - 1:T] - x) for x in cum]
    a_s = [stack(-kk_p[p][c * T:(c + 1) * T] * jnp.exp(x - y)).astype(BF16)
           for (p, c), x, y in zip(streams, cum, lw)]
    r_s = [stack(tile(r_all, p, c) * jnp.exp(x)).astype(BF16) for (p, c), x in zip(streams, cum)]
    b_s = [stack(b_p[p][c * T:(c + 1) * T] * x).astype(BF16) for (p, c), x in zip(streams, w_inv)]
    k_s = [stack(k2_p[p][c * T:(c + 1) * T] * x).astype(BF16) for (p, c), x in zip(streams, w_inv)]
    v_s = [stack(tile(v_all, p, c)).astype(BF16) for p, c in streams]
    b_tail = [stack(b_p[p][c * T:(c + 1) * T] * x).astype(BF16)
              for (p, c), x in zip(streams, w_tail)]
    k_tail = [stack(k2_p[p][c * T:(c + 1) * T] * x).astype(BF16)
              for (p, c), x in zip(streams, w_tail)]
    prod = [_mm_nt(jnp.concatenate([a, r], axis=0), jnp.concatenate([b, k], axis=0))
            for a, r, b, k in zip(a_s, r_s, b_s, k_s)]
    pre = [dict(a_s=a_s[i], r_s=r_s[i], v_s=v_s[i], decay=decay[i], b_tail=b_tail[i],
                k_tail=k_tail[i],
                a_ab=jnp.where(strict, prod[i][:2 * T, :2 * T], 0.0),
                a_ak=jnp.where(strict, prod[i][:2 * T, 2 * T:], 0.0).astype(BF16),
                a_rb=jnp.where(incl, prod[i][2 * T:, :2 * T], 0.0).astype(BF16),
                a_rk=jnp.where(incl, prod[i][2 * T:, 2 * T:], 0.0).astype(BF16))
           for i in range(len(streams))]

    invs = [x.astype(BF16) for x in _unit_lower_inverses([p["a_ab"] for p in pre], rowi, coli)]
    akv = [_dot(p["a_ak"], p["v_s"]).astype(BF16) for p in pre]
    x = [_dot(i, jnp.concatenate([p["a_s"], y], axis=1)).astype(BF16)
         for i, p, y in zip(invs, pre, akv)]
    bx = [_dot_tn(p["b_tail"], y) for p, y in zip(pre, x)]
    rx = [_dot(p["a_rb"], y) for p, y in zip(pre, x)]
    trans = [y[:, :RWKV_PAIR].astype(BF16) for y in bx]
    gain = [y[:, RWKV_PAIR:] + _dot_tn(p["k_tail"], p["v_s"]) for p, y in zip(pre, bx)]
    read = [(p["r_s"].astype(F32) + y[:, :RWKV_PAIR]).astype(BF16) for p, y in zip(pre, rx)]
    y_free = [y[:, RWKV_PAIR:] + _dot(p["a_rk"], p["v_s"]) for p, y in zip(pre, rx)]

    st = [st_ref[pair] for pair in range(n_pairs)]
    ys = [[] for _ in range(n_pairs)]
    for c in range(n_chunks):
        for pair in range(n_pairs):
            i = pair * n_chunks + c
            st_b = st[pair].astype(BF16)
            y_s = _dot(read[i], st_b) + y_free[i]
            ys[pair].append(y_s[:T] + y_s[T:])
            st[pair] = st[pair] * pre[i]["decay"] + _dot(trans[i], st_b) + gain[i]
    for pair in range(n_pairs):
        st_ref[pair] = st[pair]

    for pair in range(n_pairs):
        ln = slice(pair * RWKV_PAIR, (pair + 1) * RWKV_PAIR)
        y = jnp.concatenate(ys[pair], axis=0)
        mu = hsum(y) * (1.0 / RWKV_HEAD)
        d = y - mu
        var = hsum(d * d) * (1.0 / RWKV_HEAD)
        yn = d * lax.rsqrt(var + RWKV_LN_EPS) * ln_w[:, ln] + ln_b[:, ln]
        o_ref[0, :, ln] = ((yn + bonus[pair]) * gate_all[:, ln]).astype(o_ref.dtype)


def rwkv_mixer(p3, vecs, mu_wa, mu_g, lora_up, *, block, pairs):
    B, S, _ = p3.shape
    W = pairs * RWKV_PAIR
    kr, kk_, kv = COL_RWKV_R // W, COL_RWKV_K // W, COL_RWKV_V // W
    kwa = COL_LOW_WA // LANES
    kg = COL_LOW_G // (2 * LANES)
    return pl.pallas_call(
        functools.partial(_rwkv_kernel, block=block, chunk=RWKV_CHUNK),
        grid=(B, RWKV_PAIRS // pairs, S // block),
        in_specs=[
            pl.BlockSpec((1, block, W), lambda b, j, s: (b, s, kr + j)),
            pl.BlockSpec((1, block, W), lambda b, j, s: (b, s, kk_ + j)),
            pl.BlockSpec((1, block, W), lambda b, j, s: (b, s, kv + j)),
            pl.BlockSpec((1, block, LANES), lambda b, j, s: (b, s, kwa)),
            pl.BlockSpec((1, block, 2 * LANES), lambda b, j, s: (b, s, kg)),
            pl.BlockSpec((16, W), lambda b, j, s: (0, j)),
            pl.BlockSpec((1, LANES), lambda b, j, s: (0, 0)),
            pl.BlockSpec((1, 2 * LANES), lambda b, j, s: (0, 0)),
            pl.BlockSpec((LANES, W), lambda b, j, s: (0, j)),
            pl.BlockSpec((LANES, W), lambda b, j, s: (1, j)),
            pl.BlockSpec((2 * LANES, W), lambda b, j, s: (1, j)),
        ],
        out_specs=pl.BlockSpec((1, block, W), lambda b, j, s: (b, s, j)),
        out_shape=jax.ShapeDtypeStruct((B, S, RWKV_WIDTH), BF16),
        scratch_shapes=[
            pltpu.VMEM((pairs, RWKV_PAIR, RWKV_PAIR), F32),
            pltpu.VMEM((8, W), F32),
            pltpu.VMEM((8, LANES), F32),
            pltpu.VMEM((8, 2 * LANES), F32),
        ],
        compiler_params=pltpu.CompilerParams(
            dimension_semantics=("parallel", "parallel", "arbitrary"),
            vmem_limit_bytes=VMEM_LIMIT),
    )(p3, p3, p3, p3, p3, vecs, mu_wa, mu_g, lora_up, lora_up, lora_up)


def _moba_kernel(q_ref, k_ref, v_ref, slope_ref, o_ref, kb_ref, vt_ref, kmean_ref,
                 m_ref, l_ref, acc_ref, s_ref, p_ref, *, nb):
    BS, D = MOBA_BLOCK, MOBA_HD
    QT = 2 * BS
    heads = q_ref.shape[0]
    g = pl.program_id(2)
    log2e = 1.4426950408889634
    c1 = MOBA_HD ** -0.5 * log2e
    slope2 = [slope_ref[h] * log2e for h in range(heads)]
    s_parts = [_split(x, 3) for x in slope2]

    @pl.when(g == 0)
    def _():
        kmean_ref[...] = jnp.zeros_like(kmean_ref)
        lane = _iota2((QT, LANES), 1)
        kpos = _iota2((QT, LANES), 0)
        kpos_lo = (kpos & (BS - 1)).astype(F32)
        kpos_hi = (kpos & BS).astype(F32)
        for h in range(heads):
            sk = [x[:, :LANES].astype(F32) for x in s_parts[h]]
            piece = jnp.where((lane == 0) | (lane == 3), sk[0],
                              jnp.where((lane == 1) | (lane == 4), sk[1], sk[2]))
            feat = jnp.where(lane < 6, piece,
                             jnp.where(lane < 9, kpos_lo, jnp.where(lane < 12, kpos_hi, 0.0)))
            feat = feat.astype(BF16)
            for j in range(nb):
                half = slice((j % 2) * BS, (j % 2 + 1) * BS)
                kj = k_ref[h, pl.ds(j * BS, BS), :]
                kmean_ref[h, j:j + 1, :] = jnp.mean(kj, axis=0, keepdims=True)
                kb_ref[h, j // 2, half, :D] = kj.astype(BF16)
                vt_ref[h, j // 2, :, half] = v_ref[h, pl.ds(j * BS, BS), :].T.astype(BF16)
            for t in range(nb // 2):
                kb_ref[h, t, :, D:] = feat

    qpos = _iota2((1, QT), 1)
    blk_q = 2 * g + (qpos >= BS).astype(jnp.int32)
    nbp = kmean_ref.shape[1]
    blk = _iota2((nbp, QT), 0)
    arow = _iota2((LANES, QT), 0)
    qpos_a = _iota2((LANES, QT), 1)
    qpos_lo = (qpos_a & (BS - 1)).astype(F32)
    qpos_hi = (qpos_a & BS).astype(F32)
    kp = _iota2((QT, QT), 0)
    qp = _iota2((QT, QT), 1)
    own = (kp <= qp) & ((kp >= BS) == (qp >= BS))
    first_block = (kp <= qp) & (kp < BS)

    def select(h):
        q_t = q_ref[h].T
        gate = _mm_split(kmean_ref[h], q_t, a_parts=3, b_parts=3)
        gate = jnp.where(blk < blk_q, gate, NEG_INF)
        sel = []
        for r in range(MOBA_TOPK):
            mx = jnp.max(gate, axis=0, keepdims=True)
            idx = jnp.min(jnp.where(gate == mx, blk, nbp), axis=0, keepdims=True)
            sel.append(jnp.where(r < blk_q, idx, -1))
            gate = jnp.where(blk == idx, -jnp.inf, gate)
        sq = [x.astype(F32) for x in s_parts[h]]
        piece = jnp.where((arow == 6) | (arow == 9), sq[0],
                          jnp.where((arow == 7) | (arow == 10), sq[1], sq[2]))
        aug = jnp.where(arow < 3, -qpos_lo,
                        jnp.where(arow < 6, -qpos_hi, jnp.where(arow < 12, piece, 0.0)))
        q_aug = jnp.concatenate([(q_t * c1).astype(BF16), aug.astype(BF16)], axis=0)
        return sel, q_aug

    sel, q_aug = zip(*[select(h) for h in range(heads)])

    def picked(h, j):
        return (sel[h][0] == j) | (sel[h][1] == j) | (sel[h][2] == j)

    for h in range(heads):
        allowed = own | (first_block & picked(h, 2 * g))
        s_own = jnp.where(allowed, _dot(kb_ref[h, g], q_aug[h]), NEG_INF)
        m0 = jnp.max(s_own, axis=0, keepdims=True)
        p0 = jnp.exp2(s_own - m0)
        m_ref[h] = m0
        l_ref[h] = jnp.sum(p0, axis=0, keepdims=True)
        p_ref[h] = p0.astype(BF16)
        acc_ref[h] = jnp.zeros((D, QT), F32)
        s_ref[h, 0] = _dot(kb_ref[h, 0], q_aug[h])

    def past_pair(h, t, prev):
        s_cur = s_ref[h, t % 2]
        pv = _dot(vt_ref[h, prev], p_ref[h])
        s_ref[h, (t + 1) % 2] = _dot(kb_ref[h, jnp.minimum(t + 1, nb // 2 - 1)], q_aug[h])
        s_a = jnp.where(picked(h, 2 * t), s_cur[:BS], NEG_INF)
        s_b = jnp.where(picked(h, 2 * t + 1), s_cur[BS:], NEG_INF)
        off = slope2[h] * ((g - t) * QT).astype(F32)
        m_old = m_ref[h]
        mx = jnp.maximum(jnp.max(s_a, axis=0, keepdims=True), jnp.max(s_b, axis=0, keepdims=True))
        m_new = jnp.maximum(m_old, mx - off)
        alpha = jnp.exp2(m_old - m_new)
        shift = m_new + off
        p_a = jnp.exp2(s_a - shift)
        p_b = jnp.exp2(s_b - shift)
        l_ref[h] = (alpha * l_ref[h] + jnp.sum(p_a, axis=0, keepdims=True)
                    + jnp.sum(p_b, axis=0, keepdims=True))
        acc_ref[h] = alpha * (acc_ref[h] + pv)
        p_ref[h, :BS, :] = p_a.astype(BF16)
        p_ref[h, BS:, :] = p_b.astype(BF16)
        m_ref[h] = m_new
        return t

    last = [lax.fori_loop(0, g, functools.partial(past_pair, h), g) for h in range(heads)]
    for h in range(heads):
        acc = acc_ref[h] + _dot(vt_ref[h, last[h]], p_ref[h])
        o_ref[0, :, h * D:(h + 1) * D] = (acc / l_ref[h]).T.astype(o_ref.dtype)


def moba_attention(qkv_h, B, S, *, heads):
    H, D, BS = MOBA_HEADS, MOBA_HD, MOBA_BLOCK
    QT = 2 * BS
    nb = S // BS
    assert nb % 2 == 0, "key blocks are stored in pairs"
    nt = nb // 2
    nbp = -(-nb // 8) * 8
    hg = H // heads
    slopes = jnp.exp2(-8.0 * jnp.arange(1, H + 1, dtype=F32) / H)
    slopes = jnp.broadcast_to(slopes[:, None, None], (H, 1, QT))
    return pl.pallas_call(
        functools.partial(_moba_kernel, nb=nb),
        grid=(B, hg, nt),
        in_specs=[
            pl.BlockSpec((heads, QT, D), lambda b, h, g: (h, b * nt + g, 0)),
            pl.BlockSpec((heads, S, D), lambda b, h, g: (hg + h, b, 0)),
            pl.BlockSpec((heads, S, D), lambda b, h, g: (2 * hg + h, b, 0)),
            pl.BlockSpec((heads, 1, QT), lambda b, h, g: (h, 0, 0)),
        ],
        out_specs=pl.BlockSpec((1, QT, heads * D), lambda b, h, g: (b, g, h)),
        out_shape=jax.ShapeDtypeStruct((B, S, H * D), BF16),
        scratch_shapes=[
            pltpu.VMEM((heads, nt, QT, D + LANES), BF16),
            pltpu.VMEM((heads, nt, D, QT), BF16),
            pltpu.VMEM((heads, nbp, D), F32),
            pltpu.VMEM((heads, 1, QT), F32),
            pltpu.VMEM((heads, 1, QT), F32),
            pltpu.VMEM((heads, D, QT), F32),
            pltpu.VMEM((heads, 2, QT, QT), F32),
            pltpu.VMEM((heads, QT, QT), BF16),
        ],
        compiler_params=pltpu.CompilerParams(
            dimension_semantics=("parallel", "parallel", "arbitrary"),
            vmem_limit_bytes=VMEM_LIMIT),
    )(qkv_h, qkv_h, qkv_h, slopes)


def _pad_cols(w, n):
    return jnp.pad(w, ((0, 0), (0, n - w.shape[1])))


def _pad_rows(w, before, total):
    return jnp.pad(w, ((before, total - before - w.shape[0]), (0, 0)))


def _mix_in_layout(w):
    gk, gw = GLA_KEY_WIDTH, GLA_WIDTH
    g_end = 2 * gk + 2 * gw
    gla_main = w[:, :g_end]
    gla_gate = w[:, g_end:g_end + GLA_GATE_RANK]
    r0 = g_end + GLA_GATE_RANK
    rw_main = w[:, r0:r0 + 3 * RWKV_WIDTH]
    l0 = r0 + 3 * RWKV_WIDTH
    wa = w[:, l0:l0 + RWKV_DECAY_LORA + RWKV_AAA_LORA]
    g0 = l0 + RWKV_DECAY_LORA + RWKV_AAA_LORA
    gl = w[:, g0:g0 + RWKV_GATE_LORA]
    return jnp.concatenate(
        [gla_main, rw_main, _pad_cols(gla_gate, LANES), wa, _pad_cols(gl, 2 * LANES)], axis=1)


def mixer_layer0(hn_proj, gla_gate_w2, gla_gate_b, gla_norm, rwkv_mu, rwkv_w0, rwkv_w2, rwkv_a0,
                 rwkv_a2, rwkv_g2, rwkv_k_k, rwkv_k_a, rwkv_r_k, rwkv_ln_w, rwkv_ln_b,
                 *, gla_block, rwkv_block):
    W = RWKV_WIDTH
    o_gla = gla_mixer(hn_proj, _pad_rows(gla_gate_w2, 0, LANES), gla_gate_b, gla_norm,
                      block=gla_block, heads=4)
    mu_r, mu_k, mu_v = rwkv_mu[:W], rwkv_mu[W:2 * W], rwkv_mu[2 * W:3 * W]
    mu_low = rwkv_mu[3 * W:]
    n_wa = RWKV_DECAY_LORA + RWKV_AAA_LORA
    vecs = jnp.stack([rwkv_w0, rwkv_a0, rwkv_k_k, rwkv_k_a, rwkv_r_k.reshape(-1), rwkv_ln_w,
                      rwkv_ln_b, mu_r, mu_k, mu_v] + [jnp.zeros((W,), F32)] * 6)
    mu_wa = mu_low[:n_wa].reshape(1, -1)
    mu_g = _pad_cols(mu_low[n_wa:].reshape(1, -1), 2 * LANES)
    zeros = lambda n: jnp.zeros((n, W), F32)
    lora_up = jnp.concatenate(
        [rwkv_w2, zeros(LANES - RWKV_DECAY_LORA), zeros(RWKV_DECAY_LORA), rwkv_a2, rwkv_g2,
         zeros(2 * LANES - RWKV_GATE_LORA)], axis=0)
    o_rwkv = rwkv_mixer(hn_proj, vecs, mu_wa, mu_g, lora_up, block=rwkv_block, pairs=4)
    return o_gla, o_rwkv


def kernel(x, norm_mix, norm_ffn, norm_final, mix_in_w, gla_gate_w2, gla_gate_b, gla_norm, rwkv_mu, rwkv_w0, rwkv_w2, rwkv_a0, rwkv_a2, rwkv_g2, rwkv_k_k, rwkv_k_a, rwkv_r_k, rwkv_ln_w, rwkv_ln_b, mix_out_w, attn_qkv_w, attn_out_w, ffn_gate_w, ffn_up_w, ffn_down_w):
    B, S, D = x.shape
    M = B * S
    tm, tn, tf = 512, 512, 256
    tm_in = min(1024, M)
    tm_ffn = min(1024, M)
    h = x.reshape(M, D)

    w_in = _mix_in_layout(mix_in_w[0]).astype(BF16)
    p = norm_matmul(h, norm_mix[0], w_in, tm=tm_in, tn=tn).reshape(B, S, MIX_COLS)
    o_gla, o_rwkv = mixer_layer0(
        p, gla_gate_w2[0], gla_gate_b[0], gla_norm[0], rwkv_mu[0], rwkv_w0[0], rwkv_w2[0],
        rwkv_a0[0], rwkv_a2[0], rwkv_g2[0], rwkv_k_k[0], rwkv_k_a[0], rwkv_r_k[0], rwkv_ln_w[0],
        rwkv_ln_b[0], gla_block=512, rwkv_block=512)
    w_out = mix_out_w[0].astype(BF16)
    h = proj_residual(h, [(o_gla.reshape(M, GLA_WIDTH), w_out[:GLA_WIDTH]),
                          (o_rwkv.reshape(M, RWKV_WIDTH), w_out[GLA_WIDTH:])], tm=tm, tn=D)
    h = ffn_residual(h, norm_ffn[0], ffn_gate_w, ffn_up_w, ffn_down_w, norm_final, layer=0,
                     tm=tm_ffn, tf=tf, final_norm=False)

    qkv_h = norm_matmul(h, norm_mix[1], attn_qkv_w[0].astype(BF16), tm=tm_in, tn=2 * tn,
                        group=MOBA_HD)
    o_attn = moba_attention(qkv_h, B, S, heads=2)
    h = proj_residual(h, [(o_attn.reshape(M, D), attn_out_w[0].astype(BF16))], tm=tm, tn=D)
    h = ffn_residual(h, norm_ffn[1], ffn_gate_w, ffn_up_w, ffn_down_w, norm_final, layer=1,
                     tm=tm_ffn, tf=tf, final_norm=True)
    return h.reshape(B, S, D)
```

```python
import functools

import jax
import jax.numpy as jnp
from jax import lax
from jax.experimental import pallas as pl
from jax.experimental.pallas import tpu as pltpu

F32 = jnp.float32
BF16 = jnp.bfloat16
HI = lax.Precision.HIGHEST

D_MODEL = 2048
RMS_EPS = 1e-6

GLA_HEADS = 4
GLA_DK = 128
GLA_DV = 256
GLA_KEY_WIDTH = GLA_HEADS * GLA_DK
GLA_WIDTH = GLA_HEADS * GLA_DV
GLA_GATE_RANK = 16
GLA_GATE_TAU = 16.0
GLA_CHUNK = 64

RWKV_WIDTH = 1024
RWKV_HEAD = 64
RWKV_PAIR = 2 * RWKV_HEAD
RWKV_PAIRS = RWKV_WIDTH // RWKV_PAIR
RWKV_DECAY_LORA = 64
RWKV_AAA_LORA = 64
RWKV_GATE_LORA = 160
RWKV_LN_EPS = RWKV_HEAD * 1e-5
RWKV_CHUNK = 64

MOBA_HEADS = 16
MOBA_HD = 128
MOBA_BLOCK = 256
MOBA_TOPK = 3
NEG_INF = -1e30

FFN_HIDDEN = 5632

LANES = 128

COL_GLA_Q = 0
COL_GLA_K = 512
COL_GLA_V = 1024
COL_GLA_OG = 2048
COL_RWKV_R = 3072
COL_RWKV_K = 4096
COL_RWKV_V = 5120
COL_LOW_GATE = 6144
COL_LOW_WA = 6272
COL_LOW_G = 6400
MIX_COLS = 6656

VMEM_LIMIT = 56 * 1024 * 1024


def _mm(a, b):
    return jnp.dot(a.astype(BF16), b.astype(BF16), preferred_element_type=F32)


def _mm_nt(a, b):
    return lax.dot_general(a.astype(BF16), b.astype(BF16), (((1,), (1,)), ((), ())),
                           preferred_element_type=F32)


def _mm_hi(a, b):
    return jnp.dot(a, b, preferred_element_type=F32, precision=HI)


def _mm_nt_hi(a, b):
    return lax.dot_general(a, b, (((1,), (1,)), ((), ())), preferred_element_type=F32,
                           precision=HI)


def _dot(a, b):
    return jnp.dot(a, b, preferred_element_type=F32)


def _dot_tn(a, b):
    return lax.dot_general(a, b, (((0,), (0,)), ((), ())), preferred_element_type=F32)


def _split(x, parts):
    out = []
    for _ in range(parts - 1):
        hi = x.astype(BF16)
        out.append(hi)
        x = x - hi.astype(F32)
    out.append(x.astype(BF16))
    return out


def _mm_split(a, b, *, a_parts=1, b_parts=1):
    a_p = _split(a, a_parts) if a_parts > 1 else [a.astype(BF16)]
    b_p = _split(b, b_parts) if b_parts > 1 else [b.astype(BF16)]
    acc = None
    for i, ai in enumerate(a_p):
        for j, bj in enumerate(b_p):
            if i + j < max(a_parts, b_parts):
                t = _dot(ai, bj)
                acc = t if acc is None else acc + t
    return acc


def _sigmoid(x):
    return 1.0 / (1.0 + jnp.exp(-x))


def _softplus(x):
    return jnp.maximum(x, 0.0) + jnp.log1p(jnp.exp(-jnp.abs(x)))


def _iota2(shape, axis):
    return lax.broadcasted_iota(jnp.int32, shape, axis)


def _norm_matmul_kernel(x_ref, g_ref, w_ref, o_ref, xn_ref, *, group):
    @pl.when(pl.program_id(1) == 0)
    def _():
        x = x_ref[...]
        ms = jnp.mean(x * x, axis=-1, keepdims=True)
        xn_ref[...] = (x * lax.rsqrt(ms + RMS_EPS) * g_ref[...]).astype(BF16)

    res = jnp.dot(xn_ref[...], w_ref[...], preferred_element_type=F32).astype(o_ref.dtype)
    if group is None:
        o_ref[...] = res
    else:
        for c in range(res.shape[1] // group):
            o_ref[c] = res[:, c * group:(c + 1) * group]


def norm_matmul(x, g, w, *, tm, tn, out_dtype=F32, group=None):
    M, D = x.shape
    N = w.shape[1]
    if group is None:
        out_specs = pl.BlockSpec((tm, tn), lambda i, j: (i, j))
        out_shape = jax.ShapeDtypeStruct((M, N), out_dtype)
    else:
        out_specs = pl.BlockSpec((tn // group, tm, group), lambda i, j: (j, i, 0))
        out_shape = jax.ShapeDtypeStruct((N // group, M, group), out_dtype)
    return pl.pallas_call(
        functools.partial(_norm_matmul_kernel, group=group),
        grid=(M // tm, N // tn),
        in_specs=[
            pl.BlockSpec((tm, D), lambda i, j: (i, 0)),
            pl.BlockSpec((1, D), lambda i, j: (0, 0)),
            pl.BlockSpec((D, tn), lambda i, j: (0, j)),
        ],
        out_specs=out_specs,
        out_shape=out_shape,
        scratch_shapes=[pltpu.VMEM((tm, D), BF16)],
        compiler_params=pltpu.CompilerParams(
            dimension_semantics=("parallel", "arbitrary"), vmem_limit_bytes=VMEM_LIMIT),
    )(x, g.reshape(1, D), w)


def _proj_res_kernel(*refs, n_in):
    res_ref = refs[0]
    o_ref = refs[1 + 2 * n_in]
    acc = res_ref[...]
    for i in range(n_in):
        acc = acc + jnp.dot(refs[1 + 2 * i][...], refs[2 + 2 * i][...],
                            preferred_element_type=F32)
    o_ref[...] = acc


def proj_residual(res, pairs, *, tm, tn):
    M, N = res.shape
    in_specs = [pl.BlockSpec((tm, tn), lambda i, j: (i, j))]
    args = [res]
    for a, w in pairs:
        K = a.shape[1]
        in_specs.append(pl.BlockSpec((tm, K), lambda i, j: (i, 0)))
        in_specs.append(pl.BlockSpec((K, tn), lambda i, j: (0, j)))
        args += [a, w]
    return pl.pallas_call(
        functools.partial(_proj_res_kernel, n_in=len(pairs)),
        grid=(M // tm, N // tn),
        in_specs=in_specs,
        out_specs=pl.BlockSpec((tm, tn), lambda i, j: (i, j)),
        out_shape=jax.ShapeDtypeStruct((M, N), F32),
        compiler_params=pltpu.CompilerParams(
            dimension_semantics=("parallel", "arbitrary"), vmem_limit_bytes=VMEM_LIMIT),
    )(*args)


def _ffn_kernel(x_ref, g_ref, wg_ref, wu_ref, wd_ref, gf_ref, o_ref, xn_ref, *, final_norm):
    j = pl.program_id(1)

    @pl.when(j == 0)
    def _():
        x = x_ref[...]
        ms = jnp.mean(x * x, axis=-1, keepdims=True)
        xn_ref[...] = (x * lax.rsqrt(ms + RMS_EPS) * g_ref[...]).astype(BF16)
        o_ref[...] = x

    xn = xn_ref[...]
    gate = jnp.dot(xn, wg_ref[...].astype(BF16), preferred_element_type=F32)
    up = jnp.dot(xn, wu_ref[...].astype(BF16), preferred_element_type=F32)
    act = (gate * _sigmoid(gate) * up).astype(BF16)
    o_ref[...] += jnp.dot(act, wd_ref[...].astype(BF16), preferred_element_type=F32)

    if final_norm:
        @pl.when(j == pl.num_programs(1) - 1)
        def _():
            h = o_ref[...]
            ms = jnp.mean(h * h, axis=-1, keepdims=True)
            o_ref[...] = h * lax.rsqrt(ms + RMS_EPS) * gf_ref[...]


def ffn_residual(x, g, wg, wu, wd, g_final, *, layer, tm, tf, final_norm):
    M, D = x.shape
    F = wg.shape[2]
    return pl.pallas_call(
        functools.partial(_ffn_kernel, final_norm=final_norm),
        grid=(M // tm, F // tf),
        in_specs=[
            pl.BlockSpec((tm, D), lambda i, j: (i, 0)),
            pl.BlockSpec((1, D), lambda i, j: (0, 0)),
            pl.BlockSpec((None, D, tf), lambda i, j: (layer, 0, j)),
            pl.BlockSpec((None, D, tf), lambda i, j: (layer, 0, j)),
            pl.BlockSpec((None, tf, D), lambda i, j: (layer, j, 0)),
            pl.BlockSpec((1, D), lambda i, j: (0, 0)),
        ],
        out_specs=pl.BlockSpec((tm, D), lambda i, j: (i, 0)),
        out_shape=jax.ShapeDtypeStruct((M, D), F32),
        scratch_shapes=[pltpu.VMEM((tm, D), BF16)],
        compiler_params=pltpu.CompilerParams(
            dimension_semantics=("parallel", "arbitrary"), vmem_limit_bytes=VMEM_LIMIT),
    )(x, g.reshape(1, D), wg, wu, wd, g_final.reshape(1, D))


def _gla_kernel(q_ref, k_ref, v_ref, og_ref, gl_ref, w2_ref, b_ref, gn_ref, o_ref, st_ref,
                *, block, chunk):
    n_heads = q_ref.shape[2] // GLA_DK

    @pl.when(pl.program_id(2) == 0)
    def _():
        st_ref[...] = jnp.zeros_like(st_ref)

    logit = _mm_split(gl_ref[0], w2_ref[...], a_parts=2, b_parts=2) + b_ref[...]
    log_a = -_softplus(-logit) * (1.0 / GLA_GATE_TAU)
    tri = _iota2((chunk, chunk), 0) >= _iota2((chunk, chunk), 1)
    tri_b = tri.astype(BF16)
    tri_b3 = jnp.concatenate([tri_b, tri_b, tri_b], axis=1)
    scale = GLA_DK ** -0.5
    gn = gn_ref[...]
    n_chunks = block // chunk

    streams = [(h, c) for h in range(n_heads) for c in range(n_chunks)]

    def rows(c):
        return pl.ds(c * chunk, chunk)

    def k_lanes(h):
        return slice(h * GLA_DK, (h + 1) * GLA_DK)

    def v_lanes(h):
        return slice(h * GLA_DV, (h + 1) * GLA_DV)

    la3 = [jnp.concatenate(_split(log_a[c * chunk:(c + 1) * chunk, k_lanes(h)], 3), axis=0)
           for h, c in streams]
    cum = [_dot(tri_b3, x) for x in la3]
    q_dec = [(q_ref[0, rows(c), k_lanes(h)] * scale * jnp.exp(b)).astype(BF16)
             for (h, c), b in zip(streams, cum)]
    k_inv = [(k_ref[0, rows(c), k_lanes(h)] * jnp.exp(-b)).astype(BF16)
             for (h, c), b in zip(streams, cum)]
    k_dec = [(k_ref[0, rows(c), k_lanes(h)] * jnp.exp(b[chunk - 1:chunk] - b)).astype(BF16)
             for (h, c), b in zip(streams, cum)]
    v_b = [v_ref[0, rows(c), v_lanes(h)].astype(BF16) for h, c in streams]
    v_t = [v_ref[0, rows(c), v_lanes(h)].T.astype(BF16) for h, c in streams]
    att = [jnp.where(tri, _mm_nt(x, y), 0.0).astype(BF16) for x, y in zip(q_dec, k_inv)]
    o_intra = [_dot(x, y) for x, y in zip(att, v_b)]
    gain = [_dot(x, y) for x, y in zip(v_t, k_dec)]
    pre = [dict(q_dec=q_dec[i], o_intra=o_intra[i], gain=gain[i],
                decay=jnp.exp(cum[i][chunk - 1:chunk])) for i in range(len(streams))]

    st = [st_ref[h] for h in range(n_heads)]
    outs = [[] for _ in range(n_heads)]
    for c in range(n_chunks):
        for h in range(n_heads):
            p = pre[h * n_chunks + c]
            outs[h].append(p["o_intra"] + _mm_nt(p["q_dec"], st[h]))
            st[h] = st[h] * p["decay"] + p["gain"]
    for h in range(n_heads):
        st_ref[h] = st[h]

    for h in range(n_heads):
        lv = slice(h * GLA_DV, (h + 1) * GLA_DV)
        o = jnp.concatenate(outs[h], axis=0)
        ms = jnp.mean(o * o, axis=-1, keepdims=True)
        og = og_ref[0, :, lv]
        o_ref[0, :, lv] = (o * lax.rsqrt(ms + RMS_EPS) * gn * (og * _sigmoid(og))).astype(o_ref.dtype)


def gla_mixer(p3, gate_w2p, gate_b, gla_norm, *, block, heads):
    B, S, _ = p3.shape
    wk, wv = heads * GLA_DK, heads * GLA_DV
    kq, kk_, kv, kog = COL_GLA_Q // wk, COL_GLA_K // wk, COL_GLA_V // wv, COL_GLA_OG // wv
    klow = COL_LOW_GATE // LANES
    return pl.pallas_call(
        functools.partial(_gla_kernel, block=block, chunk=GLA_CHUNK),
        grid=(B, GLA_HEADS // heads, S // block),
        in_specs=[
            pl.BlockSpec((1, block, wk), lambda b, h, s: (b, s, kq + h)),
            pl.BlockSpec((1, block, wk), lambda b, h, s: (b, s, kk_ + h)),
            pl.BlockSpec((1, block, wv), lambda b, h, s: (b, s, kv + h)),
            pl.BlockSpec((1, block, wv), lambda b, h, s: (b, s, kog + h)),
            pl.BlockSpec((1, block, LANES), lambda b, h, s: (b, s, klow)),
            pl.BlockSpec((LANES, wk), lambda b, h, s: (0, h)),
            pl.BlockSpec((1, wk), lambda b, h, s: (0, h)),
            pl.BlockSpec((1, GLA_DV), lambda b, h, s: (0, 0)),
        ],
        out_specs=pl.BlockSpec((1, block, wv), lambda b, h, s: (b, s, h)),
        out_shape=jax.ShapeDtypeStruct((B, S, GLA_WIDTH), BF16),
        scratch_shapes=[pltpu.VMEM((heads, GLA_DV, GLA_DK), F32)],
        compiler_params=pltpu.CompilerParams(
            dimension_semantics=("parallel", "parallel", "arbitrary"),
            vmem_limit_bytes=VMEM_LIMIT),
    )(p3, p3, p3, p3, p3, gate_w2p, gate_b.reshape(1, -1), gla_norm.reshape(1, -1))


def _unit_lower_inverses(mats, rowi, coli):
    def same_block(s):
        sh = s.bit_length() - 1
        return (rowi >> sh) == (coli >> sh)

    eye = (rowi == coli).astype(F32)
    a8 = [jnp.where(same_block(8), a, 0.0).astype(BF16) for a in mats]
    inv = [eye + a.astype(F32) for a in a8]
    p = [_dot(a, a) for a in a8]
    inv = [x + _mm(x, y) for x, y in zip(inv, p)]
    p = [_mm(y, y) for y in p]
    inv = [x + _mm(x, y) for x, y in zip(inv, p)]
    s = 8
    while s < RWKV_CHUNK:
        off = same_block(2 * s) & jnp.logical_not(same_block(s))
        e = [jnp.where(off, a, 0.0).astype(BF16) for a in mats]
        inv_b = [x.astype(BF16) for x in inv]
        t = [_dot(x, y).astype(BF16) for x, y in zip(inv_b, e)]
        inv = [x + _dot(y, z) for x, y, z in zip(inv, t, inv_b)]
        s *= 2
    return inv


def _rwkv_kernel(r_ref, k_ref, v_ref, wa_ref, g_ref, vec_ref, muwa_ref, mug_ref, w2_ref, a2_ref,
                 g2_ref, o_ref, st_ref, prev_ref, prevwa_ref, prevg_ref, *, block, chunk):
    T = chunk
    n_pairs = r_ref.shape[2] // RWKV_PAIR
    first = pl.program_id(2) == 0

    @pl.when(first)
    def _():
        st_ref[...] = jnp.zeros_like(st_ref)
        prev_ref[...] = jnp.zeros_like(prev_ref)
        prevwa_ref[...] = jnp.zeros_like(prevwa_ref)
        prevg_ref[...] = jnp.zeros_like(prevg_ref)

    row0 = _iota2((block, 1), 0) == 0

    def lerp(x, prev_row, mu):
        shifted = jnp.where(row0, prev_row, pltpu.roll(x, 1, axis=0))
        return x + (shifted - x) * mu

    vec = vec_ref[...]
    w0, a0, k_k, k_a, r_k, ln_w, ln_b = (vec[i:i + 1] for i in range(7))
    mu_r, mu_k, mu_v = (vec[i:i + 1] for i in range(7, 10))

    r_raw, k_raw, v_raw, wa_raw, g_raw = r_ref[0], k_ref[0], v_ref[0], wa_ref[0], g_ref[0]
    prev = prev_ref[...]
    r_all = lerp(r_raw, prev[0:1], mu_r)
    k_all = lerp(k_raw, prev[1:2], mu_k)
    v_all = lerp(v_raw, prev[2:3], mu_v)
    wa = lerp(wa_raw, prevwa_ref[0:1], muwa_ref[...])
    g_low = lerp(g_raw, prevg_ref[0:1], mug_ref[...])
    prev_ref[0:1] = r_raw[block - 1:block]
    prev_ref[1:2] = k_raw[block - 1:block]
    prev_ref[2:3] = v_raw[block - 1:block]
    prevwa_ref[0:1] = wa_raw[block - 1:block]
    prevg_ref[0:1] = g_raw[block - 1:block]

    z = w0 + _mm_split(jnp.tanh(wa), w2_ref[...], a_parts=2, b_parts=2)
    logw_all = -jnp.exp(-_softplus(-z) - 0.5)
    lr_all = _sigmoid(a0 + _mm_split(wa, a2_ref[...], a_parts=2, b_parts=2))
    gate_all = _mm_split(_sigmoid(g_low), g2_ref[...], a_parts=2, b_parts=2)

    lane = _iota2((1, RWKV_PAIR), 1)
    m0 = (lane < RWKV_HEAD).astype(F32)
    m1 = 1.0 - m0
    rowi = _iota2((2 * T, 2 * T), 0)
    coli = _iota2((2 * T, 2 * T), 1)
    strict = rowi > coli
    incl = rowi >= coli
    head_ones = ((rowi < RWKV_HEAD) == (coli < RWKV_HEAD)).astype(BF16)
    head_ones2 = jnp.concatenate([head_ones, head_ones], axis=0)
    tri_t = (_iota2((T, T), 0) >= _iota2((T, T), 1)).astype(BF16)
    tri_t3 = jnp.concatenate([tri_t, tri_t, tri_t], axis=1)
    ones_t3 = jnp.ones((3 * T, RWKV_PAIR), BF16)

    def hsum(x):
        return _dot(jnp.concatenate(_split(x, 2), axis=1), head_ones2)

    def stack(x):
        return jnp.concatenate([x * m0, x * m1], axis=0)

    n_chunks = block // T

    kk_p, k2_p, b_p, bonus = [], [], [], []
    for pair in range(n_pairs):
        ln = slice(pair * RWKV_PAIR, (pair + 1) * RWKV_PAIR)
        kkp = k_all[:, ln] * k_k[:, ln]
        kk_p.append(kkp / jnp.maximum(jnp.sqrt(hsum(kkp * kkp)), 1e-12))
        k2_p.append(k_all[:, ln] * (1.0 + (lr_all[:, ln] - 1.0) * k_a[:, ln]))
        b_p.append(kk_p[pair] * lr_all[:, ln])
        bonus.append(hsum(r_all[:, ln] * k2_p[pair] * r_k[:, ln]) * v_all[:, ln])

    streams = [(pair, c) for pair in range(n_pairs) for c in range(n_chunks)]

    def tile(x, pair, c):
        return x[c * T:(c + 1) * T, pair * RWKV_PAIR:(pair + 1) * RWKV_PAIR]

    lw = [tile(logw_all, p, c) for p, c in streams]
    lw3 = [jnp.concatenate(_split(x, 3), axis=0) for x in lw]
    cum = [_dot(tri_t3, x) for x in lw3]
    decay = [jnp.exp(_dot_tn(x, ones_t3)) for x in lw3]
    w_inv = [jnp.exp(-x) for x in cum]
    w_tail = [jnp.exp(x[T - 1:T] - x) for x in cum]
    a_s = [stack(-kk_p[p][c * T:(c + 1) * T] * jnp.exp(x - y)).astype(BF16)
           for (p, c), x, y in zip(streams, cum, lw)]
    r_s = [stack(tile(r_all, p, c) * jnp.exp(x)).astype(BF16) for (p, c), x in zip(streams, cum)]
    b_s = [stack(b_p[p][c * T:(c + 1) * T] * x).astype(BF16) for (p, c), x in zip(streams, w_inv)]
    k_s = [stack(k2_p[p][c * T:(c + 1) * T] * x).astype(BF16) for (p, c), x in zip(streams, w_inv)]
    v_s = [stack(tile(v_all, p, c)).astype(BF16) for p, c in streams]
    b_tail = [stack(b_p[p][c * T:(c + 1) * T] * x).astype(BF16)
              for (p, c), x in zip(streams, w_tail)]
    k_tail = [stack(k2_p[p][c * T:(c + 1) * T] * x).astype(BF16)
              for (p, c), x in zip(streams, w_tail)]
    prod = [_mm_nt(jnp.concatenate([a, r], axis=0), jnp.concatenate([b, k], axis=0))
            for a, r, b, k in zip(a_s, r_s, b_s, k_s)]
    pre = [dict(a_s=a_s[i], r_s=r_s[i], v_s=v_s[i], decay=decay[i], b_tail=b_tail[i],
                k_tail=k_tail[i],
                a_ab=jnp.where(strict, prod[i][:2 * T, :2 * T], 0.0),
                a_ak=jnp.where(strict, prod[i][:2 * T, 2 * T:], 0.0).astype(BF16),
                a_rb=jnp.where(incl, prod[i][2 * T:, :2 * T], 0.0).astype(BF16),
                a_rk=jnp.where(incl, prod[i][2 * T:, 2 * T:], 0.0).astype(BF16))
           for i in range(len(streams))]

    invs = [x.astype(BF16) for x in _unit_lower_inverses([p["a_ab"] for p in pre], rowi, coli)]
    akv = [_dot(p["a_ak"], p["v_s"]).astype(BF16) for p in pre]
    x = [_dot(i, jnp.concatenate([p["a_s"], y], axis=1)).astype(BF16)
         for i, p, y in zip(invs, pre, akv)]
    bx = [_dot_tn(p["b_tail"], y) for p, y in zip(pre, x)]
    rx = [_dot(p["a_rb"], y) for p, y in zip(pre, x)]
    trans = [y[:, :RWKV_PAIR].astype(BF16) for y in bx]
    gain = [y[:, RWKV_PAIR:] + _dot_tn(p["k_tail"], p["v_s"]) for p, y in zip(pre, bx)]
    read = [(p["r_s"].astype(F32) + y[:, :RWKV_PAIR]).astype(BF16) for p, y in zip(pre, rx)]
    y_free = [y[:, RWKV_PAIR:] + _dot(p["a_rk"], p["v_s"]) for p, y in zip(pre, rx)]

    st = [st_ref[pair] for pair in range(n_pairs)]
    ys = [[] for _ in range(n_pairs)]
    for c in range(n_chunks):
        for pair in range(n_pairs):
            i = pair * n_chunks + c
            st_b = st[pair].astype(BF16)
            y_s = _dot(read[i], st_b) + y_free[i]
            ys[pair].append(y_s[:T] + y_s[T:])
            st[pair] = st[pair] * pre[i]["decay"] + _dot(trans[i], st_b) + gain[i]
    for pair in range(n_pairs):
        st_ref[pair] = st[pair]

    for pair in range(n_pairs):
        ln = slice(pair * RWKV_PAIR, (pair + 1) * RWKV_PAIR)
        y = jnp.concatenate(ys[pair], axis=0)
        mu = hsum(y) * (1.0 / RWKV_HEAD)
        d = y - mu
        var = hsum(d * d) * (1.0 / RWKV_HEAD)
        yn = d * lax.rsqrt(var + RWKV_LN_EPS) * ln_w[:, ln] + ln_b[:, ln]
        o_ref[0, :, ln] = ((yn + bonus[pair]) * gate_all[:, ln]).astype(o_ref.dtype)


def rwkv_mixer(p3, vecs, mu_wa, mu_g, lora_up, *, block, pairs):
    B, S, _ = p3.shape
    W = pairs * RWKV_PAIR
    kr, kk_, kv = COL_RWKV_R // W, COL_RWKV_K // W, COL_RWKV_V // W
    kwa = COL_LOW_WA // LANES
    kg = COL_LOW_G // (2 * LANES)
    return pl.pallas_call(
        functools.partial(_rwkv_kernel, block=block, chunk=RWKV_CHUNK),
        grid=(B, RWKV_PAIRS // pairs, S // block),
        in_specs=[
            pl.BlockSpec((1, block, W), lambda b, j, s: (b, s, kr + j)),
            pl.BlockSpec((1, block, W), lambda b, j, s: (b, s, kk_ + j)),
            pl.BlockSpec((1, block, W), lambda b, j, s: (b, s, kv + j)),
            pl.BlockSpec((1, block, LANES), lambda b, j, s: (b, s, kwa)),
            pl.BlockSpec((1, block, 2 * LANES), lambda b, j, s: (b, s, kg)),
            pl.BlockSpec((16, W), lambda b, j, s: (0, j)),
            pl.BlockSpec((1, LANES), lambda b, j, s: (0, 0)),
            pl.BlockSpec((1, 2 * LANES), lambda b, j, s: (0, 0)),
            pl.BlockSpec((LANES, W), lambda b, j, s: (0, j)),
            pl.BlockSpec((LANES, W), lambda b, j, s: (1, j)),
            pl.BlockSpec((2 * LANES, W), lambda b, j, s: (1, j)),
        ],
        out_specs=pl.BlockSpec((1, block, W), lambda b, j, s: (b, s, j)),
        out_shape=jax.ShapeDtypeStruct((B, S, RWKV_WIDTH), BF16),
        scratch_shapes=[
            pltpu.VMEM((pairs, RWKV_PAIR, RWKV_PAIR), F32),
            pltpu.VMEM((8, W), F32),
            pltpu.VMEM((8, LANES), F32),
            pltpu.VMEM((8, 2 * LANES), F32),
        ],
        compiler_params=pltpu.CompilerParams(
            dimension_semantics=("parallel", "parallel", "arbitrary"),
            vmem_limit_bytes=VMEM_LIMIT),
    )(p3, p3, p3, p3, p3, vecs, mu_wa, mu_g, lora_up, lora_up, lora_up)


def _moba_kernel(q_ref, k_ref, v_ref, slope_ref, o_ref, kb_ref, vt_ref, kmean_ref,
                 m_ref, l_ref, acc_ref, s_ref, p_ref, *, nb):
    BS, D = MOBA_BLOCK, MOBA_HD
    QT = 2 * BS
    heads = q_ref.shape[0]
    g = pl.program_id(2)
    log2e = 1.4426950408889634
    c1 = MOBA_HD ** -0.5 * log2e
    slope2 = [slope_ref[h] * log2e for h in range(heads)]
    s_parts = [_split(x, 3) for x in slope2]

    @pl.when(g == 0)
    def _():
        kmean_ref[...] = jnp.zeros_like(kmean_ref)
        lane = _iota2((QT, LANES), 1)
        kpos = _iota2((QT, LANES), 0)
        kpos_lo = (kpos & (BS - 1)).astype(F32)
        kpos_hi = (kpos & BS).astype(F32)
        for h in range(heads):
            sk = [x[:, :LANES].astype(F32) for x in s_parts[h]]
            piece = jnp.where((lane == 0) | (lane == 3), sk[0],
                              jnp.where((lane == 1) | (lane == 4), sk[1], sk[2]))
            feat = jnp.where(lane < 6, piece,
                             jnp.where(lane < 9, kpos_lo, jnp.where(lane < 12, kpos_hi, 0.0)))
            feat = feat.astype(BF16)
            for j in range(nb):
                half = slice((j % 2) * BS, (j % 2 + 1) * BS)
                kj = k_ref[h, pl.ds(j * BS, BS), :]
                kmean_ref[h, j:j + 1, :] = jnp.mean(kj, axis=0, keepdims=True)
                kb_ref[h, j // 2, half, :D] = kj.astype(BF16)
                vt_ref[h, j // 2, :, half] = v_ref[h, pl.ds(j * BS, BS), :].T.astype(BF16)
            for t in range(nb // 2):
                kb_ref[h, t, :, D:] = feat

    qpos = _iota2((1, QT), 1)
    blk_q = 2 * g + (qpos >= BS).astype(jnp.int32)
    nbp = kmean_ref.shape[1]
    blk = _iota2((nbp, QT), 0)
    arow = _iota2((LANES, QT), 0)
    qpos_a = _iota2((LANES, QT), 1)
    qpos_lo = (qpos_a & (BS - 1)).astype(F32)
    qpos_hi = (qpos_a & BS).astype(F32)
    kp = _iota2((QT, QT), 0)
    qp = _iota2((QT, QT), 1)
    own = (kp <= qp) & ((kp >= BS) == (qp >= BS))
    first_block = (kp <= qp) & (kp < BS)

    def select(h):
        q_t = q_ref[h].T
        gate = _mm_split(kmean_ref[h], q_t, a_parts=3, b_parts=3)
        gate = jnp.where(blk < blk_q, gate, NEG_INF)
        sel = []
        for r in range(MOBA_TOPK):
            mx = jnp.max(gate, axis=0, keepdims=True)
            idx = jnp.min(jnp.where(gate == mx, blk, nbp), axis=0, keepdims=True)
            sel.append(jnp.where(r < blk_q, idx, -1))
            gate = jnp.where(blk == idx, -jnp.inf, gate)
        sq = [x.astype(F32) for x in s_parts[h]]
        piece = jnp.where((arow == 6) | (arow == 9), sq[0],
                          jnp.where((arow == 7) | (arow == 10), sq[1], sq[2]))
        aug = jnp.where(arow < 3, -qpos_lo,
                        jnp.where(arow < 6, -qpos_hi, jnp.where(arow < 12, piece, 0.0)))
        q_aug = jnp.concatenate([(q_t * c1).astype(BF16), aug.astype(BF16)], axis=0)
        return sel, q_aug

    sel, q_aug = zip(*[select(h) for h in range(heads)])

    def picked(h, j):
        return (sel[h][0] == j) | (sel[h][1] == j) | (sel[h][2] == j)

    for h in range(heads):
        allowed = own | (first_block & picked(h, 2 * g))
        s_own = jnp.where(allowed, _dot(kb_ref[h, g], q_aug[h]), NEG_INF)
        m0 = jnp.max(s_own, axis=0, keepdims=True)
        p0 = jnp.exp2(s_own - m0)
        m_ref[h] = m0
        l_ref[h] = jnp.sum(p0, axis=0, keepdims=True)
        p_ref[h] = p0.astype(BF16)
        acc_ref[h] = jnp.zeros((D, QT), F32)
        s_ref[h, 0] = _dot(kb_ref[h, 0], q_aug[h])

    def past_pair(h, t, prev):
        s_cur = s_ref[h, t % 2]
        pv = _dot(vt_ref[h, prev], p_ref[h])
        s_ref[h, (t + 1) % 2] = _dot(kb_ref[h, jnp.minimum(t + 1, nb // 2 - 1)], q_aug[h])
        s_a = jnp.where(picked(h, 2 * t), s_cur[:BS], NEG_INF)
        s_b = jnp.where(picked(h, 2 * t + 1), s_cur[BS:], NEG_INF)
        off = slope2[h] * ((g - t) * QT).astype(F32)
        m_old = m_ref[h]
        mx = jnp.maximum(jnp.max(s_a, axis=0, keepdims=True), jnp.max(s_b, axis=0, keepdims=True))
        m_new = jnp.maximum(m_old, mx - off)
        alpha = jnp.exp2(m_old - m_new)
        shift = m_new + off
        p_a = jnp.exp2(s_a - shift)
        p_b = jnp.exp2(s_b - shift)
        l_ref[h] = (alpha * l_ref[h] + jnp.sum(p_a, axis=0, keepdims=True)
                    + jnp.sum(p_b, axis=0, keepdims=True))
        acc_ref[h] = alpha * (acc_ref[h] + pv)
        p_ref[h, :BS, :] = p_a.astype(BF16)
        p_ref[h, BS:, :] = p_b.astype(BF16)
        m_ref[h] = m_new
        return t

    last = [lax.fori_loop(0, g, functools.partial(past_pair, h), g) for h in range(heads)]
    for h in range(heads):
        acc = acc_ref[h] + _dot(vt_ref[h, last[h]], p_ref[h])
        o_ref[0, :, h * D:(h + 1) * D] = (acc / l_ref[h]).T.astype(o_ref.dtype)


def moba_attention(qkv_h, B, S, *, heads):
    H, D, BS = MOBA_HEADS, MOBA_HD, MOBA_BLOCK
    QT = 2 * BS
    nb = S // BS
    assert nb % 2 == 0, "key blocks are stored in pairs"
    nt = nb // 2
    nbp = -(-nb // 8) * 8
    hg = H // heads
    slopes = jnp.exp2(-8.0 * jnp.arange(1, H + 1, dtype=F32) / H)
    slopes = jnp.broadcast_to(slopes[:, None, None], (H, 1, QT))
    return pl.pallas_call(
        functools.partial(_moba_kernel, nb=nb),
        grid=(B, hg, nt),
        in_specs=[
            pl.BlockSpec((heads, QT, D), lambda b, h, g: (h, b * nt + g, 0)),
            pl.BlockSpec((heads, S, D), lambda b, h, g: (hg + h, b, 0)),
            pl.BlockSpec((heads, S, D), lambda b, h, g: (2 * hg + h, b, 0)),
            pl.BlockSpec((heads, 1, QT), lambda b, h, g: (h, 0, 0)),
        ],
        out_specs=pl.BlockSpec((1, QT, heads * D), lambda b, h, g: (b, g, h)),
        out_shape=jax.ShapeDtypeStruct((B, S, H * D), BF16),
        scratch_shapes=[
            pltpu.VMEM((heads, nt, QT, D + LANES), BF16),
            pltpu.VMEM((heads, nt, D, QT), BF16),
            pltpu.VMEM((heads, nbp, D), F32),
            pltpu.VMEM((heads, 1, QT), F32),
            pltpu.VMEM((heads, 1, QT), F32),
            pltpu.VMEM((heads, D, QT), F32),
            pltpu.VMEM((heads, 2, QT, QT), F32),
            pltpu.VMEM((heads, QT, QT), BF16),
        ],
        compiler_params=pltpu.CompilerParams(
            dimension_semantics=("parallel", "parallel", "arbitrary"),
            vmem_limit_bytes=VMEM_LIMIT),
    )(qkv_h, qkv_h, qkv_h, slopes)


def _pad_cols(w, n):
    return jnp.pad(w, ((0, 0), (0, n - w.shape[1])))


def _pad_rows(w, before, total):
    return jnp.pad(w, ((before, total - before - w.shape[0]), (0, 0)))


def _mix_in_layout(w):
    gk, gw = GLA_KEY_WIDTH, GLA_WIDTH
    g_end = 2 * gk + 2 * gw
    gla_main = w[:, :g_end]
    gla_gate = w[:, g_end:g_end + GLA_GATE_RANK]
    r0 = g_end + GLA_GATE_RANK
    rw_main = w[:, r0:r0 + 3 * RWKV_WIDTH]
    l0 = r0 + 3 * RWKV_WIDTH
    wa = w[:, l0:l0 + RWKV_DECAY_LORA + RWKV_AAA_LORA]
    g0 = l0 + RWKV_DECAY_LORA + RWKV_AAA_LORA
    gl = w[:, g0:g0 + RWKV_GATE_LORA]
    zeros = lambda n: jnp.zeros((w.shape[0], n), w.dtype)
    return jnp.concatenate(
        [gla_main, rw_main, gla_gate, zeros(LANES - GLA_GATE_RANK), wa, gl,
         zeros(2 * LANES - RWKV_GATE_LORA)], axis=1)


def mixer_layer0(hn_proj, gla_gate_w2, gla_gate_b, gla_norm, rwkv_mu, rwkv_w0, rwkv_w2, rwkv_a0,
                 rwkv_a2, rwkv_g2, rwkv_k_k, rwkv_k_a, rwkv_r_k, rwkv_ln_w, rwkv_ln_b,
                 *, gla_block, rwkv_block):
    W = RWKV_WIDTH
    o_gla = gla_mixer(hn_proj, _pad_rows(gla_gate_w2, 0, LANES), gla_gate_b, gla_norm,
                      block=gla_block, heads=4)
    mu_r, mu_k, mu_v = rwkv_mu[:W], rwkv_mu[W:2 * W], rwkv_mu[2 * W:3 * W]
    mu_low = rwkv_mu[3 * W:]
    n_wa = RWKV_DECAY_LORA + RWKV_AAA_LORA
    vecs = jnp.stack([rwkv_w0, rwkv_a0, rwkv_k_k, rwkv_k_a, rwkv_r_k.reshape(-1), rwkv_ln_w,
                      rwkv_ln_b, mu_r, mu_k, mu_v] + [jnp.zeros((W,), F32)] * 6)
    mu_wa = mu_low[:n_wa].reshape(1, -1)
    mu_g = _pad_cols(mu_low[n_wa:].reshape(1, -1), 2 * LANES)
    zeros = lambda n: jnp.zeros((n, W), F32)
    lora_up = jnp.concatenate(
        [rwkv_w2, zeros(LANES - RWKV_DECAY_LORA), zeros(RWKV_DECAY_LORA), rwkv_a2, rwkv_g2,
         zeros(2 * LANES - RWKV_GATE_LORA)], axis=0)
    o_rwkv = rwkv_mixer(hn_proj, vecs, mu_wa, mu_g, lora_up, block=rwkv_block, pairs=4)
    return o_gla, o_rwkv


def kernel(x, norm_mix, norm_ffn, norm_final, mix_in_w, gla_gate_w2, gla_gate_b, gla_norm, rwkv_mu, rwkv_w0, rwkv_w2, rwkv_a0, rwkv_a2, rwkv_g2, rwkv_k_k, rwkv_k_a, rwkv_r_k, rwkv_ln_w, rwkv_ln_b, mix_out_w, attn_qkv_w, attn_out_w, ffn_gate_w, ffn_up_w, ffn_down_w):
    B, S, D = x.shape
    M = B * S
    tm, tn, tf = 512, 512, 256
    tm_in = min(1024, M)
    tm_ffn = min(1024, M)
    h = x.reshape(M, D)

    w_in = _mix_in_layout(mix_in_w[0]).astype(BF16)
    p = norm_matmul(h, norm_mix[0], w_in, tm=tm_in, tn=tn).reshape(B, S, MIX_COLS)
    o_gla, o_rwkv = mixer_layer0(
        p, gla_gate_w2[0], gla_gate_b[0], gla_norm[0], rwkv_mu[0], rwkv_w0[0], rwkv_w2[0],
        rwkv_a0[0], rwkv_a2[0], rwkv_g2[0], rwkv_k_k[0], rwkv_k_a[0], rwkv_r_k[0], rwkv_ln_w[0],
        rwkv_ln_b[0], gla_block=512, rwkv_block=512)
    w_out = mix_out_w[0].astype(BF16)
    h = proj_residual(h, [(o_gla.reshape(M, GLA_WIDTH), w_out[:GLA_WIDTH]),
                          (o_rwkv.reshape(M, RWKV_WIDTH), w_out[GLA_WIDTH:])], tm=tm, tn=D)
    h = ffn_residual(h, norm_ffn[0], ffn_gate_w, ffn_up_w, ffn_down_w, norm_final, layer=0,
                     tm=tm_ffn, tf=tf, final_norm=False)

    qkv_h = norm_matmul(h, norm_mix[1], attn_qkv_w[0].astype(BF16), tm=tm_in, tn=2 * tn,
                        group=MOBA_HD)
    o_attn = moba_attention(qkv_h, B, S, heads=2)
    h = proj_residual(h, [(o_attn.reshape(M, D), attn_out_w[0].astype(BF16))], tm=tm, tn=D)
    h = ffn_residual(h, norm_ffn[1], ffn_gate_w, ffn_up_w, ffn_down_w, norm_final, layer=1,
                     tm=tm_ffn, tf=tf, final_norm=True)
    return h.reshape(B, S, D)
```

```python
import functools

import jax
import jax.numpy as jnp
from jax import lax
from jax.experimental import pallas as pl
from jax.experimental.pallas import tpu as pltpu

F32 = jnp.float32
BF16 = jnp.bfloat16
HI = lax.Precision.HIGHEST

D_MODEL = 2048
RMS_EPS = 1e-6

GLA_HEADS = 4
GLA_DK = 128
GLA_DV = 256
GLA_KEY_WIDTH = GLA_HEADS * GLA_DK
GLA_WIDTH = GLA_HEADS * GLA_DV
GLA_GATE_RANK = 16
GLA_GATE_TAU = 16.0
GLA_CHUNK = 64

RWKV_WIDTH = 1024
RWKV_HEAD = 64
RWKV_PAIR = 2 * RWKV_HEAD
RWKV_PAIRS = RWKV_WIDTH // RWKV_PAIR
RWKV_DECAY_LORA = 64
RWKV_AAA_LORA = 64
RWKV_GATE_LORA = 160
RWKV_LN_EPS = RWKV_HEAD * 1e-5
RWKV_CHUNK = 64

MOBA_HEADS = 16
MOBA_HD = 128
MOBA_BLOCK = 256
MOBA_TOPK = 3
NEG_INF = -1e30

FFN_HIDDEN = 5632

LANES = 128

COL_GLA_Q = 0
COL_GLA_K = 512
COL_GLA_V = 1024
COL_GLA_OG = 2048
COL_RWKV_R = 3072
COL_RWKV_K = 4096
COL_RWKV_V = 5120
COL_LOW_GATE = 6144
COL_LOW_WA = 6272
COL_LOW_G = 6400
MIX_COLS = 6656

VMEM_LIMIT = 56 * 1024 * 1024


def _mm(a, b):
    return jnp.dot(a.astype(BF16), b.astype(BF16), preferred_element_type=F32)


def _mm_nt(a, b):
    return lax.dot_general(a.astype(BF16), b.astype(BF16), (((1,), (1,)), ((), ())),
                           preferred_element_type=F32)


def _mm_hi(a, b):
    return jnp.dot(a, b, preferred_element_type=F32, precision=HI)


def _mm_nt_hi(a, b):
    return lax.dot_general(a, b, (((1,), (1,)), ((), ())), preferred_element_type=F32,
                           precision=HI)


def _dot(a, b):
    return jnp.dot(a, b, preferred_element_type=F32)


def _dot_tn(a, b):
    return lax.dot_general(a, b, (((0,), (0,)), ((), ())), preferred_element_type=F32)


def _split(x, parts):
    out = []
    for _ in range(parts - 1):
        hi = x.astype(BF16)
        out.append(hi)
        x = x - hi.astype(F32)
    out.append(x.astype(BF16))
    return out


def _mm_split(a, b, *, a_parts=1, b_parts=1):
    a_p = _split(a, a_parts) if a_parts > 1 else [a.astype(BF16)]
    b_p = _split(b, b_parts) if b_parts > 1 else [b.astype(BF16)]
    acc = None
    for i, ai in enumerate(a_p):
        for j, bj in enumerate(b_p):
            if i + j < max(a_parts, b_parts):
                t = _dot(ai, bj)
                acc = t if acc is None else acc + t
    return acc


def _sigmoid(x):
    return 1.0 / (1.0 + jnp.exp(-x))


def _softplus(x):
    return jnp.maximum(x, 0.0) + jnp.log1p(jnp.exp(-jnp.abs(x)))


def _iota2(shape, axis):
    return lax.broadcasted_iota(jnp.int32, shape, axis)


def _norm_matmul_kernel(x_ref, g_ref, w_ref, o_ref, xn_ref, *, group):
    @pl.when(pl.program_id(1) == 0)
    def _():
        x = x_ref[...]
        ms = jnp.mean(x * x, axis=-1, keepdims=True)
        xn_ref[...] = (x * lax.rsqrt(ms + RMS_EPS) * g_ref[...]).astype(BF16)

    res = jnp.dot(xn_ref[...], w_ref[...], preferred_element_type=F32).astype(o_ref.dtype)
    if group is None:
        o_ref[...] = res
    else:
        for c in range(res.shape[1] // group):
            o_ref[c] = res[:, c * group:(c + 1) * group]


def norm_matmul(x, g, w, *, tm, tn, out_dtype=F32, group=None):
    M, D = x.shape
    N = w.shape[1]
    if group is None:
        out_specs = pl.BlockSpec((tm, tn), lambda i, j: (i, j))
        out_shape = jax.ShapeDtypeStruct((M, N), out_dtype)
    else:
        out_specs = pl.BlockSpec((tn // group, tm, group), lambda i, j: (j, i, 0))
        out_shape = jax.ShapeDtypeStruct((N // group, M, group), out_dtype)
    return pl.pallas_call(
        functools.partial(_norm_matmul_kernel, group=group),
        grid=(M // tm, N // tn),
        in_specs=[
            pl.BlockSpec((tm, D), lambda i, j: (i, 0)),
            pl.BlockSpec((1, D), lambda i, j: (0, 0)),
            pl.BlockSpec((D, tn), lambda i, j: (0, j)),
        ],
        out_specs=out_specs,
        out_shape=out_shape,
        scratch_shapes=[pltpu.VMEM((tm, D), BF16)],
        compiler_params=pltpu.CompilerParams(
            dimension_semantics=("parallel", "arbitrary"), vmem_limit_bytes=VMEM_LIMIT),
    )(x, g.reshape(1, D), w)


def _proj_res_kernel(*refs, n_in):
    res_ref = refs[0]
    o_ref = refs[1 + 2 * n_in]
    acc = res_ref[...]
    for i in range(n_in):
        acc = acc + jnp.dot(refs[1 + 2 * i][...], refs[2 + 2 * i][...],
                            preferred_element_type=F32)
    o_ref[...] = acc


def proj_residual(res, pairs, *, tm, tn):
    M, N = res.shape
    in_specs = [pl.BlockSpec((tm, tn), lambda i, j: (i, j))]
    args = [res]
    for a, w in pairs:
        K = a.shape[1]
        in_specs.append(pl.BlockSpec((tm, K), lambda i, j: (i, 0)))
        in_specs.append(pl.BlockSpec((K, tn), lambda i, j: (0, j)))
        args += [a, w]
    return pl.pallas_call(
        functools.partial(_proj_res_kernel, n_in=len(pairs)),
        grid=(M // tm, N // tn),
        in_specs=in_specs,
        out_specs=pl.BlockSpec((tm, tn), lambda i, j: (i, j)),
        out_shape=jax.ShapeDtypeStruct((M, N), F32),
        compiler_params=pltpu.CompilerParams(
            dimension_semantics=("parallel", "arbitrary"), vmem_limit_bytes=VMEM_LIMIT),
    )(*args)


def _ffn_kernel(x_ref, g_ref, wg_ref, wu_ref, wd_ref, gf_ref, o_ref, xn_ref, *, final_norm):
    j = pl.program_id(1)

    @pl.when(j == 0)
    def _():
        x = x_ref[...]
        ms = jnp.mean(x * x, axis=-1, keepdims=True)
        xn_ref[...] = (x * lax.rsqrt(ms + RMS_EPS) * g_ref[...]).astype(BF16)
        o_ref[...] = x

    xn = xn_ref[...]
    gate = jnp.dot(xn, wg_ref[...].astype(BF16), preferred_element_type=F32)
    up = jnp.dot(xn, wu_ref[...].astype(BF16), preferred_element_type=F32)
    act = (gate * _sigmoid(gate) * up).astype(BF16)
    o_ref[...] += jnp.dot(act, wd_ref[...].astype(BF16), preferred_element_type=F32)

    if final_norm:
        @pl.when(j == pl.num_programs(1) - 1)
        def _():
            h = o_ref[...]
            ms = jnp.mean(h * h, axis=-1, keepdims=True)
            o_ref[...] = h * lax.rsqrt(ms + RMS_EPS) * gf_ref[...]


def ffn_residual(x, g, wg, wu, wd, g_final, *, layer, tm, tf, final_norm):
    M, D = x.shape
    F = wg.shape[2]
    return pl.pallas_call(
        functools.partial(_ffn_kernel, final_norm=final_norm),
        grid=(M // tm, F // tf),
        in_specs=[
            pl.BlockSpec((tm, D), lambda i, j: (i, 0)),
            pl.BlockSpec((1, D), lambda i, j: (0, 0)),
            pl.BlockSpec((None, D, tf), lambda i, j: (layer, 0, j)),
            pl.BlockSpec((None, D, tf), lambda i, j: (layer, 0, j)),
            pl.BlockSpec((None, tf, D), lambda i, j: (layer, j, 0)),
            pl.BlockSpec((1, D), lambda i, j: (0, 0)),
        ],
        out_specs=pl.BlockSpec((tm, D), lambda i, j: (i, 0)),
        out_shape=jax.ShapeDtypeStruct((M, D), F32),
        scratch_shapes=[pltpu.VMEM((tm, D), BF16)],
        compiler_params=pltpu.CompilerParams(
            dimension_semantics=("parallel", "arbitrary"), vmem_limit_bytes=VMEM_LIMIT),
    )(x, g.reshape(1, D), wg, wu, wd, g_final.reshape(1, D))


def _gla_kernel(q_ref, k_ref, v_ref, og_ref, gl_ref, w2_ref, b_ref, gn_ref, o_ref, st_ref,
                *, block, chunk):
    n_heads = q_ref.shape[2] // GLA_DK

    @pl.when(pl.program_id(2) == 0)
    def _():
        st_ref[...] = jnp.zeros_like(st_ref)

    logit = _mm_split(gl_ref[0], w2_ref[...], a_parts=2, b_parts=2) + b_ref[...]
    log_a = -_softplus(-logit) * (1.0 / GLA_GATE_TAU)
    tri = _iota2((chunk, chunk), 0) >= _iota2((chunk, chunk), 1)
    tri_b = tri.astype(BF16)
    tri_b3 = jnp.concatenate([tri_b, tri_b, tri_b], axis=1)
    scale = GLA_DK ** -0.5
    gn = gn_ref[...]
    n_chunks = block // chunk

    streams = [(h, c) for h in range(n_heads) for c in range(n_chunks)]

    def rows(c):
        return pl.ds(c * chunk, chunk)

    def k_lanes(h):
        return slice(h * GLA_DK, (h + 1) * GLA_DK)

    def v_lanes(h):
        return slice(h * GLA_DV, (h + 1) * GLA_DV)

    la3 = [jnp.concatenate(_split(log_a[c * chunk:(c + 1) * chunk, k_lanes(h)], 3), axis=0)
           for h, c in streams]
    cum = [_dot(tri_b3, x) for x in la3]
    q_dec = [(q_ref[0, rows(c), k_lanes(h)] * scale * jnp.exp(b)).astype(BF16)
             for (h, c), b in zip(streams, cum)]
    k_inv = [(k_ref[0, rows(c), k_lanes(h)] * jnp.exp(-b)).astype(BF16)
             for (h, c), b in zip(streams, cum)]
    k_dec = [(k_ref[0, rows(c), k_lanes(h)] * jnp.exp(b[chunk - 1:chunk] - b)).astype(BF16)
             for (h, c), b in zip(streams, cum)]
    v_b = [v_ref[0, rows(c), v_lanes(h)].astype(BF16) for h, c in streams]
    v_t = [v_ref[0, rows(c), v_lanes(h)].T.astype(BF16) for h, c in streams]
    att = [jnp.where(tri, _mm_nt(x, y), 0.0).astype(BF16) for x, y in zip(q_dec, k_inv)]
    o_intra = [_dot(x, y) for x, y in zip(att, v_b)]
    gain = [_dot(x, y) for x, y in zip(v_t, k_dec)]
    pre = [dict(q_dec=q_dec[i], o_intra=o_intra[i], gain=gain[i],
                decay=jnp.exp(cum[i][chunk - 1:chunk])) for i in range(len(streams))]

    st = [st_ref[h] for h in range(n_heads)]
    outs = [[] for _ in range(n_heads)]
    for c in range(n_chunks):
        for h in range(n_heads):
            p = pre[h * n_chunks + c]
            outs[h].append(p["o_intra"] + _mm_nt(p["q_dec"], st[h]))
            st[h] = st[h] * p["decay"] + p["gain"]
    for h in range(n_heads):
        st_ref[h] = st[h]

    for h in range(n_heads):
        lv = slice(h * GLA_DV, (h + 1) * GLA_DV)
        o = jnp.concatenate(outs[h], axis=0)
        ms = jnp.mean(o * o, axis=-1, keepdims=True)
        og = og_ref[0, :, lv]
        o_ref[0, :, lv] = (o * lax.rsqrt(ms + RMS_EPS) * gn * (og * _sigmoid(og))).astype(o_ref.dtype)


def gla_mixer(p3, gate_w2p, gate_b, gla_norm, *, block, heads):
    B, S, _ = p3.shape
    wk, wv = heads * GLA_DK, heads * GLA_DV
    kq, kk_, kv, kog = COL_GLA_Q // wk, COL_GLA_K // wk, COL_GLA_V // wv, COL_GLA_OG // wv
    klow = COL_LOW_GATE // LANES
    return pl.pallas_call(
        functools.partial(_gla_kernel, block=block, chunk=GLA_CHUNK),
        grid=(B, GLA_HEADS // heads, S // block),
        in_specs=[
            pl.BlockSpec((1, block, wk), lambda b, h, s: (b, s, kq + h)),
            pl.BlockSpec((1, block, wk), lambda b, h, s: (b, s, kk_ + h)),
            pl.BlockSpec((1, block, wv), lambda b, h, s: (b, s, kv + h)),
            pl.BlockSpec((1, block, wv), lambda b, h, s: (b, s, kog + h)),
            pl.BlockSpec((1, block, LANES), lambda b, h, s: (b, s, klow)),
            pl.BlockSpec((LANES, wk), lambda b, h, s: (0, h)),
            pl.BlockSpec((1, wk), lambda b, h, s: (0, h)),
            pl.BlockSpec((1, GLA_DV), lambda b, h, s: (0, 0)),
        ],
        out_specs=pl.BlockSpec((1, block, wv), lambda b, h, s: (b, s, h)),
        out_shape=jax.ShapeDtypeStruct((B, S, GLA_WIDTH), BF16),
        scratch_shapes=[pltpu.VMEM((heads, GLA_DV, GLA_DK), F32)],
        compiler_params=pltpu.CompilerParams(
            dimension_semantics=("parallel", "parallel", "arbitrary"),
            vmem_limit_bytes=VMEM_LIMIT),
    )(p3, p3, p3, p3, p3, gate_w2p, gate_b.reshape(1, -1), gla_norm.reshape(1, -1))


def _unit_lower_inverses(mats, rowi, coli):
    def same_block(s):
        sh = s.bit_length() - 1
        return (rowi >> sh) == (coli >> sh)

    eye = (rowi == coli).astype(F32)
    a8 = [jnp.where(same_block(8), a, 0.0).astype(BF16) for a in mats]
    inv = [eye + a.astype(F32) for a in a8]
    p = [_dot(a, a) for a in a8]
    inv = [x + _mm(x, y) for x, y in zip(inv, p)]
    p = [_mm(y, y) for y in p]
    inv = [x + _mm(x, y) for x, y in zip(inv, p)]
    s = 8
    while s < RWKV_CHUNK:
        off = same_block(2 * s) & jnp.logical_not(same_block(s))
        e = [jnp.where(off, a, 0.0).astype(BF16) for a in mats]
        inv_b = [x.astype(BF16) for x in inv]
        t = [_dot(x, y).astype(BF16) for x, y in zip(inv_b, e)]
        inv = [x + _dot(y, z) for x, y, z in zip(inv, t, inv_b)]
        s *= 2
    return inv


def _rwkv_kernel(r_ref, k_ref, v_ref, wa_ref, g_ref, vec_ref, muwa_ref, mug_ref, w2_ref, a2_ref,
                 g2_ref, o_ref, st_ref, prev_ref, prevwa_ref, prevg_ref, *, block, chunk):
    T = chunk
    n_pairs = r_ref.shape[2] // RWKV_PAIR
    first = pl.program_id(2) == 0

    @pl.when(first)
    def _():
        st_ref[...] = jnp.zeros_like(st_ref)
        prev_ref[...] = jnp.zeros_like(prev_ref)
        prevwa_ref[...] = jnp.zeros_like(prevwa_ref)
        prevg_ref[...] = jnp.zeros_like(prevg_ref)

    row0 = _iota2((block, 1), 0) == 0

    def lerp(x, prev_row, mu):
        shifted = jnp.where(row0, prev_row, pltpu.roll(x, 1, axis=0))
        return x + (shifted - x) * mu

    vec = vec_ref[...]
    w0, a0, k_k, k_a, r_k, ln_w, ln_b = (vec[i:i + 1] for i in range(7))
    mu_r, mu_k, mu_v = (vec[i:i + 1] for i in range(7, 10))

    r_raw, k_raw, v_raw, wa_raw, g_raw = r_ref[0], k_ref[0], v_ref[0], wa_ref[0], g_ref[0]
    prev = prev_ref[...]
    r_all = lerp(r_raw, prev[0:1], mu_r)
    k_all = lerp(k_raw, prev[1:2], mu_k)
    v_all = lerp(v_raw, prev[2:3], mu_v)
    wa = lerp(wa_raw, prevwa_ref[0:1], muwa_ref[...])
    g_low = lerp(g_raw, prevg_ref[0:1], mug_ref[...])
    prev_ref[0:1] = r_raw[block - 1:block]
    prev_ref[1:2] = k_raw[block - 1:block]
    prev_ref[2:3] = v_raw[block - 1:block]
    prevwa_ref[0:1] = wa_raw[block - 1:block]
    prevg_ref[0:1] = g_raw[block - 1:block]

    z = w0 + _mm_split(jnp.tanh(wa), w2_ref[...], a_parts=2, b_parts=2)
    logw_all = -jnp.exp(-_softplus(-z) - 0.5)
    lr_all = _sigmoid(a0 + _mm_split(wa, a2_ref[...], a_parts=2, b_parts=2))
    gate_all = _mm_split(_sigmoid(g_low), g2_ref[...], a_parts=2, b_parts=2)

    lane = _iota2((1, RWKV_PAIR), 1)
    m0 = (lane < RWKV_HEAD).astype(F32)
    m1 = 1.0 - m0
    rowi = _iota2((2 * T, 2 * T), 0)
    coli = _iota2((2 * T, 2 * T), 1)
    strict = rowi > coli
    incl = rowi >= coli
    head_ones = ((rowi < RWKV_HEAD) == (coli < RWKV_HEAD)).astype(BF16)
    head_ones2 = jnp.concatenate([head_ones, head_ones], axis=0)
    tri_t = (_iota2((T, T), 0) >= _iota2((T, T), 1)).astype(BF16)
    tri_t3 = jnp.concatenate([tri_t, tri_t, tri_t], axis=1)
    ones_t3 = jnp.ones((3 * T, RWKV_PAIR), BF16)

    def hsum(x):
        return _dot(jnp.concatenate(_split(x, 2), axis=1), head_ones2)

    def stack(x):
        return jnp.concatenate([x * m0, x * m1], axis=0)

    n_chunks = block // T

    kk_p, k2_p, b_p, bonus = [], [], [], []
    for pair in range(n_pairs):
        ln = slice(pair * RWKV_PAIR, (pair + 1) * RWKV_PAIR)
        kkp = k_all[:, ln] * k_k[:, ln]
        kk_p.append(kkp / jnp.maximum(jnp.sqrt(hsum(kkp * kkp)), 1e-12))
        k2_p.append(k_all[:, ln] * (1.0 + (lr_all[:, ln] - 1.0) * k_a[:, ln]))
        b_p.append(kk_p[pair] * lr_all[:, ln])
        bonus.append(hsum(r_all[:, ln] * k2_p[pair] * r_k[:, ln]) * v_all[:, ln])

    streams = [(pair, c) for pair in range(n_pairs) for c in range(n_chunks)]

    def tile(x, pair, c):
        return x[c * T:(c + 1) * T, pair * RWKV_PAIR:(pair + 1) * RWKV_PAIR]

    lw = [tile(logw_all, p, c) for p, c in streams]
    lw3 = [jnp.concatenate(_split(x, 3), axis=0) for x in lw]
    cum = [_dot(tri_t3, x) for x in lw3]
    decay = [jnp.exp(_dot_tn(x, ones_t3)) for x in lw3]
    w_inv = [jnp.exp(-x) for x in cum]
    w_tail = [jnp.exp(x[T - 1:T] - x) for x in cum]
    a_s = [stack(-kk_p[p][c * T:(c + 1) * T] * jnp.exp(x - y)).astype(BF16)
           for (p, c), x, y in zip(streams, cum, lw)]
    r_s = [stack(tile(r_all, p, c) * jnp.exp(x)).astype(BF16) for (p, c), x in zip(streams, cum)]
    b_s = [stack(b_p[p][c * T:(c + 1) * T] * x).astype(BF16) for (p, c), x in zip(streams, w_inv)]
    k_s = [stack(k2_p[p][c * T:(c + 1) * T] * x).astype(BF16) for (p, c), x in zip(streams, w_inv)]
    v_s = [stack(tile(v_all, p, c)).astype(BF16) for p, c in streams]
    b_tail = [stack(b_p[p][c * T:(c + 1) * T] * x).astype(BF16)
              for (p, c), x in zip(streams, w_tail)]
    k_tail = [stack(k2_p[p][c * T:(c + 1) * T] * x).astype(BF16)
              for (p, c), x in zip(streams, w_tail)]
    prod = [_mm_nt(jnp.concatenate([a, r], axis=0), jnp.concatenate([b, k], axis=0))
            for a, r, b, k in zip(a_s, r_s, b_s, k_s)]
    pre = [dict(a_s=a_s[i], r_s=r_s[i], v_s=v_s[i], decay=decay[i], b_tail=b_tail[i],
                k_tail=k_tail[i],
                a_ab=jnp.where(strict, prod[i][:2 * T, :2 * T], 0.0),
                a_ak=jnp.where(strict, prod[i][:2 * T, 2 * T:], 0.0).astype(BF16),
                a_rb=jnp.where(incl, prod[i][2 * T:, :2 * T], 0.0).astype(BF16),
                a_rk=jnp.where(incl, prod[i][2 * T:, 2 * T:], 0.0).astype(BF16))
           for i in range(len(streams))]

    invs = [x.astype(BF16) for x in _unit_lower_inverses([p["a_ab"] for p in pre], rowi, coli)]
    akv = [_dot(p["a_ak"], p["v_s"]).astype(BF16) for p in pre]
    x = [_dot(i, jnp.concatenate([p["a_s"], y], axis=1)).astype(BF16)
         for i, p, y in zip(invs, pre, akv)]
    bx = [_dot_tn(p["b_tail"], y) for p, y in zip(pre, x)]
    rx = [_dot(p["a_rb"], y) for p, y in zip(pre, x)]
    trans = [y[:, :RWKV_PAIR].astype(BF16) for y in bx]
    gain = [y[:, RWKV_PAIR:] + _dot_tn(p["k_tail"], p["v_s"]) for p, y in zip(pre, bx)]
    read = [(p["r_s"].astype(F32) + y[:, :RWKV_PAIR]).astype(BF16) for p, y in zip(pre, rx)]
    y_free = [y[:, RWKV_PAIR:] + _dot(p["a_rk"], p["v_s"]) for p, y in zip(pre, rx)]

    st = [st_ref[pair] for pair in range(n_pairs)]
    ys = [[] for _ in range(n_pairs)]
    for c in range(n_chunks):
        for pair in range(n_pairs):
            i = pair * n_chunks + c
            st_b = st[pair].astype(BF16)
            y_s = _dot(read[i], st_b) + y_free[i]
            ys[pair].append(y_s[:T] + y_s[T:])
            st[pair] = st[pair] * pre[i]["decay"] + _dot(trans[i], st_b) + gain[i]
    for pair in range(n_pairs):
        st_ref[pair] = st[pair]

    for pair in range(n_pairs):
        ln = slice(pair * RWKV_PAIR, (pair + 1) * RWKV_PAIR)
        y = jnp.concatenate(ys[pair], axis=0)
        mu = hsum(y) * (1.0 / RWKV_HEAD)
        d = y - mu
        var = hsum(d * d) * (1.0 / RWKV_HEAD)
        yn = d * lax.rsqrt(var + RWKV_LN_EPS) * ln_w[:, ln] + ln_b[:, ln]
        o_ref[0, :, ln] = ((yn + bonus[pair]) * gate_all[:, ln]).astype(o_ref.dtype)


def rwkv_mixer(p3, vecs, mu_wa, mu_g, lora_up, *, block, pairs):
    B, S, _ = p3.shape
    W = pairs * RWKV_PAIR
    kr, kk_, kv = COL_RWKV_R // W, COL_RWKV_K // W, COL_RWKV_V // W
    kwa = COL_LOW_WA // LANES
    kg = COL_LOW_G // (2 * LANES)
    return pl.pallas_call(
        functools.partial(_rwkv_kernel, block=block, chunk=RWKV_CHUNK),
        grid=(B, RWKV_PAIRS // pairs, S // block),
        in_specs=[
            pl.BlockSpec((1, block, W), lambda b, j, s: (b, s, kr + j)),
            pl.BlockSpec((1, block, W), lambda b, j, s: (b, s, kk_ + j)),
            pl.BlockSpec((1, block, W), lambda b, j, s: (b, s, kv + j)),
            pl.BlockSpec((1, block, LANES), lambda b, j, s: (b, s, kwa)),
            pl.BlockSpec((1, block, 2 * LANES), lambda b, j, s: (b, s, kg)),
            pl.BlockSpec((16, W), lambda b, j, s: (0, j)),
            pl.BlockSpec((1, LANES), lambda b, j, s: (0, 0)),
            pl.BlockSpec((1, 2 * LANES), lambda b, j, s: (0, 0)),
            pl.BlockSpec((LANES, W), lambda b, j, s: (0, j)),
            pl.BlockSpec((LANES, W), lambda b, j, s: (1, j)),
            pl.BlockSpec((2 * LANES, W), lambda b, j, s: (1, j)),
        ],
        out_specs=pl.BlockSpec((1, block, W), lambda b, j, s: (b, s, j)),
        out_shape=jax.ShapeDtypeStruct((B, S, RWKV_WIDTH), BF16),
        scratch_shapes=[
            pltpu.VMEM((pairs, RWKV_PAIR, RWKV_PAIR), F32),
            pltpu.VMEM((8, W), F32),
            pltpu.VMEM((8, LANES), F32),
            pltpu.VMEM((8, 2 * LANES), F32),
        ],
        compiler_params=pltpu.CompilerParams(
            dimension_semantics=("parallel", "parallel", "arbitrary"),
            vmem_limit_bytes=VMEM_LIMIT),
    )(p3, p3, p3, p3, p3, vecs, mu_wa, mu_g, lora_up, lora_up, lora_up)


def _moba_kernel(q_ref, k_ref, v_ref, slope_ref, o_ref, kb_ref, vt_ref, kmean_ref,
                 m_ref, l_ref, acc_ref, s_ref, p_ref, *, nb):
    BS, D = MOBA_BLOCK, MOBA_HD
    QT = 2 * BS
    heads = q_ref.shape[0]
    g = pl.program_id(2)
    log2e = 1.4426950408889634
    c1 = MOBA_HD ** -0.5 * log2e
    slope2 = [slope_ref[h] * log2e for h in range(heads)]
    s_parts = [_split(x, 3) for x in slope2]

    @pl.when(g == 0)
    def _():
        kmean_ref[...] = jnp.zeros_like(kmean_ref)
        lane = _iota2((QT, LANES), 1)
        kpos = _iota2((QT, LANES), 0)
        kpos_lo = (kpos & (BS - 1)).astype(F32)
        kpos_hi = (kpos & BS).astype(F32)
        for h in range(heads):
            sk = [x[:, :LANES].astype(F32) for x in s_parts[h]]
            piece = jnp.where((lane == 0) | (lane == 3), sk[0],
                              jnp.where((lane == 1) | (lane == 4), sk[1], sk[2]))
            feat = jnp.where(lane < 6, piece,
                             jnp.where(lane < 9, kpos_lo, jnp.where(lane < 12, kpos_hi, 0.0)))
            feat = feat.astype(BF16)
            for j in range(nb):
                half = slice((j % 2) * BS, (j % 2 + 1) * BS)
                kj = k_ref[h, pl.ds(j * BS, BS), :]
                kmean_ref[h, j:j + 1, :] = jnp.mean(kj, axis=0, keepdims=True)
                kb_ref[h, j // 2, half, :D] = kj.astype(BF16)
                vt_ref[h, j // 2, :, half] = v_ref[h, pl.ds(j * BS, BS), :].T.astype(BF16)
            for t in range(nb // 2):
                kb_ref[h, t, :, D:] = feat

    qpos = _iota2((1, QT), 1)
    blk_q = 2 * g + (qpos >= BS).astype(jnp.int32)
    nbp = kmean_ref.shape[1]
    blk = _iota2((nbp, QT), 0)
    arow = _iota2((LANES, QT), 0)
    qpos_a = _iota2((LANES, QT), 1)
    qpos_lo = (qpos_a & (BS - 1)).astype(F32)
    qpos_hi = (qpos_a & BS).astype(F32)
    kp = _iota2((QT, QT), 0)
    qp = _iota2((QT, QT), 1)
    own = (kp <= qp) & ((kp >= BS) == (qp >= BS))
    first_block = (kp <= qp) & (kp < BS)

    def select(h):
        q_t = q_ref[h].T
        gate = _mm_split(kmean_ref[h], q_t, a_parts=3, b_parts=3)
        gate = jnp.where(blk < blk_q, gate, NEG_INF)
        sel = []
        for r in range(MOBA_TOPK):
            mx = jnp.max(gate, axis=0, keepdims=True)
            idx = jnp.min(jnp.where(gate == mx, blk, nbp), axis=0, keepdims=True)
            sel.append(jnp.where(r < blk_q, idx, -1))
            gate = jnp.where(blk == idx, -jnp.inf, gate)
        sq = [x.astype(F32) for x in s_parts[h]]
        piece = jnp.where((arow == 6) | (arow == 9), sq[0],
                          jnp.where((arow == 7) | (arow == 10), sq[1], sq[2]))
        aug = jnp.where(arow < 3, -qpos_lo,
                        jnp.where(arow < 6, -qpos_hi, jnp.where(arow < 12, piece, 0.0)))
        q_aug = jnp.concatenate([(q_t * c1).astype(BF16), aug.astype(BF16)], axis=0)
        return sel, q_aug

    sel, q_aug = zip(*[select(h) for h in range(heads)])

    def picked(h, j):
        return (sel[h][0] == j) | (sel[h][1] == j) | (sel[h][2] == j)

    for h in range(heads):
        allowed = own | (first_block & picked(h, 2 * g))
        s_own = jnp.where(allowed, _dot(kb_ref[h, g], q_aug[h]), NEG_INF)
        m0 = jnp.max(s_own, axis=0, keepdims=True)
        p0 = jnp.exp2(s_own - m0)
        m_ref[h] = m0
        l_ref[h] = jnp.sum(p0, axis=0, keepdims=True)
        p_ref[h] = p0.astype(BF16)
        acc_ref[h] = jnp.zeros((D, QT), F32)
        s_ref[h, 0] = _dot(kb_ref[h, 0], q_aug[h])

    def past_pair(h, t, prev):
        s_cur = s_ref[h, t % 2]
        pv = _dot(vt_ref[h, prev], p_ref[h])
        s_ref[h, (t + 1) % 2] = _dot(kb_ref[h, jnp.minimum(t + 1, nb // 2 - 1)], q_aug[h])
        s_a = jnp.where(picked(h, 2 * t), s_cur[:BS], NEG_INF)
        s_b = jnp.where(picked(h, 2 * t + 1), s_cur[BS:], NEG_INF)
        off = slope2[h] * ((g - t) * QT).astype(F32)
        m_old = m_ref[h]
        mx = jnp.maximum(jnp.max(s_a, axis=0, keepdims=True), jnp.max(s_b, axis=0, keepdims=True))
        m_new = jnp.maximum(m_old, mx - off)
        alpha = jnp.exp2(m_old - m_new)
        shift = m_new + off
        p_a = jnp.exp2(s_a - shift)
        p_b = jnp.exp2(s_b - shift)
        l_ref[h] = (alpha * l_ref[h] + jnp.sum(p_a, axis=0, keepdims=True)
                    + jnp.sum(p_b, axis=0, keepdims=True))
        acc_ref[h] = alpha * (acc_ref[h] + pv)
        p_ref[h, :BS, :] = p_a.astype(BF16)
        p_ref[h, BS:, :] = p_b.astype(BF16)
        m_ref[h] = m_new
        return t

    last = [lax.fori_loop(0, g, functools.partial(past_pair, h), g) for h in range(heads)]
    for h in range(heads):
        acc = acc_ref[h] + _dot(vt_ref[h, last[h]], p_ref[h])
        o_ref[0, :, h * D:(h + 1) * D] = (acc / l_ref[h]).T.astype(o_ref.dtype)


def moba_attention(qkv_h, B, S, *, heads):
    H, D, BS = MOBA_HEADS, MOBA_HD, MOBA_BLOCK
    QT = 2 * BS
    nb = S // BS
    assert nb % 2 == 0, "key blocks are stored in pairs"
    nt = nb // 2
    nbp = -(-nb // 8) * 8
    hg = H // heads
    slopes = jnp.exp2(-8.0 * jnp.arange(1, H + 1, dtype=F32) / H)
    slopes = jnp.broadcast_to(slopes[:, None, None], (H, 1, QT))
    return pl.pallas_call(
        functools.partial(_moba_kernel, nb=nb),
        grid=(B, hg, nt),
        in_specs=[
            pl.BlockSpec((heads, QT, D), lambda b, h, g: (h, b * nt + g, 0)),
            pl.BlockSpec((heads, S, D), lambda b, h, g: (hg + h, b, 0)),
            pl.BlockSpec((heads, S, D), lambda b, h, g: (2 * hg + h, b, 0)),
            pl.BlockSpec((heads, 1, QT), lambda b, h, g: (h, 0, 0)),
        ],
        out_specs=pl.BlockSpec((1, QT, heads * D), lambda b, h, g: (b, g, h)),
        out_shape=jax.ShapeDtypeStruct((B, S, H * D), BF16),
        scratch_shapes=[
            pltpu.VMEM((heads, nt, QT, D + LANES), BF16),
            pltpu.VMEM((heads, nt, D, QT), BF16),
            pltpu.VMEM((heads, nbp, D), F32),
            pltpu.VMEM((heads, 1, QT), F32),
            pltpu.VMEM((heads, 1, QT), F32),
            pltpu.VMEM((heads, D, QT), F32),
            pltpu.VMEM((heads, 2, QT, QT), F32),
            pltpu.VMEM((heads, QT, QT), BF16),
        ],
        compiler_params=pltpu.CompilerParams(
            dimension_semantics=("parallel", "parallel", "arbitrary"),
            vmem_limit_bytes=VMEM_LIMIT),
    )(qkv_h, qkv_h, qkv_h, slopes)


def _pad_cols(w, n):
    return jnp.pad(w, ((0, 0), (0, n - w.shape[1])))


def _pad_rows(w, before, total):
    return jnp.pad(w, ((before, total - before - w.shape[0]), (0, 0)))


def _mix_in_layout_kernel(w_ref, o_ref):
    rows = w_ref.shape[0]
    g_end = 2 * GLA_KEY_WIDTH + 2 * GLA_WIDTH
    r0 = g_end + GLA_GATE_RANK
    l0 = r0 + 3 * RWKV_WIDTH
    g0 = l0 + RWKV_DECAY_LORA + RWKV_AAA_LORA
    w = w_ref[...]

    def zeros(n):
        return jnp.zeros((rows, n), F32)

    o_ref[:, :g_end] = w[:, :g_end].astype(BF16)
    o_ref[:, COL_RWKV_R:COL_LOW_GATE] = w[:, r0:l0].astype(BF16)
    o_ref[:, COL_LOW_GATE:COL_LOW_WA] = jnp.concatenate(
        [w[:, g_end:r0], zeros(LANES - GLA_GATE_RANK)], axis=1).astype(BF16)
    o_ref[:, COL_LOW_WA:COL_LOW_G] = w[:, l0:g0].astype(BF16)
    o_ref[:, COL_LOW_G:] = jnp.concatenate(
        [w[:, g0:], zeros(2 * LANES - RWKV_GATE_LORA)], axis=1).astype(BF16)


def mix_in_layout_bf16(w, *, layer, rows):
    _, K, N = w.shape
    return pl.pallas_call(
        _mix_in_layout_kernel,
        grid=(K // rows,),
        in_specs=[pl.BlockSpec((None, rows, N), lambda i: (layer, i, 0))],
        out_specs=pl.BlockSpec((rows, MIX_COLS), lambda i: (i, 0)),
        out_shape=jax.ShapeDtypeStruct((K, MIX_COLS), BF16),
        compiler_params=pltpu.CompilerParams(
            dimension_semantics=("parallel",), vmem_limit_bytes=VMEM_LIMIT),
    )(w)


def _mix_in_layout(w):
    gk, gw = GLA_KEY_WIDTH, GLA_WIDTH
    g_end = 2 * gk + 2 * gw
    gla_main = w[:, :g_end]
    gla_gate = w[:, g_end:g_end + GLA_GATE_RANK]
    r0 = g_end + GLA_GATE_RANK
    rw_main = w[:, r0:r0 + 3 * RWKV_WIDTH]
    l0 = r0 + 3 * RWKV_WIDTH
    wa = w[:, l0:l0 + RWKV_DECAY_LORA + RWKV_AAA_LORA]
    g0 = l0 + RWKV_DECAY_LORA + RWKV_AAA_LORA
    gl = w[:, g0:g0 + RWKV_GATE_LORA]
    return jnp.concatenate(
        [gla_main, rw_main, _pad_cols(gla_gate, LANES), wa, _pad_cols(gl, 2 * LANES)], axis=1)


def mixer_layer0(hn_proj, gla_gate_w2, gla_gate_b, gla_norm, rwkv_mu, rwkv_w0, rwkv_w2, rwkv_a0,
                 rwkv_a2, rwkv_g2, rwkv_k_k, rwkv_k_a, rwkv_r_k, rwkv_ln_w, rwkv_ln_b,
                 *, gla_block, rwkv_block):
    W = RWKV_WIDTH
    o_gla = gla_mixer(hn_proj, _pad_rows(gla_gate_w2, 0, LANES), gla_gate_b, gla_norm,
                      block=gla_block, heads=4)
    mu_r, mu_k, mu_v = rwkv_mu[:W], rwkv_mu[W:2 * W], rwkv_mu[2 * W:3 * W]
    mu_low = rwkv_mu[3 * W:]
    n_wa = RWKV_DECAY_LORA + RWKV_AAA_LORA
    vecs = jnp.stack([rwkv_w0, rwkv_a0, rwkv_k_k, rwkv_k_a, rwkv_r_k.reshape(-1), rwkv_ln_w,
                      rwkv_ln_b, mu_r, mu_k, mu_v] + [jnp.zeros((W,), F32)] * 6)
    mu_wa = mu_low[:n_wa].reshape(1, -1)
    mu_g = _pad_cols(mu_low[n_wa:].reshape(1, -1), 2 * LANES)
    zeros = lambda n: jnp.zeros((n, W), F32)
    lora_up = jnp.concatenate(
        [rwkv_w2, zeros(LANES - RWKV_DECAY_LORA), zeros(RWKV_DECAY_LORA), rwkv_a2, rwkv_g2,
         zeros(2 * LANES - RWKV_GATE_LORA)], axis=0)
    o_rwkv = rwkv_mixer(hn_proj, vecs, mu_wa, mu_g, lora_up, block=rwkv_block, pairs=4)
    return o_gla, o_rwkv


def kernel(x, norm_mix, norm_ffn, norm_final, mix_in_w, gla_gate_w2, gla_gate_b, gla_norm, rwkv_mu, rwkv_w0, rwkv_w2, rwkv_a0, rwkv_a2, rwkv_g2, rwkv_k_k, rwkv_k_a, rwkv_r_k, rwkv_ln_w, rwkv_ln_b, mix_out_w, attn_qkv_w, attn_out_w, ffn_gate_w, ffn_up_w, ffn_down_w):
    B, S, D = x.shape
    M = B * S
    tm, tn, tf = 512, 512, 256
    tm_in = min(1024, M)
    tm_ffn = min(1024, M)
    h = x.reshape(M, D)

    w_in = mix_in_layout_bf16(mix_in_w, layer=0, rows=256)
    p = norm_matmul(h, norm_mix[0], w_in, tm=tm_in, tn=tn).reshape(B, S, MIX_COLS)
    o_gla, o_rwkv = mixer_layer0(
        p, gla_gate_w2[0], gla_gate_b[0], gla_norm[0], rwkv_mu[0], rwkv_w0[0], rwkv_w2[0],
        rwkv_a0[0], rwkv_a2[0], rwkv_g2[0], rwkv_k_k[0], rwkv_k_a[0], rwkv_r_k[0], rwkv_ln_w[0],
        rwkv_ln_b[0], gla_block=512, rwkv_block=512)
    w_out = mix_out_w[0].astype(BF16)
    h = proj_residual(h, [(o_gla.reshape(M, GLA_WIDTH), w_out[:GLA_WIDTH]),
                          (o_rwkv.reshape(M, RWKV_WIDTH), w_out[GLA_WIDTH:])], tm=tm, tn=D)
    h = ffn_residual(h, norm_ffn[0], ffn_gate_w, ffn_up_w, ffn_down_w, norm_final, layer=0,
                     tm=tm_ffn, tf=tf, final_norm=False)

    qkv_h = norm_matmul(h, norm_mix[1], attn_qkv_w[0].astype(BF16), tm=tm_in, tn=2 * tn,
                        group=MOBA_HD)
    o_attn = moba_attention(qkv_h, B, S, heads=2)
    h = proj_residual(h, [(o_attn.reshape(M, D), attn_out_w[0].astype(BF16))], tm=tm, tn=D)
    h = ffn_residual(h, norm_ffn[1], ffn_gate_w, ffn_up_w, ffn_down_w, norm_final, layer=1,
                     tm=tm_ffn, tf=tf, final_norm=True)
    return h.reshape(B, S, D)
```
